```python
import numpy as np
import jax
import jax.numpy as jnp
from jax import lax

D_MODEL = 2048
BATCH = 2
SEQ = 4096
DEPTH = 1

PLE_DIM = 256
NSA_HEADS = 16
NSA_KV_GROUPS = 4
NSA_HPG = NSA_HEADS // NSA_KV_GROUPS
HEAD_DIM = 64
NSA_Q_WIDTH = NSA_HEADS * HEAD_DIM
NSA_KV_WIDTH = NSA_KV_GROUPS * HEAD_DIM
CMP_BLOCK = 32
CMP_STRIDE = 16
CMP_HIDDEN = 2 * HEAD_DIM
SLC_BLOCK = 64
SLC_TOPN = 16
WINDOW = 512
Q_BLOCK = 128
ATTN_SCALE = HEAD_DIM ** -0.5
NEG_INF = -1e30
FORCE_SCORE = 1e4
LRU_WIDTH = 1024
LRU_BLOCKS = 16
LRU_BW = LRU_WIDTH // LRU_BLOCKS
CONV_WIDTH = 4
LRU_C = 8.0
N_GROUPS = 4
EXPERTS_PER_GROUP = 8
N_EXPERTS = N_GROUPS * EXPERTS_PER_GROUP
EXPERT_TOPK = 2
D_EXPERT = 512
EPS = 1e-6
IN_SPLITS = (NSA_Q_WIDTH, NSA_KV_WIDTH, NSA_KV_WIDTH, NSA_KV_WIDTH, NSA_KV_WIDTH, NSA_KV_WIDTH, NSA_KV_WIDTH, 3 * NSA_HEADS, LRU_WIDTH, LRU_WIDTH, D_MODEL, D_MODEL)
IN_WIDTH = sum(IN_SPLITS)

kernel_name = 'hybrid_nsa_rglru_hmoe_block'


def _rms_norm(x, g):
    x32 = x.astype(jnp.float32)
    y = x32 * lax.rsqrt(jnp.mean(x32 * x32, axis=-1, keepdims=True) + EPS)
    return (y * g.astype(jnp.float32)).astype(x.dtype)


def _masked_softmax(s, mask):
    s = jnp.where(mask, s.astype(jnp.float32), NEG_INF)
    return jnp.where(mask, jax.nn.softmax(s, axis=-1), 0.0)


def _compress(kv, pos, w1, w2):
    b, s, g, hd = kv.shape
    nc = (s - CMP_BLOCK) // CMP_STRIDE + 1
    idx = np.arange(nc)[:, None] * CMP_STRIDE + np.arange(CMP_BLOCK)[None, :]
    blk = kv[:, idx] + pos[None, None, :, None, :]
    blk = jnp.moveaxis(blk, 3, 2).reshape(b, nc, g, CMP_BLOCK * hd)
    return jax.nn.gelu(blk @ w1) @ w2


def _overlap_matrix(nc, nsb):
    c0 = np.arange(nc) * CMP_STRIDE
    s0 = np.arange(nsb) * SLC_BLOCK
    ov = np.minimum(c0[:, None] + CMP_BLOCK, s0[None, :] + SLC_BLOCK) - np.maximum(c0[:, None], s0[None, :])
    return jnp.asarray(np.clip(ov, 0, None) / CMP_BLOCK, dtype=jnp.float32)


def _nsa(q, k_cmp, v_cmp, k_slc, v_slc, k_win, v_win, gates, ck_pos, ck_w1, ck_w2, cv_pos, cv_w1, cv_w2):
    b, s, _ = q.shape
    dt = q.dtype
    g, hpg, hd = NSA_KV_GROUPS, NSA_HPG, HEAD_DIM
    q = q.reshape(b, s, g, hpg, hd)
    kv_shape = (b, s, g, hd)
    k_cmp, v_cmp, k_slc, v_slc, k_win, v_win = [a.reshape(kv_shape) for a in (k_cmp, v_cmp, k_slc, v_slc, k_win, v_win)]
    t = jnp.arange(s)
    kc = _compress(k_cmp, ck_pos, ck_w1, ck_w2)
    vc = _compress(v_cmp, cv_pos, cv_w1, cv_w2)
    nc = kc.shape[1]
    cmp_end = jnp.arange(nc) * CMP_STRIDE + (CMP_BLOCK - 1)
    cmp_mask = cmp_end[None, :] <= t[:, None]
    sc = jnp.einsum('bsghd,bcgd->bghsc', q, kc) * ATTN_SCALE
    p_cmp = _masked_softmax(sc, cmp_mask)
    o_cmp = jnp.einsum('bghsc,bcgd->bsghd', p_cmp.astype(dt), vc)
    nsb = s // SLC_BLOCK
    imp = jnp.einsum('bghsc,cj->bgsj', p_cmp, _overlap_matrix(nc, nsb))
    blk_id = (t // SLC_BLOCK)[:, None]
    j = jnp.arange(nsb)[None, :]
    forced = (j == 0) | (j == blk_id) | (j == blk_id - 1)
    score = jnp.where(forced, FORCE_SCORE, jnp.where(j <= blk_id, imp, -1.0))
    top_n = min(SLC_TOPN, nsb)
    _, sel = lax.top_k(score, top_n)
    ks_blk = k_slc.reshape(b, nsb, SLC_BLOCK, g, hd).transpose(0, 3, 1, 2, 4)
    vs_blk = v_slc.reshape(b, nsb, SLC_BLOCK, g, hd).transpose(0, 3, 1, 2, 4)
    kw_pad = jnp.pad(k_win, ((0, 0), (WINDOW, 0), (0, 0), (0, 0)))
    vw_pad = jnp.pad(v_win, ((0, 0), (WINDOW, 0), (0, 0), (0, 0)))
    nqb = s // Q_BLOCK
    q_blocks = jnp.moveaxis(q.reshape(b, nqb, Q_BLOCK, g, hpg, hd), 1, 0)
    sel_blocks = jnp.moveaxis(sel.reshape(b, g, nqb, Q_BLOCK, top_n), 2, 0)
    bi_idx = jnp.arange(b)[:, None, None, None]
    gi_idx = jnp.arange(g)[None, :, None, None]
    offs_q = jnp.arange(Q_BLOCK)
    offs_slc = jnp.arange(SLC_BLOCK)
    offs_win = jnp.arange(WINDOW + Q_BLOCK)

    def block(args):
        qb, sb, blk = args
        tq = blk * Q_BLOCK + offs_q
        ks = ks_blk[bi_idx, gi_idx, sb]
        vs = vs_blk[bi_idx, gi_idx, sb]
        kpos = sb[..., None] * SLC_BLOCK + offs_slc
        m_s = (kpos <= tq[None, None, :, None, None]).reshape(b, g, 1, Q_BLOCK, top_n * SLC_BLOCK)
        ss = jnp.einsum('bqghd,bgqnkd->bghqnk', qb, ks).reshape(b, g, hpg, Q_BLOCK, top_n * SLC_BLOCK) * ATTN_SCALE
        ps = _masked_softmax(ss, m_s).astype(dt).reshape(b, g, hpg, Q_BLOCK, top_n, SLC_BLOCK)
        o_s = jnp.einsum('bghqnk,bgqnkd->bqghd', ps, vs)
        kw = lax.dynamic_slice_in_dim(kw_pad, blk * Q_BLOCK, WINDOW + Q_BLOCK, axis=1)
        vw = lax.dynamic_slice_in_dim(vw_pad, blk * Q_BLOCK, WINDOW + Q_BLOCK, axis=1)
        spos = blk * Q_BLOCK - WINDOW + offs_win
        diff = tq[:, None] - spos[None, :]
        m_w = (diff >= 0) & (diff < WINDOW) & (spos[None, :] >= 0)
        sw = jnp.einsum('bqghd,bkgd->bghqk', qb, kw) * ATTN_SCALE
        pw = _masked_softmax(sw, m_w).astype(dt)
        o_w = jnp.einsum('bghqk,bkgd->bqghd', pw, vw)
        return o_s, o_w

    o_slc, o_win = lax.map(block, (q_blocks, sel_blocks, jnp.arange(nqb)))
    o_slc = jnp.moveaxis(o_slc, 0, 1).reshape(b, s, g, hpg, hd)
    o_win = jnp.moveaxis(o_win, 0, 1).reshape(b, s, g, hpg, hd)
    gt = jax.nn.sigmoid(gates.astype(jnp.float32)).reshape(b, s, g, hpg, 3, 1).astype(dt)
    o = gt[..., 0, :] * o_cmp + gt[..., 1, :] * o_slc + gt[..., 2, :] * o_win
    return o.reshape(b, s, NSA_Q_WIDTH)


def _linear_combine(left, right):
    a_l, b_l = left
    a_r, b_r = right
    return a_l * a_r, a_r * b_l + b_r


def _rg_lru_branch(xb, yb, conv_w, conv_b, wa, ba, wx, bx, lam):
    b, s, _ = xb.shape
    dt = xb.dtype
    xc = lax.conv_general_dilated(xb, conv_w[:, None, :], window_strides=(1,), padding=[(CONV_WIDTH - 1, 0)], dimension_numbers=('NWC', 'WIO', 'NWC'), feature_group_count=LRU_WIDTH) + conv_b
    xg = xc.reshape(b, s, LRU_BLOCKS, LRU_BW)
    r = jax.nn.sigmoid((jnp.einsum('bsni,nij->bsnj', xg, wa) + ba).astype(jnp.float32))
    i = jax.nn.sigmoid((jnp.einsum('bsni,nij->bsnj', xg, wx) + bx).astype(jnp.float32))
    log_a = -LRU_C * jax.nn.softplus(-lam.astype(jnp.float32)).reshape(LRU_BLOCKS, LRU_BW) * r
    a = jnp.exp(log_a)
    u = jnp.sqrt(-jnp.expm1(2.0 * log_a)) * (i * xg.astype(jnp.float32))
    _, h = lax.associative_scan(_linear_combine, (a, u), axis=1)
    return h.reshape(b, s, LRU_WIDTH).astype(dt) * jax.nn.gelu(yb)


def _hier_moe(xn, w_grp, b_grp, w_exp, b_exp, w_gate, w_up, w_down):
    b, s, d = xn.shape
    xt = xn.reshape(b * s, d)
    n = xt.shape[0]
    g_prob = jax.nn.softmax((xt @ w_grp + b_grp).astype(jnp.float32), axis=-1)
    g_val, g_idx = lax.top_k(g_prob, 1)
    e_logits = (xt @ w_exp + b_exp).astype(jnp.float32).reshape(n, N_GROUPS, EXPERTS_PER_GROUP)
    e_in = jnp.take_along_axis(e_logits, jnp.broadcast_to(g_idx[:, :, None], (n, 1, EXPERTS_PER_GROUP)), axis=1)[:, 0]
    e_prob = jax.nn.softmax(e_in, axis=-1)
    e_val, e_idx = lax.top_k(e_prob, EXPERT_TOPK)
    w = g_val * e_val / jnp.sum(e_val, axis=-1, keepdims=True)
    eid = g_idx * EXPERTS_PER_GROUP + e_idx
    comb = jnp.sum(jax.nn.one_hot(eid, N_EXPERTS, dtype=jnp.float32) * w[..., None], axis=1).astype(xt.dtype)
    out = jnp.zeros_like(xt)
    for e in range(N_EXPERTS):
        hid = jax.nn.silu(xt @ w_gate[e]) * (xt @ w_up[e])
        out = out + comb[:, e:e + 1] * (hid @ w_down[e])
    return out.reshape(b, s, d)


def _dense(key, shape, fan_in):
    return jax.random.normal(key, shape, jnp.float32) * (fan_in ** -0.5)


def setup_inputs(seed: int = 0) -> dict:
    key = jax.random.key(seed)
    ks = jax.random.split(key, 40)
    L = DEPTH
    f32 = jnp.float32
    u = jax.random.uniform(ks[16], (L, LRU_WIDTH), f32, minval=0.9, maxval=0.999)
    sa = u ** (1.0 / LRU_C)
    return {
        'x': jax.random.normal(ks[0], (BATCH, SEQ, D_MODEL), f32),
        'p': jax.random.normal(ks[1], (L, BATCH, SEQ, PLE_DIM), f32),
        'ln_mix': 1.0 + 0.05 * jax.random.normal(ks[2], (L, D_MODEL), f32),
        'w_in': _dense(ks[3], (L, D_MODEL, IN_WIDTH), D_MODEL),
        'cmp_k_pos': 0.02 * jax.random.normal(ks[4], (L, CMP_BLOCK, HEAD_DIM), f32),
        'cmp_k_w1': _dense(ks[5], (L, CMP_BLOCK * HEAD_DIM, CMP_HIDDEN), CMP_BLOCK * HEAD_DIM),
        'cmp_k_w2': _dense(ks[6], (L, CMP_HIDDEN, HEAD_DIM), CMP_HIDDEN),
        'cmp_v_pos': 0.02 * jax.random.normal(ks[7], (L, CMP_BLOCK, HEAD_DIM), f32),
        'cmp_v_w1': _dense(ks[8], (L, CMP_BLOCK * HEAD_DIM, CMP_HIDDEN), CMP_BLOCK * HEAD_DIM),
        'cmp_v_w2': _dense(ks[9], (L, CMP_HIDDEN, HEAD_DIM), CMP_HIDDEN),
        'conv_w': _dense(ks[10], (L, CONV_WIDTH, LRU_WIDTH), CONV_WIDTH),
        'conv_b': 0.01 * jax.random.normal(ks[11], (L, LRU_WIDTH), f32),
        'lru_wa': _dense(ks[12], (L, LRU_BLOCKS, LRU_BW, LRU_BW), LRU_BW),
        'lru_ba': 0.01 * jax.random.normal(ks[13], (L, LRU_BLOCKS, LRU_BW), f32),
        'lru_wx': _dense(ks[14], (L, LRU_BLOCKS, LRU_BW, LRU_BW), LRU_BW),
        'lru_bx': 0.01 * jax.random.normal(ks[15], (L, LRU_BLOCKS, LRU_BW), f32),
        'lru_lambda': jnp.log(sa) - jnp.log1p(-sa),
        'w_nsa_up': _dense(ks[17], (L, NSA_Q_WIDTH, D_MODEL), NSA_Q_WIDTH),
        'w_lru_up': _dense(ks[18], (L, LRU_WIDTH, D_MODEL), LRU_WIDTH),
        'w_out': _dense(ks[19], (L, D_MODEL, D_MODEL), D_MODEL),
        'ln_ffn': 1.0 + 0.05 * jax.random.normal(ks[20], (L, D_MODEL), f32),
        'w_grp': _dense(ks[21], (L, D_MODEL, N_GROUPS), D_MODEL),
        'b_grp': 0.01 * jax.random.normal(ks[22], (L, N_GROUPS), f32),
        'w_exp': _dense(ks[23], (L, D_MODEL, N_EXPERTS), D_MODEL),
        'b_exp': 0.01 * jax.random.normal(ks[24], (L, N_EXPERTS), f32),
        'w_gate': _dense(ks[25], (L, N_EXPERTS, D_MODEL, D_EXPERT), D_MODEL),
        'w_up': _dense(ks[26], (L, N_EXPERTS, D_MODEL, D_EXPERT), D_MODEL),
        'w_down': _dense(ks[27], (L, N_EXPERTS, D_EXPERT, D_MODEL), D_EXPERT),
        'ln_ple': 1.0 + 0.05 * jax.random.normal(ks[28], (L, D_MODEL), f32),
        'w_ple': _dense(ks[29], (L, PLE_DIM, D_MODEL), PLE_DIM),
        'w_ple_gate': _dense(ks[30], (L, D_MODEL, D_MODEL), D_MODEL),
        'ln_final': 1.0 + 0.05 * jax.random.normal(ks[31], (D_MODEL,), f32),
    }


def reference(x, p, ln_mix, w_in, cmp_k_pos, cmp_k_w1, cmp_k_w2, cmp_v_pos, cmp_v_w1, cmp_v_w2, conv_w, conv_b, lru_wa, lru_ba, lru_wx, lru_bx, lru_lambda, w_nsa_up, w_lru_up, w_out, ln_ffn, w_grp, b_grp, w_exp, b_exp, w_gate, w_up, w_down, ln_ple, w_ple, w_ple_gate, ln_final):
    split_at = [int(v) for v in np.cumsum(IN_SPLITS)[:-1]]
    for i in range(DEPTH):
        h = _rms_norm(x, ln_mix[i])
        z = h @ w_in[i]
        q, k_c, v_c, k_s, v_s, k_w, v_w, nsa_g, lru_x, lru_y, mg_a, mg_b = jnp.split(z, split_at, axis=-1)
        y_a = _nsa(q, k_c, v_c, k_s, v_s, k_w, v_w, nsa_g, cmp_k_pos[i], cmp_k_w1[i], cmp_k_w2[i], cmp_v_pos[i], cmp_v_w1[i], cmp_v_w2[i]) @ w_nsa_up[i]
        y_b = _rg_lru_branch(lru_x, lru_y, conv_w[i], conv_b[i], lru_wa[i], lru_ba[i], lru_wx[i], lru_bx[i], lru_lambda[i]) @ w_lru_up[i]
        merged = jax.nn.sigmoid(mg_a) * y_a + jax.nn.sigmoid(mg_b) * y_b
        x = x + merged @ w_out[i]
        x = x + _hier_moe(_rms_norm(x, ln_ffn[i]), w_grp[i], b_grp[i], w_exp[i], b_exp[i], w_gate[i], w_up[i], w_down[i])
        ple_gate = jax.nn.sigmoid(_rms_norm(x, ln_ple[i]) @ w_ple_gate[i])
        x = x + ple_gate * (p[i] @ w_ple[i])
    return _rms_norm(x, ln_final)
```

```python
import functools

import numpy as np
import jax
import jax.numpy as jnp
from jax import lax
from jax.experimental import pallas as pl
from jax.experimental.pallas import tpu as pltpu

F32 = jnp.float32
BF16 = jnp.bfloat16

NSA_HEADS = 16
NSA_KV_GROUPS = 4
NSA_HPG = NSA_HEADS // NSA_KV_GROUPS
HEAD_DIM = 64
NSA_Q_WIDTH = NSA_HEADS * HEAD_DIM
NSA_KV_WIDTH = NSA_KV_GROUPS * HEAD_DIM
CMP_BLOCK = 32
CMP_STRIDE = 16
CMP_HIDDEN = 2 * HEAD_DIM
SLC_BLOCK = 64
SLC_TOPN = 16
WINDOW = 512
Q_BLOCK = 128
ATTN_SCALE = HEAD_DIM ** -0.5
NEG_INF = -1e30
FORCE_SCORE = 1e4
LRU_WIDTH = 1024
LRU_BLOCKS = 16
LRU_BW = LRU_WIDTH // LRU_BLOCKS
CONV_WIDTH = 4
LRU_C = 8.0
N_GROUPS = 4
EXPERTS_PER_GROUP = 8
N_EXPERTS = N_GROUPS * EXPERTS_PER_GROUP
EXPERT_TOPK = 2
D_EXPERT = 512
EPS = 1e-6

LANES = 128
SUBLANES = 8
VMEM_LIMIT_BYTES = 56 * 1024 * 1024

Z_MGA = 0
Z_MGB = 2048
Z_Q = 4096
Z_KC = 5120
Z_VC = 5376
Z_KS = 5632
Z_VS = 5888
Z_KW = 6144
Z_VW = 6400
Z_LX = 6656
Z_LY = 7680
Z_GATE = 8704
Z_WIDTH = 9216
LRU_CB = 256
MOE_TM = 256


def _cparams(sem, vmem=VMEM_LIMIT_BYTES):
    return pltpu.CompilerParams(dimension_semantics=sem, vmem_limit_bytes=vmem)


def _rms(x, g):
    return x * lax.rsqrt(jnp.mean(x * x, axis=-1, keepdims=True) + EPS) * g


def _gelu_tanh(x):
    return 0.5 * x * (1.0 + jnp.tanh(np.sqrt(2.0 / np.pi) * (x + 0.044715 * (x * x * x))))


def _sigmoid(x):
    return 1.0 / (1.0 + jnp.exp(-x))


def _in_proj_kernel(x_ref, g_ref, w_ref, o_ref, h_ref):
    @pl.when(pl.program_id(1) == 0)
    def _():
        h_ref[...] = _rms(x_ref[...], g_ref[...]).astype(BF16)

    o_ref[...] = jnp.dot(h_ref[...], w_ref[...], preferred_element_type=F32).astype(o_ref.dtype)


def _in_proj(x2d, g, w, tm=512, tn=1024):
    n, d = x2d.shape
    nw = w.shape[1]
    return pl.pallas_call(
        _in_proj_kernel,
        out_shape=jax.ShapeDtypeStruct((n, nw), BF16),
        grid=(n // tm, nw // tn),
        in_specs=[
            pl.BlockSpec((tm, d), lambda i, j: (i, 0)),
            pl.BlockSpec((1, d), lambda i, j: (0, 0)),
            pl.BlockSpec((d, tn), lambda i, j: (0, j)),
        ],
        out_specs=pl.BlockSpec((tm, tn), lambda i, j: (i, j)),
        scratch_shapes=[pltpu.VMEM((tm, d), BF16)],
        compiler_params=_cparams(("parallel", "arbitrary")),
        name="in_proj",
    )(x2d, g, w)


def _compress_kernel(kx_ref, vx_ref, kpos_ref, kw1_ref, kw2_ref, vpos_ref, vw1_ref, vw2t_ref, kc_ref, vct_ref):
    half = kx_ref.shape[-1]

    def hidden(x_ref, pos_ref, w1_ref):
        x = x_ref[0, 0]
        w1 = w1_ref[...]
        ha = jnp.dot(x, w1[:half], preferred_element_type=F32)
        hb = jnp.dot(x, w1[half:], preferred_element_type=F32)
        hb = pltpu.roll(hb, hb.shape[0] - 1, axis=0)
        pc = jnp.dot(pos_ref[...], w1, preferred_element_type=F32)[0:1]
        return _gelu_tanh(ha + hb + pc).astype(BF16)

    hk = hidden(kx_ref, kpos_ref, kw1_ref)
    kc_ref[0, 0] = jnp.dot(hk, kw2_ref[...], preferred_element_type=F32).astype(kc_ref.dtype)
    hv = hidden(vx_ref, vpos_ref, vw1_ref)
    vct_ref[0, 0] = lax.dot_general(vw2t_ref[...], hv, (((1,), (1,)), ((), ())),
                                    preferred_element_type=F32).astype(vct_ref.dtype)


def _compress(kx, vx, kpos, kw1, kw2, vpos, vw1, vw2t):
    b, g, ncp, half = kx.shape
    full = lambda a: pl.BlockSpec(a.shape, lambda i, j: (0,) * a.ndim)
    return pl.pallas_call(
        _compress_kernel,
        out_shape=(jax.ShapeDtypeStruct((b, g, ncp, HEAD_DIM), BF16),
                   jax.ShapeDtypeStruct((b, g, HEAD_DIM, ncp), BF16)),
        grid=(b, g),
        in_specs=[
            pl.BlockSpec((1, 1, ncp, half), lambda i, j: (i, j, 0, 0)),
            pl.BlockSpec((1, 1, ncp, half), lambda i, j: (i, j, 0, 0)),
            full(kpos), full(kw1), full(kw2), full(vpos), full(vw1), full(vw2t),
        ],
        out_specs=(pl.BlockSpec((1, 1, ncp, HEAD_DIM), lambda i, j: (i, j, 0, 0)),
                   pl.BlockSpec((1, 1, HEAD_DIM, ncp), lambda i, j: (i, j, 0, 0))),
        compiler_params=_cparams(("parallel", "parallel")),
        name="compress",
    )(kx, vx, kpos, kw1, kw2, vpos, vw1, vw2t)


def _nsa_kernel(qt_ref, kc_ref, vct_ref, ks_ref, vst_ref, kw_ref, vwt_ref, gate_ref, o_ref, score_ref, selb_ref):
    qb = pl.program_id(2)
    nq = Q_BLOCK * NSA_HPG
    ncp = kc_ref.shape[2]
    nsb = score_ref.shape[0]
    q = (qt_ref[0, 0, 0].astype(F32) * ATTN_SCALE).astype(BF16)
    t_lane = qb * Q_BLOCK + (lax.broadcasted_iota(jnp.int32, (1, nq), 1) & (Q_BLOCK - 1))

    sc = jnp.dot(kc_ref[0, 0], q, preferred_element_type=F32)
    cmp_end = lax.broadcasted_iota(jnp.int32, (ncp, 1), 0) * CMP_STRIDE + (CMP_BLOCK - 1)
    cmask = cmp_end <= t_lane
    sc = jnp.where(cmask, sc, NEG_INF)
    pc = jnp.where(cmask, jnp.exp(sc - jnp.max(sc, axis=0, keepdims=True)), 0.0)
    lc = jnp.sum(pc, axis=0, keepdims=True)
    pc = pc * (1.0 / jnp.where(lc > 0.0, lc, 1.0))
    o_cmp = jnp.dot(vct_ref[0, 0], pc.astype(BF16), preferred_element_type=F32)

    psum = pc[:, 0:Q_BLOCK]
    for h in range(1, NSA_HPG):
        psum = psum + pc[:, h * Q_BLOCK:(h + 1) * Q_BLOCK]
    jrow = lax.broadcasted_iota(jnp.int32, (nsb, ncp), 0)
    crel = lax.broadcasted_iota(jnp.int32, (nsb, ncp), 1) - jrow * (SLC_BLOCK // CMP_STRIDE)
    ovt = jnp.where((crel >= 0) & (crel <= 2), 1.0, jnp.where((crel == -1) | (crel == 3), 0.5, 0.0)).astype(F32)
    imp = jnp.dot(ovt, psum, preferred_element_type=F32, precision=lax.Precision.HIGHEST)
    jcol = lax.broadcasted_iota(jnp.int32, (nsb, 1), 0)
    blk = (qb * Q_BLOCK + lax.broadcasted_iota(jnp.int32, (1, Q_BLOCK), 1)) >> (SLC_BLOCK.bit_length() - 1)
    forced = (jcol == 0) | (jcol == blk) | (jcol == blk - 1)
    score = jnp.where(forced, FORCE_SCORE, jnp.where(jcol <= blk, imp, -1.0))
    score_ref[...] = score

    def rank_step(k, cnt):
        row = score_ref[pl.ds(k, 1), :]
        ahead = (row > score) | ((row == score) & (k < jcol))
        return cnt + jnp.where(ahead, 1.0, 0.0)

    n_vis = jnp.minimum(qb * (Q_BLOCK // SLC_BLOCK) + Q_BLOCK // SLC_BLOCK, nsb)
    rank = lax.fori_loop(0, n_vis, rank_step, jnp.zeros((nsb, Q_BLOCK), F32))
    selb_ref[...] = jnp.where(rank < float(min(SLC_TOPN, nsb)), 0.0, NEG_INF)

    ck = 2 * Q_BLOCK
    bpc = ck // SLC_BLOCK

    def slc_scores(c):
        k = ks_ref[0, 0, pl.ds(pl.multiple_of(c * ck, ck), ck), :]
        s = jnp.dot(k, q, preferred_element_type=F32)
        rows = [jnp.broadcast_to(selb_ref[pl.ds(c * bpc + i, 1), :], (SLC_BLOCK, Q_BLOCK)) for i in range(bpc)]
        bias = jnp.concatenate(rows, axis=0)
        return s + jnp.concatenate([bias] * NSA_HPG, axis=1)

    def slc_update(c, s, carry):
        m, l, acc = carry
        m_new = jnp.maximum(m, jnp.max(s, axis=0, keepdims=True))
        alpha = jnp.exp(m - m_new)
        p = jnp.exp(s - m_new)
        l = alpha * l + jnp.sum(p, axis=0, keepdims=True)
        vt = jnp.concatenate([vst_ref[0, 0, 2 * c], vst_ref[0, 0, 2 * c + 1]], axis=1)
        acc = acc * alpha + jnp.dot(vt, p.astype(BF16), preferred_element_type=F32)
        return m_new, l, acc

    def slc_body(c, carry):
        return slc_update(c, slc_scores(c), carry)

    init = (jnp.full((1, nq), NEG_INF, F32), jnp.zeros((1, nq), F32), jnp.zeros((HEAD_DIM, nq), F32))
    n_full = qb // 2
    carry = lax.fori_loop(0, n_full, slc_body, init)
    s_d = slc_scores(n_full)
    kpos_d = n_full * ck + lax.broadcasted_iota(jnp.int32, (ck, 1), 0)
    s_d = jnp.where(kpos_d <= t_lane, s_d, NEG_INF)
    _, l_s, acc_s = slc_update(n_full, s_d, carry)
    o_slc = acc_s * (1.0 / l_s)

    nwc = WINDOW // Q_BLOCK + 1
    ks_w, vt_w = [], []
    for i in range(nwc):
        c = jnp.maximum(qb - (nwc - 1) + i, 0)
        ks_w.append(kw_ref[0, 0, pl.ds(pl.multiple_of(c * Q_BLOCK, Q_BLOCK), Q_BLOCK), :])
        vt_w.append(vwt_ref[0, 0, c])
    sw = jnp.dot(jnp.concatenate(ks_w, axis=0), q, preferred_element_type=F32)
    spos = (qb - (nwc - 1)) * Q_BLOCK + lax.broadcasted_iota(jnp.int32, (nwc * Q_BLOCK, 1), 0)
    diff = t_lane - spos
    wmask = (diff >= 0) & (diff < WINDOW) & (spos >= 0)
    sw = jnp.where(wmask, sw, NEG_INF)
    pw = jnp.where(wmask, jnp.exp(sw - jnp.max(sw, axis=0, keepdims=True)), 0.0)
    lw = jnp.sum(pw, axis=0, keepdims=True)
    o_win = jnp.dot(jnp.concatenate(vt_w, axis=1), pw.astype(BF16), preferred_element_type=F32) * (1.0 / lw)

    gt = _sigmoid(gate_ref[0, 0, 0].astype(F32))
    o = gt[0:1] * o_cmp + gt[1:2] * o_slc + gt[2:3] * o_win
    o_ref[0, 0, 0] = o.astype(o_ref.dtype)


def _nsa_attention(qt, kc, vct, ks, vst, kw, vwt, gates):
    b, g, nqb, hd, nq = qt.shape
    s = ks.shape[2]
    ncp = kc.shape[2]
    nsb = s // SLC_BLOCK
    bg = lambda *blk: pl.BlockSpec((1, 1) + blk, lambda i, j, k: (i, j) + (0,) * len(blk))
    return pl.pallas_call(
        _nsa_kernel,
        out_shape=jax.ShapeDtypeStruct((b, g, nqb, hd, nq), BF16),
        grid=(b, g, nqb),
        in_specs=[
            pl.BlockSpec((1, 1, 1, hd, nq), lambda i, j, k: (i, j, k, 0, 0)),
            bg(ncp, hd), bg(hd, ncp),
            bg(s, hd), bg(nqb, hd, Q_BLOCK),
            bg(s, hd), bg(nqb, hd, Q_BLOCK),
            pl.BlockSpec((1, 1, 1, SUBLANES, nq), lambda i, j, k: (i, j, k, 0, 0)),
        ],
        out_specs=pl.BlockSpec((1, 1, 1, hd, nq), lambda i, j, k: (i, j, k, 0, 0)),
        scratch_shapes=[pltpu.VMEM((nsb, Q_BLOCK), F32), pltpu.VMEM((nsb, Q_BLOCK), F32)],
        compiler_params=_cparams(("parallel", "parallel", "arbitrary")),
        name="nsa_attn",
    )(qt, kc, vct, ks, vst, kw, vwt, gates)


def _rglru_kernel(x_ref, y_ref, cw_ref, cb_ref, wa_ref, ba_ref, wx_ref, bx_ref, lam_ref, o_ref,
                  tail_ref, h_ref, a_ref, u_ref):
    tc = pl.program_id(2)
    tt, cb = x_ref.shape

    @pl.when(tc == 0)
    def _():
        tail_ref[...] = jnp.zeros_like(tail_ref)
        h_ref[...] = jnp.zeros_like(h_ref)

    x = x_ref[...].astype(F32)
    xe = jnp.concatenate([tail_ref[...], x], axis=0)
    tail_ref[...] = x[tt - SUBLANES:, :]
    cw = cw_ref[...]
    xc = cb_ref[...]
    for k in range(CONV_WIDTH):
        off = SUBLANES - (CONV_WIDTH - 1) + k
        xc = xc + cw[k:k + 1, :] * xe[off:off + tt, :]
    xcb = xc.astype(BF16)
    r = _sigmoid(jnp.dot(xcb, wa_ref[0], preferred_element_type=F32) + ba_ref[...])
    ig = _sigmoid(jnp.dot(xcb, wx_ref[0], preferred_element_type=F32) + bx_ref[...])
    nl = -lam_ref[...]
    softplus = jnp.maximum(nl, 0.0) + jnp.log1p(jnp.exp(-jnp.abs(nl)))
    log_a = (-LRU_C) * softplus * r
    a_ref[...] = jnp.exp(log_a)
    th = jnp.tanh(log_a)
    u_ref[...] = jnp.sqrt(-2.0 * th / (1.0 - th)) * (ig * xc)

    row = lax.broadcasted_iota(jnp.int32, (SUBLANES, cb), 0)

    def step(i, h):
        sl = pl.ds(pl.multiple_of(i * SUBLANES, SUBLANES), SUBLANES)
        a = a_ref[sl, :]
        u = u_ref[sl, :]
        for s in (1, 2, 4):
            a_s = jnp.where(row >= s, pltpu.roll(a, s, axis=0), 1.0)
            u_s = jnp.where(row >= s, pltpu.roll(u, s, axis=0), 0.0)
            u = a * u_s + u
            a = a * a_s
        hrows = a * h + u
        u_ref[sl, :] = hrows
        return hrows[SUBLANES - 1:SUBLANES, :]

    h_ref[...] = lax.fori_loop(0, tt // SUBLANES, step, h_ref[...], unroll=2)
    o_ref[...] = (u_ref[...] * _gelu_tanh(y_ref[...].astype(F32))).astype(o_ref.dtype)


def _rglru(z, batch, seq, cw, cbias, wa_bd, ba, wx_bd, bx, lam, tt=512):
    n = z.shape[0]
    ncb = LRU_WIDTH // LRU_CB
    nt = seq // tt
    row = lambda i, j, k: i * nt + k
    vec = lambda r: pl.BlockSpec((r, LRU_CB), lambda i, j, k: (0, j))
    return pl.pallas_call(
        _rglru_kernel,
        out_shape=jax.ShapeDtypeStruct((n, LRU_WIDTH), BF16),
        grid=(batch, ncb, nt),
        in_specs=[
            pl.BlockSpec((tt, LRU_CB), lambda i, j, k: (row(i, j, k), Z_LX // LRU_CB + j)),
            pl.BlockSpec((tt, LRU_CB), lambda i, j, k: (row(i, j, k), Z_LY // LRU_CB + j)),
            vec(CONV_WIDTH), vec(1),
            pl.BlockSpec((1, LRU_CB, LRU_CB), lambda i, j, k: (j, 0, 0)), vec(1),
            pl.BlockSpec((1, LRU_CB, LRU_CB), lambda i, j, k: (j, 0, 0)), vec(1),
            vec(1),
        ],
        out_specs=pl.BlockSpec((tt, LRU_CB), lambda i, j, k: (row(i, j, k), j)),
        scratch_shapes=[pltpu.VMEM((SUBLANES, LRU_CB), F32), pltpu.VMEM((1, LRU_CB), F32),
                        pltpu.VMEM((tt, LRU_CB), F32), pltpu.VMEM((tt, LRU_CB), F32)],
        compiler_params=_cparams(("parallel", "parallel", "arbitrary")),
        name="rglru",
    )(z, z, cw, cbias, wa_bd, ba, wx_bd, bx, lam)


def _merge_kernel(o_ref, l_ref, mga_ref, mgb_ref, wn_ref, wl_ref, m_ref):
    ya = jnp.dot(o_ref[...], wn_ref[...], preferred_element_type=F32)
    yb = jnp.dot(l_ref[...], wl_ref[...], preferred_element_type=F32)
    m = _sigmoid(mga_ref[...].astype(F32)) * ya + _sigmoid(mgb_ref[...].astype(F32)) * yb
    m_ref[...] = m.astype(m_ref.dtype)


def _merge(o, lru, z, wn, wl, tm=512):
    n, d = o.shape[0], wn.shape[1]
    return pl.pallas_call(
        _merge_kernel,
        out_shape=jax.ShapeDtypeStruct((n, d), BF16),
        grid=(n // tm,),
        in_specs=[
            pl.BlockSpec((tm, o.shape[1]), lambda i: (i, 0)),
            pl.BlockSpec((tm, lru.shape[1]), lambda i: (i, 0)),
            pl.BlockSpec((tm, d), lambda i: (i, Z_MGA // d)),
            pl.BlockSpec((tm, d), lambda i: (i, Z_MGB // d)),
            pl.BlockSpec(wn.shape, lambda i: (0, 0)),
            pl.BlockSpec(wl.shape, lambda i: (0, 0)),
        ],
        out_specs=pl.BlockSpec((tm, d), lambda i: (i, 0)),
        compiler_params=_cparams(("parallel",)),
        name="merge",
    )(o, lru, z, z, wn, wl)


def _out_route_kernel(m_ref, x_ref, wo_ref, g_ref, wr_ref, br_ref, x1_ref, xn_ref, eid_ref, ew_ref):
    x1 = x_ref[...] + jnp.dot(m_ref[...], wo_ref[...], preferred_element_type=F32)
    x1_ref[...] = x1
    xn = _rms(x1, g_ref[...])
    xn_ref[...] = xn
    lg = lax.dot_general(wr_ref[...], xn, (((1,), (1,)), ((), ())), preferred_element_type=F32,
                         precision=lax.Precision.HIGHEST) + br_ref[...]
    tm = lg.shape[1]
    sub = lax.broadcasted_iota(jnp.int32, (SUBLANES, tm), 0)

    def first_argmax(v, vmax):
        return jnp.min(jnp.where(v == vmax, sub, SUBLANES), axis=0, keepdims=True)

    gl = jnp.where(sub < N_GROUPS, lg[0:SUBLANES], -jnp.inf)
    gmax = jnp.max(gl, axis=0, keepdims=True)
    ge = jnp.exp(gl - gmax)
    gprob = ge / jnp.sum(ge, axis=0, keepdims=True)
    g_val = jnp.max(gprob, axis=0, keepdims=True)
    g_idx = first_argmax(gprob, g_val)
    e_in = jnp.zeros((EXPERTS_PER_GROUP, tm), F32)
    for gi in range(N_GROUPS):
        lo = SUBLANES + gi * EXPERTS_PER_GROUP
        e_in = jnp.where(g_idx == gi, lg[lo:lo + EXPERTS_PER_GROUP], e_in)
    ee = jnp.exp(e_in - jnp.max(e_in, axis=0, keepdims=True))
    eprob = ee / jnp.sum(ee, axis=0, keepdims=True)
    v1 = jnp.max(eprob, axis=0, keepdims=True)
    i1 = first_argmax(eprob, v1)
    rest = jnp.where(sub == i1, -1.0, eprob)
    v2 = jnp.max(rest, axis=0, keepdims=True)
    i2 = first_argmax(rest, v2)
    den = v1 + v2
    eid = jnp.where(sub == 0, g_idx * EXPERTS_PER_GROUP + i1, g_idx * EXPERTS_PER_GROUP + i2)
    eid_ref[...] = eid
    ew_ref[...] = jnp.where(sub == 0, g_val * v1 / den, g_val * v2 / den)


def _out_route(m, x2d, wo, g, wr, br, tm=256):
    n, d = x2d.shape
    nr = wr.shape[0]
    return pl.pallas_call(
        _out_route_kernel,
        out_shape=(jax.ShapeDtypeStruct((n, d), F32), jax.ShapeDtypeStruct((n, d), F32),
                   jax.ShapeDtypeStruct((SUBLANES, n), jnp.int32), jax.ShapeDtypeStruct((SUBLANES, n), F32)),
        grid=(n // tm,),
        in_specs=[
            pl.BlockSpec((tm, d), lambda i: (i, 0)),
            pl.BlockSpec((tm, d), lambda i: (i, 0)),
            pl.BlockSpec((d, d), lambda i: (0, 0)),
            pl.BlockSpec((1, d), lambda i: (0, 0)),
            pl.BlockSpec((nr, d), lambda i: (0, 0)),
            pl.BlockSpec((nr, 1), lambda i: (0, 0)),
        ],
        out_specs=(pl.BlockSpec((tm, d), lambda i: (i, 0)), pl.BlockSpec((tm, d), lambda i: (i, 0)),
                   pl.BlockSpec((SUBLANES, tm), lambda i: (0, i)), pl.BlockSpec((SUBLANES, tm), lambda i: (0, i))),
        compiler_params=_cparams(("parallel",)),
        name="out_route",
    )(m, x2d, wo, g, wr, br)


def _for_rows(cnt, fn, unroll=8):
    sh = unroll.bit_length() - 1

    def group(gidx, c):
        for u in range(unroll):
            fn(gidx * unroll + u)
        return c

    def single(r, c):
        fn(r)
        return c

    lax.fori_loop(0, cnt >> sh, group, 0)
    lax.fori_loop((cnt >> sh) << sh, cnt, single, 0)


def _moe_kernel(nvalid_ref, texp_ref, cnt_ref, tok_ref, tokn_ref, dst_ref, xn_hbm, wg_ref, wu_ref, wd_ref, y_hbm,
                xbuf, ybuf, gsem, ssem):
    i = pl.program_id(0)
    nv = nvalid_ref[0]
    slot = i % 2

    def gather_row(idx_ref, s):
        def fn(r):
            pltpu.make_async_copy(xn_hbm.at[pl.ds(idx_ref[0, 0, r], 1), :], xbuf.at[s, pl.ds(r, 1), :],
                                  gsem.at[s]).start()
        return fn

    def gather_wait_row(s):
        def fn(r):
            pltpu.make_async_copy(xn_hbm.at[pl.ds(0, 1), :], xbuf.at[s, pl.ds(r, 1), :], gsem.at[s]).wait()
        return fn

    def scatter_row(s):
        def fn(r):
            pltpu.make_async_copy(ybuf.at[s, pl.ds(r, 1), :], y_hbm.at[pl.ds(dst_ref[0, 0, r], 1), :],
                                  ssem.at[s]).start()
        return fn

    def scatter_wait_row(s):
        def fn(r):
            pltpu.make_async_copy(ybuf.at[s, pl.ds(r, 1), :], y_hbm.at[pl.ds(0, 1), :], ssem.at[s]).wait()
        return fn

    @pl.when(i == 0)
    def _():
        xbuf[...] = jnp.zeros_like(xbuf)
        _for_rows(cnt_ref[0], gather_row(tok_ref, 0))

    @pl.when(i + 1 < nv)
    def _():
        _for_rows(cnt_ref[i + 1], gather_row(tokn_ref, 1 - slot))

    @pl.when(i < nv)
    def _():
        _for_rows(cnt_ref[i], gather_wait_row(slot))
        x = xbuf[slot].astype(BF16)
        gp = jnp.dot(x, wg_ref[0], preferred_element_type=F32)
        up = jnp.dot(x, wu_ref[0], preferred_element_type=F32)
        hid = (gp * _sigmoid(gp) * up).astype(BF16)
        y = jnp.dot(hid, wd_ref[0], preferred_element_type=F32)

        @pl.when(i >= 2)
        def _():
            _for_rows(cnt_ref[i - 2], scatter_wait_row(slot))

        ybuf[slot] = y
        _for_rows(cnt_ref[i], scatter_row(slot))

    @pl.when(i == nv - 1)
    def _():
        _for_rows(cnt_ref[i], scatter_wait_row(slot))

        @pl.when(nv >= 2)
        def _():
            _for_rows(cnt_ref[i - 1], scatter_wait_row(1 - slot))


def _moe(nvalid, texp, cnt, tok, dst, xn, wg, wu, wd, n_rows_out):
    nt, _, tm = tok.shape
    d = xn.shape[1]
    de = wg.shape[2]
    grid_spec = pltpu.PrefetchScalarGridSpec(
        num_scalar_prefetch=3,
        grid=(nt,),
        in_specs=[
            pl.BlockSpec((1, 1, tm), lambda i, nv, te, ct: (i, 0, 0), memory_space=pltpu.SMEM),
            pl.BlockSpec((1, 1, tm), lambda i, nv, te, ct: (jnp.minimum(i + 1, nt - 1), 0, 0),
                         memory_space=pltpu.SMEM),
            pl.BlockSpec((1, 1, tm), lambda i, nv, te, ct: (i, 0, 0), memory_space=pltpu.SMEM),
            pl.BlockSpec(memory_space=pl.ANY),
            pl.BlockSpec((1, d, de), lambda i, nv, te, ct: (te[i], 0, 0)),
            pl.BlockSpec((1, d, de), lambda i, nv, te, ct: (te[i], 0, 0)),
            pl.BlockSpec((1, de, d), lambda i, nv, te, ct: (te[i], 0, 0)),
        ],
        out_specs=pl.BlockSpec(memory_space=pl.ANY),
        scratch_shapes=[pltpu.VMEM((2, tm, d), F32), pltpu.VMEM((2, tm, d), F32),
                        pltpu.SemaphoreType.DMA((2,)), pltpu.SemaphoreType.DMA((2,))],
    )
    return pl.pallas_call(
        _moe_kernel,
        out_shape=jax.ShapeDtypeStruct((n_rows_out, d), F32),
        grid_spec=grid_spec,
        compiler_params=_cparams(("arbitrary",)),
        name="moe",
    )(nvalid, texp, cnt, tok, tok, dst, xn, wg, wu, wd)


def _moe_plan(eid, n, tm):
    e_flat = eid.reshape(-1)
    npairs = e_flat.shape[0]
    onehot = (e_flat[:, None] == jnp.arange(N_EXPERTS, dtype=jnp.int32)[None, :]).astype(jnp.int32)
    csum = jnp.cumsum(onehot, axis=0)
    rank = jnp.sum(onehot * csum, axis=1) - 1
    counts = csum[-1]
    padded = ((counts + tm - 1) // tm) * tm
    ends = jnp.cumsum(padded)
    offs = ends - padded
    pos = offs[e_flat] + rank
    n_rows = npairs + N_EXPERTS * tm
    nt = n_rows // tm
    pair = jnp.arange(npairs, dtype=jnp.int32)
    tok = jnp.zeros((n_rows,), jnp.int32).at[pos].set(pair % n)
    dst = jnp.zeros((n_rows,), jnp.int32).at[pos].set(pair)
    tile_start = jnp.arange(nt, dtype=jnp.int32) * tm
    texp = jnp.minimum(jnp.sum((tile_start[:, None] >= ends[None, :]).astype(jnp.int32), axis=1), N_EXPERTS - 1)
    cnt = jnp.clip(offs[texp] + counts[texp] - tile_start, 0, tm)
    nvalid = (ends[-1] // tm).astype(jnp.int32).reshape(1)
    return (nvalid, texp.astype(jnp.int32), cnt.astype(jnp.int32), tok.reshape(nt, 1, tm), dst.reshape(nt, 1, tm),
            npairs)


def _ple_final_kernel(x1_ref, y0_ref, y1_ref, w_ref, p_ref, gp_ref, wpg_ref, wp_ref, gf_ref, o_ref):
    w = w_ref[...]
    x2 = x1_ref[...] + w[:, 0:1] * y0_ref[...] + w[:, 1:2] * y1_ref[...]
    hn = _rms(x2, gp_ref[...]).astype(BF16)
    gate = _sigmoid(jnp.dot(hn, wpg_ref[...], preferred_element_type=F32))
    pe = jnp.dot(p_ref[...].astype(BF16), wp_ref[...], preferred_element_type=F32)
    x3 = x2 + gate * pe
    o_ref[...] = _rms(x3, gf_ref[...])


def _ple_final(x1, ypairs, wcols, p2d, gp, wpg, wp, gf, tm=256):
    n, d = x1.shape
    nb = n // tm
    return pl.pallas_call(
        _ple_final_kernel,
        out_shape=jax.ShapeDtypeStruct((n, d), F32),
        grid=(nb,),
        in_specs=[
            pl.BlockSpec((tm, d), lambda i: (i, 0)),
            pl.BlockSpec((tm, d), lambda i: (i, 0)),
            pl.BlockSpec((tm, d), lambda i: (nb + i, 0)),
            pl.BlockSpec((tm, wcols.shape[1]), lambda i: (i, 0)),
            pl.BlockSpec((tm, p2d.shape[1]), lambda i: (i, 0)),
            pl.BlockSpec((1, d), lambda i: (0, 0)),
            pl.BlockSpec((d, d), lambda i: (0, 0)),
            pl.BlockSpec(wp.shape, lambda i: (0, 0)),
            pl.BlockSpec((1, d), lambda i: (0, 0)),
        ],
        out_specs=pl.BlockSpec((tm, d), lambda i: (i, 0)),
        compiler_params=_cparams(("parallel",)),
        name="ple_final",
    )(x1, ypairs, ypairs, wcols, p2d, gp, wpg, wp, gf)


def _block_diag(w, per):
    nb, bw, _ = w.shape
    w = w.reshape(nb // per, per, bw, bw)
    eye = jnp.eye(per, dtype=w.dtype)
    return jnp.einsum("cpij,pq->cpiqj", w, eye).reshape(nb // per, per * bw, per * bw)


def _layer(x2d, batch, seq, ln_mix, w_in, cmp_k_pos, cmp_k_w1, cmp_k_w2, cmp_v_pos, cmp_v_w1, cmp_v_w2, conv_w,
           conv_b, lru_wa, lru_ba, lru_wx, lru_bx, lru_lambda, w_nsa_up, w_lru_up, w_out, ln_ffn, w_grp, b_grp, w_exp,
           b_exp, w_gate, w_up, w_down):
    n, d = x2d.shape
    g, hpg, hd = NSA_KV_GROUPS, NSA_HPG, HEAD_DIM
    nqb = seq // Q_BLOCK

    c = np.cumsum((0, NSA_Q_WIDTH) + (NSA_KV_WIDTH,) * 6 + (3 * NSA_HEADS, LRU_WIDTH, LRU_WIDTH, d, d))
    seg = lambda k: w_in[:, int(c[k]):int(c[k + 1])]
    w_in_r = jnp.concatenate(
        [seg(10), seg(11), seg(0)] + [seg(k) for k in range(1, 7)] + [seg(8), seg(9), seg(7),
         jnp.zeros((d, Z_WIDTH - int(c[-1])), w_in.dtype)], axis=1).astype(BF16)

    z = _in_proj(x2d, ln_mix.reshape(1, d), w_in_r)

    def heads_t(col):
        a = z[:, col:col + NSA_KV_WIDTH].reshape(batch, nqb, Q_BLOCK, g, hd)
        return a.transpose(0, 3, 1, 4, 2)

    def rows(col):
        return z[:, col:col + NSA_KV_WIDTH].reshape(batch, seq, g, hd).transpose(0, 2, 1, 3)

    def windows(col):
        a = z[:, col:col + NSA_KV_WIDTH].reshape(batch, seq // CMP_STRIDE, CMP_STRIDE, g, hd)
        return a.transpose(0, 3, 1, 2, 4).reshape(batch, g, seq // CMP_STRIDE, CMP_STRIDE * hd)

    qt = z[:, Z_Q:Z_Q + NSA_Q_WIDTH].reshape(batch, nqb, Q_BLOCK, g, hpg, hd)
    qt = qt.transpose(0, 3, 1, 5, 4, 2).reshape(batch, g, nqb, hd, hpg * Q_BLOCK)
    gates = z[:, Z_GATE:Z_GATE + 3 * NSA_HEADS].reshape(batch, nqb, Q_BLOCK, g, hpg, 3)
    gates = gates.transpose(0, 3, 1, 5, 4, 2).reshape(batch, g, nqb, 3, hpg * Q_BLOCK)
    gates = jnp.pad(gates, ((0, 0), (0, 0), (0, 0), (0, SUBLANES - 3), (0, 0)))

    pos8 = lambda pos: jnp.broadcast_to(pos.reshape(1, -1), (SUBLANES, pos.size)).astype(BF16)
    kc, vct = _compress(windows(Z_KC), windows(Z_VC), pos8(cmp_k_pos), cmp_k_w1.astype(BF16), cmp_k_w2.astype(BF16),
                        pos8(cmp_v_pos), cmp_v_w1.astype(BF16), cmp_v_w2.T.astype(BF16))
    ot = _nsa_attention(qt, kc, vct, rows(Z_KS), heads_t(Z_VS), rows(Z_KW), heads_t(Z_VW), gates)
    o = ot.reshape(batch, g, nqb, hd, hpg, Q_BLOCK).transpose(0, 2, 5, 1, 4, 3).reshape(n, NSA_Q_WIDTH)

    per = LRU_CB // LRU_BW
    lru = _rglru(z, batch, seq, conv_w, conv_b.reshape(1, -1), _block_diag(lru_wa, per).astype(BF16),
                 lru_ba.reshape(1, -1), _block_diag(lru_wx, per).astype(BF16), lru_bx.reshape(1, -1),
                 lru_lambda.reshape(1, -1))

    merged = _merge(o, lru, z, w_nsa_up.astype(BF16), w_lru_up.astype(BF16))

    nr = SUBLANES + N_EXPERTS
    wr = jnp.zeros((nr, d), F32).at[0:N_GROUPS].set(w_grp.T).at[SUBLANES:].set(w_exp.T)
    br = jnp.zeros((nr, 1), F32).at[0:N_GROUPS, 0].set(b_grp).at[SUBLANES:, 0].set(b_exp)
    x1, xn, eid, ew = _out_route(merged, x2d, w_out.astype(BF16), ln_ffn.reshape(1, d), wr, br)

    nvalid, texp, cnt, tok, dst, n_rows_out = _moe_plan(eid[0:EXPERT_TOPK], n, MOE_TM)
    ypairs = _moe(nvalid, texp, cnt, tok, dst, xn, w_gate.astype(BF16), w_up.astype(BF16), w_down.astype(BF16),
                  n_rows_out)

    return x1, ypairs, ew.T


def kernel(x, p, ln_mix, w_in, cmp_k_pos, cmp_k_w1, cmp_k_w2, cmp_v_pos, cmp_v_w1, cmp_v_w2, conv_w, conv_b, lru_wa, lru_ba, lru_wx, lru_bx, lru_lambda, w_nsa_up, w_lru_up, w_out, ln_ffn, w_grp, b_grp, w_exp, b_exp, w_gate, w_up, w_down, ln_ple, w_ple, w_ple_gate, ln_final):
    batch, seq, d = x.shape
    assert p.shape[0] == 1, "the final norm is fused into the (single) layer's last kernel"
    n = batch * seq
    x1, ypairs, wcols = _layer(
        x.reshape(n, d), batch, seq, ln_mix[0], w_in[0], cmp_k_pos[0], cmp_k_w1[0], cmp_k_w2[0],
        cmp_v_pos[0], cmp_v_w1[0], cmp_v_w2[0], conv_w[0], conv_b[0], lru_wa[0], lru_ba[0], lru_wx[0], lru_bx[0],
        lru_lambda[0], w_nsa_up[0], w_lru_up[0], w_out[0], ln_ffn[0], w_grp[0], b_grp[0], w_exp[0], b_exp[0],
        w_gate[0], w_up[0], w_down[0])
    out = _ple_final(x1, ypairs, wcols, p[0].reshape(n, -1), ln_ple[0].reshape(1, d),
                     w_ple_gate[0].astype(BF16), w_ple[0].astype(BF16), ln_final.reshape(1, d))
    return out.reshape(batch, seq, d)
```

```python
import functools

import numpy as np
import jax
import jax.numpy as jnp
from jax import lax
from jax.experimental import pallas as pl
from jax.experimental.pallas import tpu as pltpu

F32 = jnp.float32
BF16 = jnp.bfloat16

NSA_HEADS = 16
NSA_KV_GROUPS = 4
NSA_HPG = NSA_HEADS // NSA_KV_GROUPS
HEAD_DIM = 64
NSA_Q_WIDTH = NSA_HEADS * HEAD_DIM
NSA_KV_WIDTH = NSA_KV_GROUPS * HEAD_DIM
CMP_BLOCK = 32
CMP_STRIDE = 16
CMP_HIDDEN = 2 * HEAD_DIM
SLC_BLOCK = 64
SLC_TOPN = 16
WINDOW = 512
Q_BLOCK = 128
ATTN_SCALE = HEAD_DIM ** -0.5
NEG_INF = -1e30
FORCE_SCORE = 1e4
LRU_WIDTH = 1024
LRU_BLOCKS = 16
LRU_BW = LRU_WIDTH // LRU_BLOCKS
CONV_WIDTH = 4
LRU_C = 8.0
N_GROUPS = 4
EXPERTS_PER_GROUP = 8
N_EXPERTS = N_GROUPS * EXPERTS_PER_GROUP
EXPERT_TOPK = 2
D_EXPERT = 512
EPS = 1e-6

LANES = 128
SUBLANES = 8
VMEM_LIMIT_BYTES = 56 * 1024 * 1024

Z_MGA = 0
Z_MGB = 2048
Z_Q = 4096
Z_KC = 5120
Z_VC = 5376
Z_KS = 5632
Z_VS = 5888
Z_KW = 6144
Z_VW = 6400
Z_LX = 6656
Z_LY = 7680
Z_GATE = 8704
Z_WIDTH = 9216
LRU_CB = 256
MOE_TM = 256


def _cparams(sem, vmem=VMEM_LIMIT_BYTES):
    return pltpu.CompilerParams(dimension_semantics=sem, vmem_limit_bytes=vmem)


def _rms(x, g):
    return x * lax.rsqrt(jnp.mean(x * x, axis=-1, keepdims=True) + EPS) * g


def _gelu_tanh(x):
    return 0.5 * x * (1.0 + jnp.tanh(np.sqrt(2.0 / np.pi) * (x + 0.044715 * (x * x * x))))


def _sigmoid(x):
    return 1.0 / (1.0 + jnp.exp(-x))


def _in_proj_kernel(x_ref, g_ref, w_ref, o_ref, h_ref):
    @pl.when(pl.program_id(1) == 0)
    def _():
        h_ref[...] = _rms(x_ref[...], g_ref[...]).astype(BF16)

    o_ref[...] = jnp.dot(h_ref[...], w_ref[...], preferred_element_type=F32).astype(o_ref.dtype)


def _in_proj(x2d, g, w, tm=512, tn=1024):
    n, d = x2d.shape
    nw = w.shape[1]
    return pl.pallas_call(
        _in_proj_kernel,
        out_shape=jax.ShapeDtypeStruct((n, nw), BF16),
        grid=(n // tm, nw // tn),
        in_specs=[
            pl.BlockSpec((tm, d), lambda i, j: (i, 0)),
            pl.BlockSpec((1, d), lambda i, j: (0, 0)),
            pl.BlockSpec((d, tn), lambda i, j: (0, j)),
        ],
        out_specs=pl.BlockSpec((tm, tn), lambda i, j: (i, j)),
        scratch_shapes=[pltpu.VMEM((tm, d), BF16)],
        compiler_params=_cparams(("parallel", "arbitrary")),
        name="in_proj",
    )(x2d, g, w)


def _compress_kernel(kx_ref, vx_ref, kpos_ref, kw1_ref, kw2_ref, vpos_ref, vw1_ref, vw2t_ref, kc_ref, vct_ref):
    half = kx_ref.shape[-1]

    def hidden(x_ref, pos_ref, w1_ref):
        x = x_ref[0, 0]
        w1 = w1_ref[...]
        ha = jnp.dot(x, w1[:half], preferred_element_type=F32)
        hb = jnp.dot(x, w1[half:], preferred_element_type=F32)
        hb = pltpu.roll(hb, hb.shape[0] - 1, axis=0)
        pc = jnp.dot(pos_ref[...], w1, preferred_element_type=F32)[0:1]
        return _gelu_tanh(ha + hb + pc).astype(BF16)

    hk = hidden(kx_ref, kpos_ref, kw1_ref)
    kc_ref[0, 0] = jnp.dot(hk, kw2_ref[...], preferred_element_type=F32).astype(kc_ref.dtype)
    hv = hidden(vx_ref, vpos_ref, vw1_ref)
    vct_ref[0, 0] = lax.dot_general(vw2t_ref[...], hv, (((1,), (1,)), ((), ())),
                                    preferred_element_type=F32).astype(vct_ref.dtype)


def _compress(kx, vx, kpos, kw1, kw2, vpos, vw1, vw2t):
    b, g, ncp, half = kx.shape
    full = lambda a: pl.BlockSpec(a.shape, lambda i, j: (0,) * a.ndim)
    return pl.pallas_call(
        _compress_kernel,
        out_shape=(jax.ShapeDtypeStruct((b, g, ncp, HEAD_DIM), BF16),
                   jax.ShapeDtypeStruct((b, g, HEAD_DIM, ncp), BF16)),
        grid=(b, g),
        in_specs=[
            pl.BlockSpec((1, 1, ncp, half), lambda i, j: (i, j, 0, 0)),
            pl.BlockSpec((1, 1, ncp, half), lambda i, j: (i, j, 0, 0)),
            full(kpos), full(kw1), full(kw2), full(vpos), full(vw1), full(vw2t),
        ],
        out_specs=(pl.BlockSpec((1, 1, ncp, HEAD_DIM), lambda i, j: (i, j, 0, 0)),
                   pl.BlockSpec((1, 1, HEAD_DIM, ncp), lambda i, j: (i, j, 0, 0))),
        compiler_params=_cparams(("parallel", "parallel")),
        name="compress",
    )(kx, vx, kpos, kw1, kw2, vpos, vw1, vw2t)


def _nsa_kernel(qt_ref, kc_ref, vct_ref, ks_ref, vst_ref, kw_ref, vwt_ref, gate_ref, o_ref, score_ref, selb_ref):
    qb = pl.program_id(2)
    nq = Q_BLOCK * NSA_HPG
    ncp = kc_ref.shape[2]
    nsb = score_ref.shape[0]
    q = (qt_ref[0, 0, 0].astype(F32) * ATTN_SCALE).astype(BF16)
    t_lane = qb * Q_BLOCK + (lax.broadcasted_iota(jnp.int32, (1, nq), 1) & (Q_BLOCK - 1))

    sc = jnp.dot(kc_ref[0, 0], q, preferred_element_type=F32)
    cmp_end = lax.broadcasted_iota(jnp.int32, (ncp, 1), 0) * CMP_STRIDE + (CMP_BLOCK - 1)
    cmask = cmp_end <= t_lane
    sc = jnp.where(cmask, sc, NEG_INF)
    pc = jnp.where(cmask, jnp.exp(sc - jnp.max(sc, axis=0, keepdims=True)), 0.0)
    lc = jnp.sum(pc, axis=0, keepdims=True)
    pc = pc * (1.0 / jnp.where(lc > 0.0, lc, 1.0))
    o_cmp = jnp.dot(vct_ref[0, 0], pc.astype(BF16), preferred_element_type=F32)

    nwc = WINDOW // Q_BLOCK + 1
    ks_w, vt_w = [], []
    for i in range(nwc):
        c = jnp.maximum(qb - (nwc - 1) + i, 0)
        ks_w.append(kw_ref[0, 0, pl.ds(pl.multiple_of(c * Q_BLOCK, Q_BLOCK), Q_BLOCK), :])
        vt_w.append(vwt_ref[0, 0, c])
    sw = jnp.dot(jnp.concatenate(ks_w, axis=0), q, preferred_element_type=F32)
    spos = (qb - (nwc - 1)) * Q_BLOCK + lax.broadcasted_iota(jnp.int32, (nwc * Q_BLOCK, 1), 0)
    spos = jnp.where(spos >= 0, spos, -(1 << 24))
    wmask = lax.bitcast_convert_type(t_lane - spos, jnp.uint32) < jnp.uint32(WINDOW)
    sw = jnp.where(wmask, sw, NEG_INF)
    pw = jnp.exp(sw - jnp.max(sw, axis=0, keepdims=True))
    lw = jnp.sum(pw, axis=0, keepdims=True)
    o_win = jnp.dot(jnp.concatenate(vt_w, axis=1), pw.astype(BF16), preferred_element_type=F32) * (1.0 / lw)

    psum = pc[:, 0:Q_BLOCK]
    for h in range(1, NSA_HPG):
        psum = psum + pc[:, h * Q_BLOCK:(h + 1) * Q_BLOCK]
    jrow = lax.broadcasted_iota(jnp.int32, (nsb, ncp), 0)
    crel = lax.broadcasted_iota(jnp.int32, (nsb, ncp), 1) - jrow * (SLC_BLOCK // CMP_STRIDE)
    ovt = jnp.where((crel >= 0) & (crel <= 2), 1.0, jnp.where((crel == -1) | (crel == 3), 0.5, 0.0)).astype(F32)
    imp = jnp.dot(ovt, psum, preferred_element_type=F32, precision=lax.Precision.HIGHEST)
    jblk = lax.broadcasted_iota(jnp.int32, (nsb, Q_BLOCK), 0)
    blk = (qb * Q_BLOCK + lax.broadcasted_iota(jnp.int32, (1, Q_BLOCK), 1)) >> (SLC_BLOCK.bit_length() - 1)
    forced = (jblk == 0) | (jblk == blk) | (jblk == blk - 1)
    score = jnp.where(forced, FORCE_SCORE, jnp.where(jblk <= blk, imp, -1.0))
    score_ref[...] = score

    def rank_one(k, cnt):
        row = score_ref[pl.ds(k, 1), :]
        ge = jnp.where(row >= score, 1.0, 0.0)
        gt = jnp.where(row > score, 1.0, 0.0)
        return cnt + jnp.where(jblk > k, ge, gt)

    def rank_pair(i, cnt):
        return rank_one(2 * i + 1, rank_one(2 * i, cnt))

    n_vis = jnp.minimum(qb * (Q_BLOCK // SLC_BLOCK) + Q_BLOCK // SLC_BLOCK, nsb)
    rank = lax.fori_loop(0, n_vis // 2, rank_pair, jnp.zeros((nsb, Q_BLOCK), F32))
    selb_ref[...] = jnp.where(rank < float(min(SLC_TOPN, nsb)), 0.0, NEG_INF)

    ck = 2 * Q_BLOCK
    bpc = ck // SLC_BLOCK

    def slc_scores(c):
        k = ks_ref[0, 0, pl.ds(pl.multiple_of(c * ck, ck), ck), :]
        s = jnp.dot(k, q, preferred_element_type=F32)
        rows = [jnp.broadcast_to(selb_ref[pl.ds(c * bpc + i, 1), :], (SLC_BLOCK, Q_BLOCK)) for i in range(bpc)]
        bias = jnp.concatenate(rows, axis=0)
        return s + jnp.concatenate([bias] * NSA_HPG, axis=1)

    def slc_update(c, s, m, l, acc):
        m_new = jnp.maximum(m, jnp.max(s, axis=0, keepdims=True))
        alpha = jnp.exp(m - m_new)
        p = jnp.exp(s - m_new)
        l = alpha * l + jnp.sum(p, axis=0, keepdims=True)
        vt = jnp.concatenate([vst_ref[0, 0, 2 * c], vst_ref[0, 0, 2 * c + 1]], axis=1)
        acc = acc * alpha + jnp.dot(vt, p.astype(BF16), preferred_element_type=F32)
        return m_new, l, acc

    def slc_body(c, carry):
        m, l, acc, s = carry
        s_next = slc_scores(c + 1)
        return slc_update(c, s, m, l, acc) + (s_next,)

    n_full = qb // 2
    init = (jnp.full((1, nq), NEG_INF, F32), jnp.zeros((1, nq), F32), jnp.zeros((HEAD_DIM, nq), F32), slc_scores(0))
    m_s, l_s, acc_s, s_d = lax.fori_loop(0, n_full, slc_body, init)
    kpos_d = n_full * ck + lax.broadcasted_iota(jnp.int32, (ck, 1), 0)
    s_d = jnp.where(kpos_d <= t_lane, s_d, NEG_INF)
    _, l_s, acc_s = slc_update(n_full, s_d, m_s, l_s, acc_s)
    o_slc = acc_s * (1.0 / l_s)

    gt = _sigmoid(gate_ref[0, 0, 0].astype(F32))
    o = gt[0:1] * o_cmp + gt[1:2] * o_slc + gt[2:3] * o_win
    o_ref[0, 0, 0] = o.astype(o_ref.dtype)


def _nsa_attention(qt, kc, vct, ks, vst, kw, vwt, gates):
    b, g, nqb, hd, nq = qt.shape
    s = ks.shape[2]
    ncp = kc.shape[2]
    nsb = s // SLC_BLOCK
    bg = lambda *blk: pl.BlockSpec((1, 1) + blk, lambda i, j, k: (i, j) + (0,) * len(blk))
    return pl.pallas_call(
        _nsa_kernel,
        out_shape=jax.ShapeDtypeStruct((b, g, nqb, hd, nq), BF16),
        grid=(b, g, nqb),
        in_specs=[
            pl.BlockSpec((1, 1, 1, hd, nq), lambda i, j, k: (i, j, k, 0, 0)),
            bg(ncp, hd), bg(hd, ncp),
            bg(s, hd), bg(nqb, hd, Q_BLOCK),
            bg(s, hd), bg(nqb, hd, Q_BLOCK),
            pl.BlockSpec((1, 1, 1, SUBLANES, nq), lambda i, j, k: (i, j, k, 0, 0)),
        ],
        out_specs=pl.BlockSpec((1, 1, 1, hd, nq), lambda i, j, k: (i, j, k, 0, 0)),
        scratch_shapes=[pltpu.VMEM((nsb, Q_BLOCK), F32), pltpu.VMEM((nsb, Q_BLOCK), F32)],
        compiler_params=_cparams(("parallel", "parallel", "arbitrary")),
        name="nsa_attn",
    )(qt, kc, vct, ks, vst, kw, vwt, gates)


NSA_TQ = 256
NSA_CK = 512


def _nsa2_kernel(qt_ref, kc_ref, vct_ref, ka_ref, vst_ref, kw_ref, vwt_ref, gate_ref, o_ref, score_ref):
    qi = pl.program_id(2)
    tq = NSA_TQ
    nq = tq * NSA_HPG
    ncp = kc_ref.shape[2]
    nsb = score_ref.shape[0]
    q = (qt_ref[0, 0, 0].astype(F32) * ATTN_SCALE).astype(BF16)
    t_lane = qi * tq + (lax.broadcasted_iota(jnp.int32, (1, nq), 1) & (tq - 1))

    sc = jnp.dot(kc_ref[0, 0], q, preferred_element_type=F32)
    cmp_end = lax.broadcasted_iota(jnp.int32, (ncp, 1), 0) * CMP_STRIDE + (CMP_BLOCK - 1)
    cmask = cmp_end <= t_lane
    sc = jnp.where(cmask, sc, NEG_INF)
    pc = jnp.where(cmask, jnp.exp(sc - jnp.max(sc, axis=0, keepdims=True)), 0.0)
    lc = jnp.sum(pc, axis=0, keepdims=True)
    pc = pc * (1.0 / jnp.where(lc > 0.0, lc, 1.0))
    o_cmp = jnp.dot(vct_ref[0, 0], pc.astype(BF16), preferred_element_type=F32)

    nwc = WINDOW // tq + 1
    t_lo = t_lane - WINDOW
    ks_w, vt_w = [], []
    for i in range(nwc):
        c = qi - (nwc - 1) + i
        cc = jnp.maximum(c, 0)
        ks_w.append(kw_ref[0, 0, pl.ds(pl.multiple_of(cc * tq, tq), tq), :])
        vt_w.append(vwt_ref[0, 0, cc])
    sw = jnp.dot(jnp.concatenate(ks_w, axis=0), q, preferred_element_type=F32)
    sw_parts = []
    for i in range(nwc):
        spos = (qi - (nwc - 1) + i) * tq + lax.broadcasted_iota(jnp.int32, (tq, 1), 0)
        if i < nwc - 1:
            ok = jnp.where(spos >= 0, spos, -(1 << 24)) > t_lo
        else:
            ok = spos <= t_lane
        sw_parts.append(jnp.where(ok, sw[i * tq:(i + 1) * tq], NEG_INF))
    sw = jnp.concatenate(sw_parts, axis=0)
    pw = jnp.exp(sw - jnp.max(sw, axis=0, keepdims=True))
    lw = jnp.sum(pw, axis=0, keepdims=True)
    o_win = jnp.dot(jnp.concatenate(vt_w, axis=1), pw.astype(BF16), preferred_element_type=F32) * (1.0 / lw)

    psum = pc[:, 0:tq]
    for h in range(1, NSA_HPG):
        psum = psum + pc[:, h * tq:(h + 1) * tq]
    jrow = lax.broadcasted_iota(jnp.int32, (nsb, ncp), 0)
    crel = lax.broadcasted_iota(jnp.int32, (nsb, ncp), 1) - jrow * (SLC_BLOCK // CMP_STRIDE)
    ovt = jnp.where((crel >= 0) & (crel <= 2), 1.0, jnp.where((crel == -1) | (crel == 3), 0.5, 0.0)).astype(F32)
    imp = jnp.dot(ovt, psum, preferred_element_type=F32, precision=lax.Precision.HIGHEST)
    jblk = lax.broadcasted_iota(jnp.int32, (nsb, tq), 0)
    blk = (qi * tq + lax.broadcasted_iota(jnp.int32, (1, tq), 1)) >> (SLC_BLOCK.bit_length() - 1)
    forced = (jblk == 0) | (jblk == blk) | (jblk == blk - 1)
    score = jnp.where(forced, FORCE_SCORE, jnp.where(jblk <= blk, imp, -1.0))
    score_ref[...] = score

    def rank_one(k, cnt):
        row = score_ref[pl.ds(k, 1), :]
        ge = jnp.where(row >= score, 1.0, 0.0)
        gt = jnp.where(row > score, 1.0, 0.0)
        return cnt + jnp.where(jblk > k, ge, gt)

    bpt = tq // SLC_BLOCK

    def rank_group(i, cnt):
        for u in range(bpt):
            cnt = rank_one(bpt * i + u, cnt)
        return cnt

    n_vis = jnp.minimum((qi + 1) * bpt, nsb)
    rank = lax.fori_loop(0, n_vis // bpt, rank_group, jnp.zeros((nsb, tq), F32))
    selb = jnp.where(rank < float(min(SLC_TOPN, nsb)), 0.0, NEG_INF).astype(BF16)

    ck = NSA_CK
    qa = jnp.concatenate([q, jnp.concatenate([selb] * NSA_HPG, axis=1)], axis=0)

    def slc_scores(c):
        return jnp.dot(ka_ref[0, 0, pl.ds(pl.multiple_of(c * ck, ck), ck), :], qa, preferred_element_type=F32)

    def slc_update(c, s, m, l, acc):
        m_new = jnp.maximum(m, jnp.max(s, axis=0, keepdims=True))
        alpha = jnp.exp(m - m_new)
        p = jnp.exp(s - m_new)
        l = alpha * l + jnp.sum(p, axis=0, keepdims=True)
        vt = jnp.concatenate([vst_ref[0, 0, (ck // tq) * c + i] for i in range(ck // tq)], axis=1)
        acc = acc * alpha + jnp.dot(vt, p.astype(BF16), preferred_element_type=F32)
        return m_new, l, acc

    def slc_body(c, carry):
        m, l, acc, s = carry
        s_next = slc_scores(c + 1)
        return slc_update(c, s, m, l, acc) + (s_next,)

    n_full = (qi * tq) // ck
    init = (jnp.full((1, nq), NEG_INF, F32), jnp.zeros((1, nq), F32), jnp.zeros((HEAD_DIM, nq), F32), slc_scores(0))
    m_s, l_s, acc_s, s_d = lax.fori_loop(0, n_full, slc_body, init)
    kpos_d = n_full * ck + lax.broadcasted_iota(jnp.int32, (ck, 1), 0)
    s_d = jnp.where(kpos_d <= t_lane, s_d, NEG_INF)
    _, l_s, acc_s = slc_update(n_full, s_d, m_s, l_s, acc_s)
    o_slc = acc_s * (1.0 / l_s)

    gt = _sigmoid(gate_ref[0, 0, 0].astype(F32))
    o = gt[0:1] * o_cmp + gt[1:2] * o_slc + gt[2:3] * o_win
    o_ref[0, 0, 0] = o.astype(o_ref.dtype)


def _nsa_attention2(qt, kc, vct, ka, vst, kw, vwt, gates):
    b, g, nqt, hd, nq = qt.shape
    s = ka.shape[2]
    ncp = kc.shape[2]
    nsb = s // SLC_BLOCK
    assert s % NSA_CK == 0 and NSA_CK % NSA_TQ == 0 and WINDOW % NSA_TQ == 0
    bg = lambda *blk: pl.BlockSpec((1, 1) + blk, lambda i, j, k: (i, j) + (0,) * len(blk))
    return pl.pallas_call(
        _nsa2_kernel,
        out_shape=jax.ShapeDtypeStruct((b, g, nqt, hd, nq), BF16),
        grid=(b, g, nqt),
        in_specs=[
            pl.BlockSpec((1, 1, 1, hd, nq), lambda i, j, k: (i, j, k, 0, 0)),
            bg(ncp, hd), bg(hd, ncp),
            bg(s, hd + nsb), bg(nqt, hd, NSA_TQ),
            bg(s, hd), bg(nqt, hd, NSA_TQ),
            pl.BlockSpec((1, 1, 1, SUBLANES, nq), lambda i, j, k: (i, j, k, 0, 0)),
        ],
        out_specs=pl.BlockSpec((1, 1, 1, hd, nq), lambda i, j, k: (i, j, k, 0, 0)),
        scratch_shapes=[pltpu.VMEM((nsb, NSA_TQ), F32)],
        compiler_params=_cparams(("parallel", "parallel", "arbitrary")),
        name="nsa_attn",
    )(qt, kc, vct, ka, vst, kw, vwt, gates)


def _rglru_kernel(x_ref, y_ref, cw_ref, cb_ref, wa_ref, ba_ref, wx_ref, bx_ref, lam_ref, o_ref,
                  tail_ref, h_ref, a_ref, u_ref):
    tc = pl.program_id(2)
    tt, cb = x_ref.shape

    @pl.when(tc == 0)
    def _():
        tail_ref[...] = jnp.zeros_like(tail_ref)
        h_ref[...] = jnp.zeros_like(h_ref)

    x = x_ref[...].astype(F32)
    xe = jnp.concatenate([tail_ref[...], x], axis=0)
    tail_ref[...] = x[tt - SUBLANES:, :]
    cw = cw_ref[...]
    xc = cb_ref[...]
    for k in range(CONV_WIDTH):
        off = SUBLANES - (CONV_WIDTH - 1) + k
        xc = xc + cw[k:k + 1, :] * xe[off:off + tt, :]
    xcb = xc.astype(BF16)
    r = _sigmoid(jnp.dot(xcb, wa_ref[0], preferred_element_type=F32) + ba_ref[...])
    ig = _sigmoid(jnp.dot(xcb, wx_ref[0], preferred_element_type=F32) + bx_ref[...])
    nl = -lam_ref[...]
    softplus = jnp.maximum(nl, 0.0) + jnp.log1p(jnp.exp(-jnp.abs(nl)))
    log_a = (-LRU_C) * softplus * r
    a_ref[...] = jnp.exp(log_a)
    th = jnp.tanh(log_a)
    u_ref[...] = jnp.sqrt(-2.0 * th / (1.0 - th)) * (ig * xc)

    row = lax.broadcasted_iota(jnp.int32, (SUBLANES, cb), 0)

    def step(i, h):
        sl = pl.ds(pl.multiple_of(i * SUBLANES, SUBLANES), SUBLANES)
        a = a_ref[sl, :]
        u = u_ref[sl, :]
        for s in (1, 2, 4):
            a_s = jnp.where(row >= s, pltpu.roll(a, s, axis=0), 1.0)
            u_s = jnp.where(row >= s, pltpu.roll(u, s, axis=0), 0.0)
            u = a * u_s + u
            a = a * a_s
        hrows = a * h + u
        u_ref[sl, :] = hrows
        return hrows[SUBLANES - 1:SUBLANES, :]

    h_ref[...] = lax.fori_loop(0, tt // SUBLANES, step, h_ref[...], unroll=2)
    o_ref[...] = (u_ref[...] * _gelu_tanh(y_ref[...].astype(F32))).astype(o_ref.dtype)


def _rglru(z, batch, seq, cw, cbias, wa_bd, ba, wx_bd, bx, lam, tt=512):
    n = z.shape[0]
    ncb = LRU_WIDTH // LRU_CB
    nt = seq // tt
    row = lambda i, j, k: i * nt + k
    vec = lambda r: pl.BlockSpec((r, LRU_CB), lambda i, j, k: (0, j))
    return pl.pallas_call(
        _rglru_kernel,
        out_shape=jax.ShapeDtypeStruct((n, LRU_WIDTH), BF16),
        grid=(batch, ncb, nt),
        in_specs=[
            pl.BlockSpec((tt, LRU_CB), lambda i, j, k: (row(i, j, k), Z_LX // LRU_CB + j)),
            pl.BlockSpec((tt, LRU_CB), lambda i, j, k: (row(i, j, k), Z_LY // LRU_CB + j)),
            vec(CONV_WIDTH), vec(1),
            pl.BlockSpec((1, LRU_CB, LRU_CB), lambda i, j, k: (j, 0, 0)), vec(1),
            pl.BlockSpec((1, LRU_CB, LRU_CB), lambda i, j, k: (j, 0, 0)), vec(1),
            vec(1),
        ],
        out_specs=pl.BlockSpec((tt, LRU_CB), lambda i, j, k: (row(i, j, k), j)),
        scratch_shapes=[pltpu.VMEM((SUBLANES, LRU_CB), F32), pltpu.VMEM((1, LRU_CB), F32),
                        pltpu.VMEM((tt, LRU_CB), F32), pltpu.VMEM((tt, LRU_CB), F32)],
        compiler_params=_cparams(("parallel", "parallel", "arbitrary")),
        name="rglru",
    )(z, z, cw, cbias, wa_bd, ba, wx_bd, bx, lam)


def _merge_kernel(o_ref, l_ref, mga_ref, mgb_ref, wn_ref, wl_ref, m_ref):
    ya = jnp.dot(o_ref[...], wn_ref[...], preferred_element_type=F32)
    yb = jnp.dot(l_ref[...], wl_ref[...], preferred_element_type=F32)
    m = _sigmoid(mga_ref[...].astype(F32)) * ya + _sigmoid(mgb_ref[...].astype(F32)) * yb
    m_ref[...] = m.astype(m_ref.dtype)


def _merge(o, lru, z, wn, wl, tm=512):
    n, d = o.shape[0], wn.shape[1]
    return pl.pallas_call(
        _merge_kernel,
        out_shape=jax.ShapeDtypeStruct((n, d), BF16),
        grid=(n // tm,),
        in_specs=[
            pl.BlockSpec((tm, o.shape[1]), lambda i: (i, 0)),
            pl.BlockSpec((tm, lru.shape[1]), lambda i: (i, 0)),
            pl.BlockSpec((tm, d), lambda i: (i, Z_MGA // d)),
            pl.BlockSpec((tm, d), lambda i: (i, Z_MGB // d)),
            pl.BlockSpec(wn.shape, lambda i: (0, 0)),
            pl.BlockSpec(wl.shape, lambda i: (0, 0)),
        ],
        out_specs=pl.BlockSpec((tm, d), lambda i: (i, 0)),
        compiler_params=_cparams(("parallel",)),
        name="merge",
    )(o, lru, z, z, wn, wl)


def _out_route_kernel(m_ref, x_ref, wo_ref, g_ref, wr_ref, br_ref, x1_ref, xn_ref, eid_ref, ew_ref):
    x1 = x_ref[...] + jnp.dot(m_ref[...], wo_ref[...], preferred_element_type=F32)
    x1_ref[...] = x1
    xn = _rms(x1, g_ref[...])
    xn_ref[...] = xn
    lg = lax.dot_general(wr_ref[...], xn, (((1,), (1,)), ((), ())), preferred_element_type=F32,
                         precision=lax.Precision.HIGHEST) + br_ref[...]
    tm = lg.shape[1]
    sub = lax.broadcasted_iota(jnp.int32, (SUBLANES, tm), 0)

    def first_argmax(v, vmax):
        return jnp.min(jnp.where(v == vmax, sub, SUBLANES), axis=0, keepdims=True)

    gl = jnp.where(sub < N_GROUPS, lg[0:SUBLANES], -jnp.inf)
    gmax = jnp.max(gl, axis=0, keepdims=True)
    ge = jnp.exp(gl - gmax)
    gprob = ge / jnp.sum(ge, axis=0, keepdims=True)
    g_val = jnp.max(gprob, axis=0, keepdims=True)
    g_idx = first_argmax(gprob, g_val)
    e_in = jnp.zeros((EXPERTS_PER_GROUP, tm), F32)
    for gi in range(N_GROUPS):
        lo = SUBLANES + gi * EXPERTS_PER_GROUP
        e_in = jnp.where(g_idx == gi, lg[lo:lo + EXPERTS_PER_GROUP], e_in)
    ee = jnp.exp(e_in - jnp.max(e_in, axis=0, keepdims=True))
    eprob = ee / jnp.sum(ee, axis=0, keepdims=True)
    v1 = jnp.max(eprob, axis=0, keepdims=True)
    i1 = first_argmax(eprob, v1)
    rest = jnp.where(sub == i1, -1.0, eprob)
    v2 = jnp.max(rest, axis=0, keepdims=True)
    i2 = first_argmax(rest, v2)
    den = v1 + v2
    eid = jnp.where(sub == 0, g_idx * EXPERTS_PER_GROUP + i1, g_idx * EXPERTS_PER_GROUP + i2)
    eid_ref[...] = eid
    ew_ref[...] = jnp.where(sub == 0, g_val * v1 / den, g_val * v2 / den)


def _out_route(m, x2d, wo, g, wr, br, tm=512):
    n, d = x2d.shape
    nr = wr.shape[0]
    once = pl.Buffered(1)
    return pl.pallas_call(
        _out_route_kernel,
        out_shape=(jax.ShapeDtypeStruct((n, d), F32), jax.ShapeDtypeStruct((n, d), F32),
                   jax.ShapeDtypeStruct((SUBLANES, n), jnp.int32), jax.ShapeDtypeStruct((SUBLANES, n), F32)),
        grid=(n // tm,),
        in_specs=[
            pl.BlockSpec((tm, d), lambda i: (i, 0)),
            pl.BlockSpec((tm, d), lambda i: (i, 0)),
            pl.BlockSpec((d, d), lambda i: (0, 0), pipeline_mode=once),
            pl.BlockSpec((1, d), lambda i: (0, 0)),
            pl.BlockSpec((nr, d), lambda i: (0, 0), pipeline_mode=once),
            pl.BlockSpec((nr, 1), lambda i: (0, 0)),
        ],
        out_specs=(pl.BlockSpec((tm, d), lambda i: (i, 0)), pl.BlockSpec((tm, d), lambda i: (i, 0)),
                   pl.BlockSpec((SUBLANES, tm), lambda i: (0, i)), pl.BlockSpec((SUBLANES, tm), lambda i: (0, i))),
        compiler_params=_cparams(("parallel",)),
        name="out_route",
    )(m, x2d, wo, g, wr, br)


def _for_rows(cnt, fn, unroll=8):
    sh = unroll.bit_length() - 1

    def group(gidx, c):
        for u in range(unroll):
            fn(gidx * unroll + u)
        return c

    def single(r, c):
        fn(r)
        return c

    lax.fori_loop(0, cnt >> sh, group, 0)
    lax.fori_loop((cnt >> sh) << sh, cnt, single, 0)


def _moe_kernel(nvalid_ref, texp_ref, cnt_ref, tok_ref, tokn_ref, dst_ref, xn_hbm, wg_ref, wu_ref, wd_ref, y_hbm,
                xbuf, ybuf, gsem, ssem):
    i = pl.program_id(0)
    nv = nvalid_ref[0]
    slot = i % 2

    def gather_row(idx_ref, s):
        def fn(r):
            pltpu.make_async_copy(xn_hbm.at[pl.ds(idx_ref[0, 0, r], 1), :], xbuf.at[s, pl.ds(r, 1), :],
                                  gsem.at[s]).start()
        return fn

    def gather_wait_row(s):
        def fn(r):
            pltpu.make_async_copy(xn_hbm.at[pl.ds(0, 1), :], xbuf.at[s, pl.ds(r, 1), :], gsem.at[s]).wait()
        return fn

    def scatter_row(s):
        def fn(r):
            pltpu.make_async_copy(ybuf.at[s, pl.ds(r, 1), :], y_hbm.at[pl.ds(dst_ref[0, 0, r], 1), :],
                                  ssem.at[s]).start()
        return fn

    def scatter_wait_row(s):
        def fn(r):
            pltpu.make_async_copy(ybuf.at[s, pl.ds(r, 1), :], y_hbm.at[pl.ds(0, 1), :], ssem.at[s]).wait()
        return fn

    @pl.when(i == 0)
    def _():
        xbuf[...] = jnp.zeros_like(xbuf)
        _for_rows(cnt_ref[0], gather_row(tok_ref, 0))

    @pl.when(i + 1 < nv)
    def _():
        _for_rows(cnt_ref[i + 1], gather_row(tokn_ref, 1 - slot))

    @pl.when(i < nv)
    def _():
        _for_rows(cnt_ref[i], gather_wait_row(slot))
        x = xbuf[slot].astype(BF16)
        gp = jnp.dot(x, wg_ref[0], preferred_element_type=F32)
        up = jnp.dot(x, wu_ref[0], preferred_element_type=F32)
        hid = (gp * _sigmoid(gp) * up).astype(BF16)
        y = jnp.dot(hid, wd_ref[0], preferred_element_type=F32)

        @pl.when(i >= 2)
        def _():
            _for_rows(cnt_ref[i - 2], scatter_wait_row(slot))

        ybuf[slot] = y
        _for_rows(cnt_ref[i], scatter_row(slot))

    @pl.when(i == nv - 1)
    def _():
        _for_rows(cnt_ref[i], scatter_wait_row(slot))

        @pl.when(nv >= 2)
        def _():
            _for_rows(cnt_ref[i - 1], scatter_wait_row(1 - slot))


def _moe(nvalid, texp, cnt, tok, dst, xn, wg, wu, wd, n_rows_out):
    nt, _, tm = tok.shape
    d = xn.shape[1]
    de = wg.shape[2]
    grid_spec = pltpu.PrefetchScalarGridSpec(
        num_scalar_prefetch=3,
        grid=(nt,),
        in_specs=[
            pl.BlockSpec((1, 1, tm), lambda i, nv, te, ct: (i, 0, 0), memory_space=pltpu.SMEM),
            pl.BlockSpec((1, 1, tm), lambda i, nv, te, ct: (jnp.minimum(i + 1, nt - 1), 0, 0),
                         memory_space=pltpu.SMEM),
            pl.BlockSpec((1, 1, tm), lambda i, nv, te, ct: (i, 0, 0), memory_space=pltpu.SMEM),
            pl.BlockSpec(memory_space=pl.ANY),
            pl.BlockSpec((1, d, de), lambda i, nv, te, ct: (te[i], 0, 0)),
            pl.BlockSpec((1, d, de), lambda i, nv, te, ct: (te[i], 0, 0)),
            pl.BlockSpec((1, de, d), lambda i, nv, te, ct: (te[i], 0, 0)),
        ],
        out_specs=pl.BlockSpec(memory_space=pl.ANY),
        scratch_shapes=[pltpu.VMEM((2, tm, d), F32), pltpu.VMEM((2, tm, d), F32),
                        pltpu.SemaphoreType.DMA((2,)), pltpu.SemaphoreType.DMA((2,))],
    )
    return pl.pallas_call(
        _moe_kernel,
        out_shape=jax.ShapeDtypeStruct((n_rows_out, d), F32),
        grid_spec=grid_spec,
        compiler_params=_cparams(("arbitrary",)),
        name="moe",
    )(nvalid, texp, cnt, tok, tok, dst, xn, wg, wu, wd)


def _moe_plan(eid, n, tm):
    e_flat = eid.reshape(-1)
    npairs = e_flat.shape[0]
    onehot = (e_flat[:, None] == jnp.arange(N_EXPERTS, dtype=jnp.int32)[None, :]).astype(jnp.int32)
    csum = jnp.cumsum(onehot, axis=0)
    rank = jnp.sum(onehot * csum, axis=1) - 1
    counts = csum[-1]
    padded = ((counts + tm - 1) // tm) * tm
    ends = jnp.cumsum(padded)
    offs = ends - padded
    pos = offs[e_flat] + rank
    n_rows = npairs + N_EXPERTS * tm
    nt = n_rows // tm
    pair = jnp.arange(npairs, dtype=jnp.int32)
    dst = jnp.zeros((n_rows,), jnp.int32).at[pos].set(pair)
    tok = dst % n
    tile_start = jnp.arange(nt, dtype=jnp.int32) * tm
    texp = jnp.minimum(jnp.sum((tile_start[:, None] >= ends[None, :]).astype(jnp.int32), axis=1), N_EXPERTS - 1)
    cnt = jnp.clip(offs[texp] + counts[texp] - tile_start, 0, tm)
    nvalid = (ends[-1] // tm).astype(jnp.int32).reshape(1)
    return (nvalid, texp.astype(jnp.int32), cnt.astype(jnp.int32), tok.reshape(nt, 1, tm), dst.reshape(nt, 1, tm),
            npairs)


def _ple_final_kernel(x1_ref, y0_ref, y1_ref, w_ref, p_ref, gp_ref, wpg_ref, wp_ref, gf_ref, o_ref):
    w = w_ref[...]
    x2 = x1_ref[...] + w[:, 0:1] * y0_ref[...] + w[:, 1:2] * y1_ref[...]
    hn = _rms(x2, gp_ref[...]).astype(BF16)
    gate = _sigmoid(jnp.dot(hn, wpg_ref[...], preferred_element_type=F32))
    pe = jnp.dot(p_ref[...].astype(BF16), wp_ref[...], preferred_element_type=F32)
    x3 = x2 + gate * pe
    o_ref[...] = _rms(x3, gf_ref[...])


def _ple_final(x1, ypairs, wcols, p2d, gp, wpg, wp, gf, tm=256):
    n, d = x1.shape
    nb = n // tm
    return pl.pallas_call(
        _ple_final_kernel,
        out_shape=jax.ShapeDtypeStruct((n, d), F32),
        grid=(nb,),
        in_specs=[
            pl.BlockSpec((tm, d), lambda i: (i, 0)),
            pl.BlockSpec((tm, d), lambda i: (i, 0)),
            pl.BlockSpec((tm, d), lambda i: (nb + i, 0)),
            pl.BlockSpec((tm, wcols.shape[1]), lambda i: (i, 0)),
            pl.BlockSpec((tm, p2d.shape[1]), lambda i: (i, 0)),
            pl.BlockSpec((1, d), lambda i: (0, 0)),
            pl.BlockSpec((d, d), lambda i: (0, 0)),
            pl.BlockSpec(wp.shape, lambda i: (0, 0)),
            pl.BlockSpec((1, d), lambda i: (0, 0)),
        ],
        out_specs=pl.BlockSpec((tm, d), lambda i: (i, 0)),
        compiler_params=_cparams(("parallel",)),
        name="ple_final",
    )(x1, ypairs, ypairs, wcols, p2d, gp, wpg, wp, gf)


def _block_diag(w, per):
    nb, bw, _ = w.shape
    w = w.reshape(nb // per, per, bw, bw)
    eye = jnp.eye(per, dtype=w.dtype)
    return jnp.einsum("cpij,pq->cpiqj", w, eye).reshape(nb // per, per * bw, per * bw)


def _layer(x2d, batch, seq, ln_mix, w_in, cmp_k_pos, cmp_k_w1, cmp_k_w2, cmp_v_pos, cmp_v_w1, cmp_v_w2, conv_w,
           conv_b, lru_wa, lru_ba, lru_wx, lru_bx, lru_lambda, w_nsa_up, w_lru_up, w_out, ln_ffn, w_grp, b_grp, w_exp,
           b_exp, w_gate, w_up, w_down):
    n, d = x2d.shape
    g, hpg, hd = NSA_KV_GROUPS, NSA_HPG, HEAD_DIM
    nqb = seq // Q_BLOCK

    c = np.cumsum((0, NSA_Q_WIDTH) + (NSA_KV_WIDTH,) * 6 + (3 * NSA_HEADS, LRU_WIDTH, LRU_WIDTH, d, d))
    seg = lambda k: w_in[:, int(c[k]):int(c[k + 1])]
    w_in_r = jnp.concatenate(
        [seg(10), seg(11), seg(0)] + [seg(k) for k in range(1, 7)] + [seg(8), seg(9), seg(7),
         jnp.zeros((d, Z_WIDTH - int(c[-1])), w_in.dtype)], axis=1).astype(BF16)

    z = _in_proj(x2d, ln_mix.reshape(1, d), w_in_r)

    tq = NSA_TQ
    nqt = seq // tq

    def heads_t(col):
        a = z[:, col:col + NSA_KV_WIDTH].reshape(batch, nqt, tq, g, hd)
        return a.transpose(0, 3, 1, 4, 2)

    def rows(col):
        return z[:, col:col + NSA_KV_WIDTH].reshape(batch, seq, g, hd).transpose(0, 2, 1, 3)

    def windows(col):
        a = z[:, col:col + NSA_KV_WIDTH].reshape(batch, seq // CMP_STRIDE, CMP_STRIDE, g, hd)
        return a.transpose(0, 3, 1, 2, 4).reshape(batch, g, seq // CMP_STRIDE, CMP_STRIDE * hd)

    qt = z[:, Z_Q:Z_Q + NSA_Q_WIDTH].reshape(batch, nqt, tq, g, hpg, hd)
    qt = qt.transpose(0, 3, 1, 5, 4, 2).reshape(batch, g, nqt, hd, hpg * tq)
    gates = z[:, Z_GATE:Z_GATE + 3 * NSA_HEADS].reshape(batch, nqt, tq, g, hpg, 3)
    gates = gates.transpose(0, 3, 1, 5, 4, 2).reshape(batch, g, nqt, 3, hpg * tq)
    gates = jnp.pad(gates, ((0, 0), (0, 0), (0, 0), (0, SUBLANES - 3), (0, 0)))
    nsb = seq // SLC_BLOCK
    onehot = (jnp.arange(seq, dtype=jnp.int32)[:, None] // SLC_BLOCK == jnp.arange(nsb, dtype=jnp.int32)[None, :])
    ka = jnp.concatenate([rows(Z_KS), jnp.broadcast_to(onehot.astype(BF16), (batch, g, seq, nsb))], axis=-1)

    pos8 = lambda pos: jnp.broadcast_to(pos.reshape(1, -1), (SUBLANES, pos.size)).astype(BF16)
    kc, vct = _compress(windows(Z_KC), windows(Z_VC), pos8(cmp_k_pos), cmp_k_w1.astype(BF16), cmp_k_w2.astype(BF16),
                        pos8(cmp_v_pos), cmp_v_w1.astype(BF16), cmp_v_w2.T.astype(BF16))
    ot = _nsa_attention2(qt, kc, vct, ka, heads_t(Z_VS), rows(Z_KW), heads_t(Z_VW), gates)
    o = ot.reshape(batch, g, nqt, hd, hpg, tq).transpose(0, 2, 5, 1, 4, 3).reshape(n, NSA_Q_WIDTH)

    per = LRU_CB // LRU_BW
    lru = _rglru(z, batch, seq, conv_w, conv_b.reshape(1, -1), _block_diag(lru_wa, per).astype(BF16),
                 lru_ba.reshape(1, -1), _block_diag(lru_wx, per).astype(BF16), lru_bx.reshape(1, -1),
                 lru_lambda.reshape(1, -1))

    merged = _merge(o, lru, z, w_nsa_up.astype(BF16), w_lru_up.astype(BF16))

    nr = SUBLANES + N_EXPERTS
    wr = jnp.zeros((nr, d), F32).at[0:N_GROUPS].set(w_grp.T).at[SUBLANES:].set(w_exp.T)
    br = jnp.zeros((nr, 1), F32).at[0:N_GROUPS, 0].set(b_grp).at[SUBLANES:, 0].set(b_exp)
    x1, xn, eid, ew = _out_route(merged, x2d, w_out.astype(BF16), ln_ffn.reshape(1, d), wr, br)

    nvalid, texp, cnt, tok, dst, n_rows_out = _moe_plan(eid[0:EXPERT_TOPK], n, MOE_TM)
    ypairs = _moe(nvalid, texp, cnt, tok, dst, xn, w_gate, w_up, w_down, n_rows_out)

    return x1, ypairs, ew.T


def kernel(x, p, ln_mix, w_in, cmp_k_pos, cmp_k_w1, cmp_k_w2, cmp_v_pos, cmp_v_w1, cmp_v_w2, conv_w, conv_b, lru_wa, lru_ba, lru_wx, lru_bx, lru_lambda, w_nsa_up, w_lru_up, w_out, ln_ffn, w_grp, b_grp, w_exp, b_exp, w_gate, w_up, w_down, ln_ple, w_ple, w_ple_gate, ln_final):
    batch, seq, d = x.shape
    assert p.shape[0] == 1, "the final norm is fused into the (single) layer's last kernel"
    n = batch * seq
    x1, ypairs, wcols = _layer(
        x.reshape(n, d), batch, seq, ln_mix[0], w_in[0], cmp_k_pos[0], cmp_k_w1[0], cmp_k_w2[0],
        cmp_v_pos[0], cmp_v_w1[0], cmp_v_w2[0], conv_w[0], conv_b[0], lru_wa[0], lru_ba[0], lru_wx[0], lru_bx[0],
        lru_lambda[0], w_nsa_up[0], w_lru_up[0], w_out[0], ln_ffn[0], w_grp[0], b_grp[0], w_exp[0], b_exp[0],
        w_gate[0], w_up[0], w_down[0])
    out = _ple_final(x1, ypairs, wcols, p[0].reshape(n, -1), ln_ple[0].reshape(1, d),
                     w_ple_gate[0].astype(BF16), w_ple[0].astype(BF16), ln_final.reshape(1, d))
    return out.reshape(batch, seq, d)
```

```python
import numpy as np
import jax
import jax.numpy as jnp
from jax import lax
from jax.experimental import pallas as pl
from jax.experimental.pallas import tpu as pltpu

F32 = jnp.float32
BF16 = jnp.bfloat16

NSA_HEADS = 16
NSA_KV_GROUPS = 4
NSA_HPG = NSA_HEADS // NSA_KV_GROUPS
HEAD_DIM = 64
NSA_Q_WIDTH = NSA_HEADS * HEAD_DIM
NSA_KV_WIDTH = NSA_KV_GROUPS * HEAD_DIM
CMP_BLOCK = 32
CMP_STRIDE = 16
CMP_HIDDEN = 2 * HEAD_DIM
SLC_BLOCK = 64
SLC_TOPN = 16
WINDOW = 512
ATTN_SCALE = HEAD_DIM ** -0.5
NEG_INF = -1e30
FORCE_SCORE = 1e4
LRU_WIDTH = 1024
LRU_BLOCKS = 16
LRU_BW = LRU_WIDTH // LRU_BLOCKS
CONV_WIDTH = 4
LRU_C = 8.0
N_GROUPS = 4
EXPERTS_PER_GROUP = 8
N_EXPERTS = N_GROUPS * EXPERTS_PER_GROUP
EXPERT_TOPK = 2
D_EXPERT = 512
EPS = 1e-6

LANES = 128
SUBLANES = 8
VMEM_LIMIT_BYTES = 56 * 1024 * 1024

Z_MGA = 0
Z_MGB = 2048
Z_Q = 4096
Z_KK = 5120
Z_VV = 5632
Z_KC = 6144
Z_VC = 6400
Z_LX = 6656
Z_LY = 7680
Z_GATE = 8704
Z_WIDTH = 9216
LRU_CB = 256
MOE_TM = 256
NSA_TQ = 256
NSA_CK = 512


def _cparams(sem, vmem=VMEM_LIMIT_BYTES):
    return pltpu.CompilerParams(dimension_semantics=sem, vmem_limit_bytes=vmem)


def _rms(x, g):
    return x * lax.rsqrt(jnp.mean(x * x, axis=-1, keepdims=True) + EPS) * g


def _gelu_tanh(x):
    return 0.5 * x * (1.0 + jnp.tanh(np.sqrt(2.0 / np.pi) * (x + 0.044715 * (x * x * x))))


def _sigmoid(x):
    return 1.0 / (1.0 + jnp.exp(-x))


def _in_proj_kernel(x_ref, g_ref, w_ref, o_ref, h_ref):
    @pl.when(pl.program_id(1) == 0)
    def _():
        h_ref[...] = _rms(x_ref[...], g_ref[...]).astype(BF16)

    o_ref[...] = jnp.dot(h_ref[...], w_ref[...], preferred_element_type=F32).astype(o_ref.dtype)


def _in_proj(x2d, g, w, tm=512, tn=1024):
    n, d = x2d.shape
    nw = w.shape[1]
    return pl.pallas_call(
        _in_proj_kernel,
        out_shape=jax.ShapeDtypeStruct((n, nw), BF16),
        grid=(n // tm, nw // tn),
        in_specs=[
            pl.BlockSpec((tm, d), lambda i, j: (i, 0)),
            pl.BlockSpec((1, d), lambda i, j: (0, 0)),
            pl.BlockSpec((d, tn), lambda i, j: (0, j)),
        ],
        out_specs=pl.BlockSpec((tm, tn), lambda i, j: (i, j)),
        scratch_shapes=[pltpu.VMEM((tm, d), BF16)],
        compiler_params=_cparams(("parallel", "arbitrary")),
        name="in_proj",
    )(x2d, g, w)


def _compress_kernel(kx_ref, vx_ref, kpos_ref, kw1_ref, kw2_ref, vpos_ref, vw1_ref, vw2t_ref, kc_ref, vct_ref):
    half = kx_ref.shape[-1]

    def hidden(x_ref, pos_ref, w1_ref):
        x = x_ref[0, 0]
        w1 = w1_ref[...]
        ha = jnp.dot(x, w1[:half], preferred_element_type=F32)
        hb = jnp.dot(x, w1[half:], preferred_element_type=F32)
        hb = pltpu.roll(hb, hb.shape[0] - 1, axis=0)
        pc = jnp.dot(pos_ref[...], w1, preferred_element_type=F32)[0:1]
        return _gelu_tanh(ha + hb + pc).astype(BF16)

    hk = hidden(kx_ref, kpos_ref, kw1_ref)
    kc_ref[0, 0] = jnp.dot(hk, kw2_ref[...], preferred_element_type=F32).astype(kc_ref.dtype)
    hv = hidden(vx_ref, vpos_ref, vw1_ref)
    vct_ref[0, 0] = lax.dot_general(vw2t_ref[...], hv, (((1,), (1,)), ((), ())),
                                    preferred_element_type=F32).astype(vct_ref.dtype)


def _compress(kx, vx, kpos, kw1, kw2, vpos, vw1, vw2t):
    b, g, ncp, half = kx.shape
    full = lambda a: pl.BlockSpec(a.shape, lambda i, j: (0,) * a.ndim)
    return pl.pallas_call(
        _compress_kernel,
        out_shape=(jax.ShapeDtypeStruct((b, g, ncp, HEAD_DIM), BF16),
                   jax.ShapeDtypeStruct((b, g, HEAD_DIM, ncp), BF16)),
        grid=(b, g),
        in_specs=[
            pl.BlockSpec((1, 1, ncp, half), lambda i, j: (i, j, 0, 0)),
            pl.BlockSpec((1, 1, ncp, half), lambda i, j: (i, j, 0, 0)),
            full(kpos), full(kw1), full(kw2), full(vpos), full(vw1), full(vw2t),
        ],
        out_specs=(pl.BlockSpec((1, 1, ncp, HEAD_DIM), lambda i, j: (i, j, 0, 0)),
                   pl.BlockSpec((1, 1, HEAD_DIM, ncp), lambda i, j: (i, j, 0, 0))),
        compiler_params=_cparams(("parallel", "parallel")),
        name="compress",
    )(kx, vx, kpos, kw1, kw2, vpos, vw1, vw2t)


def _nsa_kernel(zq_ref, zkk_ref, zvv_ref, zg_ref, kc_ref, vct_ref, o_ref, kcomb_ref, score_ref, gate_ref):
    grp = pl.program_id(1)
    qi = pl.program_id(2)
    tq, hd, hpg = NSA_TQ, HEAD_DIM, NSA_HPG
    nq = tq * hpg
    seq = kcomb_ref.shape[0]
    ncp = kc_ref.shape[2]
    nsb = score_ref.shape[0]
    blk_shift = SLC_BLOCK.bit_length() - 1

    @pl.when(qi == 0)
    def _():
        kcomb_ref[:, 0:2 * hd] = zkk_ref[...]
        blk_of_row = lax.broadcasted_iota(jnp.int32, (seq, 2 * hd), 0) >> blk_shift
        lane = lax.broadcasted_iota(jnp.int32, (seq, 2 * hd), 1)
        kcomb_ref[:, 2 * hd:] = jnp.where(blk_of_row == lane, 1.0, 0.0).astype(BF16)

    qt = zq_ref[...].T
    q = jnp.concatenate([qt[h * hd:(h + 1) * hd, :] for h in range(hpg)], axis=1)
    q = (q.astype(F32) * ATTN_SCALE).astype(BF16)
    zero = jnp.zeros((hd, nq), BF16)
    t_lane = qi * tq + (lax.broadcasted_iota(jnp.int32, (1, nq), 1) & (tq - 1))
    tn_dims = (((0,), (0,)), ((), ()))

    sc = jnp.dot(kc_ref[0, 0], q, preferred_element_type=F32)
    cmp_end = lax.broadcasted_iota(jnp.int32, (ncp, 1), 0) * CMP_STRIDE + (CMP_BLOCK - 1)
    cmask = cmp_end <= t_lane
    sc = jnp.where(cmask, sc, NEG_INF)
    pc = jnp.where(cmask, jnp.exp(sc - jnp.max(sc, axis=0, keepdims=True)), 0.0)
    lc = jnp.sum(pc, axis=0, keepdims=True)
    pc = pc * (1.0 / jnp.where(lc > 0.0, lc, 1.0))
    o_cmp = jnp.dot(vct_ref[0, 0], pc.astype(BF16), preferred_element_type=F32)

    nwc = WINDOW // tq + 1
    t_lo = t_lane - WINDOW
    ks_w, vs_w = [], []
    for i in range(nwc):
        rows = pl.ds(pl.multiple_of(jnp.maximum(qi - (nwc - 1) + i, 0) * tq, tq), tq)
        ks_w.append(kcomb_ref[rows, 0:2 * hd])
        vs_w.append(zvv_ref[rows, :])
    qw = jnp.concatenate([zero, q], axis=0)
    sw = jnp.dot(jnp.concatenate(ks_w, axis=0), qw, preferred_element_type=F32)
    sw_parts = []
    for i in range(nwc):
        spos = (qi - (nwc - 1) + i) * tq + lax.broadcasted_iota(jnp.int32, (tq, 1), 0)
        if i < nwc - 1:
            ok = jnp.where(spos >= 0, spos, -(1 << 24)) > t_lo
        else:
            ok = spos <= t_lane
        sw_parts.append(jnp.where(ok, sw[i * tq:(i + 1) * tq], NEG_INF))
    sw = jnp.concatenate(sw_parts, axis=0)
    pw = jnp.exp(sw - jnp.max(sw, axis=0, keepdims=True))
    lw = jnp.sum(pw, axis=0, keepdims=True)
    o_win = lax.dot_general(jnp.concatenate(vs_w, axis=0), pw.astype(BF16), tn_dims,
                            preferred_element_type=F32)[hd:2 * hd] * (1.0 / lw)

    psum = pc[:, 0:tq]
    for h in range(1, hpg):
        psum = psum + pc[:, h * tq:(h + 1) * tq]
    jrow = lax.broadcasted_iota(jnp.int32, (nsb, ncp), 0)
    crel = lax.broadcasted_iota(jnp.int32, (nsb, ncp), 1) - jrow * (SLC_BLOCK // CMP_STRIDE)
    ovt = jnp.where((crel >= 0) & (crel <= 2), 1.0, jnp.where((crel == -1) | (crel == 3), 0.5, 0.0)).astype(F32)
    imp = jnp.dot(ovt, psum, preferred_element_type=F32, precision=lax.Precision.HIGHEST)
    jblk = lax.broadcasted_iota(jnp.int32, (nsb, tq), 0)
    blk = (qi * tq + lax.broadcasted_iota(jnp.int32, (1, tq), 1)) >> blk_shift
    forced = (jblk == 0) | (jblk == blk) | (jblk == blk - 1)
    score = jnp.where(forced, FORCE_SCORE, jnp.where(jblk <= blk, imp, -1.0))
    score_ref[...] = score

    def rank_one(k, cnt):
        row = score_ref[pl.ds(k, 1), :]
        ge = jnp.where(row >= score, 1.0, 0.0)
        gt = jnp.where(row > score, 1.0, 0.0)
        return cnt + jnp.where(jblk > k, ge, gt)

    bpt = tq // SLC_BLOCK

    def rank_group(i, cnt):
        for u in range(bpt):
            cnt = rank_one(bpt * i + u, cnt)
        return cnt

    n_vis = jnp.minimum((qi + 1) * bpt, nsb)
    rank = lax.fori_loop(0, n_vis // bpt, rank_group, jnp.zeros((nsb, tq), F32))
    selb = jnp.where(rank < float(min(SLC_TOPN, nsb)), 0.0, NEG_INF).astype(BF16)

    ck = NSA_CK
    pad = jnp.zeros((2 * hd - nsb, nq), BF16)
    qa = jnp.concatenate([q, zero, jnp.concatenate([selb] * hpg, axis=1), pad], axis=0)

    def slc_scores(c):
        return jnp.dot(kcomb_ref[pl.ds(pl.multiple_of(c * ck, ck), ck), :], qa, preferred_element_type=F32)

    def slc_update(c, s, m, l, acc):
        m_new = jnp.maximum(m, jnp.max(s, axis=0, keepdims=True))
        alpha = jnp.exp(m - m_new)
        p = jnp.exp(s - m_new)
        l = alpha * l + jnp.sum(p, axis=0, keepdims=True)
        v = zvv_ref[pl.ds(pl.multiple_of(c * ck, ck), ck), :]
        pv = lax.dot_general(v, p.astype(BF16), tn_dims, preferred_element_type=F32)[0:hd]
        return m_new, l, acc * alpha + pv

    def slc_body(c, carry):
        m, l, acc, s = carry
        s_next = slc_scores(c + 1)
        return slc_update(c, s, m, l, acc) + (s_next,)

    n_full = (qi * tq) // ck
    init = (jnp.full((1, nq), NEG_INF, F32), jnp.zeros((1, nq), F32), jnp.zeros((hd, nq), F32), slc_scores(0))
    m_s, l_s, acc_s, s_d = lax.fori_loop(0, n_full, slc_body, init)
    kpos_d = n_full * ck + lax.broadcasted_iota(jnp.int32, (ck, 1), 0)
    s_d = jnp.where(kpos_d <= t_lane, s_d, NEG_INF)
    _, l_s, acc_s = slc_update(n_full, s_d, m_s, l_s, acc_s)
    o_slc = acc_s * (1.0 / l_s)

    gate_ref[...] = _sigmoid(zg_ref[...].astype(F32)).T

    def branch_gate(br):
        rows = [gate_ref[pl.ds(grp * (hpg * 3) + h * 3 + br, 1), :] for h in range(hpg)]
        return jnp.concatenate(rows, axis=1)

    o = branch_gate(0) * o_cmp + branch_gate(1) * o_slc + branch_gate(2) * o_win
    o_heads = jnp.concatenate([o[:, h * tq:(h + 1) * tq] for h in range(hpg)], axis=0)
    o_ref[...] = o_heads.T.astype(o_ref.dtype)


def _nsa_attention(z, kc, vct, batch, seq):
    n = z.shape[0]
    g, hd, tq = NSA_KV_GROUPS, HEAD_DIM, NSA_TQ
    nqt = seq // tq
    ncp = kc.shape[2]
    nsb = seq // SLC_BLOCK
    gw = NSA_HPG * hd
    assert seq % NSA_CK == 0 and NSA_CK % tq == 0 and WINDOW % tq == 0 and nsb <= 2 * hd
    tile = lambda i, j, k: i * nqt + k
    return pl.pallas_call(
        _nsa_kernel,
        out_shape=jax.ShapeDtypeStruct((n, NSA_Q_WIDTH), BF16),
        grid=(batch, g, nqt),
        in_specs=[
            pl.BlockSpec((tq, gw), lambda i, j, k: (tile(i, j, k), Z_Q // gw + j)),
            pl.BlockSpec((seq, 2 * hd), lambda i, j, k: (i, Z_KK // (2 * hd) + j)),
            pl.BlockSpec((seq, 2 * hd), lambda i, j, k: (i, Z_VV // (2 * hd) + j)),
            pl.BlockSpec((tq, LANES), lambda i, j, k: (tile(i, j, k), Z_GATE // LANES)),
            pl.BlockSpec((1, 1, ncp, hd), lambda i, j, k: (i, j, 0, 0)),
            pl.BlockSpec((1, 1, hd, ncp), lambda i, j, k: (i, j, 0, 0)),
        ],
        out_specs=pl.BlockSpec((tq, gw), lambda i, j, k: (tile(i, j, k), j)),
        scratch_shapes=[pltpu.VMEM((seq, 4 * hd), BF16), pltpu.VMEM((nsb, tq), F32), pltpu.VMEM((LANES, tq), F32)],
        compiler_params=_cparams(("parallel", "parallel", "arbitrary")),
        name="nsa_attn",
    )(z, z, z, z, kc, vct)


def _rglru_kernel(x_ref, y_ref, cw_ref, cb_ref, wa_ref, ba_ref, wx_ref, bx_ref, lam_ref, o_ref,
                  tail_ref, h_ref, a_ref, u_ref):
    tc = pl.program_id(2)
    tt, cb = x_ref.shape

    @pl.when(tc == 0)
    def _():
        tail_ref[...] = jnp.zeros_like(tail_ref)
        h_ref[...] = jnp.zeros_like(h_ref)

    x = x_ref[...].astype(F32)
    xe = jnp.concatenate([tail_ref[...], x], axis=0)
    tail_ref[...] = x[tt - SUBLANES:, :]
    cw = cw_ref[...]
    xc = cb_ref[...]
    for k in range(CONV_WIDTH):
        off = SUBLANES - (CONV_WIDTH - 1) + k
        xc = xc + cw[k:k + 1, :] * xe[off:off + tt, :]
    xcb = xc.astype(BF16)
    r = _sigmoid(jnp.dot(xcb, wa_ref[0], preferred_element_type=F32) + ba_ref[...])
    ig = _sigmoid(jnp.dot(xcb, wx_ref[0], preferred_element_type=F32) + bx_ref[...])
    nl = -lam_ref[...]
    softplus = jnp.maximum(nl, 0.0) + jnp.log1p(jnp.exp(-jnp.abs(nl)))
    log_a = (-LRU_C) * softplus * r
    a_ref[...] = jnp.exp(log_a)
    th = jnp.tanh(log_a)
    u_ref[...] = jnp.sqrt(-2.0 * th / (1.0 - th)) * (ig * xc)

    row = lax.broadcasted_iota(jnp.int32, (SUBLANES, cb), 0)

    def step(i, h):
        sl = pl.ds(pl.multiple_of(i * SUBLANES, SUBLANES), SUBLANES)
        a = a_ref[sl, :]
        u = u_ref[sl, :]
        for s in (1, 2, 4):
            a_s = jnp.where(row >= s, pltpu.roll(a, s, axis=0), 1.0)
            u_s = jnp.where(row >= s, pltpu.roll(u, s, axis=0), 0.0)
            u = a * u_s + u
            a = a * a_s
        hrows = a * h + u
        u_ref[sl, :] = hrows
        return hrows[SUBLANES - 1:SUBLANES, :]

    h_ref[...] = lax.fori_loop(0, tt // SUBLANES, step, h_ref[...], unroll=2)
    o_ref[...] = (u_ref[...] * _gelu_tanh(y_ref[...].astype(F32))).astype(o_ref.dtype)


def _rglru(z, batch, seq, cw, cbias, wa_bd, ba, wx_bd, bx, lam, tt=512):
    n = z.shape[0]
    ncb = LRU_WIDTH // LRU_CB
    nt = seq // tt
    row = lambda i, j, k: i * nt + k
    vec = lambda r: pl.BlockSpec((r, LRU_CB), lambda i, j, k: (0, j))
    return pl.pallas_call(
        _rglru_kernel,
        out_shape=jax.ShapeDtypeStruct((n, LRU_WIDTH), BF16),
        grid=(batch, ncb, nt),
        in_specs=[
            pl.BlockSpec((tt, LRU_CB), lambda i, j, k: (row(i, j, k), Z_LX // LRU_CB + j)),
            pl.BlockSpec((tt, LRU_CB), lambda i, j, k: (row(i, j, k), Z_LY // LRU_CB + j)),
            vec(CONV_WIDTH), vec(1),
            pl.BlockSpec((1, LRU_CB, LRU_CB), lambda i, j, k: (j, 0, 0)), vec(1),
            pl.BlockSpec((1, LRU_CB, LRU_CB), lambda i, j, k: (j, 0, 0)), vec(1),
            vec(1),
        ],
        out_specs=pl.BlockSpec((tt, LRU_CB), lambda i, j, k: (row(i, j, k), j)),
        scratch_shapes=[pltpu.VMEM((SUBLANES, LRU_CB), F32), pltpu.VMEM((1, LRU_CB), F32),
                        pltpu.VMEM((tt, LRU_CB), F32), pltpu.VMEM((tt, LRU_CB), F32)],
        compiler_params=_cparams(("parallel", "parallel", "arbitrary")),
        name="rglru",
    )(z, z, cw, cbias, wa_bd, ba, wx_bd, bx, lam)


def _merge_kernel(o_ref, l_ref, mga_ref, mgb_ref, wn_ref, wl_ref, m_ref):
    ya = jnp.dot(o_ref[...], wn_ref[...], preferred_element_type=F32)
    yb = jnp.dot(l_ref[...], wl_ref[...], preferred_element_type=F32)
    m = _sigmoid(mga_ref[...].astype(F32)) * ya + _sigmoid(mgb_ref[...].astype(F32)) * yb
    m_ref[...] = m.astype(m_ref.dtype)


def _merge(o, lru, z, wn, wl, tm=512):
    n, d = o.shape[0], wn.shape[1]
    return pl.pallas_call(
        _merge_kernel,
        out_shape=jax.ShapeDtypeStruct((n, d), BF16),
        grid=(n // tm,),
        in_specs=[
            pl.BlockSpec((tm, o.shape[1]), lambda i: (i, 0)),
            pl.BlockSpec((tm, lru.shape[1]), lambda i: (i, 0)),
            pl.BlockSpec((tm, d), lambda i: (i, Z_MGA // d)),
            pl.BlockSpec((tm, d), lambda i: (i, Z_MGB // d)),
            pl.BlockSpec(wn.shape, lambda i: (0, 0)),
            pl.BlockSpec(wl.shape, lambda i: (0, 0)),
        ],
        out_specs=pl.BlockSpec((tm, d), lambda i: (i, 0)),
        compiler_params=_cparams(("parallel",)),
        name="merge",
    )(o, lru, z, z, wn, wl)


def _out_route_kernel(m_ref, x_ref, wo_ref, g_ref, wr_ref, br_ref, x1_ref, xn_ref, eid_ref, ew_ref):
    x1 = x_ref[...] + jnp.dot(m_ref[...], wo_ref[...], preferred_element_type=F32)
    x1_ref[...] = x1
    xn = _rms(x1, g_ref[...])
    xn_ref[...] = xn
    lg = lax.dot_general(wr_ref[...], xn, (((1,), (1,)), ((), ())), preferred_element_type=F32,
                         precision=lax.Precision.HIGHEST) + br_ref[...]
    tm = lg.shape[1]
    sub = lax.broadcasted_iota(jnp.int32, (SUBLANES, tm), 0)

    def first_argmax(v, vmax):
        return jnp.min(jnp.where(v == vmax, sub, SUBLANES), axis=0, keepdims=True)

    gl = jnp.where(sub < N_GROUPS, lg[0:SUBLANES], -jnp.inf)
    gmax = jnp.max(gl, axis=0, keepdims=True)
    ge = jnp.exp(gl - gmax)
    gprob = ge / jnp.sum(ge, axis=0, keepdims=True)
    g_val = jnp.max(gprob, axis=0, keepdims=True)
    g_idx = first_argmax(gprob, g_val)
    e_in = jnp.zeros((EXPERTS_PER_GROUP, tm), F32)
    for gi in range(N_GROUPS):
        lo = SUBLANES + gi * EXPERTS_PER_GROUP
        e_in = jnp.where(g_idx == gi, lg[lo:lo + EXPERTS_PER_GROUP], e_in)
    ee = jnp.exp(e_in - jnp.max(e_in, axis=0, keepdims=True))
    eprob = ee / jnp.sum(ee, axis=0, keepdims=True)
    v1 = jnp.max(eprob, axis=0, keepdims=True)
    i1 = first_argmax(eprob, v1)
    rest = jnp.where(sub == i1, -1.0, eprob)
    v2 = jnp.max(rest, axis=0, keepdims=True)
    i2 = first_argmax(rest, v2)
    den = v1 + v2
    eid = jnp.where(sub == 0, g_idx * EXPERTS_PER_GROUP + i1, g_idx * EXPERTS_PER_GROUP + i2)
    eid_ref[...] = eid
    ew_ref[...] = jnp.where(sub == 0, g_val * v1 / den, g_val * v2 / den)


def _out_route(m, x2d, wo, g, wr, br, tm=512):
    n, d = x2d.shape
    nr = wr.shape[0]
    once = pl.Buffered(1)
    return pl.pallas_call(
        _out_route_kernel,
        out_shape=(jax.ShapeDtypeStruct((n, d), F32), jax.ShapeDtypeStruct((n, d), F32),
                   jax.ShapeDtypeStruct((SUBLANES, n), jnp.int32), jax.ShapeDtypeStruct((SUBLANES, n), F32)),
        grid=(n // tm,),
        in_specs=[
            pl.BlockSpec((tm, d), lambda i: (i, 0)),
            pl.BlockSpec((tm, d), lambda i: (i, 0)),
            pl.BlockSpec((d, d), lambda i: (0, 0), pipeline_mode=once),
            pl.BlockSpec((1, d), lambda i: (0, 0)),
            pl.BlockSpec((nr, d), lambda i: (0, 0), pipeline_mode=once),
            pl.BlockSpec((nr, 1), lambda i: (0, 0)),
        ],
        out_specs=(pl.BlockSpec((tm, d), lambda i: (i, 0)), pl.BlockSpec((tm, d), lambda i: (i, 0)),
                   pl.BlockSpec((SUBLANES, tm), lambda i: (0, i)), pl.BlockSpec((SUBLANES, tm), lambda i: (0, i))),
        compiler_params=_cparams(("parallel",)),
        name="out_route",
    )(m, x2d, wo, g, wr, br)


def _for_rows(cnt, fn, unroll=8):
    sh = unroll.bit_length() - 1

    def group(gidx, c):
        for u in range(unroll):
            fn(gidx * unroll + u)
        return c

    def single(r, c):
        fn(r)
        return c

    lax.fori_loop(0, cnt >> sh, group, 0)
    lax.fori_loop((cnt >> sh) << sh, cnt, single, 0)


def _moe_kernel(nvalid_ref, texp_ref, cnt_ref, tok_ref, tokn_ref, dst_ref, xn_hbm, wg_ref, wu_ref, wd_ref, y_hbm,
                xbuf, ybuf, gsem, ssem):
    i = pl.program_id(0)
    nv = nvalid_ref[0]
    slot = i % 2

    def gather_row(idx_ref, s):
        def fn(r):
            pltpu.make_async_copy(xn_hbm.at[pl.ds(idx_ref[0, 0, r], 1), :], xbuf.at[s, pl.ds(r, 1), :],
                                  gsem.at[s]).start()
        return fn

    def gather_wait_row(s):
        def fn(r):
            pltpu.make_async_copy(xn_hbm.at[pl.ds(0, 1), :], xbuf.at[s, pl.ds(r, 1), :], gsem.at[s]).wait()
        return fn

    def scatter_row(s):
        def fn(r):
            pltpu.make_async_copy(ybuf.at[s, pl.ds(r, 1), :], y_hbm.at[pl.ds(dst_ref[0, 0, r], 1), :],
                                  ssem.at[s]).start()
        return fn

    def scatter_wait_row(s):
        def fn(r):
            pltpu.make_async_copy(ybuf.at[s, pl.ds(r, 1), :], y_hbm.at[pl.ds(0, 1), :], ssem.at[s]).wait()
        return fn

    @pl.when(i == 0)
    def _():
        xbuf[...] = jnp.zeros_like(xbuf)
        _for_rows(cnt_ref[0], gather_row(tok_ref, 0))

    @pl.when(i + 1 < nv)
    def _():
        _for_rows(cnt_ref[i + 1], gather_row(tokn_ref, 1 - slot))

    @pl.when(i < nv)
    def _():
        _for_rows(cnt_ref[i], gather_wait_row(slot))
        x = xbuf[slot].astype(BF16)
        gp = jnp.dot(x, wg_ref[0], preferred_element_type=F32)
        up = jnp.dot(x, wu_ref[0], preferred_element_type=F32)
        hid = (gp * _sigmoid(gp) * up).astype(BF16)
        y = jnp.dot(hid, wd_ref[0], preferred_element_type=F32)

        @pl.when(i >= 2)
        def _():
            _for_rows(cnt_ref[i - 2], scatter_wait_row(slot))

        ybuf[slot] = y
        _for_rows(cnt_ref[i], scatter_row(slot))

    @pl.when(i == nv - 1)
    def _():
        _for_rows(cnt_ref[i], scatter_wait_row(slot))

        @pl.when(nv >= 2)
        def _():
            _for_rows(cnt_ref[i - 1], scatter_wait_row(1 - slot))


def _moe(nvalid, texp, cnt, tok, dst, xn, wg, wu, wd, n_rows_out):
    nt, _, tm = tok.shape
    d = xn.shape[1]
    de = wg.shape[2]
    grid_spec = pltpu.PrefetchScalarGridSpec(
        num_scalar_prefetch=3,
        grid=(nt,),
        in_specs=[
            pl.BlockSpec((1, 1, tm), lambda i, nv, te, ct: (i, 0, 0), memory_space=pltpu.SMEM),
            pl.BlockSpec((1, 1, tm), lambda i, nv, te, ct: (jnp.minimum(i + 1, nt - 1), 0, 0),
                         memory_space=pltpu.SMEM),
            pl.BlockSpec((1, 1, tm), lambda i, nv, te, ct: (i, 0, 0), memory_space=pltpu.SMEM),
            pl.BlockSpec(memory_space=pl.ANY),
            pl.BlockSpec((1, d, de), lambda i, nv, te, ct: (te[i], 0, 0)),
            pl.BlockSpec((1, d, de), lambda i, nv, te, ct: (te[i], 0, 0)),
            pl.BlockSpec((1, de, d), lambda i, nv, te, ct: (te[i], 0, 0)),
        ],
        out_specs=pl.BlockSpec(memory_space=pl.ANY),
        scratch_shapes=[pltpu.VMEM((2, tm, d), F32), pltpu.VMEM((2, tm, d), F32),
                        pltpu.SemaphoreType.DMA((2,)), pltpu.SemaphoreType.DMA((2,))],
    )
    return pl.pallas_call(
        _moe_kernel,
        out_shape=jax.ShapeDtypeStruct((n_rows_out, d), F32),
        grid_spec=grid_spec,
        compiler_params=_cparams(("arbitrary",)),
        name="moe",
    )(nvalid, texp, cnt, tok, tok, dst, xn, wg, wu, wd)


def _moe_plan(eid, n, tm):
    e_flat = eid.reshape(-1)
    npairs = e_flat.shape[0]
    skeys = jnp.sort(e_flat * npairs + jnp.arange(npairs, dtype=jnp.int32))
    sorted_pair = skeys % npairs
    counts = jnp.sum((e_flat[:, None] == jnp.arange(N_EXPERTS, dtype=jnp.int32)[None, :]).astype(jnp.int32), axis=0)
    starts = jnp.cumsum(counts) - counts
    padded = ((counts + tm - 1) // tm) * tm
    ends = jnp.cumsum(padded)
    offs = ends - padded
    n_rows = npairs + N_EXPERTS * tm
    nt = n_rows // tm
    tile_start = jnp.arange(nt, dtype=jnp.int32) * tm
    texp = jnp.minimum(jnp.sum((tile_start[:, None] >= ends[None, :]).astype(jnp.int32), axis=1), N_EXPERTS - 1)
    cnt = jnp.clip(offs[texp] + counts[texp] - tile_start, 0, tm)
    within = tile_start[:, None] - offs[texp][:, None] + jnp.arange(tm, dtype=jnp.int32)[None, :]
    src = jnp.clip(starts[texp][:, None] + within, 0, npairs - 1)
    dst = jnp.where(jnp.arange(tm, dtype=jnp.int32)[None, :] < cnt[:, None], sorted_pair[src], 0)
    tok = dst % n
    nvalid = (ends[-1] // tm).astype(jnp.int32).reshape(1)
    return (nvalid, texp.astype(jnp.int32), cnt.astype(jnp.int32), tok.reshape(nt, 1, tm), dst.reshape(nt, 1, tm),
            npairs)


def _ple_final_kernel(x1_ref, y0_ref, y1_ref, w_ref, p_ref, gp_ref, wpg_ref, wp_ref, gf_ref, o_ref):
    w = w_ref[...]
    x2 = x1_ref[...] + w[:, 0:1] * y0_ref[...] + w[:, 1:2] * y1_ref[...]
    hn = _rms(x2, gp_ref[...]).astype(BF16)
    gate = _sigmoid(jnp.dot(hn, wpg_ref[...], preferred_element_type=F32))
    pe = jnp.dot(p_ref[...].astype(BF16), wp_ref[...], preferred_element_type=F32)
    x3 = x2 + gate * pe
    o_ref[...] = _rms(x3, gf_ref[...])


def _ple_final(x1, ypairs, wcols, p2d, gp, wpg, wp, gf, tm=256):
    n, d = x1.shape
    nb = n // tm
    return pl.pallas_call(
        _ple_final_kernel,
        out_shape=jax.ShapeDtypeStruct((n, d), F32),
        grid=(nb,),
        in_specs=[
            pl.BlockSpec((tm, d), lambda i: (i, 0)),
            pl.BlockSpec((tm, d), lambda i: (i, 0)),
            pl.BlockSpec((tm, d), lambda i: (nb + i, 0)),
            pl.BlockSpec((tm, wcols.shape[1]), lambda i: (i, 0)),
            pl.BlockSpec((tm, p2d.shape[1]), lambda i: (i, 0)),
            pl.BlockSpec((1, d), lambda i: (0, 0)),
            pl.BlockSpec((d, d), lambda i: (0, 0)),
            pl.BlockSpec(wp.shape, lambda i: (0, 0)),
            pl.BlockSpec((1, d), lambda i: (0, 0)),
        ],
        out_specs=pl.BlockSpec((tm, d), lambda i: (i, 0)),
        compiler_params=_cparams(("parallel",)),
        name="ple_final",
    )(x1, ypairs, ypairs, wcols, p2d, gp, wpg, wp, gf)


def _block_diag(w, per):
    nb, bw, _ = w.shape
    w = w.reshape(nb // per, per, bw, bw)
    eye = jnp.eye(per, dtype=w.dtype)
    return jnp.einsum("cpij,pq->cpiqj", w, eye).reshape(nb // per, per * bw, per * bw)


def _regroup_w_in(w_in):
    d = w_in.shape[0]
    g, hd = NSA_KV_GROUPS, HEAD_DIM
    c = np.cumsum((0, NSA_Q_WIDTH) + (NSA_KV_WIDTH,) * 6 + (3 * NSA_HEADS, LRU_WIDTH, LRU_WIDTH, d, d))
    seg = lambda k: w_in[:, int(c[k]):int(c[k + 1])]
    q, k_c, v_c, k_s, v_s, k_w, v_w, gates, lru_x, lru_y, mg_a, mg_b = (seg(k) for k in range(12))
    pair = lambda a, b: jnp.stack([a.reshape(d, g, hd), b.reshape(d, g, hd)], axis=2).reshape(d, 2 * g * hd)
    pad = jnp.zeros((d, Z_WIDTH - int(c[-1])), w_in.dtype)
    cols = [mg_a, mg_b, q, pair(k_s, k_w), pair(v_s, v_w), k_c, v_c, lru_x, lru_y, gates, pad]
    return jnp.concatenate(cols, axis=1).astype(BF16)


def _layer(x2d, batch, seq, ln_mix, w_in, cmp_k_pos, cmp_k_w1, cmp_k_w2, cmp_v_pos, cmp_v_w1, cmp_v_w2, conv_w,
           conv_b, lru_wa, lru_ba, lru_wx, lru_bx, lru_lambda, w_nsa_up, w_lru_up, w_out, ln_ffn, w_grp, b_grp, w_exp,
           b_exp, w_gate, w_up, w_down):
    n, d = x2d.shape
    g, hd = NSA_KV_GROUPS, HEAD_DIM

    z = _in_proj(x2d, ln_mix.reshape(1, d), _regroup_w_in(w_in))

    def windows(col):
        a = z[:, col:col + NSA_KV_WIDTH].reshape(batch, seq // CMP_STRIDE, CMP_STRIDE, g, hd)
        return a.transpose(0, 3, 1, 2, 4).reshape(batch, g, seq // CMP_STRIDE, CMP_STRIDE * hd)

    pos8 = lambda pos: jnp.broadcast_to(pos.reshape(1, -1), (SUBLANES, pos.size)).astype(BF16)
    kc, vct = _compress(windows(Z_KC), windows(Z_VC), pos8(cmp_k_pos), cmp_k_w1.astype(BF16), cmp_k_w2.astype(BF16),
                        pos8(cmp_v_pos), cmp_v_w1.astype(BF16), cmp_v_w2.T.astype(BF16))
    o = _nsa_attention(z, kc, vct, batch, seq)

    per = LRU_CB // LRU_BW
    lru = _rglru(z, batch, seq, conv_w, conv_b.reshape(1, -1), _block_diag(lru_wa, per).astype(BF16),
                 lru_ba.reshape(1, -1), _block_diag(lru_wx, per).astype(BF16), lru_bx.reshape(1, -1),
                 lru_lambda.reshape(1, -1))

    merged = _merge(o, lru, z, w_nsa_up.astype(BF16), w_lru_up.astype(BF16))

    nr = SUBLANES + N_EXPERTS
    wr = jnp.zeros((nr, d), F32).at[0:N_GROUPS].set(w_grp.T).at[SUBLANES:].set(w_exp.T)
    br = jnp.zeros((nr, 1), F32).at[0:N_GROUPS, 0].set(b_grp).at[SUBLANES:, 0].set(b_exp)
    x1, xn, eid, ew = _out_route(merged, x2d, w_out.astype(BF16), ln_ffn.reshape(1, d), wr, br)

    nvalid, texp, cnt, tok, dst, n_rows_out = _moe_plan(eid[0:EXPERT_TOPK], n, MOE_TM)
    ypairs = _moe(nvalid, texp, cnt, tok, dst, xn, w_gate, w_up, w_down, n_rows_out)

    return x1, ypairs, ew.T


def kernel(x, p, ln_mix, w_in, cmp_k_pos, cmp_k_w1, cmp_k_w2, cmp_v_pos, cmp_v_w1, cmp_v_w2, conv_w, conv_b, lru_wa, lru_ba, lru_wx, lru_bx, lru_lambda, w_nsa_up, w_lru_up, w_out, ln_ffn, w_grp, b_grp, w_exp, b_exp, w_gate, w_up, w_down, ln_ple, w_ple, w_ple_gate, ln_final):
    batch, seq, d = x.shape
    assert p.shape[0] == 1, "the final norm is fused into the (single) layer's last kernel"
    n = batch * seq
    x1, ypairs, wcols = _layer(
        x.reshape(n, d), batch, seq, ln_mix[0], w_in[0], cmp_k_pos[0], cmp_k_w1[0], cmp_k_w2[0],
        cmp_v_pos[0], cmp_v_w1[0], cmp_v_w2[0], conv_w[0], conv_b[0], lru_wa[0], lru_ba[0], lru_wx[0], lru_bx[0],
        lru_lambda[0], w_nsa_up[0], w_lru_up[0], w_out[0], ln_ffn[0], w_grp[0], b_grp[0], w_exp[0], b_exp[0],
        w_gate[0], w_up[0], w_down[0])
    out = _ple_final(x1, ypairs, wcols, p[0].reshape(n, -1), ln_ple[0].reshape(1, d),
                     w_ple_gate[0].astype(BF16), w_ple[0].astype(BF16), ln_final.reshape(1, d))
    return out.reshape(batch, seq, d)
```

```python
import numpy as np
import jax
import jax.numpy as jnp
from jax import lax
from jax.experimental import pallas as pl
from jax.experimental.pallas import tpu as pltpu

F32 = jnp.float32
BF16 = jnp.bfloat16

NSA_HEADS = 16
NSA_KV_GROUPS = 4
NSA_HPG = NSA_HEADS // NSA_KV_GROUPS
HEAD_DIM = 64
NSA_Q_WIDTH = NSA_HEADS * HEAD_DIM
NSA_KV_WIDTH = NSA_KV_GROUPS * HEAD_DIM
CMP_BLOCK = 32
CMP_STRIDE = 16
CMP_HIDDEN = 2 * HEAD_DIM
SLC_BLOCK = 64
SLC_TOPN = 16
WINDOW = 512
ATTN_SCALE = HEAD_DIM ** -0.5
NEG_INF = -1e30
FORCE_SCORE = 1e4
LRU_WIDTH = 1024
LRU_BLOCKS = 16
LRU_BW = LRU_WIDTH // LRU_BLOCKS
CONV_WIDTH = 4
LRU_C = 8.0
N_GROUPS = 4
EXPERTS_PER_GROUP = 8
N_EXPERTS = N_GROUPS * EXPERTS_PER_GROUP
EXPERT_TOPK = 2
D_EXPERT = 512
EPS = 1e-6

LANES = 128
SUBLANES = 8
VMEM_LIMIT_BYTES = 56 * 1024 * 1024

Z_MGA = 0
Z_MGB = 2048
Z_Q = 4096
Z_KK = 5120
Z_VV = 5632
Z_KC = 6144
Z_VC = 6400
Z_LX = 6656
Z_LY = 7680
Z_GATE = 8704
Z_WIDTH = 9216
LRU_CB = 256
MOE_TM = 256
NSA_TQ = 256
NSA_CK = 512


def _cparams(sem, vmem=VMEM_LIMIT_BYTES):
    return pltpu.CompilerParams(dimension_semantics=sem, vmem_limit_bytes=vmem)


def _rms(x, g):
    return x * lax.rsqrt(jnp.mean(x * x, axis=-1, keepdims=True) + EPS) * g


def _gelu_tanh(x):
    return 0.5 * x * (1.0 + jnp.tanh(np.sqrt(2.0 / np.pi) * (x + 0.044715 * (x * x * x))))


def _sigmoid(x):
    return 1.0 / (1.0 + jnp.exp(-x))


def _row_pitch(d):
    k = d // LANES
    return k + 1 - (k % 2)


def _store_rows(ref, row0, val):
    rows, d = val.shape
    pitch = _row_pitch(d)
    for j in range(d // LANES):
        ref[pl.ds(row0 * pitch + j, rows, stride=pitch), :] = val[:, j * LANES:(j + 1) * LANES]
    for j in range(d // LANES, pitch):
        ref[pl.ds(row0 * pitch + j, rows, stride=pitch), :] = jnp.zeros((rows, LANES), val.dtype)


def _load_rows(ref, row0, rows, d):
    pitch = _row_pitch(d)
    return jnp.concatenate([ref[pl.ds(row0 * pitch + j, rows, stride=pitch), :] for j in range(d // LANES)], axis=1)


def _in_proj_kernel(x_ref, g_ref, w_ref, o_ref, h_ref):
    @pl.when(pl.program_id(1) == 0)
    def _():
        h_ref[...] = _rms(x_ref[...], g_ref[...]).astype(BF16)

    o_ref[...] = jnp.dot(h_ref[...], w_ref[...], preferred_element_type=F32).astype(o_ref.dtype)


def _in_proj(x2d, g, w, tm=512, tn=1024):
    n, d = x2d.shape
    nw = w.shape[1]
    return pl.pallas_call(
        _in_proj_kernel,
        out_shape=jax.ShapeDtypeStruct((n, nw), BF16),
        grid=(n // tm, nw // tn),
        in_specs=[
            pl.BlockSpec((tm, d), lambda i, j: (i, 0)),
            pl.BlockSpec((1, d), lambda i, j: (0, 0)),
            pl.BlockSpec((d, tn), lambda i, j: (0, j)),
        ],
        out_specs=pl.BlockSpec((tm, tn), lambda i, j: (i, j)),
        scratch_shapes=[pltpu.VMEM((tm, d), BF16)],
        compiler_params=_cparams(("parallel", "arbitrary")),
        name="in_proj",
    )(x2d, g, w)


def _compress_kernel(kx_ref, vx_ref, kpos_ref, kw1_ref, kw2_ref, vpos_ref, vw1_ref, vw2t_ref, kc_ref, vct_ref):
    half = kx_ref.shape[-1]

    def hidden(x_ref, pos_ref, w1_ref):
        x = x_ref[0, 0]
        w1 = w1_ref[...]
        ha = jnp.dot(x, w1[:half], preferred_element_type=F32)
        hb = jnp.dot(x, w1[half:], preferred_element_type=F32)
        hb = pltpu.roll(hb, hb.shape[0] - 1, axis=0)
        pc = jnp.dot(pos_ref[...], w1, preferred_element_type=F32)[0:1]
        return _gelu_tanh(ha + hb + pc).astype(BF16)

    hk = hidden(kx_ref, kpos_ref, kw1_ref)
    kc_ref[0, 0] = jnp.dot(hk, kw2_ref[...], preferred_element_type=F32).astype(kc_ref.dtype)
    hv = hidden(vx_ref, vpos_ref, vw1_ref)
    vct_ref[0, 0] = lax.dot_general(vw2t_ref[...], hv, (((1,), (1,)), ((), ())),
                                    preferred_element_type=F32).astype(vct_ref.dtype)


def _compress(kx, vx, kpos, kw1, kw2, vpos, vw1, vw2t):
    b, g, ncp, half = kx.shape
    full = lambda a: pl.BlockSpec(a.shape, lambda i, j: (0,) * a.ndim)
    return pl.pallas_call(
        _compress_kernel,
        out_shape=(jax.ShapeDtypeStruct((b, g, ncp, HEAD_DIM), BF16),
                   jax.ShapeDtypeStruct((b, g, HEAD_DIM, ncp), BF16)),
        grid=(b, g),
        in_specs=[
            pl.BlockSpec((1, 1, ncp, half), lambda i, j: (i, j, 0, 0)),
            pl.BlockSpec((1, 1, ncp, half), lambda i, j: (i, j, 0, 0)),
            full(kpos), full(kw1), full(kw2), full(vpos), full(vw1), full(vw2t),
        ],
        out_specs=(pl.BlockSpec((1, 1, ncp, HEAD_DIM), lambda i, j: (i, j, 0, 0)),
                   pl.BlockSpec((1, 1, HEAD_DIM, ncp), lambda i, j: (i, j, 0, 0))),
        compiler_params=_cparams(("parallel", "parallel")),
        name="compress",
    )(kx, vx, kpos, kw1, kw2, vpos, vw1, vw2t)


def _nsa_kernel(zq_ref, zkk_ref, zvv_ref, zg_ref, kc_ref, vct_ref, o_ref, kcomb_ref, score_ref, gate_ref):
    grp = pl.program_id(1)
    qi = pl.program_id(2)
    tq, hd, hpg = NSA_TQ, HEAD_DIM, NSA_HPG
    nq = tq * hpg
    seq = kcomb_ref.shape[0]
    ncp = kc_ref.shape[2]
    nsb = score_ref.shape[0]
    blk_shift = SLC_BLOCK.bit_length() - 1

    @pl.when(qi == 0)
    def _():
        kcomb_ref[:, 0:2 * hd] = zkk_ref[...]
        blk_of_row = lax.broadcasted_iota(jnp.int32, (seq, 2 * hd), 0) >> blk_shift
        lane = lax.broadcasted_iota(jnp.int32, (seq, 2 * hd), 1)
        kcomb_ref[:, 2 * hd:] = jnp.where(blk_of_row == lane, 1.0, 0.0).astype(BF16)

    qt = zq_ref[...].T
    q = jnp.concatenate([qt[h * hd:(h + 1) * hd, :] for h in range(hpg)], axis=1)
    q = (q.astype(F32) * ATTN_SCALE).astype(BF16)
    zero = jnp.zeros((hd, nq), BF16)
    t_lane = qi * tq + (lax.broadcasted_iota(jnp.int32, (1, nq), 1) & (tq - 1))
    tn_dims = (((0,), (0,)), ((), ()))

    sc = jnp.dot(kc_ref[0, 0], q, preferred_element_type=F32)
    cmp_end = lax.broadcasted_iota(jnp.int32, (ncp, 1), 0) * CMP_STRIDE + (CMP_BLOCK - 1)
    cmask = cmp_end <= t_lane
    sc = jnp.where(cmask, sc, NEG_INF)
    pc = jnp.where(cmask, jnp.exp(sc - jnp.max(sc, axis=0, keepdims=True)), 0.0)
    lc = jnp.sum(pc, axis=0, keepdims=True)
    pc = pc * (1.0 / jnp.where(lc > 0.0, lc, 1.0))
    o_cmp = jnp.dot(vct_ref[0, 0], pc.astype(BF16), preferred_element_type=F32)

    nwc = WINDOW // tq + 1
    t_lo = t_lane - WINDOW
    ks_w, vs_w = [], []
    for i in range(nwc):
        rows = pl.ds(pl.multiple_of(jnp.maximum(qi - (nwc - 1) + i, 0) * tq, tq), tq)
        ks_w.append(kcomb_ref[rows, 0:2 * hd])
        vs_w.append(zvv_ref[rows, :])
    qw = jnp.concatenate([zero, q], axis=0)
    sw = jnp.dot(jnp.concatenate(ks_w, axis=0), qw, preferred_element_type=F32)
    sw_parts = []
    for i in range(nwc):
        spos = (qi - (nwc - 1) + i) * tq + lax.broadcasted_iota(jnp.int32, (tq, 1), 0)
        if i < nwc - 1:
            ok = jnp.where(spos >= 0, spos, -(1 << 24)) > t_lo
        else:
            ok = spos <= t_lane
        sw_parts.append(jnp.where(ok, sw[i * tq:(i + 1) * tq], NEG_INF))
    sw = jnp.concatenate(sw_parts, axis=0)
    pw = jnp.exp(sw - jnp.max(sw, axis=0, keepdims=True))
    lw = jnp.sum(pw, axis=0, keepdims=True)
    o_win = lax.dot_general(jnp.concatenate(vs_w, axis=0), pw.astype(BF16), tn_dims,
                            preferred_element_type=F32)[hd:2 * hd] * (1.0 / lw)

    psum = pc[:, 0:tq]
    for h in range(1, hpg):
        psum = psum + pc[:, h * tq:(h + 1) * tq]
    jrow = lax.broadcasted_iota(jnp.int32, (nsb, ncp), 0)
    crel = lax.broadcasted_iota(jnp.int32, (nsb, ncp), 1) - jrow * (SLC_BLOCK // CMP_STRIDE)
    ovt = jnp.where((crel >= 0) & (crel <= 2), 1.0, jnp.where((crel == -1) | (crel == 3), 0.5, 0.0)).astype(F32)
    imp = jnp.dot(ovt, psum, preferred_element_type=F32, precision=lax.Precision.HIGHEST)
    jblk = lax.broadcasted_iota(jnp.int32, (nsb, tq), 0)
    blk = (qi * tq + lax.broadcasted_iota(jnp.int32, (1, tq), 1)) >> blk_shift
    forced = (jblk == 0) | (jblk == blk) | (jblk == blk - 1)
    score = jnp.where(forced, FORCE_SCORE, jnp.where(jblk <= blk, imp, -1.0))
    score_ref[...] = score

    def rank_one(k, cnt):
        row = score_ref[pl.ds(k, 1), :]
        ge = jnp.where(row >= score, 1.0, 0.0)
        gt = jnp.where(row > score, 1.0, 0.0)
        return cnt + jnp.where(jblk > k, ge, gt)

    bpt = tq // SLC_BLOCK

    def rank_group(i, cnt):
        for u in range(bpt):
            cnt = rank_one(bpt * i + u, cnt)
        return cnt

    n_vis = jnp.minimum((qi + 1) * bpt, nsb)
    rank = lax.fori_loop(0, n_vis // bpt, rank_group, jnp.zeros((nsb, tq), F32))
    selb = jnp.where(rank < float(min(SLC_TOPN, nsb)), 0.0, NEG_INF).astype(BF16)

    ck = NSA_CK
    pad = jnp.zeros((2 * hd - nsb, nq), BF16)
    qa = jnp.concatenate([q, zero, jnp.concatenate([selb] * hpg, axis=1), pad], axis=0)

    def slc_scores(c):
        return jnp.dot(kcomb_ref[pl.ds(pl.multiple_of(c * ck, ck), ck), :], qa, preferred_element_type=F32)

    def slc_update(c, s, m, l, acc):
        m_new = jnp.maximum(m, jnp.max(s, axis=0, keepdims=True))
        alpha = jnp.exp(m - m_new)
        p = jnp.exp(s - m_new)
        l = alpha * l + jnp.sum(p, axis=0, keepdims=True)
        v = zvv_ref[pl.ds(pl.multiple_of(c * ck, ck), ck), :]
        pv = lax.dot_general(v, p.astype(BF16), tn_dims, preferred_element_type=F32)[0:hd]
        return m_new, l, acc * alpha + pv

    def slc_body(c, carry):
        m, l, acc, s = carry
        s_next = slc_scores(c + 1)
        return slc_update(c, s, m, l, acc) + (s_next,)

    n_full = (qi * tq) // ck
    init = (jnp.full((1, nq), NEG_INF, F32), jnp.zeros((1, nq), F32), jnp.zeros((hd, nq), F32), slc_scores(0))
    m_s, l_s, acc_s, s_d = lax.fori_loop(0, n_full, slc_body, init)
    kpos_d = n_full * ck + lax.broadcasted_iota(jnp.int32, (ck, 1), 0)
    s_d = jnp.where(kpos_d <= t_lane, s_d, NEG_INF)
    _, l_s, acc_s = slc_update(n_full, s_d, m_s, l_s, acc_s)
    o_slc = acc_s * (1.0 / l_s)

    gate_ref[...] = _sigmoid(zg_ref[...].astype(F32)).T

    def branch_gate(br):
        rows = [gate_ref[pl.ds(grp * (hpg * 3) + h * 3 + br, 1), :] for h in range(hpg)]
        return jnp.concatenate(rows, axis=1)

    o = branch_gate(0) * o_cmp + branch_gate(1) * o_slc + branch_gate(2) * o_win
    o_heads = jnp.concatenate([o[:, h * tq:(h + 1) * tq] for h in range(hpg)], axis=0)
    o_ref[...] = o_heads.T.astype(o_ref.dtype)


def _nsa_attention(z, kc, vct, batch, seq):
    n = z.shape[0]
    g, hd, tq = NSA_KV_GROUPS, HEAD_DIM, NSA_TQ
    nqt = seq // tq
    ncp = kc.shape[2]
    nsb = seq // SLC_BLOCK
    gw = NSA_HPG * hd
    assert seq % NSA_CK == 0 and NSA_CK % tq == 0 and WINDOW % tq == 0 and nsb <= 2 * hd
    tile = lambda i, j, k: i * nqt + k
    return pl.pallas_call(
        _nsa_kernel,
        out_shape=jax.ShapeDtypeStruct((n, NSA_Q_WIDTH), BF16),
        grid=(batch, g, nqt),
        in_specs=[
            pl.BlockSpec((tq, gw), lambda i, j, k: (tile(i, j, k), Z_Q // gw + j)),
            pl.BlockSpec((seq, 2 * hd), lambda i, j, k: (i, Z_KK // (2 * hd) + j)),
            pl.BlockSpec((seq, 2 * hd), lambda i, j, k: (i, Z_VV // (2 * hd) + j)),
            pl.BlockSpec((tq, LANES), lambda i, j, k: (tile(i, j, k), Z_GATE // LANES)),
            pl.BlockSpec((1, 1, ncp, hd), lambda i, j, k: (i, j, 0, 0)),
            pl.BlockSpec((1, 1, hd, ncp), lambda i, j, k: (i, j, 0, 0)),
        ],
        out_specs=pl.BlockSpec((tq, gw), lambda i, j, k: (tile(i, j, k), j)),
        scratch_shapes=[pltpu.VMEM((seq, 4 * hd), BF16), pltpu.VMEM((nsb, tq), F32), pltpu.VMEM((LANES, tq), F32)],
        compiler_params=_cparams(("parallel", "parallel", "arbitrary")),
        name="nsa_attn",
    )(z, z, z, z, kc, vct)


def _rglru_kernel(x_ref, y_ref, cw_ref, cb_ref, wa_ref, ba_ref, wx_ref, bx_ref, lam_ref, o_ref,
                  tail_ref, h_ref, a_ref, u_ref):
    tc = pl.program_id(2)
    tt, cb = x_ref.shape

    @pl.when(tc == 0)
    def _():
        tail_ref[...] = jnp.zeros_like(tail_ref)
        h_ref[...] = jnp.zeros_like(h_ref)

    x = x_ref[...].astype(F32)
    xe = jnp.concatenate([tail_ref[...], x], axis=0)
    tail_ref[...] = x[tt - SUBLANES:, :]
    cw = cw_ref[...]
    xc = cb_ref[...]
    for k in range(CONV_WIDTH):
        off = SUBLANES - (CONV_WIDTH - 1) + k
        xc = xc + cw[k:k + 1, :] * xe[off:off + tt, :]
    xcb = xc.astype(BF16)
    r = _sigmoid(jnp.dot(xcb, wa_ref[0], preferred_element_type=F32) + ba_ref[...])
    ig = _sigmoid(jnp.dot(xcb, wx_ref[0], preferred_element_type=F32) + bx_ref[...])
    nl = -lam_ref[...]
    softplus = jnp.maximum(nl, 0.0) + jnp.log1p(jnp.exp(-jnp.abs(nl)))
    log_a = (-LRU_C) * softplus * r
    a_ref[...] = jnp.exp(log_a)
    th = jnp.tanh(log_a)
    u_ref[...] = jnp.sqrt(-2.0 * th / (1.0 - th)) * (ig * xc)

    row = lax.broadcasted_iota(jnp.int32, (SUBLANES, cb), 0)

    def step(i, h):
        sl = pl.ds(pl.multiple_of(i * SUBLANES, SUBLANES), SUBLANES)
        a = a_ref[sl, :]
        u = u_ref[sl, :]
        for s in (1, 2, 4):
            a_s = jnp.where(row >= s, pltpu.roll(a, s, axis=0), 1.0)
            u_s = jnp.where(row >= s, pltpu.roll(u, s, axis=0), 0.0)
            u = a * u_s + u
            a = a * a_s
        hrows = a * h + u
        u_ref[sl, :] = hrows
        return hrows[SUBLANES - 1:SUBLANES, :]

    h_ref[...] = lax.fori_loop(0, tt // SUBLANES, step, h_ref[...], unroll=2)
    o_ref[...] = (u_ref[...] * _gelu_tanh(y_ref[...].astype(F32))).astype(o_ref.dtype)


def _rglru(z, batch, seq, cw, cbias, wa_bd, ba, wx_bd, bx, lam, tt=512):
    n = z.shape[0]
    ncb = LRU_WIDTH // LRU_CB
    nt = seq // tt
    row = lambda i, j, k: i * nt + k
    vec = lambda r: pl.BlockSpec((r, LRU_CB), lambda i, j, k: (0, j))
    return pl.pallas_call(
        _rglru_kernel,
        out_shape=jax.ShapeDtypeStruct((n, LRU_WIDTH), BF16),
        grid=(batch, ncb, nt),
        in_specs=[
            pl.BlockSpec((tt, LRU_CB), lambda i, j, k: (row(i, j, k), Z_LX // LRU_CB + j)),
            pl.BlockSpec((tt, LRU_CB), lambda i, j, k: (row(i, j, k), Z_LY // LRU_CB + j)),
            vec(CONV_WIDTH), vec(1),
            pl.BlockSpec((1, LRU_CB, LRU_CB), lambda i, j, k: (j, 0, 0)), vec(1),
            pl.BlockSpec((1, LRU_CB, LRU_CB), lambda i, j, k: (j, 0, 0)), vec(1),
            vec(1),
        ],
        out_specs=pl.BlockSpec((tt, LRU_CB), lambda i, j, k: (row(i, j, k), j)),
        scratch_shapes=[pltpu.VMEM((SUBLANES, LRU_CB), F32), pltpu.VMEM((1, LRU_CB), F32),
                        pltpu.VMEM((tt, LRU_CB), F32), pltpu.VMEM((tt, LRU_CB), F32)],
        compiler_params=_cparams(("parallel", "parallel", "arbitrary")),
        name="rglru",
    )(z, z, cw, cbias, wa_bd, ba, wx_bd, bx, lam)


def _merge_kernel(o_ref, l_ref, mga_ref, mgb_ref, wn_ref, wl_ref, m_ref):
    ya = jnp.dot(o_ref[...], wn_ref[...], preferred_element_type=F32)
    yb = jnp.dot(l_ref[...], wl_ref[...], preferred_element_type=F32)
    m = _sigmoid(mga_ref[...].astype(F32)) * ya + _sigmoid(mgb_ref[...].astype(F32)) * yb
    m_ref[...] = m.astype(m_ref.dtype)


def _merge(o, lru, z, wn, wl, tm=512):
    n, d = o.shape[0], wn.shape[1]
    return pl.pallas_call(
        _merge_kernel,
        out_shape=jax.ShapeDtypeStruct((n, d), BF16),
        grid=(n // tm,),
        in_specs=[
            pl.BlockSpec((tm, o.shape[1]), lambda i: (i, 0)),
            pl.BlockSpec((tm, lru.shape[1]), lambda i: (i, 0)),
            pl.BlockSpec((tm, d), lambda i: (i, Z_MGA // d)),
            pl.BlockSpec((tm, d), lambda i: (i, Z_MGB // d)),
            pl.BlockSpec(wn.shape, lambda i: (0, 0)),
            pl.BlockSpec(wl.shape, lambda i: (0, 0)),
        ],
        out_specs=pl.BlockSpec((tm, d), lambda i: (i, 0)),
        compiler_params=_cparams(("parallel",)),
        name="merge",
    )(o, lru, z, z, wn, wl)


def _out_route_kernel(m_ref, x_ref, wo_ref, g_ref, wr_ref, br_ref, x1_ref, xn_ref, eid_ref, ew_ref):
    x1 = x_ref[...] + jnp.dot(m_ref[...], wo_ref[...], preferred_element_type=F32)
    x1_ref[...] = x1
    xn = _rms(x1, g_ref[...])
    _store_rows(xn_ref, 0, xn)
    lg = lax.dot_general(wr_ref[...], xn, (((1,), (1,)), ((), ())), preferred_element_type=F32,
                         precision=lax.Precision.HIGHEST) + br_ref[...]
    tm = lg.shape[1]
    sub = lax.broadcasted_iota(jnp.int32, (SUBLANES, tm), 0)

    def first_argmax(v, vmax):
        return jnp.min(jnp.where(v == vmax, sub, SUBLANES), axis=0, keepdims=True)

    gl = jnp.where(sub < N_GROUPS, lg[0:SUBLANES], -jnp.inf)
    gmax = jnp.max(gl, axis=0, keepdims=True)
    ge = jnp.exp(gl - gmax)
    gprob = ge / jnp.sum(ge, axis=0, keepdims=True)
    g_val = jnp.max(gprob, axis=0, keepdims=True)
    g_idx = first_argmax(gprob, g_val)
    e_in = jnp.zeros((EXPERTS_PER_GROUP, tm), F32)
    for gi in range(N_GROUPS):
        lo = SUBLANES + gi * EXPERTS_PER_GROUP
        e_in = jnp.where(g_idx == gi, lg[lo:lo + EXPERTS_PER_GROUP], e_in)
    ee = jnp.exp(e_in - jnp.max(e_in, axis=0, keepdims=True))
    eprob = ee / jnp.sum(ee, axis=0, keepdims=True)
    v1 = jnp.max(eprob, axis=0, keepdims=True)
    i1 = first_argmax(eprob, v1)
    rest = jnp.where(sub == i1, -1.0, eprob)
    v2 = jnp.max(rest, axis=0, keepdims=True)
    i2 = first_argmax(rest, v2)
    den = v1 + v2
    eid = jnp.where(sub == 0, g_idx * EXPERTS_PER_GROUP + i1, g_idx * EXPERTS_PER_GROUP + i2)
    eid_ref[...] = eid
    ew_ref[...] = jnp.where(sub == 0, g_val * v1 / den, g_val * v2 / den)


def _out_route(m, x2d, wo, g, wr, br, tm=512):
    n, d = x2d.shape
    nr = wr.shape[0]
    once = pl.Buffered(1)
    return pl.pallas_call(
        _out_route_kernel,
        out_shape=(jax.ShapeDtypeStruct((n, d), F32), jax.ShapeDtypeStruct((n * _row_pitch(d), LANES), F32),
                   jax.ShapeDtypeStruct((SUBLANES, n), jnp.int32), jax.ShapeDtypeStruct((SUBLANES, n), F32)),
        grid=(n // tm,),
        in_specs=[
            pl.BlockSpec((tm, d), lambda i: (i, 0)),
            pl.BlockSpec((tm, d), lambda i: (i, 0)),
            pl.BlockSpec((d, d), lambda i: (0, 0), pipeline_mode=once),
            pl.BlockSpec((1, d), lambda i: (0, 0)),
            pl.BlockSpec((nr, d), lambda i: (0, 0), pipeline_mode=once),
            pl.BlockSpec((nr, 1), lambda i: (0, 0)),
        ],
        out_specs=(pl.BlockSpec((tm, d), lambda i: (i, 0)), pl.BlockSpec((tm * _row_pitch(d), LANES), lambda i: (i, 0)),
                   pl.BlockSpec((SUBLANES, tm), lambda i: (0, i)), pl.BlockSpec((SUBLANES, tm), lambda i: (0, i))),
        compiler_params=_cparams(("parallel",)),
        name="out_route",
    )(m, x2d, wo, g, wr, br)


def _for_rows(cnt, fn, unroll=8):
    sh = unroll.bit_length() - 1

    def group(gidx, c):
        for u in range(unroll):
            fn(gidx * unroll + u)
        return c

    def single(r, c):
        fn(r)
        return c

    lax.fori_loop(0, cnt >> sh, group, 0)
    lax.fori_loop((cnt >> sh) << sh, cnt, single, 0)


def _moe_kernel(nvalid_ref, texp_ref, cnt_ref, tok_ref, tokn_ref, dst_ref, xn_hbm, wg_ref, wu_ref, wd_ref, y_hbm,
                xbuf, ybuf, gsem, ssem):
    i = pl.program_id(0)
    nv = nvalid_ref[0]
    slot = i % 2
    d = wg_ref.shape[1]
    k = _row_pitch(d)
    tm = xbuf.shape[0] // (2 * k)

    def line(ref, row):
        return ref.at[pl.ds(row * k, k), :]

    def gather_row(idx_ref, s):
        def fn(r):
            pltpu.make_async_copy(line(xn_hbm, idx_ref[0, 0, r]), line(xbuf, s * tm + r), gsem.at[s]).start()
        return fn

    def gather_wait_row(s):
        def fn(r):
            pltpu.make_async_copy(line(xn_hbm, 0), line(xbuf, s * tm + r), gsem.at[s]).wait()
        return fn

    def scatter_row(s):
        def fn(r):
            pltpu.make_async_copy(line(ybuf, s * tm + r), line(y_hbm, dst_ref[0, 0, r]), ssem.at[s]).start()
        return fn

    def scatter_wait_row(s):
        def fn(r):
            pltpu.make_async_copy(line(ybuf, s * tm + r), line(y_hbm, 0), ssem.at[s]).wait()
        return fn

    @pl.when(i == 0)
    def _():
        xbuf[...] = jnp.zeros_like(xbuf)
        _for_rows(cnt_ref[0], gather_row(tok_ref, 0))

    @pl.when(i + 1 < nv)
    def _():
        _for_rows(cnt_ref[i + 1], gather_row(tokn_ref, 1 - slot))

    @pl.when(i < nv)
    def _():
        _for_rows(cnt_ref[i], gather_wait_row(slot))
        x = _load_rows(xbuf, slot * tm, tm, d).astype(BF16)
        gp = jnp.dot(x, wg_ref[0], preferred_element_type=F32)
        up = jnp.dot(x, wu_ref[0], preferred_element_type=F32)
        hid = (gp * _sigmoid(gp) * up).astype(BF16)
        y = jnp.dot(hid, wd_ref[0], preferred_element_type=F32)

        @pl.when(i >= 2)
        def _():
            _for_rows(cnt_ref[i - 2], scatter_wait_row(slot))

        _store_rows(ybuf, slot * tm, y)
        _for_rows(cnt_ref[i], scatter_row(slot))

    @pl.when(i == nv - 1)
    def _():
        _for_rows(cnt_ref[i], scatter_wait_row(slot))

        @pl.when(nv >= 2)
        def _():
            _for_rows(cnt_ref[i - 1], scatter_wait_row(1 - slot))


def _moe(nvalid, texp, cnt, tok, dst, xn, wg, wu, wd, n_rows_out):
    nt, _, tm = tok.shape
    d, de = wg.shape[1], wg.shape[2]
    k = _row_pitch(d)
    grid_spec = pltpu.PrefetchScalarGridSpec(
        num_scalar_prefetch=3,
        grid=(nt,),
        in_specs=[
            pl.BlockSpec((1, 1, tm), lambda i, nv, te, ct: (i, 0, 0), memory_space=pltpu.SMEM),
            pl.BlockSpec((1, 1, tm), lambda i, nv, te, ct: (jnp.minimum(i + 1, nt - 1), 0, 0),
                         memory_space=pltpu.SMEM),
            pl.BlockSpec((1, 1, tm), lambda i, nv, te, ct: (i, 0, 0), memory_space=pltpu.SMEM),
            pl.BlockSpec(memory_space=pl.ANY),
            pl.BlockSpec((1, d, de), lambda i, nv, te, ct: (te[i], 0, 0)),
            pl.BlockSpec((1, d, de), lambda i, nv, te, ct: (te[i], 0, 0)),
            pl.BlockSpec((1, de, d), lambda i, nv, te, ct: (te[i], 0, 0)),
        ],
        out_specs=pl.BlockSpec(memory_space=pl.ANY),
        scratch_shapes=[pltpu.VMEM((2 * tm * k, LANES), F32), pltpu.VMEM((2 * tm * k, LANES), F32),
                        pltpu.SemaphoreType.DMA((2,)), pltpu.SemaphoreType.DMA((2,))],
    )
    return pl.pallas_call(
        _moe_kernel,
        out_shape=jax.ShapeDtypeStruct((n_rows_out * k, LANES), F32),
        grid_spec=grid_spec,
        compiler_params=_cparams(("arbitrary",)),
        name="moe",
    )(nvalid, texp, cnt, tok, tok, dst, xn, wg, wu, wd)


def _moe_plan(eid, n, tm):
    e_flat = eid.reshape(-1)
    npairs = e_flat.shape[0]
    experts = jnp.arange(N_EXPERTS, dtype=jnp.int32)
    counts = jnp.sum((e_flat[:, None] == experts[None, :]).astype(jnp.int32), axis=0)
    padded = ((counts + tm - 1) // tm) * tm
    ends = jnp.cumsum(padded)
    offs = ends - padded
    n_rows = npairs + N_EXPERTS * tm
    nt = n_rows // tm
    fill = jnp.arange(n_rows - npairs, dtype=jnp.int32)
    fill_expert = jnp.sum((fill[:, None] >= jnp.cumsum(padded - counts)[None, :]).astype(jnp.int32), axis=1)
    keys = jnp.concatenate([e_flat * (2 * npairs) + jnp.arange(npairs, dtype=jnp.int32),
                            fill_expert * (2 * npairs) + npairs])
    skeys = jnp.sort(keys)
    low = skeys % (2 * npairs)
    dst = jnp.where(low < npairs, low, 0)
    tok = dst % n
    tile_start = jnp.arange(nt, dtype=jnp.int32) * tm
    texp = jnp.minimum(jnp.sum((tile_start[:, None] >= ends[None, :]).astype(jnp.int32), axis=1), N_EXPERTS - 1)
    cnt = jnp.clip(offs[texp] + counts[texp] - tile_start, 0, tm)
    nvalid = (ends[-1] // tm).astype(jnp.int32).reshape(1)
    return (nvalid, texp.astype(jnp.int32), cnt.astype(jnp.int32), tok.reshape(nt, 1, tm), dst.reshape(nt, 1, tm),
            npairs)


def _ple_final_kernel(x1_ref, y0_ref, y1_ref, w_ref, p_ref, gp_ref, wpg_ref, wp_ref, gf_ref, o_ref):
    w = w_ref[...]
    tm, d = x1_ref.shape
    x2 = x1_ref[...] + w[:, 0:1] * _load_rows(y0_ref, 0, tm, d) + w[:, 1:2] * _load_rows(y1_ref, 0, tm, d)
    hn = _rms(x2, gp_ref[...]).astype(BF16)
    gate = _sigmoid(jnp.dot(hn, wpg_ref[...], preferred_element_type=F32))
    pe = jnp.dot(p_ref[...].astype(BF16), wp_ref[...], preferred_element_type=F32)
    x3 = x2 + gate * pe
    o_ref[...] = _rms(x3, gf_ref[...])


def _ple_final(x1, ypairs, wcols, p2d, gp, wpg, wp, gf, tm=256):
    n, d = x1.shape
    nb = n // tm
    k = _row_pitch(d)
    return pl.pallas_call(
        _ple_final_kernel,
        out_shape=jax.ShapeDtypeStruct((n, d), F32),
        grid=(nb,),
        in_specs=[
            pl.BlockSpec((tm, d), lambda i: (i, 0)),
            pl.BlockSpec((tm * k, LANES), lambda i: (i, 0)),
            pl.BlockSpec((tm * k, LANES), lambda i: (nb + i, 0)),
            pl.BlockSpec((tm, wcols.shape[1]), lambda i: (i, 0)),
            pl.BlockSpec((tm, p2d.shape[1]), lambda i: (i, 0)),
            pl.BlockSpec((1, d), lambda i: (0, 0)),
            pl.BlockSpec((d, d), lambda i: (0, 0)),
            pl.BlockSpec(wp.shape, lambda i: (0, 0)),
            pl.BlockSpec((1, d), lambda i: (0, 0)),
        ],
        out_specs=pl.BlockSpec((tm, d), lambda i: (i, 0)),
        compiler_params=_cparams(("parallel",)),
        name="ple_final",
    )(x1, ypairs, ypairs, wcols, p2d, gp, wpg, wp, gf)


def _block_diag(w, per):
    nb, bw, _ = w.shape
    w = w.reshape(nb // per, per, bw, bw)
    eye = jnp.eye(per, dtype=w.dtype)
    return jnp.einsum("cpij,pq->cpiqj", w, eye).reshape(nb // per, per * bw, per * bw)


def _regroup_kernel(w_ref, o_ref):
    d = Z_MGB - Z_MGA
    c = np.cumsum((0, NSA_Q_WIDTH) + (NSA_KV_WIDTH,) * 6 + (3 * NSA_HEADS, LRU_WIDTH, LRU_WIDTH, d, d))
    seg = lambda k: w_ref[:, int(c[k]):int(c[k + 1])].astype(BF16)
    q, k_c, v_c, k_s, v_s, k_w, v_w, gates, lru_x, lru_y, mg_a, mg_b = (seg(k) for k in range(12))

    def pair(a, b):
        parts = []
        for g in range(NSA_KV_GROUPS):
            parts += [a[:, g * HEAD_DIM:(g + 1) * HEAD_DIM], b[:, g * HEAD_DIM:(g + 1) * HEAD_DIM]]
        return jnp.concatenate(parts, axis=1)

    pad = jnp.zeros((w_ref.shape[0], Z_WIDTH - int(c[-1])), BF16)
    cols = [mg_a, mg_b, q, pair(k_s, k_w), pair(v_s, v_w), k_c, v_c, lru_x, lru_y, gates, pad]
    o_ref[...] = jnp.concatenate(cols, axis=1)


def _regroup_w_in(w_in, tr=256):
    d, nw = w_in.shape
    return pl.pallas_call(
        _regroup_kernel,
        out_shape=jax.ShapeDtypeStruct((d, Z_WIDTH), BF16),
        grid=(d // tr,),
        in_specs=[pl.BlockSpec((tr, nw), lambda i: (i, 0))],
        out_specs=pl.BlockSpec((tr, Z_WIDTH), lambda i: (i, 0)),
        compiler_params=_cparams(("parallel",)),
        name="regroup_w_in",
    )(w_in)


def _layer(x2d, batch, seq, ln_mix, w_in, cmp_k_pos, cmp_k_w1, cmp_k_w2, cmp_v_pos, cmp_v_w1, cmp_v_w2, conv_w,
           conv_b, lru_wa, lru_ba, lru_wx, lru_bx, lru_lambda, w_nsa_up, w_lru_up, w_out, ln_ffn, w_grp, b_grp, w_exp,
           b_exp, w_gate, w_up, w_down):
    n, d = x2d.shape
    g, hd = NSA_KV_GROUPS, HEAD_DIM

    z = _in_proj(x2d, ln_mix.reshape(1, d), _regroup_w_in(w_in))

    def windows(col):
        a = z[:, col:col + NSA_KV_WIDTH].reshape(batch, seq // CMP_STRIDE, CMP_STRIDE, g, hd)
        return a.transpose(0, 3, 1, 2, 4).reshape(batch, g, seq // CMP_STRIDE, CMP_STRIDE * hd)

    pos8 = lambda pos: jnp.broadcast_to(pos.reshape(1, -1), (SUBLANES, pos.size)).astype(BF16)
    kc, vct = _compress(windows(Z_KC), windows(Z_VC), pos8(cmp_k_pos), cmp_k_w1.astype(BF16), cmp_k_w2.astype(BF16),
                        pos8(cmp_v_pos), cmp_v_w1.astype(BF16), cmp_v_w2.T.astype(BF16))
    o = _nsa_attention(z, kc, vct, batch, seq)

    per = LRU_CB // LRU_BW
    lru = _rglru(z, batch, seq, conv_w, conv_b.reshape(1, -1), _block_diag(lru_wa, per).astype(BF16),
                 lru_ba.reshape(1, -1), _block_diag(lru_wx, per).astype(BF16), lru_bx.reshape(1, -1),
                 lru_lambda.reshape(1, -1))

    merged = _merge(o, lru, z, w_nsa_up.astype(BF16), w_lru_up.astype(BF16))

    nr = SUBLANES + N_EXPERTS
    wr = jnp.zeros((nr, d), F32).at[0:N_GROUPS].set(w_grp.T).at[SUBLANES:].set(w_exp.T)
    br = jnp.zeros((nr, 1), F32).at[0:N_GROUPS, 0].set(b_grp).at[SUBLANES:, 0].set(b_exp)
    x1, xn, eid, ew = _out_route(merged, x2d, w_out.astype(BF16), ln_ffn.reshape(1, d), wr, br)

    nvalid, texp, cnt, tok, dst, n_rows_out = _moe_plan(eid[0:EXPERT_TOPK], n, MOE_TM)
    ypairs = _moe(nvalid, texp, cnt, tok, dst, xn, w_gate, w_up, w_down, n_rows_out)

    return x1, ypairs, ew.T


def kernel(x, p, ln_mix, w_in, cmp_k_pos, cmp_k_w1, cmp_k_w2, cmp_v_pos, cmp_v_w1, cmp_v_w2, conv_w, conv_b, lru_wa, lru_ba, lru_wx, lru_bx, lru_lambda, w_nsa_up, w_lru_up, w_out, ln_ffn, w_grp, b_grp, w_exp, b_exp, w_gate, w_up, w_down, ln_ple, w_ple, w_ple_gate, ln_final):
    batch, seq, d = x.shape
    assert p.shape[0] == 1, "the final norm is fused into the (single) layer's last kernel"
    n = batch * seq
    x1, ypairs, wcols = _layer(
        x.reshape(n, d), batch, seq, ln_mix[0], w_in[0], cmp_k_pos[0], cmp_k_w1[0], cmp_k_w2[0],
        cmp_v_pos[0], cmp_v_w1[0], cmp_v_w2[0], conv_w[0], conv_b[0], lru_wa[0], lru_ba[0], lru_wx[0], lru_bx[0],
        lru_lambda[0], w_nsa_up[0], w_lru_up[0], w_out[0], ln_ffn[0], w_grp[0], b_grp[0], w_exp[0], b_exp[0],
        w_gate[0], w_up[0], w_down[0])
    out = _ple_final(x1, ypairs, wcols, p[0].reshape(n, -1), ln_ple[0].reshape(1, d),
                     w_ple_gate[0].astype(BF16), w_ple[0].astype(BF16), ln_final.reshape(1, d))
    return out.reshape(batch, seq, d)
```

```python
import numpy as np
import jax
import jax.numpy as jnp
from jax import lax
from jax.experimental import pallas as pl
from jax.experimental.pallas import tpu as pltpu

F32 = jnp.float32
BF16 = jnp.bfloat16

NSA_HEADS = 16
NSA_KV_GROUPS = 4
NSA_HPG = NSA_HEADS // NSA_KV_GROUPS
HEAD_DIM = 64
NSA_Q_WIDTH = NSA_HEADS * HEAD_DIM
NSA_KV_WIDTH = NSA_KV_GROUPS * HEAD_DIM
CMP_BLOCK = 32
CMP_STRIDE = 16
CMP_HIDDEN = 2 * HEAD_DIM
SLC_BLOCK = 64
SLC_TOPN = 16
WINDOW = 512
ATTN_SCALE = HEAD_DIM ** -0.5
NEG_INF = -1e30
FORCE_SCORE = 1e4
LRU_WIDTH = 1024
LRU_BLOCKS = 16
LRU_BW = LRU_WIDTH // LRU_BLOCKS
CONV_WIDTH = 4
LRU_C = 8.0
N_GROUPS = 4
EXPERTS_PER_GROUP = 8
N_EXPERTS = N_GROUPS * EXPERTS_PER_GROUP
EXPERT_TOPK = 2
D_EXPERT = 512
EPS = 1e-6

LANES = 128
SUBLANES = 8
VMEM_LIMIT_BYTES = 56 * 1024 * 1024

Z_MGA = 0
Z_MGB = 2048
Z_Q = 4096
Z_KK = 5120
Z_VV = 5632
Z_KC = 6144
Z_VC = 6400
Z_LX = 6656
Z_LY = 7680
Z_GATE = 8704
Z_WIDTH = 9216
LRU_CB = 256
MOE_TM = 256
NSA_TQ = 256
NSA_CK = 512


def _cparams(sem, vmem=VMEM_LIMIT_BYTES):
    return pltpu.CompilerParams(dimension_semantics=sem, vmem_limit_bytes=vmem)


def _rms(x, g):
    return x * lax.rsqrt(jnp.mean(x * x, axis=-1, keepdims=True) + EPS) * g


def _gelu_tanh(x):
    return 0.5 * x * (1.0 + jnp.tanh(np.sqrt(2.0 / np.pi) * (x + 0.044715 * (x * x * x))))


def _sigmoid(x):
    return 1.0 / (1.0 + jnp.exp(-x))


def _row_pitch(d):
    k = d // LANES
    return k + 1 - (k % 2)


def _store_rows(ref, row0, val):
    rows, d = val.shape
    pitch = _row_pitch(d)
    for j in range(d // LANES):
        ref[pl.ds(row0 * pitch + j, rows, stride=pitch), :] = val[:, j * LANES:(j + 1) * LANES]
    for j in range(d // LANES, pitch):
        ref[pl.ds(row0 * pitch + j, rows, stride=pitch), :] = jnp.zeros((rows, LANES), val.dtype)


def _load_rows(ref, row0, rows, d):
    pitch = _row_pitch(d)
    return jnp.concatenate([ref[pl.ds(row0 * pitch + j, rows, stride=pitch), :] for j in range(d // LANES)], axis=1)


def _in_proj_kernel(x_ref, g_ref, w_ref, o_ref, h_ref):
    @pl.when(pl.program_id(1) == 0)
    def _():
        h_ref[...] = _rms(x_ref[...], g_ref[...]).astype(BF16)

    o_ref[...] = jnp.dot(h_ref[...], w_ref[...], preferred_element_type=F32).astype(o_ref.dtype)


def _in_proj(x2d, g, w, tm=512, tn=1024):
    n, d = x2d.shape
    nw = w.shape[1]
    return pl.pallas_call(
        _in_proj_kernel,
        out_shape=jax.ShapeDtypeStruct((n, nw), BF16),
        grid=(n // tm, nw // tn),
        in_specs=[
            pl.BlockSpec((tm, d), lambda i, j: (i, 0)),
            pl.BlockSpec((1, d), lambda i, j: (0, 0)),
            pl.BlockSpec((d, tn), lambda i, j: (0, j)),
        ],
        out_specs=pl.BlockSpec((tm, tn), lambda i, j: (i, j)),
        scratch_shapes=[pltpu.VMEM((tm, d), BF16)],
        compiler_params=_cparams(("parallel", "arbitrary")),
        name="in_proj",
    )(x2d, g, w)


def _compress_kernel(zk_ref, zv_ref, kpos_ref, kw1_ref, kw2_ref, vpos_ref, vw1_ref, vw2t_ref, kc_ref, vct_ref,
                     xf_ref):
    seq = zk_ref.shape[0]
    ncp = seq // CMP_STRIDE
    hd = HEAD_DIM
    half = CMP_STRIDE * hd

    def half_windows(z_ref):
        xf_ref[...] = z_ref[...].astype(F32)
        lines = [xf_ref[pl.ds(r, ncp, stride=CMP_STRIDE), :].astype(BF16) for r in range(CMP_STRIDE)]
        return [jnp.concatenate([ln[:, gg * hd:(gg + 1) * hd] for ln in lines], axis=1) for gg in range(LANES // hd)]

    def hidden(x, pos_ref, w1_ref):
        w1 = w1_ref[...]
        ha = jnp.dot(x, w1[:half], preferred_element_type=F32)
        hb = jnp.dot(x, w1[half:], preferred_element_type=F32)
        hb = pltpu.roll(hb, hb.shape[0] - 1, axis=0)
        pc = jnp.dot(pos_ref[...], w1, preferred_element_type=F32)[0:1]
        return _gelu_tanh(ha + hb + pc).astype(BF16)

    for gg, x in enumerate(half_windows(zk_ref)):
        hk = hidden(x, kpos_ref, kw1_ref)
        kc_ref[0, gg] = jnp.dot(hk, kw2_ref[...], preferred_element_type=F32).astype(kc_ref.dtype)
    for gg, x in enumerate(half_windows(zv_ref)):
        hv = hidden(x, vpos_ref, vw1_ref)
        vct_ref[0, gg] = lax.dot_general(vw2t_ref[...], hv, (((1,), (1,)), ((), ())),
                                         preferred_element_type=F32).astype(vct_ref.dtype)


def _compress(z, batch, seq, kpos, kw1, kw2, vpos, vw1, vw2t):
    g, hd = NSA_KV_GROUPS, HEAD_DIM
    gpb = LANES // hd
    ncp = seq // CMP_STRIDE
    full = lambda a: pl.BlockSpec(a.shape, lambda i, j: (0,) * a.ndim)
    return pl.pallas_call(
        _compress_kernel,
        out_shape=(jax.ShapeDtypeStruct((batch, g, ncp, hd), BF16),
                   jax.ShapeDtypeStruct((batch, g, hd, ncp), BF16)),
        grid=(batch, g // gpb),
        in_specs=[
            pl.BlockSpec((seq, LANES), lambda i, j: (i, Z_KC // LANES + j)),
            pl.BlockSpec((seq, LANES), lambda i, j: (i, Z_VC // LANES + j)),
            full(kpos), full(kw1), full(kw2), full(vpos), full(vw1), full(vw2t),
        ],
        out_specs=(pl.BlockSpec((1, gpb, ncp, hd), lambda i, j: (i, j, 0, 0)),
                   pl.BlockSpec((1, gpb, hd, ncp), lambda i, j: (i, j, 0, 0))),
        scratch_shapes=[pltpu.VMEM((seq, LANES), F32)],
        compiler_params=_cparams(("parallel", "parallel")),
        name="compress",
    )(z, z, kpos, kw1, kw2, vpos, vw1, vw2t)


def _nsa_kernel(zq_ref, zkk_ref, zvv_ref, zg_ref, kc_ref, vct_ref, o_ref, kcomb_ref, score_ref, gate_ref):
    grp = pl.program_id(1)
    qi = pl.program_id(2)
    tq, hd, hpg = NSA_TQ, HEAD_DIM, NSA_HPG
    nq = tq * hpg
    seq = kcomb_ref.shape[0]
    ncp = kc_ref.shape[2]
    nsb = score_ref.shape[0]
    blk_shift = SLC_BLOCK.bit_length() - 1

    @pl.when(qi == 0)
    def _():
        kcomb_ref[:, 0:2 * hd] = zkk_ref[...]
        blk_of_row = lax.broadcasted_iota(jnp.int32, (seq, 2 * hd), 0) >> blk_shift
        lane = lax.broadcasted_iota(jnp.int32, (seq, 2 * hd), 1)
        kcomb_ref[:, 2 * hd:] = jnp.where(blk_of_row == lane, 1.0, 0.0).astype(BF16)

    qt = zq_ref[...].T
    q = jnp.concatenate([qt[h * hd:(h + 1) * hd, :] for h in range(hpg)], axis=1)
    q = (q.astype(F32) * ATTN_SCALE).astype(BF16)
    zero = jnp.zeros((hd, nq), BF16)
    t_lane = qi * tq + (lax.broadcasted_iota(jnp.int32, (1, nq), 1) & (tq - 1))
    tn_dims = (((0,), (0,)), ((), ()))

    sc = jnp.dot(kc_ref[0, 0], q, preferred_element_type=F32)
    cmp_end = lax.broadcasted_iota(jnp.int32, (ncp, 1), 0) * CMP_STRIDE + (CMP_BLOCK - 1)
    cmask = cmp_end <= t_lane
    sc = jnp.where(cmask, sc, NEG_INF)
    pc = jnp.where(cmask, jnp.exp(sc - jnp.max(sc, axis=0, keepdims=True)), 0.0)
    lc = jnp.sum(pc, axis=0, keepdims=True)
    pc = pc * (1.0 / jnp.where(lc > 0.0, lc, 1.0))
    o_cmp = jnp.dot(vct_ref[0, 0], pc.astype(BF16), preferred_element_type=F32)

    nwc = WINDOW // tq + 1
    t_lo = t_lane - WINDOW
    ks_w, vs_w = [], []
    for i in range(nwc):
        rows = pl.ds(pl.multiple_of(jnp.maximum(qi - (nwc - 1) + i, 0) * tq, tq), tq)
        ks_w.append(kcomb_ref[rows, 0:2 * hd])
        vs_w.append(zvv_ref[rows, :])
    qw = jnp.concatenate([zero, q], axis=0)
    sw = jnp.dot(jnp.concatenate(ks_w, axis=0), qw, preferred_element_type=F32)
    sw_parts = []
    for i in range(nwc):
        spos = (qi - (nwc - 1) + i) * tq + lax.broadcasted_iota(jnp.int32, (tq, 1), 0)
        if i < nwc - 1:
            ok = jnp.where(spos >= 0, spos, -(1 << 24)) > t_lo
        else:
            ok = spos <= t_lane
        sw_parts.append(jnp.where(ok, sw[i * tq:(i + 1) * tq], NEG_INF))
    sw = jnp.concatenate(sw_parts, axis=0)
    pw = jnp.exp(sw - jnp.max(sw, axis=0, keepdims=True))
    lw = jnp.sum(pw, axis=0, keepdims=True)
    o_win = lax.dot_general(jnp.concatenate(vs_w, axis=0), pw.astype(BF16), tn_dims,
                            preferred_element_type=F32)[hd:2 * hd] * (1.0 / lw)

    psum = pc[:, 0:tq]
    for h in range(1, hpg):
        psum = psum + pc[:, h * tq:(h + 1) * tq]
    jrow = lax.broadcasted_iota(jnp.int32, (nsb, ncp), 0)
    crel = lax.broadcasted_iota(jnp.int32, (nsb, ncp), 1) - jrow * (SLC_BLOCK // CMP_STRIDE)
    ovt = jnp.where((crel >= 0) & (crel <= 2), 1.0, jnp.where((crel == -1) | (crel == 3), 0.5, 0.0)).astype(F32)
    imp = jnp.dot(ovt, psum, preferred_element_type=F32, precision=lax.Precision.HIGHEST)
    jblk = lax.broadcasted_iota(jnp.int32, (nsb, tq), 0)
    blk = (qi * tq + lax.broadcasted_iota(jnp.int32, (1, tq), 1)) >> blk_shift
    forced = (jblk == 0) | (jblk == blk) | (jblk == blk - 1)
    score = jnp.where(forced, FORCE_SCORE, jnp.where(jblk <= blk, imp, -1.0))
    score_ref[...] = score

    def rank_one(k, cnt):
        row = score_ref[pl.ds(k, 1), :]
        ge = jnp.where(row >= score, 1.0, 0.0)
        gt = jnp.where(row > score, 1.0, 0.0)
        return cnt + jnp.where(jblk > k, ge, gt)

    bpt = tq // SLC_BLOCK

    def rank_group(i, cnt):
        for u in range(bpt):
            cnt = rank_one(bpt * i + u, cnt)
        return cnt

    n_vis = jnp.minimum((qi + 1) * bpt, nsb)
    rank = lax.fori_loop(0, n_vis // bpt, rank_group, jnp.zeros((nsb, tq), F32))
    selb = jnp.where(rank < float(min(SLC_TOPN, nsb)), 0.0, NEG_INF).astype(BF16)

    ck = NSA_CK
    pad = jnp.zeros((2 * hd - nsb, nq), BF16)
    qa = jnp.concatenate([q, zero, jnp.concatenate([selb] * hpg, axis=1), pad], axis=0)

    def slc_scores(c):
        return jnp.dot(kcomb_ref[pl.ds(pl.multiple_of(c * ck, ck), ck), :], qa, preferred_element_type=F32)

    def slc_update(c, s, m, l, acc):
        m_new = jnp.maximum(m, jnp.max(s, axis=0, keepdims=True))
        alpha = jnp.exp(m - m_new)
        p = jnp.exp(s - m_new)
        l = alpha * l + jnp.sum(p, axis=0, keepdims=True)
        v = zvv_ref[pl.ds(pl.multiple_of(c * ck, ck), ck), :]
        pv = lax.dot_general(v, p.astype(BF16), tn_dims, preferred_element_type=F32)[0:hd]
        return m_new, l, acc * alpha + pv

    def slc_body(c, carry):
        m, l, acc, s = carry
        s_next = slc_scores(c + 1)
        return slc_update(c, s, m, l, acc) + (s_next,)

    n_full = (qi * tq) // ck
    init = (jnp.full((1, nq), NEG_INF, F32), jnp.zeros((1, nq), F32), jnp.zeros((hd, nq), F32), slc_scores(0))
    m_s, l_s, acc_s, s_d = lax.fori_loop(0, n_full, slc_body, init)
    kpos_d = n_full * ck + lax.broadcasted_iota(jnp.int32, (ck, 1), 0)
    s_d = jnp.where(kpos_d <= t_lane, s_d, NEG_INF)
    _, l_s, acc_s = slc_update(n_full, s_d, m_s, l_s, acc_s)
    o_slc = acc_s * (1.0 / l_s)

    gate_ref[...] = _sigmoid(zg_ref[...].astype(F32)).T

    def branch_gate(br):
        rows = [gate_ref[pl.ds(grp * (hpg * 3) + h * 3 + br, 1), :] for h in range(hpg)]
        return jnp.concatenate(rows, axis=1)

    o = branch_gate(0) * o_cmp + branch_gate(1) * o_slc + branch_gate(2) * o_win
    o_heads = jnp.concatenate([o[:, h * tq:(h + 1) * tq] for h in range(hpg)], axis=0)
    o_ref[...] = o_heads.T.astype(o_ref.dtype)


def _nsa_attention(z, kc, vct, batch, seq):
    n = z.shape[0]
    g, hd, tq = NSA_KV_GROUPS, HEAD_DIM, NSA_TQ
    nqt = seq // tq
    ncp = kc.shape[2]
    nsb = seq // SLC_BLOCK
    gw = NSA_HPG * hd
    assert seq % NSA_CK == 0 and NSA_CK % tq == 0 and WINDOW % tq == 0 and nsb <= 2 * hd
    tile = lambda i, j, k: i * nqt + k
    return pl.pallas_call(
        _nsa_kernel,
        out_shape=jax.ShapeDtypeStruct((n, NSA_Q_WIDTH), BF16),
        grid=(batch, g, nqt),
        in_specs=[
            pl.BlockSpec((tq, gw), lambda i, j, k: (tile(i, j, k), Z_Q // gw + j)),
            pl.BlockSpec((seq, 2 * hd), lambda i, j, k: (i, Z_KK // (2 * hd) + j)),
            pl.BlockSpec((seq, 2 * hd), lambda i, j, k: (i, Z_VV // (2 * hd) + j)),
            pl.BlockSpec((tq, LANES), lambda i, j, k: (tile(i, j, k), Z_GATE // LANES)),
            pl.BlockSpec((1, 1, ncp, hd), lambda i, j, k: (i, j, 0, 0)),
            pl.BlockSpec((1, 1, hd, ncp), lambda i, j, k: (i, j, 0, 0)),
        ],
        out_specs=pl.BlockSpec((tq, gw), lambda i, j, k: (tile(i, j, k), j)),
        scratch_shapes=[pltpu.VMEM((seq, 4 * hd), BF16), pltpu.VMEM((nsb, tq), F32), pltpu.VMEM((LANES, tq), F32)],
        compiler_params=_cparams(("parallel", "parallel", "arbitrary")),
        name="nsa_attn",
    )(z, z, z, z, kc, vct)


def _rglru_kernel(x_ref, y_ref, cw_ref, cb_ref, wa_ref, ba_ref, wx_ref, bx_ref, lam_ref, o_ref,
                  tail_ref, h_ref, a_ref, u_ref):
    tc = pl.program_id(2)
    tt, cb = x_ref.shape

    @pl.when(tc == 0)
    def _():
        tail_ref[...] = jnp.zeros_like(tail_ref)
        h_ref[...] = jnp.zeros_like(h_ref)

    x = x_ref[...].astype(F32)
    xe = jnp.concatenate([tail_ref[...], x], axis=0)
    tail_ref[...] = x[tt - SUBLANES:, :]
    cw = cw_ref[...]
    xc = cb_ref[...]
    for k in range(CONV_WIDTH):
        off = SUBLANES - (CONV_WIDTH - 1) + k
        xc = xc + cw[k:k + 1, :] * xe[off:off + tt, :]
    xcb = xc.astype(BF16)
    r = _sigmoid(jnp.dot(xcb, wa_ref[0], preferred_element_type=F32) + ba_ref[...])
    ig = _sigmoid(jnp.dot(xcb, wx_ref[0], preferred_element_type=F32) + bx_ref[...])
    nl = -lam_ref[...]
    softplus = jnp.maximum(nl, 0.0) + jnp.log1p(jnp.exp(-jnp.abs(nl)))
    log_a = (-LRU_C) * softplus * r
    a_ref[...] = jnp.exp(log_a)
    th = jnp.tanh(log_a)
    u_ref[...] = jnp.sqrt(-2.0 * th / (1.0 - th)) * (ig * xc)

    row = lax.broadcasted_iota(jnp.int32, (SUBLANES, cb), 0)

    def step(i, h):
        sl = pl.ds(pl.multiple_of(i * SUBLANES, SUBLANES), SUBLANES)
        a = a_ref[sl, :]
        u = u_ref[sl, :]
        for s in (1, 2, 4):
            a_s = jnp.where(row >= s, pltpu.roll(a, s, axis=0), 1.0)
            u_s = jnp.where(row >= s, pltpu.roll(u, s, axis=0), 0.0)
            u = a * u_s + u
            a = a * a_s
        hrows = a * h + u
        u_ref[sl, :] = hrows
        return hrows[SUBLANES - 1:SUBLANES, :]

    h_ref[...] = lax.fori_loop(0, tt // SUBLANES, step, h_ref[...], unroll=2)
    o_ref[...] = (u_ref[...] * _gelu_tanh(y_ref[...].astype(F32))).astype(o_ref.dtype)


def _rglru(z, batch, seq, cw, cbias, wa_bd, ba, wx_bd, bx, lam, tt=512):
    n = z.shape[0]
    ncb = LRU_WIDTH // LRU_CB
    nt = seq // tt
    row = lambda i, j, k: i * nt + k
    vec = lambda r: pl.BlockSpec((r, LRU_CB), lambda i, j, k: (0, j))
    return pl.pallas_call(
        _rglru_kernel,
        out_shape=jax.ShapeDtypeStruct((n, LRU_WIDTH), BF16),
        grid=(batch, ncb, nt),
        in_specs=[
            pl.BlockSpec((tt, LRU_CB), lambda i, j, k: (row(i, j, k), Z_LX // LRU_CB + j)),
            pl.BlockSpec((tt, LRU_CB), lambda i, j, k: (row(i, j, k), Z_LY // LRU_CB + j)),
            vec(CONV_WIDTH), vec(1),
            pl.BlockSpec((1, LRU_CB, LRU_CB), lambda i, j, k: (j, 0, 0)), vec(1),
            pl.BlockSpec((1, LRU_CB, LRU_CB), lambda i, j, k: (j, 0, 0)), vec(1),
            vec(1),
        ],
        out_specs=pl.BlockSpec((tt, LRU_CB), lambda i, j, k: (row(i, j, k), j)),
        scratch_shapes=[pltpu.VMEM((SUBLANES, LRU_CB), F32), pltpu.VMEM((1, LRU_CB), F32),
                        pltpu.VMEM((tt, LRU_CB), F32), pltpu.VMEM((tt, LRU_CB), F32)],
        compiler_params=_cparams(("parallel", "parallel", "arbitrary")),
        name="rglru",
    )(z, z, cw, cbias, wa_bd, ba, wx_bd, bx, lam)


def _merge_kernel(o_ref, l_ref, mga_ref, mgb_ref, wn_ref, wl_ref, m_ref):
    ya = jnp.dot(o_ref[...], wn_ref[...], preferred_element_type=F32)
    yb = jnp.dot(l_ref[...], wl_ref[...], preferred_element_type=F32)
    m = _sigmoid(mga_ref[...].astype(F32)) * ya + _sigmoid(mgb_ref[...].astype(F32)) * yb
    m_ref[...] = m.astype(m_ref.dtype)


def _merge(o, lru, z, wn, wl, tm=512):
    n, d = o.shape[0], wn.shape[1]
    return pl.pallas_call(
        _merge_kernel,
        out_shape=jax.ShapeDtypeStruct((n, d), BF16),
        grid=(n // tm,),
        in_specs=[
            pl.BlockSpec((tm, o.shape[1]), lambda i: (i, 0)),
            pl.BlockSpec((tm, lru.shape[1]), lambda i: (i, 0)),
            pl.BlockSpec((tm, d), lambda i: (i, Z_MGA // d)),
            pl.BlockSpec((tm, d), lambda i: (i, Z_MGB // d)),
            pl.BlockSpec(wn.shape, lambda i: (0, 0)),
            pl.BlockSpec(wl.shape, lambda i: (0, 0)),
        ],
        out_specs=pl.BlockSpec((tm, d), lambda i: (i, 0)),
        compiler_params=_cparams(("parallel",)),
        name="merge",
    )(o, lru, z, z, wn, wl)


def _out_route_kernel(m_ref, x_ref, wo_ref, g_ref, wr_ref, br_ref, x1_ref, xn_ref, eid_ref, ew_ref):
    x1 = x_ref[...] + jnp.dot(m_ref[...], wo_ref[...], preferred_element_type=F32)
    x1_ref[...] = x1
    xn = _rms(x1, g_ref[...])
    _store_rows(xn_ref, 0, xn)
    lg = lax.dot_general(wr_ref[...], xn, (((1,), (1,)), ((), ())), preferred_element_type=F32,
                         precision=lax.Precision.HIGHEST) + br_ref[...]
    tm = lg.shape[1]
    sub = lax.broadcasted_iota(jnp.int32, (SUBLANES, tm), 0)

    def first_argmax(v, vmax):
        return jnp.min(jnp.where(v == vmax, sub, SUBLANES), axis=0, keepdims=True)

    gl = jnp.where(sub < N_GROUPS, lg[0:SUBLANES], -jnp.inf)
    gmax = jnp.max(gl, axis=0, keepdims=True)
    ge = jnp.exp(gl - gmax)
    gprob = ge / jnp.sum(ge, axis=0, keepdims=True)
    g_val = jnp.max(gprob, axis=0, keepdims=True)
    g_idx = first_argmax(gprob, g_val)
    e_in = jnp.zeros((EXPERTS_PER_GROUP, tm), F32)
    for gi in range(N_GROUPS):
        lo = SUBLANES + gi * EXPERTS_PER_GROUP
        e_in = jnp.where(g_idx == gi, lg[lo:lo + EXPERTS_PER_GROUP], e_in)
    ee = jnp.exp(e_in - jnp.max(e_in, axis=0, keepdims=True))
    eprob = ee / jnp.sum(ee, axis=0, keepdims=True)
    v1 = jnp.max(eprob, axis=0, keepdims=True)
    i1 = first_argmax(eprob, v1)
    rest = jnp.where(sub == i1, -1.0, eprob)
    v2 = jnp.max(rest, axis=0, keepdims=True)
    i2 = first_argmax(rest, v2)
    den = v1 + v2
    eid = jnp.where(sub == 0, g_idx * EXPERTS_PER_GROUP + i1, g_idx * EXPERTS_PER_GROUP + i2)
    eid_ref[...] = eid
    ew_ref[...] = jnp.where(sub == 0, g_val * v1 / den, g_val * v2 / den)


def _out_route(m, x2d, wo, g, wr, br, tm=512):
    n, d = x2d.shape
    nr = wr.shape[0]
    once = pl.Buffered(1)
    return pl.pallas_call(
        _out_route_kernel,
        out_shape=(jax.ShapeDtypeStruct((n, d), F32), jax.ShapeDtypeStruct((n * _row_pitch(d), LANES), F32),
                   jax.ShapeDtypeStruct((SUBLANES, n), jnp.int32), jax.ShapeDtypeStruct((SUBLANES, n), F32)),
        grid=(n // tm,),
        in_specs=[
            pl.BlockSpec((tm, d), lambda i: (i, 0)),
            pl.BlockSpec((tm, d), lambda i: (i, 0)),
            pl.BlockSpec((d, d), lambda i: (0, 0), pipeline_mode=once),
            pl.BlockSpec((1, d), lambda i: (0, 0)),
            pl.BlockSpec((nr, d), lambda i: (0, 0), pipeline_mode=once),
            pl.BlockSpec((nr, 1), lambda i: (0, 0)),
        ],
        out_specs=(pl.BlockSpec((tm, d), lambda i: (i, 0)), pl.BlockSpec((tm * _row_pitch(d), LANES), lambda i: (i, 0)),
                   pl.BlockSpec((SUBLANES, tm), lambda i: (0, i)), pl.BlockSpec((SUBLANES, tm), lambda i: (0, i))),
        compiler_params=_cparams(("parallel",)),
        name="out_route",
    )(m, x2d, wo, g, wr, br)


def _for_rows(cnt, fn, unroll=8):
    sh = unroll.bit_length() - 1

    def group(gidx, c):
        for u in range(unroll):
            fn(gidx * unroll + u)
        return c

    def single(r, c):
        fn(r)
        return c

    lax.fori_loop(0, cnt >> sh, group, 0)
    lax.fori_loop((cnt >> sh) << sh, cnt, single, 0)


def _moe_kernel(nvalid_ref, texp_ref, cnt_ref, tok_ref, tokn_ref, dst_ref, xn_hbm, wg_ref, wu_ref, wd_ref, y_hbm,
                xbuf, ybuf, gsem, ssem):
    i = pl.program_id(0)
    nv = nvalid_ref[0]
    slot = i % 2
    d = wg_ref.shape[1]
    k = _row_pitch(d)
    tm = xbuf.shape[0] // (2 * k)

    def line(ref, row):
        return ref.at[pl.ds(row * k, k), :]

    def gather_row(idx_ref, s):
        def fn(r):
            pltpu.make_async_copy(line(xn_hbm, idx_ref[0, 0, r]), line(xbuf, s * tm + r), gsem.at[s]).start()
        return fn

    def gather_wait_row(s):
        def fn(r):
            pltpu.make_async_copy(line(xn_hbm, 0), line(xbuf, s * tm + r), gsem.at[s]).wait()
        return fn

    def scatter_row(s):
        def fn(r):
            pltpu.make_async_copy(line(ybuf, s * tm + r), line(y_hbm, dst_ref[0, 0, r]), ssem.at[s]).start()
        return fn

    def scatter_wait_row(s):
        def fn(r):
            pltpu.make_async_copy(line(ybuf, s * tm + r), line(y_hbm, 0), ssem.at[s]).wait()
        return fn

    @pl.when(i == 0)
    def _():
        xbuf[...] = jnp.zeros_like(xbuf)
        _for_rows(cnt_ref[0], gather_row(tok_ref, 0))

    @pl.when(i + 1 < nv)
    def _():
        _for_rows(cnt_ref[i + 1], gather_row(tokn_ref, 1 - slot))

    @pl.when(i < nv)
    def _():
        _for_rows(cnt_ref[i], gather_wait_row(slot))
        x = _load_rows(xbuf, slot * tm, tm, d).astype(BF16)
        gp = jnp.dot(x, wg_ref[0], preferred_element_type=F32)
        up = jnp.dot(x, wu_ref[0], preferred_element_type=F32)
        hid = (gp * _sigmoid(gp) * up).astype(BF16)
        y = jnp.dot(hid, wd_ref[0], preferred_element_type=F32)

        @pl.when(i >= 2)
        def _():
            _for_rows(cnt_ref[i - 2], scatter_wait_row(slot))

        _store_rows(ybuf, slot * tm, y)
        _for_rows(cnt_ref[i], scatter_row(slot))

    @pl.when(i == nv - 1)
    def _():
        _for_rows(cnt_ref[i], scatter_wait_row(slot))

        @pl.when(nv >= 2)
        def _():
            _for_rows(cnt_ref[i - 1], scatter_wait_row(1 - slot))


def _moe(nvalid, texp, cnt, tok, dst, xn, wg, wu, wd, n_rows_out):
    nt, _, tm = tok.shape
    d, de = wg.shape[1], wg.shape[2]
    k = _row_pitch(d)
    grid_spec = pltpu.PrefetchScalarGridSpec(
        num_scalar_prefetch=3,
        grid=(nt,),
        in_specs=[
            pl.BlockSpec((1, 1, tm), lambda i, nv, te, ct: (i, 0, 0), memory_space=pltpu.SMEM),
            pl.BlockSpec((1, 1, tm), lambda i, nv, te, ct: (jnp.minimum(i + 1, nt - 1), 0, 0),
                         memory_space=pltpu.SMEM),
            pl.BlockSpec((1, 1, tm), lambda i, nv, te, ct: (i, 0, 0), memory_space=pltpu.SMEM),
            pl.BlockSpec(memory_space=pl.ANY),
            pl.BlockSpec((1, d, de), lambda i, nv, te, ct: (te[i], 0, 0)),
            pl.BlockSpec((1, d, de), lambda i, nv, te, ct: (te[i], 0, 0)),
            pl.BlockSpec((1, de, d), lambda i, nv, te, ct: (te[i], 0, 0)),
        ],
        out_specs=pl.BlockSpec(memory_space=pl.ANY),
        scratch_shapes=[pltpu.VMEM((2 * tm * k, LANES), F32), pltpu.VMEM((2 * tm * k, LANES), F32),
                        pltpu.SemaphoreType.DMA((2,)), pltpu.SemaphoreType.DMA((2,))],
    )
    return pl.pallas_call(
        _moe_kernel,
        out_shape=jax.ShapeDtypeStruct((n_rows_out * k, LANES), F32),
        grid_spec=grid_spec,
        compiler_params=_cparams(("arbitrary",)),
        name="moe",
    )(nvalid, texp, cnt, tok, tok, dst, xn, wg, wu, wd)


def _moe_plan(eid, n, tm):
    e_flat = eid.reshape(-1)
    npairs = e_flat.shape[0]
    experts = jnp.arange(N_EXPERTS, dtype=jnp.int32)
    counts = jnp.sum((e_flat[:, None] == experts[None, :]).astype(jnp.int32), axis=0)
    padded = ((counts + tm - 1) // tm) * tm
    ends = jnp.cumsum(padded)
    offs = ends - padded
    n_rows = npairs + N_EXPERTS * tm
    nt = n_rows // tm
    fill = jnp.arange(n_rows - npairs, dtype=jnp.int32)
    fill_expert = jnp.sum((fill[:, None] >= jnp.cumsum(padded - counts)[None, :]).astype(jnp.int32), axis=1)
    keys = jnp.concatenate([e_flat * (2 * npairs) + jnp.arange(npairs, dtype=jnp.int32),
                            fill_expert * (2 * npairs) + npairs])
    skeys = jnp.sort(keys)
    low = skeys % (2 * npairs)
    dst = jnp.where(low < npairs, low, 0)
    tok = dst % n
    tile_start = jnp.arange(nt, dtype=jnp.int32) * tm
    texp = jnp.minimum(jnp.sum((tile_start[:, None] >= ends[None, :]).astype(jnp.int32), axis=1), N_EXPERTS - 1)
    cnt = jnp.clip(offs[texp] + counts[texp] - tile_start, 0, tm)
    nvalid = (ends[-1] // tm).astype(jnp.int32).reshape(1)
    return (nvalid, texp.astype(jnp.int32), cnt.astype(jnp.int32), tok.reshape(nt, 1, tm), dst.reshape(nt, 1, tm),
            npairs)


def _ple_final_kernel(x1_ref, y0_ref, y1_ref, w_ref, p_ref, gp_ref, wpg_ref, wp_ref, gf_ref, o_ref):
    w = w_ref[...]
    tm, d = x1_ref.shape
    x2 = x1_ref[...] + w[:, 0:1] * _load_rows(y0_ref, 0, tm, d) + w[:, 1:2] * _load_rows(y1_ref, 0, tm, d)
    hn = _rms(x2, gp_ref[...]).astype(BF16)
    gate = _sigmoid(jnp.dot(hn, wpg_ref[...], preferred_element_type=F32))
    pe = jnp.dot(p_ref[...].astype(BF16), wp_ref[...], preferred_element_type=F32)
    x3 = x2 + gate * pe
    o_ref[...] = _rms(x3, gf_ref[...])


def _ple_final(x1, ypairs, wcols, p2d, gp, wpg, wp, gf, tm=256):
    n, d = x1.shape
    nb = n // tm
    k = _row_pitch(d)
    return pl.pallas_call(
        _ple_final_kernel,
        out_shape=jax.ShapeDtypeStruct((n, d), F32),
        grid=(nb,),
        in_specs=[
            pl.BlockSpec((tm, d), lambda i: (i, 0)),
            pl.BlockSpec((tm * k, LANES), lambda i: (i, 0)),
            pl.BlockSpec((tm * k, LANES), lambda i: (nb + i, 0)),
            pl.BlockSpec((tm, wcols.shape[1]), lambda i: (i, 0)),
            pl.BlockSpec((tm, p2d.shape[1]), lambda i: (i, 0)),
            pl.BlockSpec((1, d), lambda i: (0, 0)),
            pl.BlockSpec((d, d), lambda i: (0, 0)),
            pl.BlockSpec(wp.shape, lambda i: (0, 0)),
            pl.BlockSpec((1, d), lambda i: (0, 0)),
        ],
        out_specs=pl.BlockSpec((tm, d), lambda i: (i, 0)),
        compiler_params=_cparams(("parallel",)),
        name="ple_final",
    )(x1, ypairs, ypairs, wcols, p2d, gp, wpg, wp, gf)


def _block_diag(w, per):
    nb, bw, _ = w.shape
    w = w.reshape(nb // per, per, bw, bw)
    eye = jnp.eye(per, dtype=w.dtype)
    return jnp.einsum("cpij,pq->cpiqj", w, eye).reshape(nb // per, per * bw, per * bw)


def _regroup_kernel(wt_ref, o_ref):
    d = Z_MGB - Z_MGA
    c = np.cumsum((0, NSA_Q_WIDTH) + (NSA_KV_WIDTH,) * 6 + (3 * NSA_HEADS, LRU_WIDTH, LRU_WIDTH, d, d))
    seg = lambda k: wt_ref[int(c[k]):int(c[k + 1]), :]
    q, k_c, v_c, k_s, v_s, k_w, v_w, gates, lru_x, lru_y, mg_a, mg_b = (seg(k) for k in range(12))

    def pair(a, b):
        parts = []
        for g in range(NSA_KV_GROUPS):
            parts += [a[g * HEAD_DIM:(g + 1) * HEAD_DIM], b[g * HEAD_DIM:(g + 1) * HEAD_DIM]]
        return jnp.concatenate(parts, axis=0)

    pad = jnp.zeros((Z_WIDTH - int(c[-1]), wt_ref.shape[1]), wt_ref.dtype)
    rows = [mg_a, mg_b, q, pair(k_s, k_w), pair(v_s, v_w), k_c, v_c, lru_x, lru_y, gates, pad]
    o_ref[...] = jnp.concatenate(rows, axis=0).T.astype(o_ref.dtype)


def _regroup_w_in(w_in_t, tr=256):
    nw, d = w_in_t.shape
    return pl.pallas_call(
        _regroup_kernel,
        out_shape=jax.ShapeDtypeStruct((d, Z_WIDTH), BF16),
        grid=(d // tr,),
        in_specs=[pl.BlockSpec((nw, tr), lambda i: (0, i))],
        out_specs=pl.BlockSpec((tr, Z_WIDTH), lambda i: (i, 0)),
        compiler_params=_cparams(("parallel",)),
        name="regroup_w_in",
    )(w_in_t)


def _layer(x2d, batch, seq, ln_mix, w_in, cmp_k_pos, cmp_k_w1, cmp_k_w2, cmp_v_pos, cmp_v_w1, cmp_v_w2, conv_w,
           conv_b, lru_wa, lru_ba, lru_wx, lru_bx, lru_lambda, w_nsa_up, w_lru_up, w_out, ln_ffn, w_grp, b_grp, w_exp,
           b_exp, w_gate, w_up, w_down):
    n, d = x2d.shape
    g, hd = NSA_KV_GROUPS, HEAD_DIM

    z = _in_proj(x2d, ln_mix.reshape(1, d), _regroup_w_in(w_in.T))

    pos8 = lambda pos: jnp.broadcast_to(pos.reshape(1, -1), (SUBLANES, pos.size)).astype(BF16)
    kc, vct = _compress(z, batch, seq, pos8(cmp_k_pos), cmp_k_w1.astype(BF16), cmp_k_w2.astype(BF16),
                        pos8(cmp_v_pos), cmp_v_w1.astype(BF16), cmp_v_w2.T.astype(BF16))
    o = _nsa_attention(z, kc, vct, batch, seq)

    per = LRU_CB // LRU_BW
    lru = _rglru(z, batch, seq, conv_w, conv_b.reshape(1, -1), _block_diag(lru_wa, per).astype(BF16),
                 lru_ba.reshape(1, -1), _block_diag(lru_wx, per).astype(BF16), lru_bx.reshape(1, -1),
                 lru_lambda.reshape(1, -1))

    merged = _merge(o, lru, z, w_nsa_up.astype(BF16), w_lru_up.astype(BF16))

    nr = SUBLANES + N_EXPERTS
    wr = jnp.zeros((nr, d), F32).at[0:N_GROUPS].set(w_grp.T).at[SUBLANES:].set(w_exp.T)
    br = jnp.zeros((nr, 1), F32).at[0:N_GROUPS, 0].set(b_grp).at[SUBLANES:, 0].set(b_exp)
    x1, xn, eid, ew = _out_route(merged, x2d, w_out.astype(BF16), ln_ffn.reshape(1, d), wr, br)

    nvalid, texp, cnt, tok, dst, n_rows_out = _moe_plan(eid[0:EXPERT_TOPK], n, MOE_TM)
    ypairs = _moe(nvalid, texp, cnt, tok, dst, xn, w_gate, w_up, w_down, n_rows_out)

    return x1, ypairs, ew.T


def kernel(x, p, ln_mix, w_in, cmp_k_pos, cmp_k_w1, cmp_k_w2, cmp_v_pos, cmp_v_w1, cmp_v_w2, conv_w, conv_b, lru_wa, lru_ba, lru_wx, lru_bx, lru_lambda, w_nsa_up, w_lru_up, w_out, ln_ffn, w_grp, b_grp, w_exp, b_exp, w_gate, w_up, w_down, ln_ple, w_ple, w_ple_gate, ln_final):
    batch, seq, d = x.shape
    assert p.shape[0] == 1, "the final norm is fused into the (single) layer's last kernel"
    n = batch * seq
    x1, ypairs, wcols = _layer(
        x.reshape(n, d), batch, seq, ln_mix[0], w_in[0], cmp_k_pos[0], cmp_k_w1[0], cmp_k_w2[0],
        cmp_v_pos[0], cmp_v_w1[0], cmp_v_w2[0], conv_w[0], conv_b[0], lru_wa[0], lru_ba[0], lru_wx[0], lru_bx[0],
        lru_lambda[0], w_nsa_up[0], w_lru_up[0], w_out[0], ln_ffn[0], w_grp[0], b_grp[0], w_exp[0], b_exp[0],
        w_gate[0], w_up[0], w_down[0])
    out = _ple_final(x1, ypairs, wcols, p[0].reshape(n, -1), ln_ple[0].reshape(1, d),
                     w_ple_gate[0].astype(BF16), w_ple[0].astype(BF16), ln_final.reshape(1, d))
    return out.reshape(batch, seq, d)
```

```python
import numpy as np
import jax
import jax.numpy as jnp
from jax import lax
from jax.experimental import pallas as pl
from jax.experimental.pallas import tpu as pltpu

F32 = jnp.float32
BF16 = jnp.bfloat16

NSA_HEADS = 16
NSA_KV_GROUPS = 4
NSA_HPG = NSA_HEADS // NSA_KV_GROUPS
HEAD_DIM = 64
NSA_Q_WIDTH = NSA_HEADS * HEAD_DIM
NSA_KV_WIDTH = NSA_KV_GROUPS * HEAD_DIM
CMP_BLOCK = 32
CMP_STRIDE = 16
CMP_HIDDEN = 2 * HEAD_DIM
SLC_BLOCK = 64
SLC_TOPN = 16
WINDOW = 512
ATTN_SCALE = HEAD_DIM ** -0.5
NEG_INF = -1e30
LOG2E = float(np.log2(np.e))
FORCE_SCORE = 1e4
LRU_WIDTH = 1024
LRU_BLOCKS = 16
LRU_BW = LRU_WIDTH // LRU_BLOCKS
CONV_WIDTH = 4
LRU_C = 8.0
N_GROUPS = 4
EXPERTS_PER_GROUP = 8
N_EXPERTS = N_GROUPS * EXPERTS_PER_GROUP
EXPERT_TOPK = 2
D_EXPERT = 512
EPS = 1e-6

LANES = 128
SUBLANES = 8
VMEM_LIMIT_BYTES = 56 * 1024 * 1024

Z_MGA = 0
Z_MGB = 2048
Z_Q = 4096
Z_KK = 5120
Z_VV = 5632
Z_KC = 6144
Z_VC = 6400
Z_LX = 6656
Z_LY = 7680
Z_GATE = 8704
Z_WIDTH = 9216
LRU_CB = 512
LRU_WT = 256
MOE_TM = 256
NSA_TQ = 256
NSA_CK = 512


def _cparams(sem, vmem=VMEM_LIMIT_BYTES):
    return pltpu.CompilerParams(dimension_semantics=sem, vmem_limit_bytes=vmem)


def _rms(x, g):
    return x * lax.rsqrt(jnp.mean(x * x, axis=-1, keepdims=True) + EPS) * g


def _gelu_tanh(x):
    return 0.5 * x * (1.0 + jnp.tanh(np.sqrt(2.0 / np.pi) * (x + 0.044715 * (x * x * x))))


def _sigmoid(x):
    return 1.0 / (1.0 + jnp.exp(-x))


def _row_pitch(d):
    k = d // LANES
    return k + 1 - (k % 2)


def _store_rows(ref, row0, val):
    rows, d = val.shape
    pitch = _row_pitch(d)
    for j in range(d // LANES):
        ref[pl.ds(row0 * pitch + j, rows, stride=pitch), :] = val[:, j * LANES:(j + 1) * LANES]
    for j in range(d // LANES, pitch):
        ref[pl.ds(row0 * pitch + j, rows, stride=pitch), :] = jnp.zeros((rows, LANES), val.dtype)


def _load_rows(ref, row0, rows, d):
    pitch = _row_pitch(d)
    return jnp.concatenate([ref[pl.ds(row0 * pitch + j, rows, stride=pitch), :] for j in range(d // LANES)], axis=1)


def _in_proj_kernel(x_ref, g_ref, w_ref, o_ref, h_ref):
    @pl.when(pl.program_id(1) == 0)
    def _():
        h_ref[...] = _rms(x_ref[...], g_ref[...]).astype(BF16)

    o_ref[...] = jnp.dot(h_ref[...], w_ref[...], preferred_element_type=F32).astype(o_ref.dtype)


def _in_proj(x2d, g, w, tm=1024, tn=1024):
    n, d = x2d.shape
    nw = w.shape[1]
    return pl.pallas_call(
        _in_proj_kernel,
        out_shape=jax.ShapeDtypeStruct((n, nw), BF16),
        grid=(n // tm, nw // tn),
        in_specs=[
            pl.BlockSpec((tm, d), lambda i, j: (i, 0)),
            pl.BlockSpec((1, d), lambda i, j: (0, 0)),
            pl.BlockSpec((d, tn), lambda i, j: (0, j)),
        ],
        out_specs=pl.BlockSpec((tm, tn), lambda i, j: (i, j)),
        scratch_shapes=[pltpu.VMEM((tm, d), BF16)],
        compiler_params=_cparams(("parallel", "arbitrary")),
        name="in_proj",
    )(x2d, g, w)


def _compress_kernel(zk_ref, zv_ref, kpos_ref, kw1_ref, kw2_ref, vpos_ref, vw1_ref, vw2t_ref, kc_ref, vct_ref,
                     xf_ref):
    seq = zk_ref.shape[0]
    ncp = seq // CMP_STRIDE
    hd = HEAD_DIM
    half = CMP_STRIDE * hd

    def half_windows(z_ref):
        xf_ref[...] = z_ref[...].astype(F32)
        lines = [xf_ref[pl.ds(r, ncp, stride=CMP_STRIDE), :].astype(BF16) for r in range(CMP_STRIDE)]
        return [jnp.concatenate([ln[:, gg * hd:(gg + 1) * hd] for ln in lines], axis=1) for gg in range(LANES // hd)]

    def hidden(x, pos_ref, w1_ref):
        w1 = w1_ref[...]
        ha = jnp.dot(x, w1[:half], preferred_element_type=F32)
        hb = jnp.dot(x, w1[half:], preferred_element_type=F32)
        hb = pltpu.roll(hb, hb.shape[0] - 1, axis=0)
        pc = jnp.dot(pos_ref[...], w1, preferred_element_type=F32)[0:1]
        return _gelu_tanh(ha + hb + pc).astype(BF16)

    for gg, x in enumerate(half_windows(zk_ref)):
        hk = hidden(x, kpos_ref, kw1_ref)
        kc_ref[0, gg] = jnp.dot(hk, kw2_ref[...], preferred_element_type=F32).astype(kc_ref.dtype)
    for gg, x in enumerate(half_windows(zv_ref)):
        hv = hidden(x, vpos_ref, vw1_ref)
        vct_ref[0, gg] = lax.dot_general(vw2t_ref[...], hv, (((1,), (1,)), ((), ())),
                                         preferred_element_type=F32).astype(vct_ref.dtype)


def _compress(z, batch, seq, kpos, kw1, kw2, vpos, vw1, vw2t):
    g, hd = NSA_KV_GROUPS, HEAD_DIM
    gpb = LANES // hd
    ncp = seq // CMP_STRIDE
    full = lambda a: pl.BlockSpec(a.shape, lambda i, j: (0,) * a.ndim)
    return pl.pallas_call(
        _compress_kernel,
        out_shape=(jax.ShapeDtypeStruct((batch, g, ncp, hd), BF16),
                   jax.ShapeDtypeStruct((batch, g, hd, ncp), BF16)),
        grid=(batch, g // gpb),
        in_specs=[
            pl.BlockSpec((seq, LANES), lambda i, j: (i, Z_KC // LANES + j)),
            pl.BlockSpec((seq, LANES), lambda i, j: (i, Z_VC // LANES + j)),
            full(kpos), full(kw1), full(kw2), full(vpos), full(vw1), full(vw2t),
        ],
        out_specs=(pl.BlockSpec((1, gpb, ncp, hd), lambda i, j: (i, j, 0, 0)),
                   pl.BlockSpec((1, gpb, hd, ncp), lambda i, j: (i, j, 0, 0))),
        scratch_shapes=[pltpu.VMEM((seq, LANES), F32)],
        compiler_params=_cparams(("parallel", "parallel")),
        name="compress",
    )(z, z, kpos, kw1, kw2, vpos, vw1, vw2t)


def _nsa_kernel(zq_ref, zkk_ref, zvv_ref, zg_ref, kc_ref, vct_ref, o_ref, kcomb_ref, vs_ref, vw_ref, score_ref, gate_ref,
                sa_ref, sb_ref):
    grp = pl.program_id(1)
    qi = pl.program_id(2)
    tq, hd, hpg = NSA_TQ, HEAD_DIM, NSA_HPG
    nq = tq * hpg
    seq = kcomb_ref.shape[0]
    ncp = kc_ref.shape[2]
    nsb = score_ref.shape[0]
    blk_shift = SLC_BLOCK.bit_length() - 1

    @pl.when(qi == 0)
    def _():
        kcomb_ref[:, 0:2 * hd] = zkk_ref[...]
        blk_of_row = lax.broadcasted_iota(jnp.int32, (seq, 2 * hd), 0) >> blk_shift
        lane = lax.broadcasted_iota(jnp.int32, (seq, 2 * hd), 1)
        kcomb_ref[:, 2 * hd:] = jnp.where(blk_of_row == lane, 1.0, 0.0).astype(BF16)
        ones = jnp.ones((seq, hd), BF16)
        vs_ref[...] = jnp.concatenate([zvv_ref[:, 0:hd], ones], axis=1)
        vw_ref[...] = jnp.concatenate([zvv_ref[:, hd:2 * hd], ones], axis=1)

    qt = zq_ref[...].T
    q = jnp.concatenate([qt[h * hd:(h + 1) * hd, :] for h in range(hpg)], axis=1)
    q = (q.astype(F32) * (ATTN_SCALE * LOG2E)).astype(BF16)
    zero = jnp.zeros((hd, nq), BF16)
    t_lane = qi * tq + (lax.broadcasted_iota(jnp.int32, (1, nq), 1) & (tq - 1))
    tn_dims = (((0,), (0,)), ((), ()))

    sc = jnp.dot(kc_ref[0, 0], q, preferred_element_type=F32)
    cmp_end = lax.broadcasted_iota(jnp.int32, (ncp, 1), 0) * CMP_STRIDE + (CMP_BLOCK - 1)
    cmask = cmp_end <= t_lane
    sc = jnp.where(cmask, sc, NEG_INF)
    pc = jnp.where(cmask, jnp.exp2(sc - jnp.max(sc, axis=0, keepdims=True)), 0.0)
    lc = jnp.sum(pc, axis=0, keepdims=True)
    pc = pc * (1.0 / jnp.where(lc > 0.0, lc, 1.0))
    o_cmp = jnp.dot(vct_ref[0, 0], pc.astype(BF16), preferred_element_type=F32)

    nwc = WINDOW // tq + 1
    t_lo = t_lane - WINDOW
    ks_w, vs_w = [], []
    for i in range(nwc):
        rows = pl.ds(pl.multiple_of(jnp.maximum(qi - (nwc - 1) + i, 0) * tq, tq), tq)
        ks_w.append(kcomb_ref[rows, 0:2 * hd])
        vs_w.append(vw_ref[rows, :])
    qw = jnp.concatenate([zero, q], axis=0)
    sw = jnp.dot(jnp.concatenate(ks_w, axis=0), qw, preferred_element_type=F32)
    sw_parts = []
    for i in range(nwc):
        spos = (qi - (nwc - 1) + i) * tq + lax.broadcasted_iota(jnp.int32, (tq, 1), 0)
        if i < nwc - 1:
            ok = jnp.where(spos >= 0, spos, -(1 << 24)) > t_lo
        else:
            ok = spos <= t_lane
        sw_parts.append(jnp.where(ok, sw[i * tq:(i + 1) * tq], NEG_INF))
    sw = jnp.concatenate(sw_parts, axis=0)
    pw = jnp.exp2(sw - jnp.max(sw, axis=0, keepdims=True))
    ow = lax.dot_general(jnp.concatenate(vs_w, axis=0), pw.astype(BF16), tn_dims, preferred_element_type=F32)
    o_win = ow[0:hd] * (1.0 / ow[hd:hd + 1])

    psum = pc[:, 0:tq]
    for h in range(1, hpg):
        psum = psum + pc[:, h * tq:(h + 1) * tq]
    jrow = lax.broadcasted_iota(jnp.int32, (nsb, ncp), 0)
    crel = lax.broadcasted_iota(jnp.int32, (nsb, ncp), 1) - jrow * (SLC_BLOCK // CMP_STRIDE)
    ovt = jnp.where((crel >= 0) & (crel <= 2), 1.0, jnp.where((crel == -1) | (crel == 3), 0.5, 0.0)).astype(F32)
    imp = jnp.dot(ovt, psum, preferred_element_type=F32, precision=lax.Precision.HIGHEST)
    jblk = lax.broadcasted_iota(jnp.int32, (nsb, tq), 0)
    blk = (qi * tq + lax.broadcasted_iota(jnp.int32, (1, tq), 1)) >> blk_shift
    forced = (jblk == 0) | (jblk == blk) | (jblk == blk - 1)
    score = jnp.where(forced, FORCE_SCORE, jnp.where(jblk <= blk, imp, -1.0))
    score_ref[...] = score

    def rank_one(k, cnt):
        row = score_ref[pl.ds(k, 1), :]
        ge = jnp.where(row >= score, 1.0, 0.0)
        gt = jnp.where(row > score, 1.0, 0.0)
        return cnt + jnp.where(jblk > k, ge, gt)

    bpt = tq // SLC_BLOCK

    def rank_group(i, cnt):
        for u in range(bpt):
            cnt = rank_one(bpt * i + u, cnt)
        return cnt

    n_vis = jnp.minimum((qi + 1) * bpt, nsb)
    rank = lax.fori_loop(0, n_vis // bpt, rank_group, jnp.zeros((nsb, tq), F32))
    selb = jnp.where(rank < float(min(SLC_TOPN, nsb)), 0.0, NEG_INF).astype(BF16)

    ck = NSA_CK
    pad = jnp.zeros((2 * hd - nsb, nq), BF16)
    qa = jnp.concatenate([q, zero, jnp.concatenate([selb] * hpg, axis=1), pad], axis=0)

    def slc_scores(c):
        return jnp.dot(kcomb_ref[pl.ds(pl.multiple_of(c * ck, ck), ck), :], qa, preferred_element_type=F32)

    n_full = (qi * tq) // ck

    def slc_update(buf, c, carry, causal):
        def scores():
            if not causal:
                return buf[...]
            return jnp.where(c * ck + lax.broadcasted_iota(jnp.int32, (ck, 1), 0) <= t_lane, buf[...], NEG_INF)

        m, l, acc = carry
        m_new = jnp.maximum(m, jnp.max(scores(), axis=0, keepdims=True))
        alpha = jnp.exp2(m - m_new)
        p = jnp.exp2(scores() - m_new).astype(BF16)
        v = vs_ref[pl.ds(pl.multiple_of(c * ck, ck), ck), :]
        pv = lax.dot_general(v, p, tn_dims, preferred_element_type=F32)
        return m_new, alpha * l + pv[hd:hd + 1], acc * alpha + pv[0:hd]

    def slc_pair(i, carry):
        sb_ref[...] = slc_scores(2 * i + 1)
        carry = slc_update(sa_ref, 2 * i, carry, False)
        sa_ref[...] = slc_scores(2 * i + 2)
        return slc_update(sb_ref, 2 * i + 1, carry, False)

    def tail_two(carry):
        sb_ref[...] = slc_scores(n_full)
        return slc_update(sb_ref, n_full, slc_update(sa_ref, n_full - 1, carry, False), True)

    def tail_one(carry):
        return slc_update(sa_ref, n_full, carry, True)

    sa_ref[...] = slc_scores(0)
    init = (jnp.full((1, nq), NEG_INF, F32), jnp.zeros((1, nq), F32), jnp.zeros((hd, nq), F32))
    carry = lax.fori_loop(0, n_full // 2, slc_pair, init)
    _, l_s, acc_s = lax.cond(n_full % 2 == 1, tail_two, tail_one, carry)
    o_slc = acc_s * (1.0 / l_s)

    gate_ref[...] = _sigmoid(zg_ref[...].astype(F32)).T

    def branch_gate(br):
        rows = [gate_ref[pl.ds(grp * (hpg * 3) + h * 3 + br, 1), :] for h in range(hpg)]
        return jnp.concatenate(rows, axis=1)

    o = branch_gate(0) * o_cmp + branch_gate(1) * o_slc + branch_gate(2) * o_win
    o_heads = jnp.concatenate([o[:, h * tq:(h + 1) * tq] for h in range(hpg)], axis=0)
    o_ref[...] = o_heads.T.astype(o_ref.dtype)


def _nsa_attention(z, kc, vct, batch, seq):
    n = z.shape[0]
    g, hd, tq = NSA_KV_GROUPS, HEAD_DIM, NSA_TQ
    nqt = seq // tq
    ncp = kc.shape[2]
    nsb = seq // SLC_BLOCK
    gw = NSA_HPG * hd
    assert seq % NSA_CK == 0 and NSA_CK % tq == 0 and WINDOW % tq == 0 and nsb <= 2 * hd
    tile = lambda i, j, k: i * nqt + k
    return pl.pallas_call(
        _nsa_kernel,
        out_shape=jax.ShapeDtypeStruct((n, NSA_Q_WIDTH), BF16),
        grid=(batch, g, nqt),
        in_specs=[
            pl.BlockSpec((tq, gw), lambda i, j, k: (tile(i, j, k), Z_Q // gw + j)),
            pl.BlockSpec((seq, 2 * hd), lambda i, j, k: (i, Z_KK // (2 * hd) + j)),
            pl.BlockSpec((seq, 2 * hd), lambda i, j, k: (i, Z_VV // (2 * hd) + j)),
            pl.BlockSpec((tq, LANES), lambda i, j, k: (tile(i, j, k), Z_GATE // LANES)),
            pl.BlockSpec((1, 1, ncp, hd), lambda i, j, k: (i, j, 0, 0)),
            pl.BlockSpec((1, 1, hd, ncp), lambda i, j, k: (i, j, 0, 0)),
        ],
        out_specs=pl.BlockSpec((tq, gw), lambda i, j, k: (tile(i, j, k), j)),
        scratch_shapes=[pltpu.VMEM((seq, 4 * hd), BF16), pltpu.VMEM((seq, 2 * hd), BF16),
                        pltpu.VMEM((seq, 2 * hd), BF16), pltpu.VMEM((nsb, tq), F32),
                        pltpu.VMEM((LANES, tq), F32), pltpu.VMEM((NSA_CK, NSA_HPG * tq), F32),
                        pltpu.VMEM((NSA_CK, NSA_HPG * tq), F32)],
        compiler_params=_cparams(("parallel", "parallel", "arbitrary")),
        name="nsa_attn",
    )(z, z, z, z, kc, vct)


def _rglru_kernel(x_ref, y_ref, cw_ref, cb_ref, wa_ref, ba_ref, wx_ref, bx_ref, lam_ref, o_ref,
                  tail_ref, h_ref, a_ref, u_ref):
    tc = pl.program_id(2)
    tt, cb = x_ref.shape

    @pl.when(tc == 0)
    def _():
        tail_ref[...] = jnp.zeros_like(tail_ref)
        h_ref[...] = jnp.zeros_like(h_ref)

    x = x_ref[...].astype(F32)
    xe = jnp.concatenate([tail_ref[...], x], axis=0)
    tail_ref[...] = x[tt - SUBLANES:, :]
    cw = cw_ref[...]
    xc = cb_ref[...]
    for k in range(CONV_WIDTH):
        off = SUBLANES - (CONV_WIDTH - 1) + k
        xc = xc + cw[k:k + 1, :] * xe[off:off + tt, :]
    xcb = xc.astype(BF16)
    def gate(w_ref, b_ref):
        parts = [jnp.dot(xcb[:, c * LRU_WT:(c + 1) * LRU_WT], w_ref[c], preferred_element_type=F32)
                 for c in range(cb // LRU_WT)]
        return _sigmoid(jnp.concatenate(parts, axis=1) + b_ref[...])

    r = gate(wa_ref, ba_ref)
    ig = gate(wx_ref, bx_ref)
    nl = -lam_ref[...]
    softplus = jnp.maximum(nl, 0.0) + jnp.log1p(jnp.exp(-jnp.abs(nl)))
    log_a = (-LRU_C) * softplus * r
    a_ref[...] = jnp.exp(log_a)
    th = jnp.tanh(log_a)
    u_ref[...] = jnp.sqrt(-2.0 * th / (1.0 - th)) * (ig * xc)

    row = lax.broadcasted_iota(jnp.int32, (SUBLANES, cb), 0)

    def step(i, h):
        sl = pl.ds(pl.multiple_of(i * SUBLANES, SUBLANES), SUBLANES)
        a = a_ref[sl, :]
        u = u_ref[sl, :]
        for s in (1, 2, 4):
            a_s = jnp.where(row >= s, pltpu.roll(a, s, axis=0), 1.0)
            u_s = jnp.where(row >= s, pltpu.roll(u, s, axis=0), 0.0)
            u = a * u_s + u
            a = a * a_s
        hrows = a * h + u
        u_ref[sl, :] = hrows
        return hrows[SUBLANES - 1:SUBLANES, :]

    h_ref[...] = lax.fori_loop(0, tt // SUBLANES, step, h_ref[...], unroll=2)
    o_ref[...] = (u_ref[...] * _gelu_tanh(y_ref[...].astype(F32))).astype(o_ref.dtype)


def _rglru(z, batch, seq, cw, cbias, wa_bd, ba, wx_bd, bx, lam, tt=512):
    n = z.shape[0]
    ncb = LRU_WIDTH // LRU_CB
    nt = seq // tt
    row = lambda i, j, k: i * nt + k
    vec = lambda r: pl.BlockSpec((r, LRU_CB), lambda i, j, k: (0, j))
    return pl.pallas_call(
        _rglru_kernel,
        out_shape=jax.ShapeDtypeStruct((n, LRU_WIDTH), BF16),
        grid=(batch, ncb, nt),
        in_specs=[
            pl.BlockSpec((tt, LRU_CB), lambda i, j, k: (row(i, j, k), Z_LX // LRU_CB + j)),
            pl.BlockSpec((tt, LRU_CB), lambda i, j, k: (row(i, j, k), Z_LY // LRU_CB + j)),
            vec(CONV_WIDTH), vec(1),
            pl.BlockSpec((LRU_CB // LRU_WT, LRU_WT, LRU_WT), lambda i, j, k: (j, 0, 0)), vec(1),
            pl.BlockSpec((LRU_CB // LRU_WT, LRU_WT, LRU_WT), lambda i, j, k: (j, 0, 0)), vec(1),
            vec(1),
        ],
        out_specs=pl.BlockSpec((tt, LRU_CB), lambda i, j, k: (row(i, j, k), j)),
        scratch_shapes=[pltpu.VMEM((SUBLANES, LRU_CB), F32), pltpu.VMEM((1, LRU_CB), F32),
                        pltpu.VMEM((tt, LRU_CB), F32), pltpu.VMEM((tt, LRU_CB), F32)],
        compiler_params=_cparams(("parallel", "parallel", "arbitrary")),
        name="rglru",
    )(z, z, cw, cbias, wa_bd, ba, wx_bd, bx, lam)


def _merge_kernel(o_ref, l_ref, mga_ref, mgb_ref, wn_ref, wl_ref, m_ref):
    ya = jnp.dot(o_ref[...], wn_ref[...], preferred_element_type=F32)
    yb = jnp.dot(l_ref[...], wl_ref[...], preferred_element_type=F32)
    m = _sigmoid(mga_ref[...].astype(F32)) * ya + _sigmoid(mgb_ref[...].astype(F32)) * yb
    m_ref[...] = m.astype(m_ref.dtype)


def _merge(o, lru, z, wn, wl, tm=512):
    n, d = o.shape[0], wn.shape[1]
    return pl.pallas_call(
        _merge_kernel,
        out_shape=jax.ShapeDtypeStruct((n, d), BF16),
        grid=(n // tm,),
        in_specs=[
            pl.BlockSpec((tm, o.shape[1]), lambda i: (i, 0)),
            pl.BlockSpec((tm, lru.shape[1]), lambda i: (i, 0)),
            pl.BlockSpec((tm, d), lambda i: (i, Z_MGA // d)),
            pl.BlockSpec((tm, d), lambda i: (i, Z_MGB // d)),
            pl.BlockSpec(wn.shape, lambda i: (0, 0)),
            pl.BlockSpec(wl.shape, lambda i: (0, 0)),
        ],
        out_specs=pl.BlockSpec((tm, d), lambda i: (i, 0)),
        compiler_params=_cparams(("parallel",)),
        name="merge",
    )(o, lru, z, z, wn, wl)


def _out_route_kernel(m_ref, x_ref, wo_ref, g_ref, wr_ref, br_ref, x1_ref, xn_ref, eid_ref, ew_ref):
    x1 = x_ref[...] + jnp.dot(m_ref[...], wo_ref[...], preferred_element_type=F32)
    x1_ref[...] = x1
    xn = _rms(x1, g_ref[...])
    _store_rows(xn_ref, 0, xn)
    lg = lax.dot_general(wr_ref[...], xn, (((1,), (1,)), ((), ())), preferred_element_type=F32,
                         precision=lax.Precision.HIGHEST) + br_ref[...]
    tm = lg.shape[1]
    sub = lax.broadcasted_iota(jnp.int32, (SUBLANES, tm), 0)

    def first_argmax(v, vmax):
        return jnp.min(jnp.where(v == vmax, sub, SUBLANES), axis=0, keepdims=True)

    gl = jnp.where(sub < N_GROUPS, lg[0:SUBLANES], -jnp.inf)
    gmax = jnp.max(gl, axis=0, keepdims=True)
    ge = jnp.exp(gl - gmax)
    gprob = ge / jnp.sum(ge, axis=0, keepdims=True)
    g_val = jnp.max(gprob, axis=0, keepdims=True)
    g_idx = first_argmax(gprob, g_val)
    e_in = jnp.zeros((EXPERTS_PER_GROUP, tm), F32)
    for gi in range(N_GROUPS):
        lo = SUBLANES + gi * EXPERTS_PER_GROUP
        e_in = jnp.where(g_idx == gi, lg[lo:lo + EXPERTS_PER_GROUP], e_in)
    ee = jnp.exp(e_in - jnp.max(e_in, axis=0, keepdims=True))
    eprob = ee / jnp.sum(ee, axis=0, keepdims=True)
    v1 = jnp.max(eprob, axis=0, keepdims=True)
    i1 = first_argmax(eprob, v1)
    rest = jnp.where(sub == i1, -1.0, eprob)
    v2 = jnp.max(rest, axis=0, keepdims=True)
    i2 = first_argmax(rest, v2)
    den = v1 + v2
    eid = jnp.where(sub == 0, g_idx * EXPERTS_PER_GROUP + i1, g_idx * EXPERTS_PER_GROUP + i2)
    eid_ref[...] = eid
    ew_ref[...] = jnp.where(sub == 0, g_val * v1 / den, g_val * v2 / den)


def _out_route(m, x2d, wo, g, wr, br, tm=512):
    n, d = x2d.shape
    nr = wr.shape[0]
    once = pl.Buffered(1)
    return pl.pallas_call(
        _out_route_kernel,
        out_shape=(jax.ShapeDtypeStruct((n, d), F32), jax.ShapeDtypeStruct((n * _row_pitch(d), LANES), F32),
                   jax.ShapeDtypeStruct((SUBLANES, n), jnp.int32), jax.ShapeDtypeStruct((SUBLANES, n), F32)),
        grid=(n // tm,),
        in_specs=[
            pl.BlockSpec((tm, d), lambda i: (i, 0)),
            pl.BlockSpec((tm, d), lambda i: (i, 0)),
            pl.BlockSpec((d, d), lambda i: (0, 0), pipeline_mode=once),
            pl.BlockSpec((1, d), lambda i: (0, 0)),
            pl.BlockSpec((nr, d), lambda i: (0, 0), pipeline_mode=once),
            pl.BlockSpec((nr, 1), lambda i: (0, 0)),
        ],
        out_specs=(pl.BlockSpec((tm, d), lambda i: (i, 0)), pl.BlockSpec((tm * _row_pitch(d), LANES), lambda i: (i, 0)),
                   pl.BlockSpec((SUBLANES, tm), lambda i: (0, i)), pl.BlockSpec((SUBLANES, tm), lambda i: (0, i))),
        compiler_params=_cparams(("parallel",)),
        name="out_route",
    )(m, x2d, wo, g, wr, br)


def _for_rows(cnt, fn, unroll=8):
    sh = unroll.bit_length() - 1

    def group(gidx, c):
        for u in range(unroll):
            fn(gidx * unroll + u)
        return c

    def single(r, c):
        fn(r)
        return c

    lax.fori_loop(0, cnt >> sh, group, 0)
    lax.fori_loop((cnt >> sh) << sh, cnt, single, 0)


def _moe_kernel(nvalid_ref, texp_ref, cnt_ref, tok_ref, tokn_ref, dst_ref, xn_hbm, wg_ref, wu_ref, wd_ref, y_hbm,
                xbuf, ybuf, gsem, ssem):
    i = pl.program_id(0)
    nv = nvalid_ref[0]
    slot = i % 2
    d = wg_ref.shape[1]
    k = _row_pitch(d)
    tm = xbuf.shape[0] // (2 * k)

    def line(ref, row):
        return ref.at[pl.ds(row * k, k), :]

    def gather_row(idx_ref, s):
        def fn(r):
            pltpu.make_async_copy(line(xn_hbm, idx_ref[0, 0, r]), line(xbuf, s * tm + r), gsem.at[s]).start()
        return fn

    def gather_wait_row(s):
        def fn(r):
            pltpu.make_async_copy(line(xn_hbm, 0), line(xbuf, s * tm + r), gsem.at[s]).wait()
        return fn

    def scatter_row(s):
        def fn(r):
            pltpu.make_async_copy(line(ybuf, s * tm + r), line(y_hbm, dst_ref[0, 0, r]), ssem.at[s]).start()
        return fn

    def scatter_wait_row(s):
        def fn(r):
            pltpu.make_async_copy(line(ybuf, s * tm + r), line(y_hbm, 0), ssem.at[s]).wait()
        return fn

    @pl.when(i == 0)
    def _():
        xbuf[...] = jnp.zeros_like(xbuf)
        _for_rows(cnt_ref[0], gather_row(tok_ref, 0))

    @pl.when(i + 1 < nv)
    def _():
        _for_rows(cnt_ref[i + 1], gather_row(tokn_ref, 1 - slot))

    @pl.when(i < nv)
    def _():
        _for_rows(cnt_ref[i], gather_wait_row(slot))
        x = _load_rows(xbuf, slot * tm, tm, d).astype(BF16)
        gp = jnp.dot(x, wg_ref[0], preferred_element_type=F32)
        up = jnp.dot(x, wu_ref[0], preferred_element_type=F32)
        hid = (gp * _sigmoid(gp) * up).astype(BF16)
        y = jnp.dot(hid, wd_ref[0], preferred_element_type=F32)

        @pl.when(i >= 2)
        def _():
            _for_rows(cnt_ref[i - 2], scatter_wait_row(slot))

        _store_rows(ybuf, slot * tm, y)
        _for_rows(cnt_ref[i], scatter_row(slot))

    @pl.when(i == nv - 1)
    def _():
        _for_rows(cnt_ref[i], scatter_wait_row(slot))

        @pl.when(nv >= 2)
        def _():
            _for_rows(cnt_ref[i - 1], scatter_wait_row(1 - slot))


def _moe(nvalid, texp, cnt, tok, dst, xn, wg, wu, wd, n_rows_out):
    nt, _, tm = tok.shape
    d, de = wg.shape[1], wg.shape[2]
    k = _row_pitch(d)
    grid_spec = pltpu.PrefetchScalarGridSpec(
        num_scalar_prefetch=3,
        grid=(nt,),
        in_specs=[
            pl.BlockSpec((1, 1, tm), lambda i, nv, te, ct: (i, 0, 0), memory_space=pltpu.SMEM),
            pl.BlockSpec((1, 1, tm), lambda i, nv, te, ct: (jnp.minimum(i + 1, nt - 1), 0, 0),
                         memory_space=pltpu.SMEM),
            pl.BlockSpec((1, 1, tm), lambda i, nv, te, ct: (i, 0, 0), memory_space=pltpu.SMEM),
            pl.BlockSpec(memory_space=pl.ANY),
            pl.BlockSpec((1, d, de), lambda i, nv, te, ct: (te[i], 0, 0)),
            pl.BlockSpec((1, d, de), lambda i, nv, te, ct: (te[i], 0, 0)),
            pl.BlockSpec((1, de, d), lambda i, nv, te, ct: (te[i], 0, 0)),
        ],
        out_specs=pl.BlockSpec(memory_space=pl.ANY),
        scratch_shapes=[pltpu.VMEM((2 * tm * k, LANES), F32), pltpu.VMEM((2 * tm * k, LANES), F32),
                        pltpu.SemaphoreType.DMA((2,)), pltpu.SemaphoreType.DMA((2,))],
    )
    return pl.pallas_call(
        _moe_kernel,
        out_shape=jax.ShapeDtypeStruct((n_rows_out * k, LANES), F32),
        grid_spec=grid_spec,
        compiler_params=_cparams(("arbitrary",)),
        name="moe",
    )(nvalid, texp, cnt, tok, tok, dst, xn, wg, wu, wd)


def _moe_plan(eid, n, tm):
    e_flat = eid.reshape(-1)
    npairs = e_flat.shape[0]
    experts = jnp.arange(N_EXPERTS, dtype=jnp.int32)
    counts = jnp.sum((e_flat[:, None] == experts[None, :]).astype(jnp.int32), axis=0)
    padded = ((counts + tm - 1) // tm) * tm
    ends = jnp.cumsum(padded)
    offs = ends - padded
    n_rows = npairs + N_EXPERTS * tm
    nt = n_rows // tm
    fill = jnp.arange(n_rows - npairs, dtype=jnp.int32)
    fill_expert = jnp.sum((fill[:, None] >= jnp.cumsum(padded - counts)[None, :]).astype(jnp.int32), axis=1)
    keys = jnp.concatenate([e_flat * (2 * npairs) + jnp.arange(npairs, dtype=jnp.int32),
                            fill_expert * (2 * npairs) + npairs])
    skeys = jnp.sort(keys)
    low = skeys % (2 * npairs)
    dst = jnp.where(low < npairs, low, 0)
    tok = dst % n
    tile_start = jnp.arange(nt, dtype=jnp.int32) * tm
    texp = jnp.minimum(jnp.sum((tile_start[:, None] >= ends[None, :]).astype(jnp.int32), axis=1), N_EXPERTS - 1)
    cnt = jnp.clip(offs[texp] + counts[texp] - tile_start, 0, tm)
    nvalid = (ends[-1] // tm).astype(jnp.int32).reshape(1)
    return (nvalid, texp.astype(jnp.int32), cnt.astype(jnp.int32), tok.reshape(nt, 1, tm), dst.reshape(nt, 1, tm),
            npairs)


def _ple_final_kernel(x1_ref, y0_ref, y1_ref, w_ref, p_ref, gp_ref, wpg_ref, wp_ref, gf_ref, o_ref):
    w = w_ref[...]
    tm, d = x1_ref.shape
    x2 = x1_ref[...] + w[:, 0:1] * _load_rows(y0_ref, 0, tm, d) + w[:, 1:2] * _load_rows(y1_ref, 0, tm, d)
    hn = _rms(x2, gp_ref[...]).astype(BF16)
    gate = _sigmoid(jnp.dot(hn, wpg_ref[...], preferred_element_type=F32))
    pe = jnp.dot(p_ref[...].astype(BF16), wp_ref[...], preferred_element_type=F32)
    x3 = x2 + gate * pe
    o_ref[...] = _rms(x3, gf_ref[...])


def _ple_final(x1, ypairs, wcols, p2d, gp, wpg, wp, gf, tm=256):
    n, d = x1.shape
    nb = n // tm
    k = _row_pitch(d)
    return pl.pallas_call(
        _ple_final_kernel,
        out_shape=jax.ShapeDtypeStruct((n, d), F32),
        grid=(nb,),
        in_specs=[
            pl.BlockSpec((tm, d), lambda i: (i, 0)),
            pl.BlockSpec((tm * k, LANES), lambda i: (i, 0)),
            pl.BlockSpec((tm * k, LANES), lambda i: (nb + i, 0)),
            pl.BlockSpec((tm, wcols.shape[1]), lambda i: (i, 0)),
            pl.BlockSpec((tm, p2d.shape[1]), lambda i: (i, 0)),
            pl.BlockSpec((1, d), lambda i: (0, 0)),
            pl.BlockSpec((d, d), lambda i: (0, 0)),
            pl.BlockSpec(wp.shape, lambda i: (0, 0)),
            pl.BlockSpec((1, d), lambda i: (0, 0)),
        ],
        out_specs=pl.BlockSpec((tm, d), lambda i: (i, 0)),
        compiler_params=_cparams(("parallel",)),
        name="ple_final",
    )(x1, ypairs, ypairs, wcols, p2d, gp, wpg, wp, gf)


def _block_diag(w, per):
    nb, bw, _ = w.shape
    w = w.reshape(nb // per, per, bw, bw)
    eye = jnp.eye(per, dtype=w.dtype)
    return jnp.einsum("cpij,pq->cpiqj", w, eye).reshape(nb // per, per * bw, per * bw)


def _regroup_kernel(wt_ref, o_ref):
    d = Z_MGB - Z_MGA
    c = np.cumsum((0, NSA_Q_WIDTH) + (NSA_KV_WIDTH,) * 6 + (3 * NSA_HEADS, LRU_WIDTH, LRU_WIDTH, d, d))
    seg = lambda k: wt_ref[int(c[k]):int(c[k + 1]), :]
    q, k_c, v_c, k_s, v_s, k_w, v_w, gates, lru_x, lru_y, mg_a, mg_b = (seg(k) for k in range(12))

    def pair(a, b):
        parts = []
        for g in range(NSA_KV_GROUPS):
            parts += [a[g * HEAD_DIM:(g + 1) * HEAD_DIM], b[g * HEAD_DIM:(g + 1) * HEAD_DIM]]
        return jnp.concatenate(parts, axis=0)

    pad = jnp.zeros((Z_WIDTH - int(c[-1]), wt_ref.shape[1]), wt_ref.dtype)
    rows = [mg_a, mg_b, q, pair(k_s, k_w), pair(v_s, v_w), k_c, v_c, lru_x, lru_y, gates, pad]
    o_ref[...] = jnp.concatenate(rows, axis=0).T.astype(o_ref.dtype)


def _regroup_w_in(w_in_t, tr=256):
    nw, d = w_in_t.shape
    return pl.pallas_call(
        _regroup_kernel,
        out_shape=jax.ShapeDtypeStruct((d, Z_WIDTH), BF16),
        grid=(d // tr,),
        in_specs=[pl.BlockSpec((nw, tr), lambda i: (0, i))],
        out_specs=pl.BlockSpec((tr, Z_WIDTH), lambda i: (i, 0)),
        compiler_params=_cparams(("parallel",)),
        name="regroup_w_in",
    )(w_in_t)


def _layer(x2d, batch, seq, ln_mix, w_in, cmp_k_pos, cmp_k_w1, cmp_k_w2, cmp_v_pos, cmp_v_w1, cmp_v_w2, conv_w,
           conv_b, lru_wa, lru_ba, lru_wx, lru_bx, lru_lambda, w_nsa_up, w_lru_up, w_out, ln_ffn, w_grp, b_grp, w_exp,
           b_exp, w_gate, w_up, w_down):
    n, d = x2d.shape
    g, hd = NSA_KV_GROUPS, HEAD_DIM

    z = _in_proj(x2d, ln_mix.reshape(1, d), _regroup_w_in(w_in.T))

    pos8 = lambda pos: jnp.broadcast_to(pos.reshape(1, -1), (SUBLANES, pos.size)).astype(BF16)
    kc, vct = _compress(z, batch, seq, pos8(cmp_k_pos), cmp_k_w1.astype(BF16), cmp_k_w2.astype(BF16),
                        pos8(cmp_v_pos), cmp_v_w1.astype(BF16), cmp_v_w2.T.astype(BF16))
    o = _nsa_attention(z, kc, vct, batch, seq)

    per = LRU_WT // LRU_BW
    lru = _rglru(z, batch, seq, conv_w, conv_b.reshape(1, -1), _block_diag(lru_wa, per).astype(BF16),
                 lru_ba.reshape(1, -1), _block_diag(lru_wx, per).astype(BF16), lru_bx.reshape(1, -1),
                 lru_lambda.reshape(1, -1))

    merged = _merge(o, lru, z, w_nsa_up.astype(BF16), w_lru_up.astype(BF16))

    nr = SUBLANES + N_EXPERTS
    wr = jnp.zeros((nr, d), F32).at[0:N_GROUPS].set(w_grp.T).at[SUBLANES:].set(w_exp.T)
    br = jnp.zeros((nr, 1), F32).at[0:N_GROUPS, 0].set(b_grp).at[SUBLANES:, 0].set(b_exp)
    x1, xn, eid, ew = _out_route(merged, x2d, w_out.astype(BF16), ln_ffn.reshape(1, d), wr, br)

    nvalid, texp, cnt, tok, dst, n_rows_out = _moe_plan(eid[0:EXPERT_TOPK], n, MOE_TM)
    ypairs = _moe(nvalid, texp, cnt, tok, dst, xn, w_gate, w_up, w_down, n_rows_out)

    return x1, ypairs, ew.T


def kernel(x, p, ln_mix, w_in, cmp_k_pos, cmp_k_w1, cmp_k_w2, cmp_v_pos, cmp_v_w1, cmp_v_w2, conv_w, conv_b, lru_wa, lru_ba, lru_wx, lru_bx, lru_lambda, w_nsa_up, w_lru_up, w_out, ln_ffn, w_grp, b_grp, w_exp, b_exp, w_gate, w_up, w_down, ln_ple, w_ple, w_ple_gate, ln_final):
    batch, seq, d = x.shape
    assert p.shape[0] == 1, "the final norm is fused into the (single) layer's last kernel"
    n = batch * seq
    x1, ypairs, wcols = _layer(
        x.reshape(n, d), batch, seq, ln_mix[0], w_in[0], cmp_k_pos[0], cmp_k_w1[0], cmp_k_w2[0],
        cmp_v_pos[0], cmp_v_w1[0], cmp_v_w2[0], conv_w[0], conv_b[0], lru_wa[0], lru_ba[0], lru_wx[0], lru_bx[0],
        lru_lambda[0], w_nsa_up[0], w_lru_up[0], w_out[0], ln_ffn[0], w_grp[0], b_grp[0], w_exp[0], b_exp[0],
        w_gate[0], w_up[0], w_down[0])
    out = _ple_final(x1, ypairs, wcols, p[0].reshape(n, -1), ln_ple[0].reshape(1, d),
                     w_ple_gate[0].astype(BF16), w_ple[0].astype(BF16), ln_final.reshape(1, d))
    return out.reshape(batch, seq, d)
```

```python
import numpy as np
import jax
import jax.numpy as jnp
from jax import lax
from jax.experimental import pallas as pl
from jax.experimental.pallas import tpu as pltpu

F32 = jnp.float32
BF16 = jnp.bfloat16

NSA_HEADS = 16
NSA_KV_GROUPS = 4
NSA_HPG = NSA_HEADS // NSA_KV_GROUPS
HEAD_DIM = 64
NSA_Q_WIDTH = NSA_HEADS * HEAD_DIM
NSA_KV_WIDTH = NSA_KV_GROUPS * HEAD_DIM
CMP_BLOCK = 32
CMP_STRIDE = 16
CMP_HIDDEN = 2 * HEAD_DIM
SLC_BLOCK = 64
SLC_TOPN = 16
WINDOW = 512
ATTN_SCALE = HEAD_DIM ** -0.5
NEG_INF = -1e30
LOG2E = float(np.log2(np.e))
FORCE_SCORE = 1e4
LRU_WIDTH = 1024
LRU_BLOCKS = 16
LRU_BW = LRU_WIDTH // LRU_BLOCKS
CONV_WIDTH = 4
LRU_C = 8.0
N_GROUPS = 4
EXPERTS_PER_GROUP = 8
N_EXPERTS = N_GROUPS * EXPERTS_PER_GROUP
EXPERT_TOPK = 2
D_EXPERT = 512
EPS = 1e-6

LANES = 128
SUBLANES = 8
VMEM_LIMIT_BYTES = 56 * 1024 * 1024

Z_MGA = 0
Z_MGB = 2048
Z_Q = 4096
Z_KK = 5120
Z_VV = 5632
Z_KC = 6144
Z_VC = 6400
Z_LX = 6656
Z_LY = 7680
Z_GATE = 8704
Z_WIDTH = 9216
LRU_CB = 512
LRU_WT = 256
MOE_TM = 256
NSA_TQ = 256
NSA_CK = 512


def _cparams(sem, vmem=VMEM_LIMIT_BYTES):
    return pltpu.CompilerParams(dimension_semantics=sem, vmem_limit_bytes=vmem)


def _rms(x, g):
    return x * lax.rsqrt(jnp.mean(x * x, axis=-1, keepdims=True) + EPS) * g


def _gelu_tanh(x):
    return 0.5 * x * (1.0 + jnp.tanh(np.sqrt(2.0 / np.pi) * (x + 0.044715 * (x * x * x))))


def _sigmoid(x):
    return 1.0 / (1.0 + jnp.exp(-x))


def _row_pitch(d):
    k = d // LANES
    return k + 1 - (k % 2)


def _store_rows(ref, row0, val):
    rows, d = val.shape
    pitch = _row_pitch(d)
    for j in range(d // LANES):
        ref[pl.ds(row0 * pitch + j, rows, stride=pitch), :] = val[:, j * LANES:(j + 1) * LANES]
    for j in range(d // LANES, pitch):
        ref[pl.ds(row0 * pitch + j, rows, stride=pitch), :] = jnp.zeros((rows, LANES), val.dtype)


def _load_rows(ref, row0, rows, d):
    pitch = _row_pitch(d)
    return jnp.concatenate([ref[pl.ds(row0 * pitch + j, rows, stride=pitch), :] for j in range(d // LANES)], axis=1)


def _in_proj_kernel(x_ref, g_ref, w_ref, o_ref, h_ref):
    @pl.when(pl.program_id(1) == 0)
    def _():
        h_ref[...] = _rms(x_ref[...], g_ref[...]).astype(BF16)

    o_ref[...] = jnp.dot(h_ref[...], w_ref[...], preferred_element_type=F32).astype(o_ref.dtype)


def _in_proj(x2d, g, w, tm=1024, tn=1024):
    n, d = x2d.shape
    nw = w.shape[1]
    return pl.pallas_call(
        _in_proj_kernel,
        out_shape=jax.ShapeDtypeStruct((n, nw), BF16),
        grid=(n // tm, nw // tn),
        in_specs=[
            pl.BlockSpec((tm, d), lambda i, j: (i, 0)),
            pl.BlockSpec((1, d), lambda i, j: (0, 0)),
            pl.BlockSpec((d, tn), lambda i, j: (0, j)),
        ],
        out_specs=pl.BlockSpec((tm, tn), lambda i, j: (i, j)),
        scratch_shapes=[pltpu.VMEM((tm, d), BF16)],
        compiler_params=_cparams(("parallel", "arbitrary")),
        name="in_proj",
    )(x2d, g, w)


def _compress_kernel(zk_ref, zv_ref, kpos_ref, kw1_ref, kw2_ref, vpos_ref, vw1_ref, vw2t_ref, kc_ref, vct_ref,
                     xf_ref):
    seq = zk_ref.shape[0]
    ncp = seq // CMP_STRIDE
    hd = HEAD_DIM
    half = CMP_STRIDE * hd

    def half_windows(z_ref):
        xf_ref[...] = z_ref[...].astype(F32)
        lines = [xf_ref[pl.ds(r, ncp, stride=CMP_STRIDE), :].astype(BF16) for r in range(CMP_STRIDE)]
        return [jnp.concatenate([ln[:, gg * hd:(gg + 1) * hd] for ln in lines], axis=1) for gg in range(LANES // hd)]

    def hidden(x, pos_ref, w1_ref):
        w1 = w1_ref[...]
        ha = jnp.dot(x, w1[:half], preferred_element_type=F32)
        hb = jnp.dot(x, w1[half:], preferred_element_type=F32)
        hb = pltpu.roll(hb, hb.shape[0] - 1, axis=0)
        pc = jnp.dot(pos_ref[...], w1, preferred_element_type=F32)[0:1]
        return _gelu_tanh(ha + hb + pc).astype(BF16)

    for gg, x in enumerate(half_windows(zk_ref)):
        hk = hidden(x, kpos_ref, kw1_ref)
        kc_ref[0, gg] = jnp.dot(hk, kw2_ref[...], preferred_element_type=F32).astype(kc_ref.dtype)
    for gg, x in enumerate(half_windows(zv_ref)):
        hv = hidden(x, vpos_ref, vw1_ref)
        vct_ref[0, gg] = lax.dot_general(vw2t_ref[...], hv, (((1,), (1,)), ((), ())),
                                         preferred_element_type=F32).astype(vct_ref.dtype)


def _compress(z, batch, seq, kpos, kw1, kw2, vpos, vw1, vw2t):
    g, hd = NSA_KV_GROUPS, HEAD_DIM
    gpb = LANES // hd
    ncp = seq // CMP_STRIDE
    full = lambda a: pl.BlockSpec(a.shape, lambda i, j: (0,) * a.ndim)
    return pl.pallas_call(
        _compress_kernel,
        out_shape=(jax.ShapeDtypeStruct((batch, g, ncp, hd), BF16),
                   jax.ShapeDtypeStruct((batch, g, hd, ncp), BF16)),
        grid=(batch, g // gpb),
        in_specs=[
            pl.BlockSpec((seq, LANES), lambda i, j: (i, Z_KC // LANES + j)),
            pl.BlockSpec((seq, LANES), lambda i, j: (i, Z_VC // LANES + j)),
            full(kpos), full(kw1), full(kw2), full(vpos), full(vw1), full(vw2t),
        ],
        out_specs=(pl.BlockSpec((1, gpb, ncp, hd), lambda i, j: (i, j, 0, 0)),
                   pl.BlockSpec((1, gpb, hd, ncp), lambda i, j: (i, j, 0, 0))),
        scratch_shapes=[pltpu.VMEM((seq, LANES), F32)],
        compiler_params=_cparams(("parallel", "parallel")),
        name="compress",
    )(z, z, kpos, kw1, kw2, vpos, vw1, vw2t)


def _nsa_kernel(zq_ref, zkk_ref, zvv_ref, zg_ref, kc_ref, vct_ref, o_ref, kcomb_ref, vs_ref, vw_ref, score_ref, gate_ref,
                sa_ref, sb_ref):
    grp = pl.program_id(1)
    qi = pl.program_id(2)
    tq, hd, hpg = NSA_TQ, HEAD_DIM, NSA_HPG
    nq = tq * hpg
    seq = kcomb_ref.shape[0]
    ncp = kc_ref.shape[2]
    nsb = score_ref.shape[0]
    blk_shift = SLC_BLOCK.bit_length() - 1

    @pl.when(qi == 0)
    def _():
        kcomb_ref[:, 0:2 * hd] = zkk_ref[...]
        blk_of_row = lax.broadcasted_iota(jnp.int32, (seq, 2 * hd), 0) >> blk_shift
        lane = lax.broadcasted_iota(jnp.int32, (seq, 2 * hd), 1)
        kcomb_ref[:, 2 * hd:] = jnp.where(blk_of_row == lane, 1.0, 0.0).astype(BF16)
        ones = jnp.ones((seq, hd), BF16)
        vs_ref[...] = jnp.concatenate([zvv_ref[:, 0:hd], ones], axis=1)
        vw_ref[...] = jnp.concatenate([zvv_ref[:, hd:2 * hd], ones], axis=1)

    qt = zq_ref[...].T
    q = jnp.concatenate([qt[h * hd:(h + 1) * hd, :] for h in range(hpg)], axis=1)
    q = (q.astype(F32) * (ATTN_SCALE * LOG2E)).astype(BF16)
    zero = jnp.zeros((hd, nq), BF16)
    t_lane = qi * tq + (lax.broadcasted_iota(jnp.int32, (1, nq), 1) & (tq - 1))
    tn_dims = (((0,), (0,)), ((), ()))

    sc = jnp.dot(kc_ref[0, 0], q, preferred_element_type=F32)
    cmp_end = lax.broadcasted_iota(jnp.int32, (ncp, 1), 0) * CMP_STRIDE + (CMP_BLOCK - 1)
    cmask = cmp_end <= t_lane
    sc = jnp.where(cmask, sc, NEG_INF)
    pc = jnp.where(cmask, jnp.exp2(sc - jnp.max(sc, axis=0, keepdims=True)), 0.0)
    lc = jnp.sum(pc, axis=0, keepdims=True)
    pc = pc * (1.0 / jnp.where(lc > 0.0, lc, 1.0))
    o_cmp = jnp.dot(vct_ref[0, 0], pc.astype(BF16), preferred_element_type=F32)

    nwc = WINDOW // tq + 1
    t_lo = t_lane - WINDOW
    ks_w, vs_w = [], []
    for i in range(nwc):
        rows = pl.ds(pl.multiple_of(jnp.maximum(qi - (nwc - 1) + i, 0) * tq, tq), tq)
        ks_w.append(kcomb_ref[rows, 0:2 * hd])
        vs_w.append(vw_ref[rows, :])
    qw = jnp.concatenate([zero, q], axis=0)
    sw = jnp.dot(jnp.concatenate(ks_w, axis=0), qw, preferred_element_type=F32)
    sw_parts = []
    for i in range(nwc):
        spos = (qi - (nwc - 1) + i) * tq + lax.broadcasted_iota(jnp.int32, (tq, 1), 0)
        if i < nwc - 1:
            ok = jnp.where(spos >= 0, spos, -(1 << 24)) > t_lo
        else:
            ok = spos <= t_lane
        sw_parts.append(jnp.where(ok, sw[i * tq:(i + 1) * tq], NEG_INF))
    sw = jnp.concatenate(sw_parts, axis=0)
    pw = jnp.exp2(sw - jnp.max(sw, axis=0, keepdims=True))
    ow = lax.dot_general(jnp.concatenate(vs_w, axis=0), pw.astype(BF16), tn_dims, preferred_element_type=F32)
    o_win = ow[0:hd] * (1.0 / ow[hd:hd + 1])

    psum = pc[:, 0:tq]
    for h in range(1, hpg):
        psum = psum + pc[:, h * tq:(h + 1) * tq]
    jrow = lax.broadcasted_iota(jnp.int32, (nsb, ncp), 0)
    crel = lax.broadcasted_iota(jnp.int32, (nsb, ncp), 1) - jrow * (SLC_BLOCK // CMP_STRIDE)
    ovt = jnp.where((crel >= 0) & (crel <= 2), 1.0, jnp.where((crel == -1) | (crel == 3), 0.5, 0.0)).astype(F32)
    imp = jnp.dot(ovt, psum, preferred_element_type=F32, precision=lax.Precision.HIGHEST)
    jblk = lax.broadcasted_iota(jnp.int32, (nsb, tq), 0)
    blk = (qi * tq + lax.broadcasted_iota(jnp.int32, (1, tq), 1)) >> blk_shift
    forced = (jblk == 0) | (jblk == blk) | (jblk == blk - 1)
    score = jnp.where(forced, FORCE_SCORE, jnp.where(jblk <= blk, imp, -1.0))
    score_ref[...] = score

    def rank_one(k, cnt):
        row = score_ref[pl.ds(k, 1), :]
        ge = jnp.where(row >= score, 1.0, 0.0)
        gt = jnp.where(row > score, 1.0, 0.0)
        return cnt + jnp.where(jblk > k, ge, gt)

    bpt = tq // SLC_BLOCK

    def rank_group(i, cnt):
        for u in range(bpt):
            cnt = rank_one(bpt * i + u, cnt)
        return cnt

    n_vis = jnp.minimum((qi + 1) * bpt, nsb)
    rank = lax.fori_loop(0, n_vis // bpt, rank_group, jnp.zeros((nsb, tq), F32))
    selb = jnp.where(rank < float(min(SLC_TOPN, nsb)), 0.0, NEG_INF).astype(BF16)

    ck = NSA_CK
    pad = jnp.zeros((2 * hd - nsb, nq), BF16)
    qa = jnp.concatenate([q, zero, jnp.concatenate([selb] * hpg, axis=1), pad], axis=0)

    def slc_scores(c):
        return jnp.dot(kcomb_ref[pl.ds(pl.multiple_of(c * ck, ck), ck), :], qa, preferred_element_type=F32)

    n_full = (qi * tq) // ck

    def slc_update(buf, c, carry, causal):
        def scores():
            if not causal:
                return buf[...]
            return jnp.where(c * ck + lax.broadcasted_iota(jnp.int32, (ck, 1), 0) <= t_lane, buf[...], NEG_INF)

        m, l, acc = carry
        m_new = jnp.maximum(m, jnp.max(scores(), axis=0, keepdims=True))
        alpha = jnp.exp2(m - m_new)
        p = jnp.exp2(scores() - m_new).astype(BF16)
        v = vs_ref[pl.ds(pl.multiple_of(c * ck, ck), ck), :]
        pv = lax.dot_general(v, p, tn_dims, preferred_element_type=F32)
        return m_new, alpha * l + pv[hd:hd + 1], acc * alpha + pv[0:hd]

    def slc_pair(i, carry):
        sb_ref[...] = slc_scores(2 * i + 1)
        carry = slc_update(sa_ref, 2 * i, carry, False)
        sa_ref[...] = slc_scores(2 * i + 2)
        return slc_update(sb_ref, 2 * i + 1, carry, False)

    def tail_two(carry):
        sb_ref[...] = slc_scores(n_full)
        return slc_update(sb_ref, n_full, slc_update(sa_ref, n_full - 1, carry, False), True)

    def tail_one(carry):
        return slc_update(sa_ref, n_full, carry, True)

    sa_ref[...] = slc_scores(0)
    init = (jnp.full((1, nq), NEG_INF, F32), jnp.zeros((1, nq), F32), jnp.zeros((hd, nq), F32))
    carry = lax.fori_loop(0, n_full // 2, slc_pair, init)
    _, l_s, acc_s = lax.cond(n_full % 2 == 1, tail_two, tail_one, carry)
    o_slc = acc_s * (1.0 / l_s)

    gate_ref[...] = _sigmoid(zg_ref[...].astype(F32)).T

    def branch_gate(br):
        rows = [gate_ref[pl.ds(grp * (hpg * 3) + h * 3 + br, 1), :] for h in range(hpg)]
        return jnp.concatenate(rows, axis=1)

    o = branch_gate(0) * o_cmp + branch_gate(1) * o_slc + branch_gate(2) * o_win
    o_heads = jnp.concatenate([o[:, h * tq:(h + 1) * tq] for h in range(hpg)], axis=0)
    o_ref[...] = o_heads.T.astype(o_ref.dtype)


def _nsa_attention(z, kc, vct, batch, seq):
    n = z.shape[0]
    g, hd, tq = NSA_KV_GROUPS, HEAD_DIM, NSA_TQ
    nqt = seq // tq
    ncp = kc.shape[2]
    nsb = seq // SLC_BLOCK
    gw = NSA_HPG * hd
    assert seq % NSA_CK == 0 and NSA_CK % tq == 0 and WINDOW % tq == 0 and nsb <= 2 * hd
    tile = lambda i, j, k: i * nqt + k
    return pl.pallas_call(
        _nsa_kernel,
        out_shape=jax.ShapeDtypeStruct((n, NSA_Q_WIDTH), BF16),
        grid=(batch, g, nqt),
        in_specs=[
            pl.BlockSpec((tq, gw), lambda i, j, k: (tile(i, j, k), Z_Q // gw + j)),
            pl.BlockSpec((seq, 2 * hd), lambda i, j, k: (i, Z_KK // (2 * hd) + j)),
            pl.BlockSpec((seq, 2 * hd), lambda i, j, k: (i, Z_VV // (2 * hd) + j)),
            pl.BlockSpec((tq, LANES), lambda i, j, k: (tile(i, j, k), Z_GATE // LANES)),
            pl.BlockSpec((1, 1, ncp, hd), lambda i, j, k: (i, j, 0, 0)),
            pl.BlockSpec((1, 1, hd, ncp), lambda i, j, k: (i, j, 0, 0)),
        ],
        out_specs=pl.BlockSpec((tq, gw), lambda i, j, k: (tile(i, j, k), j)),
        scratch_shapes=[pltpu.VMEM((seq, 4 * hd), BF16), pltpu.VMEM((seq, 2 * hd), BF16),
                        pltpu.VMEM((seq, 2 * hd), BF16), pltpu.VMEM((nsb, tq), F32),
                        pltpu.VMEM((LANES, tq), F32), pltpu.VMEM((NSA_CK, NSA_HPG * tq), F32),
                        pltpu.VMEM((NSA_CK, NSA_HPG * tq), F32)],
        compiler_params=_cparams(("parallel", "parallel", "arbitrary")),
        name="nsa_attn",
    )(z, z, z, z, kc, vct)


def _rglru_kernel(x_ref, y_ref, cw_ref, cb_ref, wa_ref, ba_ref, wx_ref, bx_ref, lam_ref, o_ref,
                  tail_ref, h_ref, a_ref, u_ref):
    tc = pl.program_id(2)
    tt, cb = x_ref.shape

    @pl.when(tc == 0)
    def _():
        tail_ref[...] = jnp.zeros_like(tail_ref)
        h_ref[...] = jnp.zeros_like(h_ref)

    x = x_ref[...].astype(F32)
    xe = jnp.concatenate([tail_ref[...], x], axis=0)
    tail_ref[...] = x[tt - SUBLANES:, :]
    cw = cw_ref[...]
    xc = cb_ref[...]
    for k in range(CONV_WIDTH):
        off = SUBLANES - (CONV_WIDTH - 1) + k
        xc = xc + cw[k:k + 1, :] * xe[off:off + tt, :]
    xcb = xc.astype(BF16)
    def gate(w_ref, b_ref):
        parts = [jnp.dot(xcb[:, c * LRU_WT:(c + 1) * LRU_WT], w_ref[c], preferred_element_type=F32)
                 for c in range(cb // LRU_WT)]
        return _sigmoid(jnp.concatenate(parts, axis=1) + b_ref[...])

    r = gate(wa_ref, ba_ref)
    ig = gate(wx_ref, bx_ref)
    nl = -lam_ref[...]
    softplus = jnp.maximum(nl, 0.0) + jnp.log1p(jnp.exp(-jnp.abs(nl)))
    log_a = (-LRU_C) * softplus * r
    a_ref[...] = jnp.exp(log_a)
    th = jnp.tanh(log_a)
    u_ref[...] = jnp.sqrt(-2.0 * th / (1.0 - th)) * (ig * xc)

    row = lax.broadcasted_iota(jnp.int32, (SUBLANES, cb), 0)

    def step(i, h):
        sl = pl.ds(pl.multiple_of(i * SUBLANES, SUBLANES), SUBLANES)
        a = a_ref[sl, :]
        u = u_ref[sl, :]
        for s in (1, 2, 4):
            a_s = jnp.where(row >= s, pltpu.roll(a, s, axis=0), 1.0)
            u_s = jnp.where(row >= s, pltpu.roll(u, s, axis=0), 0.0)
            u = a * u_s + u
            a = a * a_s
        hrows = a * h + u
        u_ref[sl, :] = hrows
        return hrows[SUBLANES - 1:SUBLANES, :]

    h_ref[...] = lax.fori_loop(0, tt // SUBLANES, step, h_ref[...], unroll=8)
    o_ref[...] = (u_ref[...] * _gelu_tanh(y_ref[...].astype(F32))).astype(o_ref.dtype)


def _rglru(z, batch, seq, cw, cbias, wa_bd, ba, wx_bd, bx, lam, tt=512):
    n = z.shape[0]
    ncb = LRU_WIDTH // LRU_CB
    nt = seq // tt
    row = lambda i, j, k: i * nt + k
    vec = lambda r: pl.BlockSpec((r, LRU_CB), lambda i, j, k: (0, j))
    return pl.pallas_call(
        _rglru_kernel,
        out_shape=jax.ShapeDtypeStruct((n, LRU_WIDTH), BF16),
        grid=(batch, ncb, nt),
        in_specs=[
            pl.BlockSpec((tt, LRU_CB), lambda i, j, k: (row(i, j, k), Z_LX // LRU_CB + j)),
            pl.BlockSpec((tt, LRU_CB), lambda i, j, k: (row(i, j, k), Z_LY // LRU_CB + j)),
            vec(CONV_WIDTH), vec(1),
            pl.BlockSpec((LRU_CB // LRU_WT, LRU_WT, LRU_WT), lambda i, j, k: (j, 0, 0)), vec(1),
            pl.BlockSpec((LRU_CB // LRU_WT, LRU_WT, LRU_WT), lambda i, j, k: (j, 0, 0)), vec(1),
            vec(1),
        ],
        out_specs=pl.BlockSpec((tt, LRU_CB), lambda i, j, k: (row(i, j, k), j)),
        scratch_shapes=[pltpu.VMEM((SUBLANES, LRU_CB), F32), pltpu.VMEM((1, LRU_CB), F32),
                        pltpu.VMEM((tt, LRU_CB), F32), pltpu.VMEM((tt, LRU_CB), F32)],
        compiler_params=_cparams(("parallel", "parallel", "arbitrary")),
        name="rglru",
    )(z, z, cw, cbias, wa_bd, ba, wx_bd, bx, lam)


def _merge_kernel(o_ref, l_ref, mga_ref, mgb_ref, wn_ref, wl_ref, m_ref):
    ya = jnp.dot(o_ref[...], wn_ref[...], preferred_element_type=F32)
    yb = jnp.dot(l_ref[...], wl_ref[...], preferred_element_type=F32)
    m = _sigmoid(mga_ref[...].astype(F32)) * ya + _sigmoid(mgb_ref[...].astype(F32)) * yb
    m_ref[...] = m.astype(m_ref.dtype)


def _merge(o, lru, z, wn, wl, tm=512):
    n, d = o.shape[0], wn.shape[1]
    return pl.pallas_call(
        _merge_kernel,
        out_shape=jax.ShapeDtypeStruct((n, d), BF16),
        grid=(n // tm,),
        in_specs=[
            pl.BlockSpec((tm, o.shape[1]), lambda i: (i, 0)),
            pl.BlockSpec((tm, lru.shape[1]), lambda i: (i, 0)),
            pl.BlockSpec((tm, d), lambda i: (i, Z_MGA // d)),
            pl.BlockSpec((tm, d), lambda i: (i, Z_MGB // d)),
            pl.BlockSpec(wn.shape, lambda i: (0, 0)),
            pl.BlockSpec(wl.shape, lambda i: (0, 0)),
        ],
        out_specs=pl.BlockSpec((tm, d), lambda i: (i, 0)),
        compiler_params=_cparams(("parallel",)),
        name="merge",
    )(o, lru, z, z, wn, wl)


def _out_route_kernel(m_ref, x_ref, wo_ref, g_ref, wr_ref, br_ref, x1_ref, xn_ref, eid_ref, ew_ref):
    x1 = x_ref[...] + jnp.dot(m_ref[...], wo_ref[...], preferred_element_type=F32)
    x1_ref[...] = x1
    xn = _rms(x1, g_ref[...])
    _store_rows(xn_ref, 0, xn)
    lg = lax.dot_general(wr_ref[...], xn, (((1,), (1,)), ((), ())), preferred_element_type=F32,
                         precision=lax.Precision.HIGHEST) + br_ref[...]
    tm = lg.shape[1]
    sub = lax.broadcasted_iota(jnp.int32, (SUBLANES, tm), 0)

    def first_argmax(v, vmax):
        return jnp.min(jnp.where(v == vmax, sub, SUBLANES), axis=0, keepdims=True)

    gl = jnp.where(sub < N_GROUPS, lg[0:SUBLANES], -jnp.inf)
    gmax = jnp.max(gl, axis=0, keepdims=True)
    ge = jnp.exp(gl - gmax)
    gprob = ge / jnp.sum(ge, axis=0, keepdims=True)
    g_val = jnp.max(gprob, axis=0, keepdims=True)
    g_idx = first_argmax(gprob, g_val)
    e_in = jnp.zeros((EXPERTS_PER_GROUP, tm), F32)
    for gi in range(N_GROUPS):
        lo = SUBLANES + gi * EXPERTS_PER_GROUP
        e_in = jnp.where(g_idx == gi, lg[lo:lo + EXPERTS_PER_GROUP], e_in)
    ee = jnp.exp(e_in - jnp.max(e_in, axis=0, keepdims=True))
    eprob = ee / jnp.sum(ee, axis=0, keepdims=True)
    v1 = jnp.max(eprob, axis=0, keepdims=True)
    i1 = first_argmax(eprob, v1)
    rest = jnp.where(sub == i1, -1.0, eprob)
    v2 = jnp.max(rest, axis=0, keepdims=True)
    i2 = first_argmax(rest, v2)
    den = v1 + v2
    eid = jnp.where(sub == 0, g_idx * EXPERTS_PER_GROUP + i1, g_idx * EXPERTS_PER_GROUP + i2)
    eid_ref[...] = eid
    ew_ref[...] = jnp.where(sub == 0, g_val * v1 / den, g_val * v2 / den)


def _out_route(m, x2d, wo, g, wr, br, tm=512):
    n, d = x2d.shape
    nr = wr.shape[0]
    once = pl.Buffered(1)
    return pl.pallas_call(
        _out_route_kernel,
        out_shape=(jax.ShapeDtypeStruct((n, d), F32), jax.ShapeDtypeStruct((n * _row_pitch(d), LANES), F32),
                   jax.ShapeDtypeStruct((SUBLANES, n), jnp.int32), jax.ShapeDtypeStruct((SUBLANES, n), F32)),
        grid=(n // tm,),
        in_specs=[
            pl.BlockSpec((tm, d), lambda i: (i, 0)),
            pl.BlockSpec((tm, d), lambda i: (i, 0)),
            pl.BlockSpec((d, d), lambda i: (0, 0), pipeline_mode=once),
            pl.BlockSpec((1, d), lambda i: (0, 0)),
            pl.BlockSpec((nr, d), lambda i: (0, 0), pipeline_mode=once),
            pl.BlockSpec((nr, 1), lambda i: (0, 0)),
        ],
        out_specs=(pl.BlockSpec((tm, d), lambda i: (i, 0)), pl.BlockSpec((tm * _row_pitch(d), LANES), lambda i: (i, 0)),
                   pl.BlockSpec((SUBLANES, tm), lambda i: (0, i)), pl.BlockSpec((SUBLANES, tm), lambda i: (0, i))),
        compiler_params=_cparams(("parallel",)),
        name="out_route",
    )(m, x2d, wo, g, wr, br)


def _moe_kernel(nvalid_ref, texp_ref, cnt_ref, tok_ref, tokn_ref, dstp_ref, dstc_ref, xn_hbm, wg_ref, wu_ref, wd_ref,
                y_hbm, xbuf, ybuf, gsem, ssem, fsem):
    i = pl.program_id(0)
    nv = nvalid_ref[0]
    slot = i % 2
    other = 1 - slot
    d = wg_ref.shape[1]
    k = _row_pitch(d)
    tm = xbuf.shape[0] // (2 * k)
    trash0 = y_hbm.shape[0] // k - 2 * tm

    def line(ref, row):
        return ref.at[pl.ds(row * k, k), :]

    def fetch_tile(idx_ref, s):
        for r in range(tm):
            pltpu.make_async_copy(line(xn_hbm, idx_ref[0, 0, r]), line(xbuf, s * tm + r), gsem.at[s]).start()

    def fetch_wait(s):
        for r in range(tm):
            pltpu.make_async_copy(line(xn_hbm, 0), line(xbuf, s * tm + r), gsem.at[s]).wait()

    def send_tile(dst_ref, n_real, s):
        for r in range(tm):
            row = jnp.where(r < n_real, dst_ref[0, 0, r], trash0 + s * tm + r)
            pltpu.make_async_copy(line(ybuf, s * tm + r), line(y_hbm, row), ssem.at[s]).start()

    def send_wait(s):
        for r in range(tm):
            pltpu.make_async_copy(line(ybuf, s * tm + r), line(y_hbm, 0), ssem.at[s]).wait()

    @pl.when(i == 0)
    def _():
        xbuf[...] = jnp.zeros_like(xbuf)
        ybuf[...] = jnp.zeros_like(ybuf)
        fill = pltpu.make_async_copy(ybuf, y_hbm.at[pl.ds(trash0 * k, 2 * tm * k), :], fsem.at[0])
        fill.start()
        fill.wait()
        fetch_tile(tok_ref, 0)

    @pl.when(i < nv)
    def _():
        fetch_wait(slot)
        send_tile(dstp_ref, jnp.where(i > 0, cnt_ref[jnp.maximum(i - 1, 0)], 0), other)
        fetch_tile(tokn_ref, other)
        x = _load_rows(xbuf, slot * tm, tm, d).astype(BF16)
        gp = jnp.dot(x, wg_ref[0], preferred_element_type=F32)
        up = jnp.dot(x, wu_ref[0], preferred_element_type=F32)
        hid = (gp * _sigmoid(gp) * up).astype(BF16)
        y = jnp.dot(hid, wd_ref[0], preferred_element_type=F32)

        @pl.when(i >= 1)
        def _():
            send_wait(slot)

        _store_rows(ybuf, slot * tm, y)

    @pl.when(i == nv - 1)
    def _():
        send_tile(dstc_ref, cnt_ref[i], slot)
        send_wait(slot)
        send_wait(other)
        fetch_wait(other)


def _moe(nvalid, texp, cnt, tok, dst, xn, wg, wu, wd, n_pairs):
    nt, _, tm = tok.shape
    d, de = wg.shape[1], wg.shape[2]
    k = _row_pitch(d)
    smem_tile = lambda f: pl.BlockSpec((1, 1, tm), lambda i, nv, te, ct: (f(i), 0, 0), memory_space=pltpu.SMEM)
    grid_spec = pltpu.PrefetchScalarGridSpec(
        num_scalar_prefetch=3,
        grid=(nt,),
        in_specs=[
            smem_tile(lambda i: i),
            smem_tile(lambda i: jnp.minimum(i + 1, nt - 1)),
            smem_tile(lambda i: jnp.maximum(i - 1, 0)),
            smem_tile(lambda i: i),
            pl.BlockSpec(memory_space=pl.ANY),
            pl.BlockSpec((1, d, de), lambda i, nv, te, ct: (te[i], 0, 0)),
            pl.BlockSpec((1, d, de), lambda i, nv, te, ct: (te[i], 0, 0)),
            pl.BlockSpec((1, de, d), lambda i, nv, te, ct: (te[i], 0, 0)),
        ],
        out_specs=pl.BlockSpec(memory_space=pl.ANY),
        scratch_shapes=[pltpu.VMEM((2 * tm * k, LANES), F32), pltpu.VMEM((2 * tm * k, LANES), F32),
                        pltpu.SemaphoreType.DMA((2,)), pltpu.SemaphoreType.DMA((2,)), pltpu.SemaphoreType.DMA((1,))],
    )
    return pl.pallas_call(
        _moe_kernel,
        out_shape=jax.ShapeDtypeStruct(((n_pairs + 2 * tm) * k, LANES), F32),
        grid_spec=grid_spec,
        compiler_params=_cparams(("arbitrary",)),
        name="moe",
    )(nvalid, texp, cnt, tok, tok, dst, dst, xn, wg, wu, wd)


def _moe_plan(eid, n, tm):
    e_flat = eid.reshape(-1)
    npairs = e_flat.shape[0]
    experts = jnp.arange(N_EXPERTS, dtype=jnp.int32)
    counts = jnp.sum((e_flat[:, None] == experts[None, :]).astype(jnp.int32), axis=0)
    padded = ((counts + tm - 1) // tm) * tm
    ends = jnp.cumsum(padded)
    offs = ends - padded
    n_rows = npairs + N_EXPERTS * tm
    nt = n_rows // tm
    fill = jnp.arange(n_rows - npairs, dtype=jnp.int32)
    fill_expert = jnp.sum((fill[:, None] >= jnp.cumsum(padded - counts)[None, :]).astype(jnp.int32), axis=1)
    keys = jnp.concatenate([e_flat * (2 * npairs) + jnp.arange(npairs, dtype=jnp.int32),
                            fill_expert * (2 * npairs) + npairs])
    skeys = jnp.sort(keys)
    low = skeys % (2 * npairs)
    dst = jnp.where(low < npairs, low, 0)
    tok = dst % n
    tile_start = jnp.arange(nt, dtype=jnp.int32) * tm
    texp = jnp.minimum(jnp.sum((tile_start[:, None] >= ends[None, :]).astype(jnp.int32), axis=1), N_EXPERTS - 1)
    cnt = jnp.clip(offs[texp] + counts[texp] - tile_start, 0, tm)
    nvalid = (ends[-1] // tm).astype(jnp.int32).reshape(1)
    return (nvalid, texp.astype(jnp.int32), cnt.astype(jnp.int32), tok.reshape(nt, 1, tm), dst.reshape(nt, 1, tm),
            npairs)


def _ple_final_kernel(x1_ref, y0_ref, y1_ref, w_ref, p_ref, gp_ref, wpg_ref, wp_ref, gf_ref, o_ref):
    w = w_ref[...]
    tm, d = x1_ref.shape
    x2 = x1_ref[...] + w[:, 0:1] * _load_rows(y0_ref, 0, tm, d) + w[:, 1:2] * _load_rows(y1_ref, 0, tm, d)
    hn = _rms(x2, gp_ref[...]).astype(BF16)
    gate = _sigmoid(jnp.dot(hn, wpg_ref[...], preferred_element_type=F32))
    pe = jnp.dot(p_ref[...].astype(BF16), wp_ref[...], preferred_element_type=F32)
    x3 = x2 + gate * pe
    o_ref[...] = _rms(x3, gf_ref[...])


def _ple_final(x1, ypairs, wcols, p2d, gp, wpg, wp, gf, tm=256):
    n, d = x1.shape
    nb = n // tm
    k = _row_pitch(d)
    return pl.pallas_call(
        _ple_final_kernel,
        out_shape=jax.ShapeDtypeStruct((n, d), F32),
        grid=(nb,),
        in_specs=[
            pl.BlockSpec((tm, d), lambda i: (i, 0)),
            pl.BlockSpec((tm * k, LANES), lambda i: (i, 0)),
            pl.BlockSpec((tm * k, LANES), lambda i: (nb + i, 0)),
            pl.BlockSpec((tm, wcols.shape[1]), lambda i: (i, 0)),
            pl.BlockSpec((tm, p2d.shape[1]), lambda i: (i, 0)),
            pl.BlockSpec((1, d), lambda i: (0, 0)),
            pl.BlockSpec((d, d), lambda i: (0, 0)),
            pl.BlockSpec(wp.shape, lambda i: (0, 0)),
            pl.BlockSpec((1, d), lambda i: (0, 0)),
        ],
        out_specs=pl.BlockSpec((tm, d), lambda i: (i, 0)),
        compiler_params=_cparams(("parallel",)),
        name="ple_final",
    )(x1, ypairs, ypairs, wcols, p2d, gp, wpg, wp, gf)


def _block_diag(w, per):
    nb, bw, _ = w.shape
    w = w.reshape(nb // per, per, bw, bw)
    eye = jnp.eye(per, dtype=w.dtype)
    return jnp.einsum("cpij,pq->cpiqj", w, eye).reshape(nb // per, per * bw, per * bw)


def _regroup_kernel(wt_ref, o_ref):
    d = Z_MGB - Z_MGA
    c = np.cumsum((0, NSA_Q_WIDTH) + (NSA_KV_WIDTH,) * 6 + (3 * NSA_HEADS, LRU_WIDTH, LRU_WIDTH, d, d))
    seg = lambda k: wt_ref[int(c[k]):int(c[k + 1]), :]
    q, k_c, v_c, k_s, v_s, k_w, v_w, gates, lru_x, lru_y, mg_a, mg_b = (seg(k) for k in range(12))

    def pair(a, b):
        parts = []
        for g in range(NSA_KV_GROUPS):
            parts += [a[g * HEAD_DIM:(g + 1) * HEAD_DIM], b[g * HEAD_DIM:(g + 1) * HEAD_DIM]]
        return jnp.concatenate(parts, axis=0)

    pad = jnp.zeros((Z_WIDTH - int(c[-1]), wt_ref.shape[1]), wt_ref.dtype)
    rows = [mg_a, mg_b, q, pair(k_s, k_w), pair(v_s, v_w), k_c, v_c, lru_x, lru_y, gates, pad]
    o_ref[...] = jnp.concatenate(rows, axis=0).T.astype(o_ref.dtype)


def _regroup_w_in(w_in_t, tr=256):
    nw, d = w_in_t.shape
    return pl.pallas_call(
        _regroup_kernel,
        out_shape=jax.ShapeDtypeStruct((d, Z_WIDTH), BF16),
        grid=(d // tr,),
        in_specs=[pl.BlockSpec((nw, tr), lambda i: (0, i))],
        out_specs=pl.BlockSpec((tr, Z_WIDTH), lambda i: (i, 0)),
        compiler_params=_cparams(("parallel",)),
        name="regroup_w_in",
    )(w_in_t)


def _layer(x2d, batch, seq, ln_mix, w_in, cmp_k_pos, cmp_k_w1, cmp_k_w2, cmp_v_pos, cmp_v_w1, cmp_v_w2, conv_w,
           conv_b, lru_wa, lru_ba, lru_wx, lru_bx, lru_lambda, w_nsa_up, w_lru_up, w_out, ln_ffn, w_grp, b_grp, w_exp,
           b_exp, w_gate, w_up, w_down):
    n, d = x2d.shape
    g, hd = NSA_KV_GROUPS, HEAD_DIM

    z = _in_proj(x2d, ln_mix.reshape(1, d), _regroup_w_in(w_in.T))

    pos8 = lambda pos: jnp.broadcast_to(pos.reshape(1, -1), (SUBLANES, pos.size)).astype(BF16)
    kc, vct = _compress(z, batch, seq, pos8(cmp_k_pos), cmp_k_w1.astype(BF16), cmp_k_w2.astype(BF16),
                        pos8(cmp_v_pos), cmp_v_w1.astype(BF16), cmp_v_w2.T.astype(BF16))
    o = _nsa_attention(z, kc, vct, batch, seq)

    per = LRU_WT // LRU_BW
    lru = _rglru(z, batch, seq, conv_w, conv_b.reshape(1, -1), _block_diag(lru_wa, per).astype(BF16),
                 lru_ba.reshape(1, -1), _block_diag(lru_wx, per).astype(BF16), lru_bx.reshape(1, -1),
                 lru_lambda.reshape(1, -1))

    merged = _merge(o, lru, z, w_nsa_up.astype(BF16), w_lru_up.astype(BF16))

    nr = SUBLANES + N_EXPERTS
    wr = jnp.zeros((nr, d), F32).at[0:N_GROUPS].set(w_grp.T).at[SUBLANES:].set(w_exp.T)
    br = jnp.zeros((nr, 1), F32).at[0:N_GROUPS, 0].set(b_grp).at[SUBLANES:, 0].set(b_exp)
    x1, xn, eid, ew = _out_route(merged, x2d, w_out.astype(BF16), ln_ffn.reshape(1, d), wr, br)

    nvalid, texp, cnt, tok, dst, n_rows_out = _moe_plan(eid[0:EXPERT_TOPK], n, MOE_TM)
    ypairs = _moe(nvalid, texp, cnt, tok, dst, xn, w_gate, w_up, w_down, n_rows_out)

    return x1, ypairs, ew.T


def kernel(x, p, ln_mix, w_in, cmp_k_pos, cmp_k_w1, cmp_k_w2, cmp_v_pos, cmp_v_w1, cmp_v_w2, conv_w, conv_b, lru_wa, lru_ba, lru_wx, lru_bx, lru_lambda, w_nsa_up, w_lru_up, w_out, ln_ffn, w_grp, b_grp, w_exp, b_exp, w_gate, w_up, w_down, ln_ple, w_ple, w_ple_gate, ln_final):
    batch, seq, d = x.shape
    assert p.shape[0] == 1, "the final norm is fused into the (single) layer's last kernel"
    n = batch * seq
    x1, ypairs, wcols = _layer(
        x.reshape(n, d), batch, seq, ln_mix[0], w_in[0], cmp_k_pos[0], cmp_k_w1[0], cmp_k_w2[0],
        cmp_v_pos[0], cmp_v_w1[0], cmp_v_w2[0], conv_w[0], conv_b[0], lru_wa[0], lru_ba[0], lru_wx[0], lru_bx[0],
        lru_lambda[0], w_nsa_up[0], w_lru_up[0], w_out[0], ln_ffn[0], w_grp[0], b_grp[0], w_exp[0], b_exp[0],
        w_gate[0], w_up[0], w_down[0])
    out = _ple_final(x1, ypairs, wcols, p[0].reshape(n, -1), ln_ple[0].reshape(1, d),
                     w_ple_gate[0].astype(BF16), w_ple[0].astype(BF16), ln_final.reshape(1, d))
    return out.reshape(batch, seq, d)
```

```python
import numpy as np
import jax
import jax.numpy as jnp
from jax import lax
from jax.experimental import pallas as pl
from jax.experimental.pallas import tpu as pltpu

F32 = jnp.float32
BF16 = jnp.bfloat16

NSA_HEADS = 16
NSA_KV_GROUPS = 4
NSA_HPG = NSA_HEADS // NSA_KV_GROUPS
HEAD_DIM = 64
NSA_Q_WIDTH = NSA_HEADS * HEAD_DIM
NSA_KV_WIDTH = NSA_KV_GROUPS * HEAD_DIM
CMP_BLOCK = 32
CMP_STRIDE = 16
CMP_HIDDEN = 2 * HEAD_DIM
SLC_BLOCK = 64
SLC_TOPN = 16
WINDOW = 512
ATTN_SCALE = HEAD_DIM ** -0.5
NEG_INF = -1e30
LOG2E = float(np.log2(np.e))
FORCE_SCORE = 1e4
LRU_WIDTH = 1024
LRU_BLOCKS = 16
LRU_BW = LRU_WIDTH // LRU_BLOCKS
CONV_WIDTH = 4
LRU_C = 8.0
N_GROUPS = 4
EXPERTS_PER_GROUP = 8
N_EXPERTS = N_GROUPS * EXPERTS_PER_GROUP
EXPERT_TOPK = 2
D_EXPERT = 512
EPS = 1e-6

LANES = 128
SUBLANES = 8
VMEM_LIMIT_BYTES = 56 * 1024 * 1024

Z_MGA = 0
Z_MGB = 2048
Z_Q = 4096
Z_KK = 5120
Z_VV = 5632
Z_KC = 6144
Z_VC = 6400
Z_LX = 6656
Z_LY = 7680
Z_GATE = 8704
Z_WIDTH = 9216
LRU_CB = 512
LRU_WT = 256
MOE_TM = 256
NSA_TQ = 256
NSA_CK = 512


def _cparams(sem, vmem=VMEM_LIMIT_BYTES):
    return pltpu.CompilerParams(dimension_semantics=sem, vmem_limit_bytes=vmem)


def _rms(x, g):
    return x * lax.rsqrt(jnp.mean(x * x, axis=-1, keepdims=True) + EPS) * g


def _gelu_tanh(x):
    return 0.5 * x * (1.0 + jnp.tanh(np.sqrt(2.0 / np.pi) * (x + 0.044715 * (x * x * x))))


def _sigmoid(x):
    return 1.0 / (1.0 + jnp.exp(-x))


def _row_pitch(d):
    k = d // LANES
    return k + 1 - (k % 2)


def _store_rows(ref, row0, val):
    rows, d = val.shape
    pitch = _row_pitch(d)
    for j in range(d // LANES):
        ref[pl.ds(row0 * pitch + j, rows, stride=pitch), :] = val[:, j * LANES:(j + 1) * LANES]
    for j in range(d // LANES, pitch):
        ref[pl.ds(row0 * pitch + j, rows, stride=pitch), :] = jnp.zeros((rows, LANES), val.dtype)


def _load_rows(ref, row0, rows, d):
    pitch = _row_pitch(d)
    return jnp.concatenate([ref[pl.ds(row0 * pitch + j, rows, stride=pitch), :] for j in range(d // LANES)], axis=1)


def _in_proj_kernel(x_ref, g_ref, w_ref, o_ref, h_ref):
    @pl.when(pl.program_id(1) == 0)
    def _():
        h_ref[...] = _rms(x_ref[...], g_ref[...]).astype(BF16)

    o_ref[...] = jnp.dot(h_ref[...], w_ref[...], preferred_element_type=F32).astype(o_ref.dtype)


def _in_proj(x2d, g, w, tm=1024, tn=1024):
    n, d = x2d.shape
    nw = w.shape[1]
    return pl.pallas_call(
        _in_proj_kernel,
        out_shape=jax.ShapeDtypeStruct((n, nw), BF16),
        grid=(n // tm, nw // tn),
        in_specs=[
            pl.BlockSpec((tm, d), lambda i, j: (i, 0)),
            pl.BlockSpec((1, d), lambda i, j: (0, 0)),
            pl.BlockSpec((d, tn), lambda i, j: (0, j)),
        ],
        out_specs=pl.BlockSpec((tm, tn), lambda i, j: (i, j)),
        scratch_shapes=[pltpu.VMEM((tm, d), BF16)],
        compiler_params=_cparams(("parallel", "arbitrary")),
        name="in_proj",
    )(x2d, g, w)


def _compress_kernel(zk_ref, zv_ref, kpos_ref, kw1_ref, kw2_ref, vpos_ref, vw1_ref, vw2t_ref, kc_ref, vct_ref,
                     xf_ref):
    seq = zk_ref.shape[0]
    ncp = seq // CMP_STRIDE
    hd = HEAD_DIM
    half = CMP_STRIDE * hd

    def half_windows(z_ref):
        xf_ref[...] = z_ref[...].astype(F32)
        lines = [xf_ref[pl.ds(r, ncp, stride=CMP_STRIDE), :].astype(BF16) for r in range(CMP_STRIDE)]
        return [jnp.concatenate([ln[:, gg * hd:(gg + 1) * hd] for ln in lines], axis=1) for gg in range(LANES // hd)]

    def hidden(x, pos_ref, w1_ref):
        w1 = w1_ref[...]
        ha = jnp.dot(x, w1[:half], preferred_element_type=F32)
        hb = jnp.dot(x, w1[half:], preferred_element_type=F32)
        hb = pltpu.roll(hb, hb.shape[0] - 1, axis=0)
        pc = jnp.dot(pos_ref[...], w1, preferred_element_type=F32)[0:1]
        return _gelu_tanh(ha + hb + pc).astype(BF16)

    for gg, x in enumerate(half_windows(zk_ref)):
        hk = hidden(x, kpos_ref, kw1_ref)
        kc_ref[0, gg] = jnp.dot(hk, kw2_ref[...], preferred_element_type=F32).astype(kc_ref.dtype)
    for gg, x in enumerate(half_windows(zv_ref)):
        hv = hidden(x, vpos_ref, vw1_ref)
        vct_ref[0, gg] = lax.dot_general(vw2t_ref[...], hv, (((1,), (1,)), ((), ())),
                                         preferred_element_type=F32).astype(vct_ref.dtype)


def _compress(z, batch, seq, kpos, kw1, kw2, vpos, vw1, vw2t):
    g, hd = NSA_KV_GROUPS, HEAD_DIM
    gpb = LANES // hd
    ncp = seq // CMP_STRIDE
    full = lambda a: pl.BlockSpec(a.shape, lambda i, j: (0,) * a.ndim)
    return pl.pallas_call(
        _compress_kernel,
        out_shape=(jax.ShapeDtypeStruct((batch, g, ncp, hd), BF16),
                   jax.ShapeDtypeStruct((batch, g, hd, ncp), BF16)),
        grid=(batch, g // gpb),
        in_specs=[
            pl.BlockSpec((seq, LANES), lambda i, j: (i, Z_KC // LANES + j)),
            pl.BlockSpec((seq, LANES), lambda i, j: (i, Z_VC // LANES + j)),
            full(kpos), full(kw1), full(kw2), full(vpos), full(vw1), full(vw2t),
        ],
        out_specs=(pl.BlockSpec((1, gpb, ncp, hd), lambda i, j: (i, j, 0, 0)),
                   pl.BlockSpec((1, gpb, hd, ncp), lambda i, j: (i, j, 0, 0))),
        scratch_shapes=[pltpu.VMEM((seq, LANES), F32)],
        compiler_params=_cparams(("parallel", "parallel")),
        name="compress",
    )(z, z, kpos, kw1, kw2, vpos, vw1, vw2t)


def _nsa_kernel(zq_ref, zkk_ref, zvv_ref, zg_ref, kc_ref, vct_ref, o_ref, kcomb_ref, vs_ref, vw_ref, score_ref, gate_ref,
                sa_ref, sb_ref):
    grp = pl.program_id(1)
    qi = pl.program_id(2)
    tq, hd, hpg = NSA_TQ, HEAD_DIM, NSA_HPG
    nq = tq * hpg
    seq = kcomb_ref.shape[0]
    ncp = kc_ref.shape[2]
    nsb = score_ref.shape[0]
    blk_shift = SLC_BLOCK.bit_length() - 1

    @pl.when(qi == 0)
    def _():
        kcomb_ref[:, 0:2 * hd] = zkk_ref[...]
        blk_of_row = lax.broadcasted_iota(jnp.int32, (seq, 2 * hd), 0) >> blk_shift
        lane = lax.broadcasted_iota(jnp.int32, (seq, 2 * hd), 1)
        kcomb_ref[:, 2 * hd:] = jnp.where(blk_of_row == lane, 1.0, 0.0).astype(BF16)
        ones = jnp.ones((seq, hd), BF16)
        vs_ref[...] = jnp.concatenate([zvv_ref[:, 0:hd], ones], axis=1)
        vw_ref[...] = jnp.concatenate([zvv_ref[:, hd:2 * hd], ones], axis=1)

    qt = zq_ref[...].T
    q = jnp.concatenate([qt[h * hd:(h + 1) * hd, :] for h in range(hpg)], axis=1)
    q = (q.astype(F32) * (ATTN_SCALE * LOG2E)).astype(BF16)
    zero = jnp.zeros((hd, nq), BF16)
    t_lane = qi * tq + (lax.broadcasted_iota(jnp.int32, (1, nq), 1) & (tq - 1))
    tn_dims = (((0,), (0,)), ((), ()))

    sc = jnp.dot(kc_ref[0, 0], q, preferred_element_type=F32)
    cmp_end = lax.broadcasted_iota(jnp.int32, (ncp, 1), 0) * CMP_STRIDE + (CMP_BLOCK - 1)
    cmask = cmp_end <= t_lane
    sc = jnp.where(cmask, sc, NEG_INF)
    pc = jnp.where(cmask, jnp.exp2(sc - jnp.max(sc, axis=0, keepdims=True)), 0.0)
    lc = jnp.sum(pc, axis=0, keepdims=True)
    pc = pc * (1.0 / jnp.where(lc > 0.0, lc, 1.0))
    o_cmp = jnp.dot(vct_ref[0, 0], pc.astype(BF16), preferred_element_type=F32)

    psum = pc[:, 0:tq]
    for h in range(1, hpg):
        psum = psum + pc[:, h * tq:(h + 1) * tq]
    jrow = lax.broadcasted_iota(jnp.int32, (nsb, ncp), 0)
    crel = lax.broadcasted_iota(jnp.int32, (nsb, ncp), 1) - jrow * (SLC_BLOCK // CMP_STRIDE)
    ovt = jnp.where((crel >= 0) & (crel <= 2), 1.0, jnp.where((crel == -1) | (crel == 3), 0.5, 0.0)).astype(F32)
    imp = jnp.dot(ovt, psum, preferred_element_type=F32, precision=lax.Precision.HIGHEST)
    jblk = lax.broadcasted_iota(jnp.int32, (nsb, tq), 0)
    blk = (qi * tq + lax.broadcasted_iota(jnp.int32, (1, tq), 1)) >> blk_shift
    forced = (jblk == 0) | (jblk == blk) | (jblk == blk - 1)
    score = jnp.where(forced, FORCE_SCORE, jnp.where(jblk <= blk, imp, -1.0))
    score_ref[...] = score

    def rank_one(k, cnt):
        row = score_ref[pl.ds(k, 1), :]
        ge = jnp.where(row >= score, 1.0, 0.0)
        gt = jnp.where(row > score, 1.0, 0.0)
        return cnt + jnp.where(jblk > k, ge, gt)

    bpt = tq // SLC_BLOCK

    def rank_group(i, cnt):
        for u in range(bpt):
            cnt = rank_one(bpt * i + u, cnt)
        return cnt

    n_vis = jnp.minimum((qi + 1) * bpt, nsb)
    rank = lax.fori_loop(0, n_vis // bpt, rank_group, jnp.zeros((nsb, tq), F32))
    selb = jnp.where(rank < float(min(SLC_TOPN, nsb)), 0.0, NEG_INF).astype(BF16)

    ck = NSA_CK
    pad = jnp.zeros((2 * hd - nsb, nq), BF16)
    qa = jnp.concatenate([q, zero, jnp.concatenate([selb] * hpg, axis=1), pad], axis=0)

    def slc_scores(c):
        return jnp.dot(kcomb_ref[pl.ds(pl.multiple_of(c * ck, ck), ck), :], qa, preferred_element_type=F32)

    n_full = (qi * tq) // ck

    def slc_update(buf, c, carry, causal):
        def scores():
            if not causal:
                return buf[...]
            return jnp.where(c * ck + lax.broadcasted_iota(jnp.int32, (ck, 1), 0) <= t_lane, buf[...], NEG_INF)

        m, l, acc = carry
        m_new = jnp.maximum(m, jnp.max(scores(), axis=0, keepdims=True))
        alpha = jnp.exp2(m - m_new)
        p = jnp.exp2(scores() - m_new).astype(BF16)
        v = vs_ref[pl.ds(pl.multiple_of(c * ck, ck), ck), :]
        pv = lax.dot_general(v, p, tn_dims, preferred_element_type=F32)
        return m_new, alpha * l + pv[hd:hd + 1], acc * alpha + pv[0:hd]

    def slc_pair(i, carry):
        sb_ref[...] = slc_scores(2 * i + 1)
        carry = slc_update(sa_ref, 2 * i, carry, False)
        sa_ref[...] = slc_scores(2 * i + 2)
        return slc_update(sb_ref, 2 * i + 1, carry, False)

    def tail_two(carry):
        sb_ref[...] = slc_scores(n_full)
        return slc_update(sb_ref, n_full, slc_update(sa_ref, n_full - 1, carry, False), True)

    def tail_one(carry):
        return slc_update(sa_ref, n_full, carry, True)

    sa_ref[...] = slc_scores(0)

    nwc = WINDOW // tq + 1
    t_lo = t_lane - WINDOW
    ks_w, vs_w = [], []
    for i in range(nwc):
        rows = pl.ds(pl.multiple_of(jnp.maximum(qi - (nwc - 1) + i, 0) * tq, tq), tq)
        ks_w.append(kcomb_ref[rows, 0:2 * hd])
        vs_w.append(vw_ref[rows, :])
    qw = jnp.concatenate([zero, q], axis=0)
    sw = jnp.dot(jnp.concatenate(ks_w, axis=0), qw, preferred_element_type=F32)
    sw_parts = []
    for i in range(nwc):
        spos = (qi - (nwc - 1) + i) * tq + lax.broadcasted_iota(jnp.int32, (tq, 1), 0)
        if i < nwc - 1:
            ok = jnp.where(spos >= 0, spos, -(1 << 24)) > t_lo
        else:
            ok = spos <= t_lane
        sw_parts.append(jnp.where(ok, sw[i * tq:(i + 1) * tq], NEG_INF))
    sw = jnp.concatenate(sw_parts, axis=0)
    pw = jnp.exp2(sw - jnp.max(sw, axis=0, keepdims=True))
    ow = lax.dot_general(jnp.concatenate(vs_w, axis=0), pw.astype(BF16), tn_dims, preferred_element_type=F32)
    o_win = ow[0:hd] * (1.0 / ow[hd:hd + 1])
    init = (jnp.full((1, nq), NEG_INF, F32), jnp.zeros((1, nq), F32), jnp.zeros((hd, nq), F32))
    carry = lax.fori_loop(0, n_full // 2, slc_pair, init)
    _, l_s, acc_s = lax.cond(n_full % 2 == 1, tail_two, tail_one, carry)
    o_slc = acc_s * (1.0 / l_s)

    gate_ref[...] = _sigmoid(zg_ref[...].astype(F32)).T

    def branch_gate(br):
        rows = [gate_ref[pl.ds(grp * (hpg * 3) + h * 3 + br, 1), :] for h in range(hpg)]
        return jnp.concatenate(rows, axis=1)

    o = branch_gate(0) * o_cmp + branch_gate(1) * o_slc + branch_gate(2) * o_win
    o_heads = jnp.concatenate([o[:, h * tq:(h + 1) * tq] for h in range(hpg)], axis=0)
    o_ref[...] = o_heads.T.astype(o_ref.dtype)


def _nsa_attention(z, kc, vct, batch, seq):
    n = z.shape[0]
    g, hd, tq = NSA_KV_GROUPS, HEAD_DIM, NSA_TQ
    nqt = seq // tq
    ncp = kc.shape[2]
    nsb = seq // SLC_BLOCK
    gw = NSA_HPG * hd
    assert seq % NSA_CK == 0 and NSA_CK % tq == 0 and WINDOW % tq == 0 and nsb <= 2 * hd
    tile = lambda i, j, k: i * nqt + k
    return pl.pallas_call(
        _nsa_kernel,
        out_shape=jax.ShapeDtypeStruct((n, NSA_Q_WIDTH), BF16),
        grid=(batch, g, nqt),
        in_specs=[
            pl.BlockSpec((tq, gw), lambda i, j, k: (tile(i, j, k), Z_Q // gw + j)),
            pl.BlockSpec((seq, 2 * hd), lambda i, j, k: (i, Z_KK // (2 * hd) + j)),
            pl.BlockSpec((seq, 2 * hd), lambda i, j, k: (i, Z_VV // (2 * hd) + j)),
            pl.BlockSpec((tq, LANES), lambda i, j, k: (tile(i, j, k), Z_GATE // LANES)),
            pl.BlockSpec((1, 1, ncp, hd), lambda i, j, k: (i, j, 0, 0)),
            pl.BlockSpec((1, 1, hd, ncp), lambda i, j, k: (i, j, 0, 0)),
        ],
        out_specs=pl.BlockSpec((tq, gw), lambda i, j, k: (tile(i, j, k), j)),
        scratch_shapes=[pltpu.VMEM((seq, 4 * hd), BF16), pltpu.VMEM((seq, 2 * hd), BF16),
                        pltpu.VMEM((seq, 2 * hd), BF16), pltpu.VMEM((nsb, tq), F32),
                        pltpu.VMEM((LANES, tq), F32), pltpu.VMEM((NSA_CK, NSA_HPG * tq), F32),
                        pltpu.VMEM((NSA_CK, NSA_HPG * tq), F32)],
        compiler_params=_cparams(("parallel", "parallel", "arbitrary")),
        name="nsa_attn",
    )(z, z, z, z, kc, vct)


def _rglru_kernel(x_ref, y_ref, cw_ref, cb_ref, wa_ref, ba_ref, wx_ref, bx_ref, lam_ref, o_ref,
                  tail_ref, h_ref, a_ref, u_ref):
    tc = pl.program_id(2)
    tt, cb = x_ref.shape

    @pl.when(tc == 0)
    def _():
        tail_ref[...] = jnp.zeros_like(tail_ref)
        h_ref[...] = jnp.zeros_like(h_ref)

    x = x_ref[...].astype(F32)
    xe = jnp.concatenate([tail_ref[...], x], axis=0)
    tail_ref[...] = x[tt - SUBLANES:, :]
    cw = cw_ref[...]
    xc = cb_ref[...]
    for k in range(CONV_WIDTH):
        off = SUBLANES - (CONV_WIDTH - 1) + k
        xc = xc + cw[k:k + 1, :] * xe[off:off + tt, :]
    xcb = xc.astype(BF16)
    def gate(w_ref, b_ref):
        parts = [jnp.dot(xcb[:, c * LRU_WT:(c + 1) * LRU_WT], w_ref[c], preferred_element_type=F32)
                 for c in range(cb // LRU_WT)]
        return _sigmoid(jnp.concatenate(parts, axis=1) + b_ref[...])

    r = gate(wa_ref, ba_ref)
    ig = gate(wx_ref, bx_ref)
    nl = -lam_ref[...]
    softplus = jnp.maximum(nl, 0.0) + jnp.log1p(jnp.exp(-jnp.abs(nl)))
    log_a = (-LRU_C) * softplus * r
    a_ref[...] = jnp.exp(log_a)
    th = jnp.tanh(log_a)
    u_ref[...] = jnp.sqrt(-2.0 * th / (1.0 - th)) * (ig * xc)

    row = lax.broadcasted_iota(jnp.int32, (SUBLANES, cb), 0)

    def step(i, h):
        sl = pl.ds(pl.multiple_of(i * SUBLANES, SUBLANES), SUBLANES)
        a = a_ref[sl, :]
        u = u_ref[sl, :]
        for s in (1, 2, 4):
            a_s = jnp.where(row >= s, pltpu.roll(a, s, axis=0), 1.0)
            u_s = jnp.where(row >= s, pltpu.roll(u, s, axis=0), 0.0)
            u = a * u_s + u
            a = a * a_s
        hrows = a * h + u
        u_ref[sl, :] = hrows
        return hrows[SUBLANES - 1:SUBLANES, :]

    h_ref[...] = lax.fori_loop(0, tt // SUBLANES, step, h_ref[...], unroll=8)
    o_ref[...] = (u_ref[...] * _gelu_tanh(y_ref[...].astype(F32))).astype(o_ref.dtype)


def _rglru(z, batch, seq, cw, cbias, wa_bd, ba, wx_bd, bx, lam, tt=512):
    n = z.shape[0]
    ncb = LRU_WIDTH // LRU_CB
    nt = seq // tt
    row = lambda i, j, k: i * nt + k
    vec = lambda r: pl.BlockSpec((r, LRU_CB), lambda i, j, k: (0, j))
    return pl.pallas_call(
        _rglru_kernel,
        out_shape=jax.ShapeDtypeStruct((n, LRU_WIDTH), BF16),
        grid=(batch, ncb, nt),
        in_specs=[
            pl.BlockSpec((tt, LRU_CB), lambda i, j, k: (row(i, j, k), Z_LX // LRU_CB + j)),
            pl.BlockSpec((tt, LRU_CB), lambda i, j, k: (row(i, j, k), Z_LY // LRU_CB + j)),
            vec(CONV_WIDTH), vec(1),
            pl.BlockSpec((LRU_CB // LRU_WT, LRU_WT, LRU_WT), lambda i, j, k: (j, 0, 0)), vec(1),
            pl.BlockSpec((LRU_CB // LRU_WT, LRU_WT, LRU_WT), lambda i, j, k: (j, 0, 0)), vec(1),
            vec(1),
        ],
        out_specs=pl.BlockSpec((tt, LRU_CB), lambda i, j, k: (row(i, j, k), j)),
        scratch_shapes=[pltpu.VMEM((SUBLANES, LRU_CB), F32), pltpu.VMEM((1, LRU_CB), F32),
                        pltpu.VMEM((tt, LRU_CB), F32), pltpu.VMEM((tt, LRU_CB), F32)],
        compiler_params=_cparams(("parallel", "parallel", "arbitrary")),
        name="rglru",
    )(z, z, cw, cbias, wa_bd, ba, wx_bd, bx, lam)


def _merge_kernel(o_ref, l_ref, mga_ref, mgb_ref, wn_ref, wl_ref, m_ref):
    ya = jnp.dot(o_ref[...], wn_ref[...], preferred_element_type=F32)
    yb = jnp.dot(l_ref[...], wl_ref[...], preferred_element_type=F32)
    m = _sigmoid(mga_ref[...].astype(F32)) * ya + _sigmoid(mgb_ref[...].astype(F32)) * yb
    m_ref[...] = m.astype(m_ref.dtype)


def _merge(o, lru, z, wn, wl, tm=512):
    n, d = o.shape[0], wn.shape[1]
    return pl.pallas_call(
        _merge_kernel,
        out_shape=jax.ShapeDtypeStruct((n, d), BF16),
        grid=(n // tm,),
        in_specs=[
            pl.BlockSpec((tm, o.shape[1]), lambda i: (i, 0)),
            pl.BlockSpec((tm, lru.shape[1]), lambda i: (i, 0)),
            pl.BlockSpec((tm, d), lambda i: (i, Z_MGA // d)),
            pl.BlockSpec((tm, d), lambda i: (i, Z_MGB // d)),
            pl.BlockSpec(wn.shape, lambda i: (0, 0)),
            pl.BlockSpec(wl.shape, lambda i: (0, 0)),
        ],
        out_specs=pl.BlockSpec((tm, d), lambda i: (i, 0)),
        compiler_params=_cparams(("parallel",)),
        name="merge",
    )(o, lru, z, z, wn, wl)


def _out_route_kernel(m_ref, x_ref, wo_ref, g_ref, wr_ref, br_ref, x1_ref, xn_ref, eid_ref, ew_ref):
    x1 = x_ref[...] + jnp.dot(m_ref[...], wo_ref[...], preferred_element_type=F32)
    x1_ref[...] = x1
    xn = _rms(x1, g_ref[...])
    _store_rows(xn_ref, 0, xn)
    lg = jnp.dot(xn, wr_ref[...], preferred_element_type=F32).T + br_ref[...]
    tm = lg.shape[1]
    sub = lax.broadcasted_iota(jnp.int32, (SUBLANES, tm), 0)

    def first_argmax(v, vmax):
        return jnp.min(jnp.where(v == vmax, sub, SUBLANES), axis=0, keepdims=True)

    gl = jnp.where(sub < N_GROUPS, lg[0:SUBLANES], -jnp.inf)
    gmax = jnp.max(gl, axis=0, keepdims=True)
    ge = jnp.exp(gl - gmax)
    gprob = ge / jnp.sum(ge, axis=0, keepdims=True)
    g_val = jnp.max(gprob, axis=0, keepdims=True)
    g_idx = first_argmax(gprob, g_val)
    e_in = jnp.zeros((EXPERTS_PER_GROUP, tm), F32)
    for gi in range(N_GROUPS):
        lo = SUBLANES + gi * EXPERTS_PER_GROUP
        e_in = jnp.where(g_idx == gi, lg[lo:lo + EXPERTS_PER_GROUP], e_in)
    ee = jnp.exp(e_in - jnp.max(e_in, axis=0, keepdims=True))
    eprob = ee / jnp.sum(ee, axis=0, keepdims=True)
    v1 = jnp.max(eprob, axis=0, keepdims=True)
    i1 = first_argmax(eprob, v1)
    rest = jnp.where(sub == i1, -1.0, eprob)
    v2 = jnp.max(rest, axis=0, keepdims=True)
    i2 = first_argmax(rest, v2)
    den = v1 + v2
    eid = jnp.where(sub == 0, g_idx * EXPERTS_PER_GROUP + i1, g_idx * EXPERTS_PER_GROUP + i2)
    eid_ref[...] = eid
    ew_ref[...] = jnp.where(sub == 0, g_val * v1 / den, g_val * v2 / den)


def _out_route(m, x2d, wo, g, wr, br, tm=512):
    n, d = x2d.shape
    once = pl.Buffered(1)
    return pl.pallas_call(
        _out_route_kernel,
        out_shape=(jax.ShapeDtypeStruct((n, d), F32), jax.ShapeDtypeStruct((n * _row_pitch(d), LANES), F32),
                   jax.ShapeDtypeStruct((SUBLANES, n), jnp.int32), jax.ShapeDtypeStruct((SUBLANES, n), F32)),
        grid=(n // tm,),
        in_specs=[
            pl.BlockSpec((tm, d), lambda i: (i, 0)),
            pl.BlockSpec((tm, d), lambda i: (i, 0)),
            pl.BlockSpec((d, d), lambda i: (0, 0), pipeline_mode=once),
            pl.BlockSpec((1, d), lambda i: (0, 0)),
            pl.BlockSpec(wr.shape, lambda i: (0, 0), pipeline_mode=once),
            pl.BlockSpec(br.shape, lambda i: (0, 0)),
        ],
        out_specs=(pl.BlockSpec((tm, d), lambda i: (i, 0)), pl.BlockSpec((tm * _row_pitch(d), LANES), lambda i: (i, 0)),
                   pl.BlockSpec((SUBLANES, tm), lambda i: (0, i)), pl.BlockSpec((SUBLANES, tm), lambda i: (0, i))),
        compiler_params=_cparams(("parallel",)),
        name="out_route",
    )(m, x2d, wo, g, wr, br)


def _for_rows(cnt, fn, unroll=8):
    sh = unroll.bit_length() - 1

    def group(gidx, c):
        for u in range(unroll):
            fn(gidx * unroll + u)
        return c

    def single(r, c):
        fn(r)
        return c

    lax.fori_loop(0, cnt >> sh, group, 0)
    lax.fori_loop((cnt >> sh) << sh, cnt, single, 0)


def _moe_kernel(nvalid_ref, texp_ref, cnt_ref, tok_ref, tokn_ref, dst_ref, xn_hbm, wg_ref, wu_ref, wd_ref, y_hbm,
                xbuf, ybuf, gsem, ssem):
    i = pl.program_id(0)
    nv = nvalid_ref[0]
    slot = i % 2
    d = wg_ref.shape[1]
    k = _row_pitch(d)
    tm = xbuf.shape[0] // (2 * k)

    def line(ref, row):
        return ref.at[pl.ds(row * k, k), :]

    def gather_row(idx_ref, s):
        def fn(r):
            pltpu.make_async_copy(line(xn_hbm, idx_ref[0, 0, r]), line(xbuf, s * tm + r), gsem.at[s]).start()
        return fn

    def gather_wait_row(s):
        def fn(r):
            pltpu.make_async_copy(line(xn_hbm, 0), line(xbuf, s * tm + r), gsem.at[s]).wait()
        return fn

    def scatter_row(s):
        def fn(r):
            pltpu.make_async_copy(line(ybuf, s * tm + r), line(y_hbm, dst_ref[0, 0, r]), ssem.at[s]).start()
        return fn

    def scatter_wait_row(s):
        def fn(r):
            pltpu.make_async_copy(line(ybuf, s * tm + r), line(y_hbm, 0), ssem.at[s]).wait()
        return fn

    @pl.when(i == 0)
    def _():
        xbuf[...] = jnp.zeros_like(xbuf)
        _for_rows(cnt_ref[0], gather_row(tok_ref, 0))

    @pl.when(i + 1 < nv)
    def _():
        _for_rows(cnt_ref[i + 1], gather_row(tokn_ref, 1 - slot))

    @pl.when(i < nv)
    def _():
        _for_rows(cnt_ref[i], gather_wait_row(slot))
        x = _load_rows(xbuf, slot * tm, tm, d).astype(BF16)
        gp = jnp.dot(x, wg_ref[0], preferred_element_type=F32)
        up = jnp.dot(x, wu_ref[0], preferred_element_type=F32)
        hid = (gp * _sigmoid(gp) * up).astype(BF16)
        y = jnp.dot(hid, wd_ref[0], preferred_element_type=F32)

        @pl.when(i >= 2)
        def _():
            _for_rows(cnt_ref[i - 2], scatter_wait_row(slot))

        _store_rows(ybuf, slot * tm, y)
        _for_rows(cnt_ref[i], scatter_row(slot))

    @pl.when(i == nv - 1)
    def _():
        _for_rows(cnt_ref[i], scatter_wait_row(slot))

        @pl.when(nv >= 2)
        def _():
            _for_rows(cnt_ref[i - 1], scatter_wait_row(1 - slot))


def _moe(nvalid, texp, cnt, tok, dst, xn, wg, wu, wd, n_rows_out):
    nt, _, tm = tok.shape
    d, de = wg.shape[1], wg.shape[2]
    k = _row_pitch(d)
    grid_spec = pltpu.PrefetchScalarGridSpec(
        num_scalar_prefetch=3,
        grid=(nt,),
        in_specs=[
            pl.BlockSpec((1, 1, tm), lambda i, nv, te, ct: (i, 0, 0), memory_space=pltpu.SMEM),
            pl.BlockSpec((1, 1, tm), lambda i, nv, te, ct: (jnp.minimum(i + 1, nt - 1), 0, 0),
                         memory_space=pltpu.SMEM),
            pl.BlockSpec((1, 1, tm), lambda i, nv, te, ct: (i, 0, 0), memory_space=pltpu.SMEM),
            pl.BlockSpec(memory_space=pl.ANY),
            pl.BlockSpec((1, d, de), lambda i, nv, te, ct: (te[i], 0, 0)),
            pl.BlockSpec((1, d, de), lambda i, nv, te, ct: (te[i], 0, 0)),
            pl.BlockSpec((1, de, d), lambda i, nv, te, ct: (te[i], 0, 0)),
        ],
        out_specs=pl.BlockSpec(memory_space=pl.ANY),
        scratch_shapes=[pltpu.VMEM((2 * tm * k, LANES), F32), pltpu.VMEM((2 * tm * k, LANES), F32),
                        pltpu.SemaphoreType.DMA((2,)), pltpu.SemaphoreType.DMA((2,))],
    )
    return pl.pallas_call(
        _moe_kernel,
        out_shape=jax.ShapeDtypeStruct((n_rows_out * k, LANES), F32),
        grid_spec=grid_spec,
        compiler_params=_cparams(("arbitrary",)),
        name="moe",
    )(nvalid, texp, cnt, tok, tok, dst, xn, wg, wu, wd)


def _moe_plan(eid, n, tm):
    e_flat = eid.reshape(-1)
    npairs = e_flat.shape[0]
    experts = jnp.arange(N_EXPERTS, dtype=jnp.int32)
    counts = jnp.sum((e_flat[:, None] == experts[None, :]).astype(jnp.int32), axis=0)
    padded = ((counts + tm - 1) // tm) * tm
    ends = jnp.cumsum(padded)
    offs = ends - padded
    n_rows = npairs + N_EXPERTS * tm
    nt = n_rows // tm
    fill = jnp.arange(n_rows - npairs, dtype=jnp.int32)
    fill_expert = jnp.sum((fill[:, None] >= jnp.cumsum(padded - counts)[None, :]).astype(jnp.int32), axis=1)
    keys = jnp.concatenate([e_flat * (2 * npairs) + jnp.arange(npairs, dtype=jnp.int32),
                            fill_expert * (2 * npairs) + npairs])
    skeys = jnp.sort(keys)
    low = skeys % (2 * npairs)
    dst = jnp.where(low < npairs, low, 0)
    tok = dst % n
    tile_start = jnp.arange(nt, dtype=jnp.int32) * tm
    texp = jnp.minimum(jnp.sum((tile_start[:, None] >= ends[None, :]).astype(jnp.int32), axis=1), N_EXPERTS - 1)
    cnt = jnp.clip(offs[texp] + counts[texp] - tile_start, 0, tm)
    nvalid = (ends[-1] // tm).astype(jnp.int32).reshape(1)
    return (nvalid, texp.astype(jnp.int32), cnt.astype(jnp.int32), tok.reshape(nt, 1, tm), dst.reshape(nt, 1, tm),
            npairs)


def _ple_final_kernel(x1_ref, y0_ref, y1_ref, w_ref, p_ref, gp_ref, wpg_ref, wp_ref, gf_ref, o_ref):
    w = w_ref[...]
    tm, d = x1_ref.shape
    x2 = x1_ref[...] + w[:, 0:1] * _load_rows(y0_ref, 0, tm, d) + w[:, 1:2] * _load_rows(y1_ref, 0, tm, d)
    hn = _rms(x2, gp_ref[...]).astype(BF16)
    gate = _sigmoid(jnp.dot(hn, wpg_ref[...], preferred_element_type=F32))
    pe = jnp.dot(p_ref[...].astype(BF16), wp_ref[...], preferred_element_type=F32)
    x3 = x2 + gate * pe
    o_ref[...] = _rms(x3, gf_ref[...])


def _ple_final(x1, ypairs, wcols, p2d, gp, wpg, wp, gf, tm=256):
    n, d = x1.shape
    nb = n // tm
    k = _row_pitch(d)
    return pl.pallas_call(
        _ple_final_kernel,
        out_shape=jax.ShapeDtypeStruct((n, d), F32),
        grid=(nb,),
        in_specs=[
            pl.BlockSpec((tm, d), lambda i: (i, 0)),
            pl.BlockSpec((tm * k, LANES), lambda i: (i, 0)),
            pl.BlockSpec((tm * k, LANES), lambda i: (nb + i, 0)),
            pl.BlockSpec((tm, wcols.shape[1]), lambda i: (i, 0)),
            pl.BlockSpec((tm, p2d.shape[1]), lambda i: (i, 0)),
            pl.BlockSpec((1, d), lambda i: (0, 0)),
            pl.BlockSpec((d, d), lambda i: (0, 0)),
            pl.BlockSpec(wp.shape, lambda i: (0, 0)),
            pl.BlockSpec((1, d), lambda i: (0, 0)),
        ],
        out_specs=pl.BlockSpec((tm, d), lambda i: (i, 0)),
        compiler_params=_cparams(("parallel",)),
        name="ple_final",
    )(x1, ypairs, ypairs, wcols, p2d, gp, wpg, wp, gf)


def _block_diag(w, per):
    nb, bw, _ = w.shape
    w = w.reshape(nb // per, per, bw, bw)
    eye = jnp.eye(per, dtype=w.dtype)
    return jnp.einsum("cpij,pq->cpiqj", w, eye).reshape(nb // per, per * bw, per * bw)


def _regroup_kernel(wt_ref, o_ref):
    d = Z_MGB - Z_MGA
    c = np.cumsum((0, NSA_Q_WIDTH) + (NSA_KV_WIDTH,) * 6 + (3 * NSA_HEADS, LRU_WIDTH, LRU_WIDTH, d, d))
    seg = lambda k: wt_ref[int(c[k]):int(c[k + 1]), :]
    q, k_c, v_c, k_s, v_s, k_w, v_w, gates, lru_x, lru_y, mg_a, mg_b = (seg(k) for k in range(12))

    def pair(a, b):
        parts = []
        for g in range(NSA_KV_GROUPS):
            parts += [a[g * HEAD_DIM:(g + 1) * HEAD_DIM], b[g * HEAD_DIM:(g + 1) * HEAD_DIM]]
        return jnp.concatenate(parts, axis=0)

    pad = jnp.zeros((Z_WIDTH - int(c[-1]), wt_ref.shape[1]), wt_ref.dtype)
    rows = [mg_a, mg_b, q, pair(k_s, k_w), pair(v_s, v_w), k_c, v_c, lru_x, lru_y, gates, pad]
    o_ref[...] = jnp.concatenate(rows, axis=0).T.astype(o_ref.dtype)


def _regroup_w_in(w_in_t, tr=256):
    nw, d = w_in_t.shape
    return pl.pallas_call(
        _regroup_kernel,
        out_shape=jax.ShapeDtypeStruct((d, Z_WIDTH), BF16),
        grid=(d // tr,),
        in_specs=[pl.BlockSpec((nw, tr), lambda i: (0, i))],
        out_specs=pl.BlockSpec((tr, Z_WIDTH), lambda i: (i, 0)),
        compiler_params=_cparams(("parallel",)),
        name="regroup_w_in",
    )(w_in_t)


def _layer(x2d, batch, seq, ln_mix, w_in, cmp_k_pos, cmp_k_w1, cmp_k_w2, cmp_v_pos, cmp_v_w1, cmp_v_w2, conv_w,
           conv_b, lru_wa, lru_ba, lru_wx, lru_bx, lru_lambda, w_nsa_up, w_lru_up, w_out, ln_ffn, w_grp, b_grp, w_exp,
           b_exp, w_gate, w_up, w_down):
    n, d = x2d.shape
    g, hd = NSA_KV_GROUPS, HEAD_DIM

    z = _in_proj(x2d, ln_mix.reshape(1, d), _regroup_w_in(w_in.T))

    pos8 = lambda pos: jnp.broadcast_to(pos.reshape(1, -1), (SUBLANES, pos.size)).astype(BF16)
    kc, vct = _compress(z, batch, seq, pos8(cmp_k_pos), cmp_k_w1.astype(BF16), cmp_k_w2.astype(BF16),
                        pos8(cmp_v_pos), cmp_v_w1.astype(BF16), cmp_v_w2.T.astype(BF16))
    o = _nsa_attention(z, kc, vct, batch, seq)

    per = LRU_WT // LRU_BW
    lru = _rglru(z, batch, seq, conv_w, conv_b.reshape(1, -1), _block_diag(lru_wa, per).astype(BF16),
                 lru_ba.reshape(1, -1), _block_diag(lru_wx, per).astype(BF16), lru_bx.reshape(1, -1),
                 lru_lambda.reshape(1, -1))

    merged = _merge(o, lru, z, w_nsa_up.astype(BF16), w_lru_up.astype(BF16))

    wr = jnp.zeros((d, LANES), F32).at[:, 0:N_GROUPS].set(w_grp).at[:, SUBLANES:SUBLANES + N_EXPERTS].set(w_exp)
    br = jnp.zeros((LANES, 1), F32).at[0:N_GROUPS, 0].set(b_grp).at[SUBLANES:SUBLANES + N_EXPERTS, 0].set(b_exp)
    x1, xn, eid, ew = _out_route(merged, x2d, w_out.astype(BF16), ln_ffn.reshape(1, d), wr, br)

    nvalid, texp, cnt, tok, dst, n_rows_out = _moe_plan(eid[0:EXPERT_TOPK], n, MOE_TM)
    ypairs = _moe(nvalid, texp, cnt, tok, dst, xn, w_gate, w_up, w_down, n_rows_out)

    return x1, ypairs, ew.T


def kernel(x, p, ln_mix, w_in, cmp_k_pos, cmp_k_w1, cmp_k_w2, cmp_v_pos, cmp_v_w1, cmp_v_w2, conv_w, conv_b, lru_wa, lru_ba, lru_wx, lru_bx, lru_lambda, w_nsa_up, w_lru_up, w_out, ln_ffn, w_grp, b_grp, w_exp, b_exp, w_gate, w_up, w_down, ln_ple, w_ple, w_ple_gate, ln_final):
    batch, seq, d = x.shape
    assert p.shape[0] == 1, "the final norm is fused into the (single) layer's last kernel"
    n = batch * seq
    x1, ypairs, wcols = _layer(
        x.reshape(n, d), batch, seq, ln_mix[0], w_in[0], cmp_k_pos[0], cmp_k_w1[0], cmp_k_w2[0],
        cmp_v_pos[0], cmp_v_w1[0], cmp_v_w2[0], conv_w[0], conv_b[0], lru_wa[0], lru_ba[0], lru_wx[0], lru_bx[0],
        lru_lambda[0], w_nsa_up[0], w_lru_up[0], w_out[0], ln_ffn[0], w_grp[0], b_grp[0], w_exp[0], b_exp[0],
        w_gate[0], w_up[0], w_down[0])
    out = _ple_final(x1, ypairs, wcols, p[0].reshape(n, -1), ln_ple[0].reshape(1, d),
                     w_ple_gate[0].astype(BF16), w_ple[0].astype(BF16), ln_final.reshape(1, d))
    return out.reshape(batch, seq, d)
```

```python
import numpy as np
import jax
import jax.numpy as jnp
from jax import lax
from jax.experimental import pallas as pl
from jax.experimental.pallas import tpu as pltpu

F32 = jnp.float32
BF16 = jnp.bfloat16

NSA_HEADS = 16
NSA_KV_GROUPS = 4
NSA_HPG = NSA_HEADS // NSA_KV_GROUPS
HEAD_DIM = 64
NSA_Q_WIDTH = NSA_HEADS * HEAD_DIM
NSA_KV_WIDTH = NSA_KV_GROUPS * HEAD_DIM
CMP_BLOCK = 32
CMP_STRIDE = 16
CMP_HIDDEN = 2 * HEAD_DIM
SLC_BLOCK = 64
SLC_TOPN = 16
WINDOW = 512
ATTN_SCALE = HEAD_DIM ** -0.5
NEG_INF = -1e30
LOG2E = float(np.log2(np.e))
FORCE_SCORE = 1e4
LRU_WIDTH = 1024
LRU_BLOCKS = 16
LRU_BW = LRU_WIDTH // LRU_BLOCKS
CONV_WIDTH = 4
LRU_C = 8.0
N_GROUPS = 4
EXPERTS_PER_GROUP = 8
N_EXPERTS = N_GROUPS * EXPERTS_PER_GROUP
EXPERT_TOPK = 2
D_EXPERT = 512
EPS = 1e-6

LANES = 128
SUBLANES = 8
VMEM_LIMIT_BYTES = 56 * 1024 * 1024

Z_MGA = 0
Z_MGB = 2048
Z_Q = 4096
Z_KK = 5120
Z_VV = 5632
Z_KC = 6144
Z_VC = 6400
Z_LX = 6656
Z_LY = 7680
Z_GATE = 8704
Z_WIDTH = 9216
LRU_CB = 512
LRU_WT = 256
MOE_TM = 256
NSA_TQ = 256
NSA_CK = 512


def _cparams(sem, vmem=VMEM_LIMIT_BYTES):
    return pltpu.CompilerParams(dimension_semantics=sem, vmem_limit_bytes=vmem)


def _rms(x, g):
    return x * lax.rsqrt(jnp.mean(x * x, axis=-1, keepdims=True) + EPS) * g


def _gelu_tanh(x):
    return 0.5 * x * (1.0 + jnp.tanh(np.sqrt(2.0 / np.pi) * (x + 0.044715 * (x * x * x))))


def _sigmoid(x):
    return 1.0 / (1.0 + jnp.exp(-x))


def _row_pitch(d):
    k = d // LANES
    return k + 1 - (k % 2)


def _store_rows(ref, row0, val):
    rows, d = val.shape
    pitch = _row_pitch(d)
    for j in range(d // LANES):
        ref[pl.ds(row0 * pitch + j, rows, stride=pitch), :] = val[:, j * LANES:(j + 1) * LANES]
    for j in range(d // LANES, pitch):
        ref[pl.ds(row0 * pitch + j, rows, stride=pitch), :] = jnp.zeros((rows, LANES), val.dtype)


def _load_rows(ref, row0, rows, d):
    pitch = _row_pitch(d)
    return jnp.concatenate([ref[pl.ds(row0 * pitch + j, rows, stride=pitch), :] for j in range(d // LANES)], axis=1)


def _in_proj_kernel(x_ref, g_ref, w_ref, o_ref, h_ref):
    @pl.when(pl.program_id(1) == 0)
    def _():
        h_ref[...] = _rms(x_ref[...], g_ref[...]).astype(BF16)

    o_ref[...] = jnp.dot(h_ref[...], w_ref[...], preferred_element_type=F32).astype(o_ref.dtype)


def _in_proj(x2d, g, w, tm=1024, tn=1536):
    n, d = x2d.shape
    nw = w.shape[1]
    return pl.pallas_call(
        _in_proj_kernel,
        out_shape=jax.ShapeDtypeStruct((n, nw), BF16),
        grid=(n // tm, nw // tn),
        in_specs=[
            pl.BlockSpec((tm, d), lambda i, j: (i, 0)),
            pl.BlockSpec((1, d), lambda i, j: (0, 0)),
            pl.BlockSpec((d, tn), lambda i, j: (0, j)),
        ],
        out_specs=pl.BlockSpec((tm, tn), lambda i, j: (i, j)),
        scratch_shapes=[pltpu.VMEM((tm, d), BF16)],
        compiler_params=_cparams(("parallel", "arbitrary")),
        name="in_proj",
    )(x2d, g, w)


def _compress_kernel(zk_ref, zv_ref, kpos_ref, kw1_ref, kw2_ref, vpos_ref, vw1_ref, vw2t_ref, kc_ref, vct_ref,
                     xf_ref):
    seq = zk_ref.shape[0]
    ncp = seq // CMP_STRIDE
    hd = HEAD_DIM
    half = CMP_STRIDE * hd

    def half_windows(z_ref):
        xf_ref[...] = z_ref[...].astype(F32)
        lines = [xf_ref[pl.ds(r, ncp, stride=CMP_STRIDE), :].astype(BF16) for r in range(CMP_STRIDE)]
        return [jnp.concatenate([ln[:, gg * hd:(gg + 1) * hd] for ln in lines], axis=1) for gg in range(LANES // hd)]

    def hidden(x, pos_ref, w1_ref):
        w1 = w1_ref[...]
        ha = jnp.dot(x, w1[:half], preferred_element_type=F32)
        hb = jnp.dot(x, w1[half:], preferred_element_type=F32)
        hb = pltpu.roll(hb, hb.shape[0] - 1, axis=0)
        pc = jnp.dot(pos_ref[...], w1, preferred_element_type=F32)[0:1]
        return _gelu_tanh(ha + hb + pc).astype(BF16)

    for gg, x in enumerate(half_windows(zk_ref)):
        hk = hidden(x, kpos_ref, kw1_ref)
        kc_ref[0, gg] = jnp.dot(hk, kw2_ref[...], preferred_element_type=F32).astype(kc_ref.dtype)
    for gg, x in enumerate(half_windows(zv_ref)):
        hv = hidden(x, vpos_ref, vw1_ref)
        vct_ref[0, gg] = lax.dot_general(vw2t_ref[...], hv, (((1,), (1,)), ((), ())),
                                         preferred_element_type=F32).astype(vct_ref.dtype)


def _compress(z, batch, seq, kpos, kw1, kw2, vpos, vw1, vw2t):
    g, hd = NSA_KV_GROUPS, HEAD_DIM
    gpb = LANES // hd
    ncp = seq // CMP_STRIDE
    full = lambda a: pl.BlockSpec(a.shape, lambda i, j: (0,) * a.ndim)
    return pl.pallas_call(
        _compress_kernel,
        out_shape=(jax.ShapeDtypeStruct((batch, g, ncp, hd), BF16),
                   jax.ShapeDtypeStruct((batch, g, hd, ncp), BF16)),
        grid=(batch, g // gpb),
        in_specs=[
            pl.BlockSpec((seq, LANES), lambda i, j: (i, Z_KC // LANES + j)),
            pl.BlockSpec((seq, LANES), lambda i, j: (i, Z_VC // LANES + j)),
            full(kpos), full(kw1), full(kw2), full(vpos), full(vw1), full(vw2t),
        ],
        out_specs=(pl.BlockSpec((1, gpb, ncp, hd), lambda i, j: (i, j, 0, 0)),
                   pl.BlockSpec((1, gpb, hd, ncp), lambda i, j: (i, j, 0, 0))),
        scratch_shapes=[pltpu.VMEM((seq, LANES), F32)],
        compiler_params=_cparams(("parallel", "parallel")),
        name="compress",
    )(z, z, kpos, kw1, kw2, vpos, vw1, vw2t)


def _nsa_kernel(zq_ref, zkk_ref, zvv_ref, zg_ref, kc_ref, vct_ref, o_ref, kcomb_ref, vs_ref, vw_ref, score_ref, gate_ref,
                sa_ref, sb_ref):
    grp = pl.program_id(1)
    qi = pl.program_id(2)
    tq, hd, hpg = NSA_TQ, HEAD_DIM, NSA_HPG
    nq = tq * hpg
    seq = kcomb_ref.shape[0]
    ncp = kc_ref.shape[2]
    nsb = score_ref.shape[0]
    blk_shift = SLC_BLOCK.bit_length() - 1

    @pl.when(qi == 0)
    def _():
        kcomb_ref[:, 0:2 * hd] = zkk_ref[...]
        blk_of_row = lax.broadcasted_iota(jnp.int32, (seq, 2 * hd), 0) >> blk_shift
        lane = lax.broadcasted_iota(jnp.int32, (seq, 2 * hd), 1)
        kcomb_ref[:, 2 * hd:] = jnp.where(blk_of_row == lane, 1.0, 0.0).astype(BF16)
        ones = jnp.ones((seq, hd), BF16)
        vs_ref[...] = jnp.concatenate([zvv_ref[:, 0:hd], ones], axis=1)
        vw_ref[...] = jnp.concatenate([zvv_ref[:, hd:2 * hd], ones], axis=1)

    qt = zq_ref[...].T
    q = jnp.concatenate([qt[h * hd:(h + 1) * hd, :] for h in range(hpg)], axis=1)
    q = (q.astype(F32) * (ATTN_SCALE * LOG2E)).astype(BF16)
    zero = jnp.zeros((hd, nq), BF16)
    t_lane = qi * tq + (lax.broadcasted_iota(jnp.int32, (1, nq), 1) & (tq - 1))
    tn_dims = (((0,), (0,)), ((), ()))

    sc = jnp.dot(kc_ref[0, 0], q, preferred_element_type=F32)
    cmp_end = lax.broadcasted_iota(jnp.int32, (ncp, 1), 0) * CMP_STRIDE + (CMP_BLOCK - 1)
    cmask = cmp_end <= t_lane
    sc = jnp.where(cmask, sc, NEG_INF)
    pc = jnp.where(cmask, jnp.exp2(sc - jnp.max(sc, axis=0, keepdims=True)), 0.0)
    lc = jnp.sum(pc, axis=0, keepdims=True)
    pc = pc * (1.0 / jnp.where(lc > 0.0, lc, 1.0))
    o_cmp = jnp.dot(vct_ref[0, 0], pc.astype(BF16), preferred_element_type=F32)

    psum = pc[:, 0:tq]
    for h in range(1, hpg):
        psum = psum + pc[:, h * tq:(h + 1) * tq]
    jrow = lax.broadcasted_iota(jnp.int32, (nsb, ncp), 0)
    crel = lax.broadcasted_iota(jnp.int32, (nsb, ncp), 1) - jrow * (SLC_BLOCK // CMP_STRIDE)
    ovt = jnp.where((crel >= 0) & (crel <= 2), 1.0, jnp.where((crel == -1) | (crel == 3), 0.5, 0.0)).astype(F32)
    imp = jnp.dot(ovt, psum, preferred_element_type=F32, precision=lax.Precision.HIGHEST)
    jblk = lax.broadcasted_iota(jnp.int32, (nsb, tq), 0)
    blk = (qi * tq + lax.broadcasted_iota(jnp.int32, (1, tq), 1)) >> blk_shift
    forced = (jblk == 0) | (jblk == blk) | (jblk == blk - 1)
    score = jnp.where(forced, FORCE_SCORE, jnp.where(jblk <= blk, imp, -1.0))
    score_ref[...] = score

    def rank_one(k, cnt):
        row = score_ref[pl.ds(k, 1), :]
        ge = jnp.where(row >= score, 1.0, 0.0)
        gt = jnp.where(row > score, 1.0, 0.0)
        return cnt + jnp.where(jblk > k, ge, gt)

    bpt = tq // SLC_BLOCK

    def rank_group(i, cnt):
        for u in range(bpt):
            cnt = rank_one(bpt * i + u, cnt)
        return cnt

    n_vis = jnp.minimum((qi + 1) * bpt, nsb)
    rank = lax.fori_loop(0, n_vis // bpt, rank_group, jnp.zeros((nsb, tq), F32))
    selb = jnp.where(rank < float(min(SLC_TOPN, nsb)), 0.0, NEG_INF).astype(BF16)

    ck = NSA_CK
    pad = jnp.zeros((2 * hd - nsb, nq), BF16)
    qa = jnp.concatenate([q, zero, jnp.concatenate([selb] * hpg, axis=1), pad], axis=0)

    def slc_scores(c):
        return jnp.dot(kcomb_ref[pl.ds(pl.multiple_of(c * ck, ck), ck), :], qa, preferred_element_type=F32)

    n_full = (qi * tq) // ck

    def slc_update(buf, c, carry, causal):
        def scores():
            if not causal:
                return buf[...]
            return jnp.where(c * ck + lax.broadcasted_iota(jnp.int32, (ck, 1), 0) <= t_lane, buf[...], NEG_INF)

        m, l, acc = carry
        m_new = jnp.maximum(m, jnp.max(scores(), axis=0, keepdims=True))
        alpha = jnp.exp2(m - m_new)
        p = jnp.exp2(scores() - m_new).astype(BF16)
        v = vs_ref[pl.ds(pl.multiple_of(c * ck, ck), ck), :]
        pv = lax.dot_general(v, p, tn_dims, preferred_element_type=F32)
        return m_new, alpha * l + pv[hd:hd + 1], acc * alpha + pv[0:hd]

    def slc_pair(i, carry):
        sb_ref[...] = slc_scores(2 * i + 1)
        carry = slc_update(sa_ref, 2 * i, carry, False)
        sa_ref[...] = slc_scores(2 * i + 2)
        return slc_update(sb_ref, 2 * i + 1, carry, False)

    def tail_two(carry):
        sb_ref[...] = slc_scores(n_full)
        return slc_update(sb_ref, n_full, slc_update(sa_ref, n_full - 1, carry, False), True)

    def tail_one(carry):
        return slc_update(sa_ref, n_full, carry, True)

    sa_ref[...] = slc_scores(0)

    nwc = WINDOW // tq + 1
    t_lo = t_lane - WINDOW
    ks_w, vs_w = [], []
    for i in range(nwc):
        rows = pl.ds(pl.multiple_of(jnp.maximum(qi - (nwc - 1) + i, 0) * tq, tq), tq)
        ks_w.append(kcomb_ref[rows, 0:2 * hd])
        vs_w.append(vw_ref[rows, :])
    qw = jnp.concatenate([zero, q], axis=0)
    sw = jnp.dot(jnp.concatenate(ks_w, axis=0), qw, preferred_element_type=F32)
    sw_parts = []
    for i in range(nwc):
        spos = (qi - (nwc - 1) + i) * tq + lax.broadcasted_iota(jnp.int32, (tq, 1), 0)
        if i < nwc - 1:
            ok = jnp.where(spos >= 0, spos, -(1 << 24)) > t_lo
        else:
            ok = spos <= t_lane
        sw_parts.append(jnp.where(ok, sw[i * tq:(i + 1) * tq], NEG_INF))
    sw = jnp.concatenate(sw_parts, axis=0)
    pw = jnp.exp2(sw - jnp.max(sw, axis=0, keepdims=True))
    ow = lax.dot_general(jnp.concatenate(vs_w, axis=0), pw.astype(BF16), tn_dims, preferred_element_type=F32)
    o_win = ow[0:hd] * (1.0 / ow[hd:hd + 1])
    init = (jnp.full((1, nq), NEG_INF, F32), jnp.zeros((1, nq), F32), jnp.zeros((hd, nq), F32))
    carry = lax.fori_loop(0, n_full // 2, slc_pair, init)
    _, l_s, acc_s = lax.cond(n_full % 2 == 1, tail_two, tail_one, carry)
    o_slc = acc_s * (1.0 / l_s)

    gate_ref[...] = _sigmoid(zg_ref[...].astype(F32)).T

    def branch_gate(br):
        rows = [gate_ref[pl.ds(grp * (hpg * 3) + h * 3 + br, 1), :] for h in range(hpg)]
        return jnp.concatenate(rows, axis=1)

    o = branch_gate(0) * o_cmp + branch_gate(1) * o_slc + branch_gate(2) * o_win
    o_heads = jnp.concatenate([o[:, h * tq:(h + 1) * tq] for h in range(hpg)], axis=0)
    o_ref[...] = o_heads.T.astype(o_ref.dtype)


def _nsa_attention(z, kc, vct, batch, seq):
    n = z.shape[0]
    g, hd, tq = NSA_KV_GROUPS, HEAD_DIM, NSA_TQ
    nqt = seq // tq
    ncp = kc.shape[2]
    nsb = seq // SLC_BLOCK
    gw = NSA_HPG * hd
    assert seq % NSA_CK == 0 and NSA_CK % tq == 0 and WINDOW % tq == 0 and nsb <= 2 * hd
    tile = lambda i, j, k: i * nqt + k
    return pl.pallas_call(
        _nsa_kernel,
        out_shape=jax.ShapeDtypeStruct((n, NSA_Q_WIDTH), BF16),
        grid=(batch, g, nqt),
        in_specs=[
            pl.BlockSpec((tq, gw), lambda i, j, k: (tile(i, j, k), Z_Q // gw + j)),
            pl.BlockSpec((seq, 2 * hd), lambda i, j, k: (i, Z_KK // (2 * hd) + j)),
            pl.BlockSpec((seq, 2 * hd), lambda i, j, k: (i, Z_VV // (2 * hd) + j)),
            pl.BlockSpec((tq, LANES), lambda i, j, k: (tile(i, j, k), Z_GATE // LANES)),
            pl.BlockSpec((1, 1, ncp, hd), lambda i, j, k: (i, j, 0, 0)),
            pl.BlockSpec((1, 1, hd, ncp), lambda i, j, k: (i, j, 0, 0)),
        ],
        out_specs=pl.BlockSpec((tq, gw), lambda i, j, k: (tile(i, j, k), j)),
        scratch_shapes=[pltpu.VMEM((seq, 4 * hd), BF16), pltpu.VMEM((seq, 2 * hd), BF16),
                        pltpu.VMEM((seq, 2 * hd), BF16), pltpu.VMEM((nsb, tq), F32),
                        pltpu.VMEM((LANES, tq), F32), pltpu.VMEM((NSA_CK, NSA_HPG * tq), F32),
                        pltpu.VMEM((NSA_CK, NSA_HPG * tq), F32)],
        compiler_params=_cparams(("parallel", "parallel", "arbitrary")),
        name="nsa_attn",
    )(z, z, z, z, kc, vct)


def _rglru_kernel(x_ref, y_ref, cw_ref, cb_ref, wa_ref, ba_ref, wx_ref, bx_ref, lam_ref, o_ref,
                  tail_ref, h_ref, a_ref, u_ref):
    tc = pl.program_id(2)
    tt, cb = x_ref.shape

    @pl.when(tc == 0)
    def _():
        tail_ref[...] = jnp.zeros_like(tail_ref)
        h_ref[...] = jnp.zeros_like(h_ref)

    x = x_ref[...].astype(F32)
    xe = jnp.concatenate([tail_ref[...], x], axis=0)
    tail_ref[...] = x[tt - SUBLANES:, :]
    cw = cw_ref[...]
    xc = cb_ref[...]
    for k in range(CONV_WIDTH):
        off = SUBLANES - (CONV_WIDTH - 1) + k
        xc = xc + cw[k:k + 1, :] * xe[off:off + tt, :]
    xcb = xc.astype(BF16)
    def gate(w_ref, b_ref):
        parts = [jnp.dot(xcb[:, c * LRU_WT:(c + 1) * LRU_WT], w_ref[c], preferred_element_type=F32)
                 for c in range(cb // LRU_WT)]
        return _sigmoid(jnp.concatenate(parts, axis=1) + b_ref[...])

    r = gate(wa_ref, ba_ref)
    ig = gate(wx_ref, bx_ref)
    nl = -lam_ref[...]
    softplus = jnp.maximum(nl, 0.0) + jnp.log1p(jnp.exp(-jnp.abs(nl)))
    log_a = (-LRU_C) * softplus * r
    a_ref[...] = jnp.exp(log_a)
    th = jnp.tanh(log_a)
    u_ref[...] = jnp.sqrt(-2.0 * th / (1.0 - th)) * (ig * xc)

    row = lax.broadcasted_iota(jnp.int32, (SUBLANES, cb), 0)

    def step(i, h):
        sl = pl.ds(pl.multiple_of(i * SUBLANES, SUBLANES), SUBLANES)
        a = a_ref[sl, :]
        u = u_ref[sl, :]
        for s in (1, 2, 4):
            a_s = jnp.where(row >= s, pltpu.roll(a, s, axis=0), 1.0)
            u_s = jnp.where(row >= s, pltpu.roll(u, s, axis=0), 0.0)
            u = a * u_s + u
            a = a * a_s
        hrows = a * h + u
        u_ref[sl, :] = hrows
        return hrows[SUBLANES - 1:SUBLANES, :]

    h_ref[...] = lax.fori_loop(0, tt // SUBLANES, step, h_ref[...], unroll=8)
    o_ref[...] = (u_ref[...] * _gelu_tanh(y_ref[...].astype(F32))).astype(o_ref.dtype)


def _rglru(z, batch, seq, cw, cbias, wa_bd, ba, wx_bd, bx, lam, tt=512):
    n = z.shape[0]
    ncb = LRU_WIDTH // LRU_CB
    nt = seq // tt
    row = lambda i, j, k: i * nt + k
    vec = lambda r: pl.BlockSpec((r, LRU_CB), lambda i, j, k: (0, j))
    return pl.pallas_call(
        _rglru_kernel,
        out_shape=jax.ShapeDtypeStruct((n, LRU_WIDTH), BF16),
        grid=(batch, ncb, nt),
        in_specs=[
            pl.BlockSpec((tt, LRU_CB), lambda i, j, k: (row(i, j, k), Z_LX // LRU_CB + j)),
            pl.BlockSpec((tt, LRU_CB), lambda i, j, k: (row(i, j, k), Z_LY // LRU_CB + j)),
            vec(CONV_WIDTH), vec(1),
            pl.BlockSpec((LRU_CB // LRU_WT, LRU_WT, LRU_WT), lambda i, j, k: (j, 0, 0)), vec(1),
            pl.BlockSpec((LRU_CB // LRU_WT, LRU_WT, LRU_WT), lambda i, j, k: (j, 0, 0)), vec(1),
            vec(1),
        ],
        out_specs=pl.BlockSpec((tt, LRU_CB), lambda i, j, k: (row(i, j, k), j)),
        scratch_shapes=[pltpu.VMEM((SUBLANES, LRU_CB), F32), pltpu.VMEM((1, LRU_CB), F32),
                        pltpu.VMEM((tt, LRU_CB), F32), pltpu.VMEM((tt, LRU_CB), F32)],
        compiler_params=_cparams(("parallel", "parallel", "arbitrary")),
        name="rglru",
    )(z, z, cw, cbias, wa_bd, ba, wx_bd, bx, lam)


def _merge_kernel(o_ref, l_ref, mga_ref, mgb_ref, wn_ref, wl_ref, m_ref):
    ya = jnp.dot(o_ref[...], wn_ref[...], preferred_element_type=F32)
    yb = jnp.dot(l_ref[...], wl_ref[...], preferred_element_type=F32)
    m = _sigmoid(mga_ref[...].astype(F32)) * ya + _sigmoid(mgb_ref[...].astype(F32)) * yb
    m_ref[...] = m.astype(m_ref.dtype)


def _merge(o, lru, z, wn, wl, tm=512):
    n, d = o.shape[0], wn.shape[1]
    return pl.pallas_call(
        _merge_kernel,
        out_shape=jax.ShapeDtypeStruct((n, d), BF16),
        grid=(n // tm,),
        in_specs=[
            pl.BlockSpec((tm, o.shape[1]), lambda i: (i, 0)),
            pl.BlockSpec((tm, lru.shape[1]), lambda i: (i, 0)),
            pl.BlockSpec((tm, d), lambda i: (i, Z_MGA // d)),
            pl.BlockSpec((tm, d), lambda i: (i, Z_MGB // d)),
            pl.BlockSpec(wn.shape, lambda i: (0, 0)),
            pl.BlockSpec(wl.shape, lambda i: (0, 0)),
        ],
        out_specs=pl.BlockSpec((tm, d), lambda i: (i, 0)),
        compiler_params=_cparams(("parallel",)),
        name="merge",
    )(o, lru, z, z, wn, wl)


def _out_route_kernel(m_ref, x_ref, wo_ref, g_ref, wr_ref, br_ref, x1_ref, xn_ref, eid_ref, ew_ref):
    x1 = x_ref[...] + jnp.dot(m_ref[...], wo_ref[...], preferred_element_type=F32)
    x1_ref[...] = x1
    xn = _rms(x1, g_ref[...])
    _store_rows(xn_ref, 0, xn)
    lg = jnp.dot(xn, wr_ref[...], preferred_element_type=F32).T + br_ref[...]
    tm = lg.shape[1]
    sub = lax.broadcasted_iota(jnp.int32, (SUBLANES, tm), 0)

    def first_argmax(v, vmax):
        return jnp.min(jnp.where(v == vmax, sub, SUBLANES), axis=0, keepdims=True)

    gl = jnp.where(sub < N_GROUPS, lg[0:SUBLANES], -jnp.inf)
    gmax = jnp.max(gl, axis=0, keepdims=True)
    ge = jnp.exp(gl - gmax)
    gprob = ge / jnp.sum(ge, axis=0, keepdims=True)
    g_val = jnp.max(gprob, axis=0, keepdims=True)
    g_idx = first_argmax(gprob, g_val)
    e_in = jnp.zeros((EXPERTS_PER_GROUP, tm), F32)
    for gi in range(N_GROUPS):
        lo = SUBLANES + gi * EXPERTS_PER_GROUP
        e_in = jnp.where(g_idx == gi, lg[lo:lo + EXPERTS_PER_GROUP], e_in)
    ee = jnp.exp(e_in - jnp.max(e_in, axis=0, keepdims=True))
    eprob = ee / jnp.sum(ee, axis=0, keepdims=True)
    v1 = jnp.max(eprob, axis=0, keepdims=True)
    i1 = first_argmax(eprob, v1)
    rest = jnp.where(sub == i1, -1.0, eprob)
    v2 = jnp.max(rest, axis=0, keepdims=True)
    i2 = first_argmax(rest, v2)
    den = v1 + v2
    eid = jnp.where(sub == 0, g_idx * EXPERTS_PER_GROUP + i1, g_idx * EXPERTS_PER_GROUP + i2)
    eid_ref[...] = eid
    ew_ref[...] = jnp.where(sub == 0, g_val * v1 / den, g_val * v2 / den)


def _out_route(m, x2d, wo, g, wr, br, tm=512):
    n, d = x2d.shape
    once = pl.Buffered(1)
    return pl.pallas_call(
        _out_route_kernel,
        out_shape=(jax.ShapeDtypeStruct((n, d), F32), jax.ShapeDtypeStruct((n * _row_pitch(d), LANES), F32),
                   jax.ShapeDtypeStruct((SUBLANES, n), jnp.int32), jax.ShapeDtypeStruct((SUBLANES, n), F32)),
        grid=(n // tm,),
        in_specs=[
            pl.BlockSpec((tm, d), lambda i: (i, 0)),
            pl.BlockSpec((tm, d), lambda i: (i, 0)),
            pl.BlockSpec((d, d), lambda i: (0, 0), pipeline_mode=once),
            pl.BlockSpec((1, d), lambda i: (0, 0)),
            pl.BlockSpec(wr.shape, lambda i: (0, 0), pipeline_mode=once),
            pl.BlockSpec(br.shape, lambda i: (0, 0)),
        ],
        out_specs=(pl.BlockSpec((tm, d), lambda i: (i, 0)), pl.BlockSpec((tm * _row_pitch(d), LANES), lambda i: (i, 0)),
                   pl.BlockSpec((SUBLANES, tm), lambda i: (0, i)), pl.BlockSpec((SUBLANES, tm), lambda i: (0, i))),
        compiler_params=_cparams(("parallel",)),
        name="out_route",
    )(m, x2d, wo, g, wr, br)


def _for_rows(cnt, fn, unroll=8):
    sh = unroll.bit_length() - 1

    def group(gidx, c):
        for u in range(unroll):
            fn(gidx * unroll + u)
        return c

    def single(r, c):
        fn(r)
        return c

    lax.fori_loop(0, cnt >> sh, group, 0)
    lax.fori_loop((cnt >> sh) << sh, cnt, single, 0)


def _moe_kernel(nvalid_ref, texp_ref, cnt_ref, tok_ref, tokn_ref, dst_ref, xn_hbm, wg_ref, wu_ref, wd_ref, y_hbm,
                xbuf, ybuf, wgb, wub, wdb, gsem, ssem):
    i = pl.program_id(0)
    nv = nvalid_ref[0]
    slot = i % 2
    d = wg_ref.shape[1]
    k = _row_pitch(d)
    tm = xbuf.shape[0] // (2 * k)

    def line(ref, row):
        return ref.at[pl.ds(row * k, k), :]

    def gather_row(idx_ref, s):
        def fn(r):
            pltpu.make_async_copy(line(xn_hbm, idx_ref[0, 0, r]), line(xbuf, s * tm + r), gsem.at[s]).start()
        return fn

    def gather_wait_row(s):
        def fn(r):
            pltpu.make_async_copy(line(xn_hbm, 0), line(xbuf, s * tm + r), gsem.at[s]).wait()
        return fn

    def scatter_row(s):
        def fn(r):
            pltpu.make_async_copy(line(ybuf, s * tm + r), line(y_hbm, dst_ref[0, 0, r]), ssem.at[s]).start()
        return fn

    def scatter_wait_row(s):
        def fn(r):
            pltpu.make_async_copy(line(ybuf, s * tm + r), line(y_hbm, 0), ssem.at[s]).wait()
        return fn

    @pl.when(i == 0)
    def _():
        xbuf[...] = jnp.zeros_like(xbuf)
        _for_rows(cnt_ref[0], gather_row(tok_ref, 0))

    @pl.when(i + 1 < nv)
    def _():
        _for_rows(cnt_ref[i + 1], gather_row(tokn_ref, 1 - slot))

    @pl.when(i < nv)
    def _():
        @pl.when((i == 0) | (texp_ref[i] != texp_ref[jnp.maximum(i - 1, 0)]))
        def _():
            wgb[...] = wg_ref[0].astype(BF16)
            wub[...] = wu_ref[0].astype(BF16)
            wdb[...] = wd_ref[0].astype(BF16)

        _for_rows(cnt_ref[i], gather_wait_row(slot))
        x = _load_rows(xbuf, slot * tm, tm, d).astype(BF16)
        gp = jnp.dot(x, wgb[...], preferred_element_type=F32)
        up = jnp.dot(x, wub[...], preferred_element_type=F32)
        hid = (gp * _sigmoid(gp) * up).astype(BF16)
        y = jnp.dot(hid, wdb[...], preferred_element_type=F32)

        @pl.when(i >= 2)
        def _():
            _for_rows(cnt_ref[i - 2], scatter_wait_row(slot))

        _store_rows(ybuf, slot * tm, y)
        _for_rows(cnt_ref[i], scatter_row(slot))

    @pl.when(i == nv - 1)
    def _():
        _for_rows(cnt_ref[i], scatter_wait_row(slot))

        @pl.when(nv >= 2)
        def _():
            _for_rows(cnt_ref[i - 1], scatter_wait_row(1 - slot))


def _moe(nvalid, texp, cnt, tok, dst, xn, wg, wu, wd, n_rows_out):
    nt, _, tm = tok.shape
    d, de = wg.shape[1], wg.shape[2]
    k = _row_pitch(d)
    grid_spec = pltpu.PrefetchScalarGridSpec(
        num_scalar_prefetch=3,
        grid=(nt,),
        in_specs=[
            pl.BlockSpec((1, 1, tm), lambda i, nv, te, ct: (i, 0, 0), memory_space=pltpu.SMEM),
            pl.BlockSpec((1, 1, tm), lambda i, nv, te, ct: (jnp.minimum(i + 1, nt - 1), 0, 0),
                         memory_space=pltpu.SMEM),
            pl.BlockSpec((1, 1, tm), lambda i, nv, te, ct: (i, 0, 0), memory_space=pltpu.SMEM),
            pl.BlockSpec(memory_space=pl.ANY),
            pl.BlockSpec((1, d, de), lambda i, nv, te, ct: (te[i], 0, 0)),
            pl.BlockSpec((1, d, de), lambda i, nv, te, ct: (te[i], 0, 0)),
            pl.BlockSpec((1, de, d), lambda i, nv, te, ct: (te[i], 0, 0)),
        ],
        out_specs=pl.BlockSpec(memory_space=pl.ANY),
        scratch_shapes=[pltpu.VMEM((2 * tm * k, LANES), F32), pltpu.VMEM((2 * tm * k, LANES), F32),
                        pltpu.VMEM((d, de), BF16), pltpu.VMEM((d, de), BF16), pltpu.VMEM((de, d), BF16),
                        pltpu.SemaphoreType.DMA((2,)), pltpu.SemaphoreType.DMA((2,))],
    )
    return pl.pallas_call(
        _moe_kernel,
        out_shape=jax.ShapeDtypeStruct((n_rows_out * k, LANES), F32),
        grid_spec=grid_spec,
        compiler_params=_cparams(("arbitrary",)),
        name="moe",
    )(nvalid, texp, cnt, tok, tok, dst, xn, wg, wu, wd)


def _moe_plan(eid, n, tm):
    e_flat = eid.reshape(-1)
    npairs = e_flat.shape[0]
    experts = jnp.arange(N_EXPERTS, dtype=jnp.int32)
    counts = jnp.sum((e_flat[:, None] == experts[None, :]).astype(jnp.int32), axis=0)
    padded = ((counts + tm - 1) // tm) * tm
    ends = jnp.cumsum(padded)
    offs = ends - padded
    n_rows = npairs + N_EXPERTS * tm
    nt = n_rows // tm
    fill = jnp.arange(n_rows - npairs, dtype=jnp.int32)
    fill_expert = jnp.sum((fill[:, None] >= jnp.cumsum(padded - counts)[None, :]).astype(jnp.int32), axis=1)
    keys = jnp.concatenate([e_flat * (2 * npairs) + jnp.arange(npairs, dtype=jnp.int32),
                            fill_expert * (2 * npairs) + npairs])
    skeys = jnp.sort(keys)
    low = skeys % (2 * npairs)
    dst = jnp.where(low < npairs, low, 0)
    tok = dst % n
    tile_start = jnp.arange(nt, dtype=jnp.int32) * tm
    texp = jnp.minimum(jnp.sum((tile_start[:, None] >= ends[None, :]).astype(jnp.int32), axis=1), N_EXPERTS - 1)
    cnt = jnp.clip(offs[texp] + counts[texp] - tile_start, 0, tm)
    nvalid = (ends[-1] // tm).astype(jnp.int32).reshape(1)
    return (nvalid, texp.astype(jnp.int32), cnt.astype(jnp.int32), tok.reshape(nt, 1, tm), dst.reshape(nt, 1, tm),
            npairs)


def _ple_final_kernel(x1_ref, y0_ref, y1_ref, w_ref, p_ref, gp_ref, wpg_ref, wp_ref, gf_ref, o_ref):
    w = w_ref[...]
    tm, d = x1_ref.shape
    x2 = x1_ref[...] + w[:, 0:1] * _load_rows(y0_ref, 0, tm, d) + w[:, 1:2] * _load_rows(y1_ref, 0, tm, d)
    hn = _rms(x2, gp_ref[...]).astype(BF16)
    gate = _sigmoid(jnp.dot(hn, wpg_ref[...], preferred_element_type=F32))
    pe = jnp.dot(p_ref[...].astype(BF16), wp_ref[...], preferred_element_type=F32)
    x3 = x2 + gate * pe
    o_ref[...] = _rms(x3, gf_ref[...])


def _ple_final(x1, ypairs, wcols, p2d, gp, wpg, wp, gf, tm=256):
    n, d = x1.shape
    nb = n // tm
    k = _row_pitch(d)
    return pl.pallas_call(
        _ple_final_kernel,
        out_shape=jax.ShapeDtypeStruct((n, d), F32),
        grid=(nb,),
        in_specs=[
            pl.BlockSpec((tm, d), lambda i: (i, 0)),
            pl.BlockSpec((tm * k, LANES), lambda i: (i, 0)),
            pl.BlockSpec((tm * k, LANES), lambda i: (nb + i, 0)),
            pl.BlockSpec((tm, wcols.shape[1]), lambda i: (i, 0)),
            pl.BlockSpec((tm, p2d.shape[1]), lambda i: (i, 0)),
            pl.BlockSpec((1, d), lambda i: (0, 0)),
            pl.BlockSpec((d, d), lambda i: (0, 0)),
            pl.BlockSpec(wp.shape, lambda i: (0, 0)),
            pl.BlockSpec((1, d), lambda i: (0, 0)),
        ],
        out_specs=pl.BlockSpec((tm, d), lambda i: (i, 0)),
        compiler_params=_cparams(("parallel",)),
        name="ple_final",
    )(x1, ypairs, ypairs, wcols, p2d, gp, wpg, wp, gf)


def _block_diag(w, per):
    nb, bw, _ = w.shape
    w = w.reshape(nb // per, per, bw, bw)
    eye = jnp.eye(per, dtype=w.dtype)
    return jnp.einsum("cpij,pq->cpiqj", w, eye).reshape(nb // per, per * bw, per * bw)


def _regroup_kernel(wt_ref, o_ref):
    d = Z_MGB - Z_MGA
    c = np.cumsum((0, NSA_Q_WIDTH) + (NSA_KV_WIDTH,) * 6 + (3 * NSA_HEADS, LRU_WIDTH, LRU_WIDTH, d, d))
    seg = lambda k: wt_ref[int(c[k]):int(c[k + 1]), :]
    q, k_c, v_c, k_s, v_s, k_w, v_w, gates, lru_x, lru_y, mg_a, mg_b = (seg(k) for k in range(12))

    def pair(a, b):
        parts = []
        for g in range(NSA_KV_GROUPS):
            parts += [a[g * HEAD_DIM:(g + 1) * HEAD_DIM], b[g * HEAD_DIM:(g + 1) * HEAD_DIM]]
        return jnp.concatenate(parts, axis=0)

    pad = jnp.zeros((Z_WIDTH - int(c[-1]), wt_ref.shape[1]), wt_ref.dtype)
    rows = [mg_a, mg_b, q, pair(k_s, k_w), pair(v_s, v_w), k_c, v_c, lru_x, lru_y, gates, pad]
    o_ref[...] = jnp.concatenate(rows, axis=0).T.astype(o_ref.dtype)


def _regroup_w_in(w_in_t, tr=256):
    nw, d = w_in_t.shape
    return pl.pallas_call(
        _regroup_kernel,
        out_shape=jax.ShapeDtypeStruct((d, Z_WIDTH), BF16),
        grid=(d // tr,),
        in_specs=[pl.BlockSpec((nw, tr), lambda i: (0, i))],
        out_specs=pl.BlockSpec((tr, Z_WIDTH), lambda i: (i, 0)),
        compiler_params=_cparams(("parallel",)),
        name="regroup_w_in",
    )(w_in_t)


def _layer(x2d, batch, seq, ln_mix, w_in, cmp_k_pos, cmp_k_w1, cmp_k_w2, cmp_v_pos, cmp_v_w1, cmp_v_w2, conv_w,
           conv_b, lru_wa, lru_ba, lru_wx, lru_bx, lru_lambda, w_nsa_up, w_lru_up, w_out, ln_ffn, w_grp, b_grp, w_exp,
           b_exp, w_gate, w_up, w_down):
    n, d = x2d.shape
    g, hd = NSA_KV_GROUPS, HEAD_DIM

    z = _in_proj(x2d, ln_mix.reshape(1, d), _regroup_w_in(w_in.T))

    pos8 = lambda pos: jnp.broadcast_to(pos.reshape(1, -1), (SUBLANES, pos.size)).astype(BF16)
    kc, vct = _compress(z, batch, seq, pos8(cmp_k_pos), cmp_k_w1.astype(BF16), cmp_k_w2.astype(BF16),
                        pos8(cmp_v_pos), cmp_v_w1.astype(BF16), cmp_v_w2.T.astype(BF16))
    o = _nsa_attention(z, kc, vct, batch, seq)

    per = LRU_WT // LRU_BW
    lru = _rglru(z, batch, seq, conv_w, conv_b.reshape(1, -1), _block_diag(lru_wa, per).astype(BF16),
                 lru_ba.reshape(1, -1), _block_diag(lru_wx, per).astype(BF16), lru_bx.reshape(1, -1),
                 lru_lambda.reshape(1, -1))

    merged = _merge(o, lru, z, w_nsa_up.astype(BF16), w_lru_up.astype(BF16))

    wr = jnp.zeros((d, LANES), F32).at[:, 0:N_GROUPS].set(w_grp).at[:, SUBLANES:SUBLANES + N_EXPERTS].set(w_exp)
    br = jnp.zeros((LANES, 1), F32).at[0:N_GROUPS, 0].set(b_grp).at[SUBLANES:SUBLANES + N_EXPERTS, 0].set(b_exp)
    x1, xn, eid, ew = _out_route(merged, x2d, w_out.astype(BF16), ln_ffn.reshape(1, d), wr, br)

    nvalid, texp, cnt, tok, dst, n_rows_out = _moe_plan(eid[0:EXPERT_TOPK], n, MOE_TM)
    ypairs = _moe(nvalid, texp, cnt, tok, dst, xn, w_gate, w_up, w_down, n_rows_out)

    return x1, ypairs, ew.T


def kernel(x, p, ln_mix, w_in, cmp_k_pos, cmp_k_w1, cmp_k_w2, cmp_v_pos, cmp_v_w1, cmp_v_w2, conv_w, conv_b, lru_wa, lru_ba, lru_wx, lru_bx, lru_lambda, w_nsa_up, w_lru_up, w_out, ln_ffn, w_grp, b_grp, w_exp, b_exp, w_gate, w_up, w_down, ln_ple, w_ple, w_ple_gate, ln_final):
    batch, seq, d = x.shape
    assert p.shape[0] == 1, "the final norm is fused into the (single) layer's last kernel"
    n = batch * seq
    x1, ypairs, wcols = _layer(
        x.reshape(n, d), batch, seq, ln_mix[0], w_in[0], cmp_k_pos[0], cmp_k_w1[0], cmp_k_w2[0],
        cmp_v_pos[0], cmp_v_w1[0], cmp_v_w2[0], conv_w[0], conv_b[0], lru_wa[0], lru_ba[0], lru_wx[0], lru_bx[0],
        lru_lambda[0], w_nsa_up[0], w_lru_up[0], w_out[0], ln_ffn[0], w_grp[0], b_grp[0], w_exp[0], b_exp[0],
        w_gate[0], w_up[0], w_down[0])
    out = _ple_final(x1, ypairs, wcols, p[0].reshape(n, -1), ln_ple[0].reshape(1, d),
                     w_ple_gate[0].astype(BF16), w_ple[0].astype(BF16), ln_final.reshape(1, d))
    return out.reshape(batch, seq, d)
```

```python
import numpy as np
import jax
import jax.numpy as jnp
from jax import lax
from jax.experimental import pallas as pl
from jax.experimental.pallas import tpu as pltpu

F32 = jnp.float32
BF16 = jnp.bfloat16

NSA_HEADS = 16
NSA_KV_GROUPS = 4
NSA_HPG = NSA_HEADS // NSA_KV_GROUPS
HEAD_DIM = 64
NSA_Q_WIDTH = NSA_HEADS * HEAD_DIM
NSA_KV_WIDTH = NSA_KV_GROUPS * HEAD_DIM
CMP_BLOCK = 32
CMP_STRIDE = 16
CMP_HIDDEN = 2 * HEAD_DIM
SLC_BLOCK = 64
SLC_TOPN = 16
WINDOW = 512
ATTN_SCALE = HEAD_DIM ** -0.5
NEG_INF = -1e30
LOG2E = float(np.log2(np.e))
FORCE_SCORE = 1e4
LRU_WIDTH = 1024
LRU_BLOCKS = 16
LRU_BW = LRU_WIDTH // LRU_BLOCKS
CONV_WIDTH = 4
LRU_C = 8.0
N_GROUPS = 4
EXPERTS_PER_GROUP = 8
N_EXPERTS = N_GROUPS * EXPERTS_PER_GROUP
EXPERT_TOPK = 2
D_EXPERT = 512
EPS = 1e-6

LANES = 128
SUBLANES = 8
VMEM_LIMIT_BYTES = 56 * 1024 * 1024

Z_MGA = 0
Z_MGB = 2048
Z_Q = 4096
Z_KK = 5120
Z_VV = 5632
Z_KC = 6144
Z_VC = 6400
Z_LX = 6656
Z_LY = 7680
Z_GATE = 8704
Z_WIDTH = 9216
LRU_CB = 512
LRU_WT = 256
MOE_TM = 256
NSA_TQ = 256
NSA_CK = 512


def _cparams(sem, vmem=VMEM_LIMIT_BYTES):
    return pltpu.CompilerParams(dimension_semantics=sem, vmem_limit_bytes=vmem)


def _rms(x, g):
    return x * lax.rsqrt(jnp.mean(x * x, axis=-1, keepdims=True) + EPS) * g


def _gelu_tanh(x):
    return 0.5 * x * (1.0 + jnp.tanh(np.sqrt(2.0 / np.pi) * (x + 0.044715 * (x * x * x))))


def _sigmoid(x):
    return 1.0 / (1.0 + jnp.exp(-x))


def _row_pitch(d):
    k = d // LANES
    return k + 1 - (k % 2)


def _store_rows(ref, row0, val):
    rows, d = val.shape
    pitch = _row_pitch(d)
    for j in range(d // LANES):
        ref[pl.ds(row0 * pitch + j, rows, stride=pitch), :] = val[:, j * LANES:(j + 1) * LANES]
    for j in range(d // LANES, pitch):
        ref[pl.ds(row0 * pitch + j, rows, stride=pitch), :] = jnp.zeros((rows, LANES), val.dtype)


def _load_rows(ref, row0, rows, d):
    pitch = _row_pitch(d)
    return jnp.concatenate([ref[pl.ds(row0 * pitch + j, rows, stride=pitch), :] for j in range(d // LANES)], axis=1)


def _in_proj_kernel(x_ref, g_ref, w_ref, o_ref, h_ref):
    @pl.when(pl.program_id(1) == 0)
    def _():
        h_ref[...] = _rms(x_ref[...], g_ref[...]).astype(BF16)

    o_ref[...] = jnp.dot(h_ref[...], w_ref[...], preferred_element_type=F32).astype(o_ref.dtype)


def _in_proj(x2d, g, w, tm=1024, tn=1536):
    n, d = x2d.shape
    nw = w.shape[1]
    return pl.pallas_call(
        _in_proj_kernel,
        out_shape=jax.ShapeDtypeStruct((n, nw), BF16),
        grid=(n // tm, nw // tn),
        in_specs=[
            pl.BlockSpec((tm, d), lambda i, j: (i, 0)),
            pl.BlockSpec((1, d), lambda i, j: (0, 0)),
            pl.BlockSpec((d, tn), lambda i, j: (0, j)),
        ],
        out_specs=pl.BlockSpec((tm, tn), lambda i, j: (i, j)),
        scratch_shapes=[pltpu.VMEM((tm, d), BF16)],
        compiler_params=_cparams(("parallel", "arbitrary")),
        name="in_proj",
    )(x2d, g, w)


def _compress_kernel(zk_ref, zv_ref, kpos_ref, kw1_ref, kw2_ref, vpos_ref, vw1_ref, vw2t_ref, kc_ref, vct_ref,
                     xf_ref):
    seq = zk_ref.shape[0]
    ncp = seq // CMP_STRIDE
    hd = HEAD_DIM
    half = CMP_STRIDE * hd

    def half_windows(z_ref):
        xf_ref[...] = z_ref[...].astype(F32)
        lines = [xf_ref[pl.ds(r, ncp, stride=CMP_STRIDE), :].astype(BF16) for r in range(CMP_STRIDE)]
        return [jnp.concatenate([ln[:, gg * hd:(gg + 1) * hd] for ln in lines], axis=1) for gg in range(LANES // hd)]

    def hidden(x, pos_ref, w1_ref):
        w1 = w1_ref[...]
        ha = jnp.dot(x, w1[:half], preferred_element_type=F32)
        hb = jnp.dot(x, w1[half:], preferred_element_type=F32)
        hb = pltpu.roll(hb, hb.shape[0] - 1, axis=0)
        pc = jnp.dot(pos_ref[...], w1, preferred_element_type=F32)[0:1]
        return _gelu_tanh(ha + hb + pc).astype(BF16)

    for gg, x in enumerate(half_windows(zk_ref)):
        hk = hidden(x, kpos_ref, kw1_ref)
        kc_ref[0, gg] = jnp.dot(hk, kw2_ref[...], preferred_element_type=F32).astype(kc_ref.dtype)
    for gg, x in enumerate(half_windows(zv_ref)):
        hv = hidden(x, vpos_ref, vw1_ref)
        vct_ref[0, gg] = lax.dot_general(vw2t_ref[...], hv, (((1,), (1,)), ((), ())),
                                         preferred_element_type=F32).astype(vct_ref.dtype)


def _compress(z, batch, seq, kpos, kw1, kw2, vpos, vw1, vw2t):
    g, hd = NSA_KV_GROUPS, HEAD_DIM
    gpb = LANES // hd
    ncp = seq // CMP_STRIDE
    full = lambda a: pl.BlockSpec(a.shape, lambda i, j: (0,) * a.ndim)
    return pl.pallas_call(
        _compress_kernel,
        out_shape=(jax.ShapeDtypeStruct((batch, g, ncp, hd), BF16),
                   jax.ShapeDtypeStruct((batch, g, hd, ncp), BF16)),
        grid=(batch, g // gpb),
        in_specs=[
            pl.BlockSpec((seq, LANES), lambda i, j: (i, Z_KC // LANES + j)),
            pl.BlockSpec((seq, LANES), lambda i, j: (i, Z_VC // LANES + j)),
            full(kpos), full(kw1), full(kw2), full(vpos), full(vw1), full(vw2t),
        ],
        out_specs=(pl.BlockSpec((1, gpb, ncp, hd), lambda i, j: (i, j, 0, 0)),
                   pl.BlockSpec((1, gpb, hd, ncp), lambda i, j: (i, j, 0, 0))),
        scratch_shapes=[pltpu.VMEM((seq, LANES), F32)],
        compiler_params=_cparams(("parallel", "parallel")),
        name="compress",
    )(z, z, kpos, kw1, kw2, vpos, vw1, vw2t)


def _nsa_kernel(zq_ref, zkk_ref, zvv_ref, zg_ref, kc_ref, vct_ref, o_ref, kcomb_ref, vs_ref, vw_ref, score_ref, gate_ref,
                sa_ref, sb_ref):
    grp = pl.program_id(1)
    qi = pl.program_id(2)
    tq, hd, hpg = NSA_TQ, HEAD_DIM, NSA_HPG
    nq = tq * hpg
    seq = kcomb_ref.shape[0]
    ncp = kc_ref.shape[2]
    nsb = score_ref.shape[0]
    blk_shift = SLC_BLOCK.bit_length() - 1

    @pl.when(qi == 0)
    def _():
        kcomb_ref[:, 0:2 * hd] = zkk_ref[...]
        blk_of_row = lax.broadcasted_iota(jnp.int32, (seq, 2 * hd), 0) >> blk_shift
        lane = lax.broadcasted_iota(jnp.int32, (seq, 2 * hd), 1)
        kcomb_ref[:, 2 * hd:] = jnp.where(blk_of_row == lane, 1.0, 0.0).astype(BF16)
        ones = jnp.ones((seq, hd), BF16)
        vs_ref[...] = jnp.concatenate([zvv_ref[:, 0:hd], ones], axis=1)
        vw_ref[...] = jnp.concatenate([zvv_ref[:, hd:2 * hd], ones], axis=1)

    qt = zq_ref[...].T
    q = jnp.concatenate([qt[h * hd:(h + 1) * hd, :] for h in range(hpg)], axis=1)
    q = (q.astype(F32) * (ATTN_SCALE * LOG2E)).astype(BF16)
    zero = jnp.zeros((hd, nq), BF16)
    t_lane = qi * tq + (lax.broadcasted_iota(jnp.int32, (1, nq), 1) & (tq - 1))
    tn_dims = (((0,), (0,)), ((), ()))

    sc = jnp.dot(kc_ref[0, 0], q, preferred_element_type=F32)
    cmp_end = lax.broadcasted_iota(jnp.int32, (ncp, 1), 0) * CMP_STRIDE + (CMP_BLOCK - 1)
    cmask = cmp_end <= t_lane
    sc = jnp.where(cmask, sc, NEG_INF)
    pc = jnp.where(cmask, jnp.exp2(sc - jnp.max(sc, axis=0, keepdims=True)), 0.0)
    lc = jnp.sum(pc, axis=0, keepdims=True)
    pc = pc * (1.0 / jnp.where(lc > 0.0, lc, 1.0))
    o_cmp = jnp.dot(vct_ref[0, 0], pc.astype(BF16), preferred_element_type=F32)

    psum = pc[:, 0:tq]
    for h in range(1, hpg):
        psum = psum + pc[:, h * tq:(h + 1) * tq]
    jrow = lax.broadcasted_iota(jnp.int32, (nsb, ncp), 0)
    crel = lax.broadcasted_iota(jnp.int32, (nsb, ncp), 1) - jrow * (SLC_BLOCK // CMP_STRIDE)
    ovt = jnp.where((crel >= 0) & (crel <= 2), 1.0, jnp.where((crel == -1) | (crel == 3), 0.5, 0.0)).astype(F32)
    imp = jnp.dot(ovt, psum, preferred_element_type=F32, precision=lax.Precision.HIGHEST)
    jblk = lax.broadcasted_iota(jnp.int32, (nsb, tq), 0)
    blk = (qi * tq + lax.broadcasted_iota(jnp.int32, (1, tq), 1)) >> blk_shift
    forced = (jblk == 0) | (jblk == blk) | (jblk == blk - 1)
    score = jnp.where(forced, FORCE_SCORE, jnp.where(jblk <= blk, imp, -1.0))
    score_ref[...] = score

    def rank_one(k, cnt):
        row = score_ref[pl.ds(k, 1), :]
        ge = jnp.where(row >= score, 1.0, 0.0)
        gt = jnp.where(row > score, 1.0, 0.0)
        return cnt + jnp.where(jblk > k, ge, gt)

    bpt = tq // SLC_BLOCK

    def rank_group(i, cnt):
        for u in range(bpt):
            cnt = rank_one(bpt * i + u, cnt)
        return cnt

    n_vis = jnp.minimum((qi + 1) * bpt, nsb)
    rank = lax.fori_loop(0, n_vis // bpt, rank_group, jnp.zeros((nsb, tq), F32))
    selb = jnp.where(rank < float(min(SLC_TOPN, nsb)), 0.0, NEG_INF).astype(BF16)

    ck = NSA_CK
    pad = jnp.zeros((2 * hd - nsb, nq), BF16)
    qa = jnp.concatenate([q, zero, jnp.concatenate([selb] * hpg, axis=1), pad], axis=0)

    def slc_scores(c):
        return jnp.dot(kcomb_ref[pl.ds(pl.multiple_of(c * ck, ck), ck), :], qa, preferred_element_type=F32)

    n_full = (qi * tq) // ck

    def slc_update(buf, c, carry, causal):
        def scores():
            if not causal:
                return buf[...]
            return jnp.where(c * ck + lax.broadcasted_iota(jnp.int32, (ck, 1), 0) <= t_lane, buf[...], NEG_INF)

        m, l, acc = carry
        m_new = jnp.maximum(m, jnp.max(scores(), axis=0, keepdims=True))
        alpha = jnp.exp2(m - m_new)
        p = jnp.exp2(scores() - m_new).astype(BF16)
        v = vs_ref[pl.ds(pl.multiple_of(c * ck, ck), ck), :]
        pv = lax.dot_general(v, p, tn_dims, preferred_element_type=F32)
        return m_new, alpha * l + pv[hd:hd + 1], acc * alpha + pv[0:hd]

    def slc_pair(i, carry):
        sb_ref[...] = slc_scores(2 * i + 1)
        carry = slc_update(sa_ref, 2 * i, carry, False)
        sa_ref[...] = slc_scores(2 * i + 2)
        return slc_update(sb_ref, 2 * i + 1, carry, False)

    def tail_two(carry):
        sb_ref[...] = slc_scores(n_full)
        return slc_update(sb_ref, n_full, slc_update(sa_ref, n_full - 1, carry, False), True)

    def tail_one(carry):
        return slc_update(sa_ref, n_full, carry, True)

    sa_ref[...] = slc_scores(0)

    nwc = WINDOW // tq + 1
    t_lo = t_lane - WINDOW
    ks_w, vs_w = [], []
    for i in range(nwc):
        rows = pl.ds(pl.multiple_of(jnp.maximum(qi - (nwc - 1) + i, 0) * tq, tq), tq)
        ks_w.append(kcomb_ref[rows, 0:2 * hd])
        vs_w.append(vw_ref[rows, :])
    qw = jnp.concatenate([zero, q], axis=0)
    sw = jnp.dot(jnp.concatenate(ks_w, axis=0), qw, preferred_element_type=F32)
    sw_parts = []
    for i in range(nwc):
        spos = (qi - (nwc - 1) + i) * tq + lax.broadcasted_iota(jnp.int32, (tq, 1), 0)
        if i < nwc - 1:
            ok = jnp.where(spos >= 0, spos, -(1 << 24)) > t_lo
        else:
            ok = spos <= t_lane
        sw_parts.append(jnp.where(ok, sw[i * tq:(i + 1) * tq], NEG_INF))
    sw = jnp.concatenate(sw_parts, axis=0)
    pw = jnp.exp2(sw - jnp.max(sw, axis=0, keepdims=True))
    ow = lax.dot_general(jnp.concatenate(vs_w, axis=0), pw.astype(BF16), tn_dims, preferred_element_type=F32)
    o_win = ow[0:hd] * (1.0 / ow[hd:hd + 1])
    init = (jnp.full((1, nq), NEG_INF, F32), jnp.zeros((1, nq), F32), jnp.zeros((hd, nq), F32))
    carry = lax.fori_loop(0, n_full // 2, slc_pair, init)
    _, l_s, acc_s = lax.cond(n_full % 2 == 1, tail_two, tail_one, carry)
    o_slc = acc_s * (1.0 / l_s)

    gate_ref[...] = _sigmoid(zg_ref[...].astype(F32)).T

    def branch_gate(br):
        rows = [gate_ref[pl.ds(grp * (hpg * 3) + h * 3 + br, 1), :] for h in range(hpg)]
        return jnp.concatenate(rows, axis=1)

    o = branch_gate(0) * o_cmp + branch_gate(1) * o_slc + branch_gate(2) * o_win
    o_heads = jnp.concatenate([o[:, h * tq:(h + 1) * tq] for h in range(hpg)], axis=0)
    o_ref[...] = o_heads.T.astype(o_ref.dtype)


def _nsa_attention(z, kc, vct, batch, seq):
    n = z.shape[0]
    g, hd, tq = NSA_KV_GROUPS, HEAD_DIM, NSA_TQ
    nqt = seq // tq
    ncp = kc.shape[2]
    nsb = seq // SLC_BLOCK
    gw = NSA_HPG * hd
    assert seq % NSA_CK == 0 and NSA_CK % tq == 0 and WINDOW % tq == 0 and nsb <= 2 * hd
    tile = lambda i, j, k: i * nqt + k
    return pl.pallas_call(
        _nsa_kernel,
        out_shape=jax.ShapeDtypeStruct((n, NSA_Q_WIDTH), BF16),
        grid=(batch, g, nqt),
        in_specs=[
            pl.BlockSpec((tq, gw), lambda i, j, k: (tile(i, j, k), Z_Q // gw + j)),
            pl.BlockSpec((seq, 2 * hd), lambda i, j, k: (i, Z_KK // (2 * hd) + j)),
            pl.BlockSpec((seq, 2 * hd), lambda i, j, k: (i, Z_VV // (2 * hd) + j)),
            pl.BlockSpec((tq, LANES), lambda i, j, k: (tile(i, j, k), Z_GATE // LANES)),
            pl.BlockSpec((1, 1, ncp, hd), lambda i, j, k: (i, j, 0, 0)),
            pl.BlockSpec((1, 1, hd, ncp), lambda i, j, k: (i, j, 0, 0)),
        ],
        out_specs=pl.BlockSpec((tq, gw), lambda i, j, k: (tile(i, j, k), j)),
        scratch_shapes=[pltpu.VMEM((seq, 4 * hd), BF16), pltpu.VMEM((seq, 2 * hd), BF16),
                        pltpu.VMEM((seq, 2 * hd), BF16), pltpu.VMEM((nsb, tq), F32),
                        pltpu.VMEM((LANES, tq), F32), pltpu.VMEM((NSA_CK, NSA_HPG * tq), F32),
                        pltpu.VMEM((NSA_CK, NSA_HPG * tq), F32)],
        compiler_params=_cparams(("parallel", "parallel", "arbitrary")),
        name="nsa_attn",
    )(z, z, z, z, kc, vct)


def _rglru_kernel(x_ref, y_ref, cw_ref, cb_ref, wa_ref, ba_ref, wx_ref, bx_ref, lam_ref, o_ref,
                  tail_ref, h_ref, a_ref, u_ref):
    tc = pl.program_id(2)
    tt, cb = x_ref.shape

    @pl.when(tc == 0)
    def _():
        tail_ref[...] = jnp.zeros_like(tail_ref)
        h_ref[...] = jnp.zeros_like(h_ref)

    x = x_ref[...].astype(F32)
    xe = jnp.concatenate([tail_ref[...], x], axis=0)
    tail_ref[...] = x[tt - SUBLANES:, :]
    cw = cw_ref[...]
    xc = cb_ref[...]
    for k in range(CONV_WIDTH):
        off = SUBLANES - (CONV_WIDTH - 1) + k
        xc = xc + cw[k:k + 1, :] * xe[off:off + tt, :]
    xcb = xc.astype(BF16)
    def gate(w_ref, b_ref):
        parts = [jnp.dot(xcb[:, c * LRU_WT:(c + 1) * LRU_WT], w_ref[c], preferred_element_type=F32)
                 for c in range(cb // LRU_WT)]
        return _sigmoid(jnp.concatenate(parts, axis=1) + b_ref[...])

    r = gate(wa_ref, ba_ref)
    ig = gate(wx_ref, bx_ref)
    nl = -lam_ref[...]
    softplus = jnp.maximum(nl, 0.0) + jnp.log1p(jnp.exp(-jnp.abs(nl)))
    log_a = (-LRU_C) * softplus * r
    a_ref[...] = jnp.exp(log_a)
    th = jnp.tanh(log_a)
    u_ref[...] = jnp.sqrt(-2.0 * th / (1.0 - th)) * (ig * xc)

    row = lax.broadcasted_iota(jnp.int32, (SUBLANES, cb), 0)

    def step(i, h):
        sl = pl.ds(pl.multiple_of(i * SUBLANES, SUBLANES), SUBLANES)
        a = a_ref[sl, :]
        u = u_ref[sl, :]
        for s in (1, 2, 4):
            a_s = jnp.where(row >= s, pltpu.roll(a, s, axis=0), 1.0)
            u_s = jnp.where(row >= s, pltpu.roll(u, s, axis=0), 0.0)
            u = a * u_s + u
            a = a * a_s
        hrows = a * h + u
        u_ref[sl, :] = hrows
        return hrows[SUBLANES - 1:SUBLANES, :]

    h_ref[...] = lax.fori_loop(0, tt // SUBLANES, step, h_ref[...], unroll=8)
    o_ref[...] = (u_ref[...] * _gelu_tanh(y_ref[...].astype(F32))).astype(o_ref.dtype)


def _rglru(z, batch, seq, cw, cbias, wa_bd, ba, wx_bd, bx, lam, tt=512):
    n = z.shape[0]
    ncb = LRU_WIDTH // LRU_CB
    nt = seq // tt
    row = lambda i, j, k: i * nt + k
    vec = lambda r: pl.BlockSpec((r, LRU_CB), lambda i, j, k: (0, j))
    return pl.pallas_call(
        _rglru_kernel,
        out_shape=jax.ShapeDtypeStruct((n, LRU_WIDTH), BF16),
        grid=(batch, ncb, nt),
        in_specs=[
            pl.BlockSpec((tt, LRU_CB), lambda i, j, k: (row(i, j, k), Z_LX // LRU_CB + j)),
            pl.BlockSpec((tt, LRU_CB), lambda i, j, k: (row(i, j, k), Z_LY // LRU_CB + j)),
            vec(CONV_WIDTH), vec(1),
            pl.BlockSpec((LRU_CB // LRU_WT, LRU_WT, LRU_WT), lambda i, j, k: (j, 0, 0)), vec(1),
            pl.BlockSpec((LRU_CB // LRU_WT, LRU_WT, LRU_WT), lambda i, j, k: (j, 0, 0)), vec(1),
            vec(1),
        ],
        out_specs=pl.BlockSpec((tt, LRU_CB), lambda i, j, k: (row(i, j, k), j)),
        scratch_shapes=[pltpu.VMEM((SUBLANES, LRU_CB), F32), pltpu.VMEM((1, LRU_CB), F32),
                        pltpu.VMEM((tt, LRU_CB), F32), pltpu.VMEM((tt, LRU_CB), F32)],
        compiler_params=_cparams(("parallel", "parallel", "arbitrary")),
        name="rglru",
    )(z, z, cw, cbias, wa_bd, ba, wx_bd, bx, lam)


def _merge_kernel(o_ref, l_ref, mga_ref, mgb_ref, wn_ref, wl_ref, m_ref):
    ya = jnp.dot(o_ref[...], wn_ref[...], preferred_element_type=F32)
    yb = jnp.dot(l_ref[...], wl_ref[...], preferred_element_type=F32)
    m = _sigmoid(mga_ref[...].astype(F32)) * ya + _sigmoid(mgb_ref[...].astype(F32)) * yb
    m_ref[...] = m.astype(m_ref.dtype)


def _merge(o, lru, z, wn, wl, tm=512):
    n, d = o.shape[0], wn.shape[1]
    return pl.pallas_call(
        _merge_kernel,
        out_shape=jax.ShapeDtypeStruct((n, d), BF16),
        grid=(n // tm,),
        in_specs=[
            pl.BlockSpec((tm, o.shape[1]), lambda i: (i, 0)),
            pl.BlockSpec((tm, lru.shape[1]), lambda i: (i, 0)),
            pl.BlockSpec((tm, d), lambda i: (i, Z_MGA // d)),
            pl.BlockSpec((tm, d), lambda i: (i, Z_MGB // d)),
            pl.BlockSpec(wn.shape, lambda i: (0, 0)),
            pl.BlockSpec(wl.shape, lambda i: (0, 0)),
        ],
        out_specs=pl.BlockSpec((tm, d), lambda i: (i, 0)),
        compiler_params=_cparams(("parallel",)),
        name="merge",
    )(o, lru, z, z, wn, wl)


def _out_route_kernel(m_ref, x_ref, wo_ref, g_ref, wr_ref, br_ref, x1_ref, xn_ref, eid_ref, ew_ref):
    x1 = x_ref[...] + jnp.dot(m_ref[...], wo_ref[...], preferred_element_type=F32)
    x1_ref[...] = x1
    xn = _rms(x1, g_ref[...])
    _store_rows(xn_ref, 0, xn)
    lg = jnp.dot(xn, wr_ref[...], preferred_element_type=F32).T + br_ref[...]
    tm = lg.shape[1]
    sub = lax.broadcasted_iota(jnp.int32, (SUBLANES, tm), 0)

    def first_argmax(v, vmax):
        return jnp.min(jnp.where(v == vmax, sub, SUBLANES), axis=0, keepdims=True)

    gl = jnp.where(sub < N_GROUPS, lg[0:SUBLANES], -jnp.inf)
    gmax = jnp.max(gl, axis=0, keepdims=True)
    ge = jnp.exp(gl - gmax)
    gprob = ge / jnp.sum(ge, axis=0, keepdims=True)
    g_val = jnp.max(gprob, axis=0, keepdims=True)
    g_idx = first_argmax(gprob, g_val)
    e_in = jnp.zeros((EXPERTS_PER_GROUP, tm), F32)
    for gi in range(N_GROUPS):
        lo = SUBLANES + gi * EXPERTS_PER_GROUP
        e_in = jnp.where(g_idx == gi, lg[lo:lo + EXPERTS_PER_GROUP], e_in)
    ee = jnp.exp(e_in - jnp.max(e_in, axis=0, keepdims=True))
    eprob = ee / jnp.sum(ee, axis=0, keepdims=True)
    v1 = jnp.max(eprob, axis=0, keepdims=True)
    i1 = first_argmax(eprob, v1)
    rest = jnp.where(sub == i1, -1.0, eprob)
    v2 = jnp.max(rest, axis=0, keepdims=True)
    i2 = first_argmax(rest, v2)
    den = v1 + v2
    eid = jnp.where(sub == 0, g_idx * EXPERTS_PER_GROUP + i1, g_idx * EXPERTS_PER_GROUP + i2)
    eid_ref[...] = eid
    ew_ref[...] = jnp.where(sub == 0, g_val * v1 / den, g_val * v2 / den)


def _out_route(m, x2d, wo, g, wr, br, tm=512):
    n, d = x2d.shape
    once = pl.Buffered(1)
    return pl.pallas_call(
        _out_route_kernel,
        out_shape=(jax.ShapeDtypeStruct((n, d), F32), jax.ShapeDtypeStruct((n * _row_pitch(d), LANES), F32),
                   jax.ShapeDtypeStruct((SUBLANES, n), jnp.int32), jax.ShapeDtypeStruct((SUBLANES, n), F32)),
        grid=(n // tm,),
        in_specs=[
            pl.BlockSpec((tm, d), lambda i: (i, 0)),
            pl.BlockSpec((tm, d), lambda i: (i, 0)),
            pl.BlockSpec((d, d), lambda i: (0, 0), pipeline_mode=once),
            pl.BlockSpec((1, d), lambda i: (0, 0)),
            pl.BlockSpec(wr.shape, lambda i: (0, 0), pipeline_mode=once),
            pl.BlockSpec(br.shape, lambda i: (0, 0)),
        ],
        out_specs=(pl.BlockSpec((tm, d), lambda i: (i, 0)), pl.BlockSpec((tm * _row_pitch(d), LANES), lambda i: (i, 0)),
                   pl.BlockSpec((SUBLANES, tm), lambda i: (0, i)), pl.BlockSpec((SUBLANES, tm), lambda i: (0, i))),
        compiler_params=_cparams(("parallel",)),
        name="out_route",
    )(m, x2d, wo, g, wr, br)


def _for_rows(cnt, fn, unroll=8):
    sh = unroll.bit_length() - 1

    def group(gidx, c):
        for u in range(unroll):
            fn(gidx * unroll + u)
        return c

    def single(r, c):
        fn(r)
        return c

    lax.fori_loop(0, cnt >> sh, group, 0)
    lax.fori_loop((cnt >> sh) << sh, cnt, single, 0)


def _moe_kernel(nvalid_ref, texp_ref, cnt_ref, tok_ref, tokn_ref, dst_ref, xn_hbm, wg_ref, wu_ref, wd_ref, y_hbm,
                xbuf, ybuf, gsem, ssem):
    i = pl.program_id(0)
    nv = nvalid_ref[0]
    slot = i % 2
    d = wg_ref.shape[1]
    k = _row_pitch(d)
    tm = xbuf.shape[0] // (2 * k)

    def line(ref, row):
        return ref.at[pl.ds(row * k, k), :]

    def gather_row(idx_ref, s):
        def fn(r):
            pltpu.make_async_copy(line(xn_hbm, idx_ref[0, 0, r]), line(xbuf, s * tm + r), gsem.at[s]).start(priority=1)
        return fn

    def gather_wait_row(s):
        def fn(r):
            pltpu.make_async_copy(line(xn_hbm, 0), line(xbuf, s * tm + r), gsem.at[s]).wait()
        return fn

    def scatter_row(s):
        def fn(r):
            pltpu.make_async_copy(line(ybuf, s * tm + r), line(y_hbm, dst_ref[0, 0, r]), ssem.at[s]).start()
        return fn

    def scatter_wait_row(s):
        def fn(r):
            pltpu.make_async_copy(line(ybuf, s * tm + r), line(y_hbm, 0), ssem.at[s]).wait()
        return fn

    @pl.when(i == 0)
    def _():
        xbuf[...] = jnp.zeros_like(xbuf)
        _for_rows(cnt_ref[0], gather_row(tok_ref, 0))

    @pl.when(i + 1 < nv)
    def _():
        _for_rows(cnt_ref[i + 1], gather_row(tokn_ref, 1 - slot))

    @pl.when(i < nv)
    def _():
        _for_rows(cnt_ref[i], gather_wait_row(slot))
        x = _load_rows(xbuf, slot * tm, tm, d).astype(BF16)
        gp = jnp.dot(x, wg_ref[0], preferred_element_type=F32)
        up = jnp.dot(x, wu_ref[0], preferred_element_type=F32)
        hid = (gp * _sigmoid(gp) * up).astype(BF16)
        y = jnp.dot(hid, wd_ref[0], preferred_element_type=F32)

        @pl.when(i >= 2)
        def _():
            _for_rows(cnt_ref[i - 2], scatter_wait_row(slot))

        _store_rows(ybuf, slot * tm, y)
        _for_rows(cnt_ref[i], scatter_row(slot))

    @pl.when(i == nv - 1)
    def _():
        _for_rows(cnt_ref[i], scatter_wait_row(slot))

        @pl.when(nv >= 2)
        def _():
            _for_rows(cnt_ref[i - 1], scatter_wait_row(1 - slot))


def _moe(nvalid, texp, cnt, tok, dst, xn, wg, wu, wd, n_rows_out):
    nt, _, tm = tok.shape
    d, de = wg.shape[1], wg.shape[2]
    k = _row_pitch(d)
    grid_spec = pltpu.PrefetchScalarGridSpec(
        num_scalar_prefetch=3,
        grid=(nt,),
        in_specs=[
            pl.BlockSpec((1, 1, tm), lambda i, nv, te, ct: (i, 0, 0), memory_space=pltpu.SMEM),
            pl.BlockSpec((1, 1, tm), lambda i, nv, te, ct: (jnp.minimum(i + 1, nt - 1), 0, 0),
                         memory_space=pltpu.SMEM),
            pl.BlockSpec((1, 1, tm), lambda i, nv, te, ct: (i, 0, 0), memory_space=pltpu.SMEM),
            pl.BlockSpec(memory_space=pl.ANY),
            pl.BlockSpec((1, d, de), lambda i, nv, te, ct: (te[i], 0, 0)),
            pl.BlockSpec((1, d, de), lambda i, nv, te, ct: (te[i], 0, 0)),
            pl.BlockSpec((1, de, d), lambda i, nv, te, ct: (te[i], 0, 0)),
        ],
        out_specs=pl.BlockSpec(memory_space=pl.ANY),
        scratch_shapes=[pltpu.VMEM((2 * tm * k, LANES), F32), pltpu.VMEM((2 * tm * k, LANES), F32),
                        pltpu.SemaphoreType.DMA((2,)), pltpu.SemaphoreType.DMA((2,))],
    )
    return pl.pallas_call(
        _moe_kernel,
        out_shape=jax.ShapeDtypeStruct((n_rows_out * k, LANES), F32),
        grid_spec=grid_spec,
        compiler_params=_cparams(("arbitrary",)),
        name="moe",
    )(nvalid, texp, cnt, tok, tok, dst, xn, wg, wu, wd)


def _moe_plan(eid, n, tm):
    e_flat = eid.reshape(-1)
    npairs = e_flat.shape[0]
    experts = jnp.arange(N_EXPERTS, dtype=jnp.int32)
    counts = jnp.sum((e_flat[:, None] == experts[None, :]).astype(jnp.int32), axis=0)
    padded = ((counts + tm - 1) // tm) * tm
    ends = jnp.cumsum(padded)
    offs = ends - padded
    n_rows = npairs + N_EXPERTS * tm
    nt = n_rows // tm
    fill = jnp.arange(n_rows - npairs, dtype=jnp.int32)
    fill_expert = jnp.sum((fill[:, None] >= jnp.cumsum(padded - counts)[None, :]).astype(jnp.int32), axis=1)
    keys = jnp.concatenate([e_flat * (2 * npairs) + jnp.arange(npairs, dtype=jnp.int32),
                            fill_expert * (2 * npairs) + npairs])
    skeys = jnp.sort(keys)
    low = skeys % (2 * npairs)
    dst = jnp.where(low < npairs, low, 0)
    tok = dst % n
    tile_start = jnp.arange(nt, dtype=jnp.int32) * tm
    texp = jnp.minimum(jnp.sum((tile_start[:, None] >= ends[None, :]).astype(jnp.int32), axis=1), N_EXPERTS - 1)
    cnt = jnp.clip(offs[texp] + counts[texp] - tile_start, 0, tm)
    nvalid = (ends[-1] // tm).astype(jnp.int32).reshape(1)
    return (nvalid, texp.astype(jnp.int32), cnt.astype(jnp.int32), tok.reshape(nt, 1, tm), dst.reshape(nt, 1, tm),
            npairs)


def _ple_final_kernel(x1_ref, y0_ref, y1_ref, w_ref, p_ref, gp_ref, wpg_ref, wp_ref, gf_ref, o_ref):
    w = w_ref[...]
    tm, d = x1_ref.shape
    x2 = x1_ref[...] + w[:, 0:1] * _load_rows(y0_ref, 0, tm, d) + w[:, 1:2] * _load_rows(y1_ref, 0, tm, d)
    hn = _rms(x2, gp_ref[...]).astype(BF16)
    gate = _sigmoid(jnp.dot(hn, wpg_ref[...], preferred_element_type=F32))
    pe = jnp.dot(p_ref[...].astype(BF16), wp_ref[...], preferred_element_type=F32)
    x3 = x2 + gate * pe
    o_ref[...] = _rms(x3, gf_ref[...])


def _ple_final(x1, ypairs, wcols, p2d, gp, wpg, wp, gf, tm=256):
    n, d = x1.shape
    nb = n // tm
    k = _row_pitch(d)
    return pl.pallas_call(
        _ple_final_kernel,
        out_shape=jax.ShapeDtypeStruct((n, d), F32),
        grid=(nb,),
        in_specs=[
            pl.BlockSpec((tm, d), lambda i: (i, 0)),
            pl.BlockSpec((tm * k, LANES), lambda i: (i, 0)),
            pl.BlockSpec((tm * k, LANES), lambda i: (nb + i, 0)),
            pl.BlockSpec((tm, wcols.shape[1]), lambda i: (i, 0)),
            pl.BlockSpec((tm, p2d.shape[1]), lambda i: (i, 0)),
            pl.BlockSpec((1, d), lambda i: (0, 0)),
            pl.BlockSpec((d, d), lambda i: (0, 0)),
            pl.BlockSpec(wp.shape, lambda i: (0, 0)),
            pl.BlockSpec((1, d), lambda i: (0, 0)),
        ],
        out_specs=pl.BlockSpec((tm, d), lambda i: (i, 0)),
        compiler_params=_cparams(("parallel",)),
        name="ple_final",
    )(x1, ypairs, ypairs, wcols, p2d, gp, wpg, wp, gf)


def _block_diag(w, per):
    nb, bw, _ = w.shape
    w = w.reshape(nb // per, per, bw, bw)
    eye = jnp.eye(per, dtype=w.dtype)
    return jnp.einsum("cpij,pq->cpiqj", w, eye).reshape(nb // per, per * bw, per * bw)


def _regroup_kernel(wt_ref, o_ref):
    d = Z_MGB - Z_MGA
    c = np.cumsum((0, NSA_Q_WIDTH) + (NSA_KV_WIDTH,) * 6 + (3 * NSA_HEADS, LRU_WIDTH, LRU_WIDTH, d, d))
    seg = lambda k: wt_ref[int(c[k]):int(c[k + 1]), :]
    q, k_c, v_c, k_s, v_s, k_w, v_w, gates, lru_x, lru_y, mg_a, mg_b = (seg(k) for k in range(12))

    def pair(a, b):
        parts = []
        for g in range(NSA_KV_GROUPS):
            parts += [a[g * HEAD_DIM:(g + 1) * HEAD_DIM], b[g * HEAD_DIM:(g + 1) * HEAD_DIM]]
        return jnp.concatenate(parts, axis=0)

    pad = jnp.zeros((Z_WIDTH - int(c[-1]), wt_ref.shape[1]), wt_ref.dtype)
    rows = [mg_a, mg_b, q, pair(k_s, k_w), pair(v_s, v_w), k_c, v_c, lru_x, lru_y, gates, pad]
    o_ref[...] = jnp.concatenate(rows, axis=0).T.astype(o_ref.dtype)


def _regroup_w_in(w_in_t, tr=256):
    nw, d = w_in_t.shape
    return pl.pallas_call(
        _regroup_kernel,
        out_shape=jax.ShapeDtypeStruct((d, Z_WIDTH), BF16),
        grid=(d // tr,),
        in_specs=[pl.BlockSpec((nw, tr), lambda i: (0, i))],
        out_specs=pl.BlockSpec((tr, Z_WIDTH), lambda i: (i, 0)),
        compiler_params=_cparams(("parallel",)),
        name="regroup_w_in",
    )(w_in_t)


def _layer(x2d, batch, seq, ln_mix, w_in, cmp_k_pos, cmp_k_w1, cmp_k_w2, cmp_v_pos, cmp_v_w1, cmp_v_w2, conv_w,
           conv_b, lru_wa, lru_ba, lru_wx, lru_bx, lru_lambda, w_nsa_up, w_lru_up, w_out, ln_ffn, w_grp, b_grp, w_exp,
           b_exp, w_gate, w_up, w_down):
    n, d = x2d.shape
    g, hd = NSA_KV_GROUPS, HEAD_DIM

    z = _in_proj(x2d, ln_mix.reshape(1, d), _regroup_w_in(w_in.T))

    pos8 = lambda pos: jnp.broadcast_to(pos.reshape(1, -1), (SUBLANES, pos.size)).astype(BF16)
    kc, vct = _compress(z, batch, seq, pos8(cmp_k_pos), cmp_k_w1.astype(BF16), cmp_k_w2.astype(BF16),
                        pos8(cmp_v_pos), cmp_v_w1.astype(BF16), cmp_v_w2.T.astype(BF16))
    o = _nsa_attention(z, kc, vct, batch, seq)

    per = LRU_WT // LRU_BW
    lru = _rglru(z, batch, seq, conv_w, conv_b.reshape(1, -1), _block_diag(lru_wa, per).astype(BF16),
                 lru_ba.reshape(1, -1), _block_diag(lru_wx, per).astype(BF16), lru_bx.reshape(1, -1),
                 lru_lambda.reshape(1, -1))

    merged = _merge(o, lru, z, w_nsa_up.astype(BF16), w_lru_up.astype(BF16))

    wr = jnp.zeros((d, LANES), F32).at[:, 0:N_GROUPS].set(w_grp).at[:, SUBLANES:SUBLANES + N_EXPERTS].set(w_exp)
    br = jnp.zeros((LANES, 1), F32).at[0:N_GROUPS, 0].set(b_grp).at[SUBLANES:SUBLANES + N_EXPERTS, 0].set(b_exp)
    x1, xn, eid, ew = _out_route(merged, x2d, w_out.astype(BF16), ln_ffn.reshape(1, d), wr, br)

    nvalid, texp, cnt, tok, dst, n_rows_out = _moe_plan(eid[0:EXPERT_TOPK], n, MOE_TM)
    ypairs = _moe(nvalid, texp, cnt, tok, dst, xn, w_gate, w_up, w_down, n_rows_out)

    return x1, ypairs, ew.T


def kernel(x, p, ln_mix, w_in, cmp_k_pos, cmp_k_w1, cmp_k_w2, cmp_v_pos, cmp_v_w1, cmp_v_w2, conv_w, conv_b, lru_wa, lru_ba, lru_wx, lru_bx, lru_lambda, w_nsa_up, w_lru_up, w_out, ln_ffn, w_grp, b_grp, w_exp, b_exp, w_gate, w_up, w_down, ln_ple, w_ple, w_ple_gate, ln_final):
    batch, seq, d = x.shape
    assert p.shape[0] == 1, "the final norm is fused into the (single) layer's last kernel"
    n = batch * seq
    x1, ypairs, wcols = _layer(
        x.reshape(n, d), batch, seq, ln_mix[0], w_in[0], cmp_k_pos[0], cmp_k_w1[0], cmp_k_w2[0],
        cmp_v_pos[0], cmp_v_w1[0], cmp_v_w2[0], conv_w[0], conv_b[0], lru_wa[0], lru_ba[0], lru_wx[0], lru_bx[0],
        lru_lambda[0], w_nsa_up[0], w_lru_up[0], w_out[0], ln_ffn[0], w_grp[0], b_grp[0], w_exp[0], b_exp[0],
        w_gate[0], w_up[0], w_down[0])
    out = _ple_final(x1, ypairs, wcols, p[0].reshape(n, -1), ln_ple[0].reshape(1, d),
                     w_ple_gate[0].astype(BF16), w_ple[0].astype(BF16), ln_final.reshape(1, d))
    return out.reshape(batch, seq, d)
```

```python
import numpy as np
import jax
import jax.numpy as jnp
from jax import lax
from jax.experimental import pallas as pl
from jax.experimental.pallas import tpu as pltpu

F32 = jnp.float32
BF16 = jnp.bfloat16

NSA_HEADS = 16
NSA_KV_GROUPS = 4
NSA_HPG = NSA_HEADS // NSA_KV_GROUPS
HEAD_DIM = 64
NSA_Q_WIDTH = NSA_HEADS * HEAD_DIM
NSA_KV_WIDTH = NSA_KV_GROUPS * HEAD_DIM
CMP_BLOCK = 32
CMP_STRIDE = 16
CMP_HIDDEN = 2 * HEAD_DIM
SLC_BLOCK = 64
SLC_TOPN = 16
WINDOW = 512
ATTN_SCALE = HEAD_DIM ** -0.5
NEG_INF = -1e30
LOG2E = float(np.log2(np.e))
FORCE_SCORE = 1e4
LRU_WIDTH = 1024
LRU_BLOCKS = 16
LRU_BW = LRU_WIDTH // LRU_BLOCKS
CONV_WIDTH = 4
LRU_C = 8.0
N_GROUPS = 4
EXPERTS_PER_GROUP = 8
N_EXPERTS = N_GROUPS * EXPERTS_PER_GROUP
EXPERT_TOPK = 2
D_EXPERT = 512
EPS = 1e-6

LANES = 128
SUBLANES = 8
VMEM_LIMIT_BYTES = 56 * 1024 * 1024

Z_MGA = 0
Z_MGB = 2048
Z_Q = 4096
Z_KK = 5120
Z_VV = 5632
Z_KC = 6144
Z_VC = 6400
Z_LX = 6656
Z_LY = 7680
Z_GATE = 8704
Z_WIDTH = 9216
LRU_CB = 512
LRU_WT = 256
MOE_TM = 256
NSA_TQ = 256
NSA_CK = 512


def _cparams(sem, vmem=VMEM_LIMIT_BYTES):
    return pltpu.CompilerParams(dimension_semantics=sem, vmem_limit_bytes=vmem)


def _rms(x, g):
    return x * lax.rsqrt(jnp.mean(x * x, axis=-1, keepdims=True) + EPS) * g


def _gelu_tanh(x):
    return 0.5 * x * (1.0 + jnp.tanh(np.sqrt(2.0 / np.pi) * (x + 0.044715 * (x * x * x))))


def _sigmoid(x):
    return 1.0 / (1.0 + jnp.exp(-x))


def _row_pitch(d):
    k = d // LANES
    return k + 1 - (k % 2)


def _store_rows(ref, row0, val):
    rows, d = val.shape
    pitch = _row_pitch(d)
    for j in range(d // LANES):
        ref[pl.ds(row0 * pitch + j, rows, stride=pitch), :] = val[:, j * LANES:(j + 1) * LANES]
    for j in range(d // LANES, pitch):
        ref[pl.ds(row0 * pitch + j, rows, stride=pitch), :] = jnp.zeros((rows, LANES), val.dtype)


def _load_rows(ref, row0, rows, d):
    pitch = _row_pitch(d)
    return jnp.concatenate([ref[pl.ds(row0 * pitch + j, rows, stride=pitch), :] for j in range(d // LANES)], axis=1)


def _in_proj_kernel(x_ref, g_ref, w_ref, o_ref, h_ref):
    @pl.when(pl.program_id(1) == 0)
    def _():
        h_ref[...] = _rms(x_ref[...], g_ref[...]).astype(BF16)

    o_ref[...] = jnp.dot(h_ref[...], w_ref[...], preferred_element_type=F32).astype(o_ref.dtype)


def _in_proj(x2d, g, w, tm=1024, tn=1536):
    n, d = x2d.shape
    nw = w.shape[1]
    return pl.pallas_call(
        _in_proj_kernel,
        out_shape=jax.ShapeDtypeStruct((n, nw), BF16),
        grid=(n // tm, nw // tn),
        in_specs=[
            pl.BlockSpec((tm, d), lambda i, j: (i, 0)),
            pl.BlockSpec((1, d), lambda i, j: (0, 0)),
            pl.BlockSpec((d, tn), lambda i, j: (0, j)),
        ],
        out_specs=pl.BlockSpec((tm, tn), lambda i, j: (i, j)),
        scratch_shapes=[pltpu.VMEM((tm, d), BF16)],
        compiler_params=_cparams(("parallel", "arbitrary")),
        name="in_proj",
    )(x2d, g, w)


def _compress_kernel(zk_ref, zv_ref, kpos_ref, kw1_ref, kw2_ref, vpos_ref, vw1_ref, vw2t_ref, kc_ref, vct_ref,
                     xf_ref):
    seq = zk_ref.shape[0]
    ncp = seq // CMP_STRIDE
    hd = HEAD_DIM
    half = CMP_STRIDE * hd

    def half_windows(z_ref):
        xf_ref[...] = z_ref[...].astype(F32)
        lines = [xf_ref[pl.ds(r, ncp, stride=CMP_STRIDE), :].astype(BF16) for r in range(CMP_STRIDE)]
        return [jnp.concatenate([ln[:, gg * hd:(gg + 1) * hd] for ln in lines], axis=1) for gg in range(LANES // hd)]

    def hidden(x, pos_ref, w1_ref):
        w1 = w1_ref[...]
        ha = jnp.dot(x, w1[:half], preferred_element_type=F32)
        hb = jnp.dot(x, w1[half:], preferred_element_type=F32)
        hb = pltpu.roll(hb, hb.shape[0] - 1, axis=0)
        pc = jnp.dot(pos_ref[...], w1, preferred_element_type=F32)[0:1]
        return _gelu_tanh(ha + hb + pc).astype(BF16)

    for gg, x in enumerate(half_windows(zk_ref)):
        hk = hidden(x, kpos_ref, kw1_ref)
        kc_ref[0, gg] = jnp.dot(hk, kw2_ref[...], preferred_element_type=F32).astype(kc_ref.dtype)
    for gg, x in enumerate(half_windows(zv_ref)):
        hv = hidden(x, vpos_ref, vw1_ref)
        vct_ref[0, gg] = lax.dot_general(vw2t_ref[...], hv, (((1,), (1,)), ((), ())),
                                         preferred_element_type=F32).astype(vct_ref.dtype)


def _compress(z, batch, seq, kpos, kw1, kw2, vpos, vw1, vw2t):
    g, hd = NSA_KV_GROUPS, HEAD_DIM
    gpb = LANES // hd
    ncp = seq // CMP_STRIDE
    full = lambda a: pl.BlockSpec(a.shape, lambda i, j: (0,) * a.ndim)
    return pl.pallas_call(
        _compress_kernel,
        out_shape=(jax.ShapeDtypeStruct((batch, g, ncp, hd), BF16),
                   jax.ShapeDtypeStruct((batch, g, hd, ncp), BF16)),
        grid=(batch, g // gpb),
        in_specs=[
            pl.BlockSpec((seq, LANES), lambda i, j: (i, Z_KC // LANES + j)),
            pl.BlockSpec((seq, LANES), lambda i, j: (i, Z_VC // LANES + j)),
            full(kpos), full(kw1), full(kw2), full(vpos), full(vw1), full(vw2t),
        ],
        out_specs=(pl.BlockSpec((1, gpb, ncp, hd), lambda i, j: (i, j, 0, 0)),
                   pl.BlockSpec((1, gpb, hd, ncp), lambda i, j: (i, j, 0, 0))),
        scratch_shapes=[pltpu.VMEM((seq, LANES), F32)],
        compiler_params=_cparams(("parallel", "parallel")),
        name="compress",
    )(z, z, kpos, kw1, kw2, vpos, vw1, vw2t)


def _nsa_kernel(zq_ref, zkk_ref, zvv_ref, zg_ref, kc_ref, vct_ref, o_ref, kcomb_ref, vs_ref, vw_ref, gate_ref,
                sa_ref, sb_ref):
    grp = pl.program_id(1)
    qi = pl.program_id(2)
    tq, hd, hpg = NSA_TQ, HEAD_DIM, NSA_HPG
    nq = tq * hpg
    seq = kcomb_ref.shape[0]
    ncp = kc_ref.shape[2]
    nsb = seq // SLC_BLOCK
    blk_shift = SLC_BLOCK.bit_length() - 1

    @pl.when(qi == 0)
    def _():
        kcomb_ref[:, 0:2 * hd] = zkk_ref[...]
        blk_of_row = lax.broadcasted_iota(jnp.int32, (seq, 2 * hd), 0) >> blk_shift
        lane = lax.broadcasted_iota(jnp.int32, (seq, 2 * hd), 1)
        kcomb_ref[:, 2 * hd:] = jnp.where(blk_of_row == lane, 1.0, 0.0).astype(BF16)
        ones = jnp.ones((seq, hd), BF16)
        vs_ref[...] = jnp.concatenate([zvv_ref[:, 0:hd], ones], axis=1)
        vw_ref[...] = jnp.concatenate([zvv_ref[:, hd:2 * hd], ones], axis=1)

    qt = zq_ref[...].T
    q = jnp.concatenate([qt[h * hd:(h + 1) * hd, :] for h in range(hpg)], axis=1)
    q = (q.astype(F32) * (ATTN_SCALE * LOG2E)).astype(BF16)
    zero = jnp.zeros((hd, nq), BF16)
    t_lane = qi * tq + (lax.broadcasted_iota(jnp.int32, (1, nq), 1) & (tq - 1))
    tn_dims = (((0,), (0,)), ((), ()))

    sc = jnp.dot(kc_ref[0, 0], q, preferred_element_type=F32)
    cmp_end = lax.broadcasted_iota(jnp.int32, (ncp, 1), 0) * CMP_STRIDE + (CMP_BLOCK - 1)
    cmask = cmp_end <= t_lane
    sc = jnp.where(cmask, sc, NEG_INF)
    pc = jnp.where(cmask, jnp.exp2(sc - jnp.max(sc, axis=0, keepdims=True)), 0.0)
    lc = jnp.sum(pc, axis=0, keepdims=True)
    pc = pc * (1.0 / jnp.where(lc > 0.0, lc, 1.0))
    o_cmp = jnp.dot(vct_ref[0, 0], pc.astype(BF16), preferred_element_type=F32)

    psum = pc[:, 0:tq]
    for h in range(1, hpg):
        psum = psum + pc[:, h * tq:(h + 1) * tq]
    jrow = lax.broadcasted_iota(jnp.int32, (nsb, ncp), 0)
    crel = lax.broadcasted_iota(jnp.int32, (nsb, ncp), 1) - jrow * (SLC_BLOCK // CMP_STRIDE)
    ovt = jnp.where((crel >= 0) & (crel <= 2), 1.0, jnp.where((crel == -1) | (crel == 3), 0.5, 0.0)).astype(F32)
    imp = jnp.dot(ovt, psum, preferred_element_type=F32, precision=lax.Precision.HIGHEST)
    jblk = lax.broadcasted_iota(jnp.int32, (nsb, tq), 0)
    blk = (qi * tq + lax.broadcasted_iota(jnp.int32, (1, tq), 1)) >> blk_shift
    forced = (jblk == 0) | (jblk == blk) | (jblk == blk - 1)
    score = jnp.where(forced, FORCE_SCORE, jnp.where(jblk <= blk, imp, -1.0))
    selb = jnp.full((nsb, tq), NEG_INF, F32)
    for _ in range(min(SLC_TOPN, nsb)):
        top = jnp.max(score, axis=0, keepdims=True)
        first = jnp.min(jnp.where(score == top, jblk, nsb), axis=0, keepdims=True)
        pick = jblk == first
        selb = jnp.where(pick, 0.0, selb)
        score = jnp.where(pick, -2.0, score)
    selb = selb.astype(BF16)

    ck = NSA_CK
    pad = jnp.zeros((2 * hd - nsb, nq), BF16)
    qa = jnp.concatenate([q, zero, jnp.concatenate([selb] * hpg, axis=1), pad], axis=0)

    def slc_scores(c):
        return jnp.dot(kcomb_ref[pl.ds(pl.multiple_of(c * ck, ck), ck), :], qa, preferred_element_type=F32)

    n_full = (qi * tq) // ck

    def slc_update(buf, c, carry, causal):
        def scores():
            if not causal:
                return buf[...]
            return jnp.where(c * ck + lax.broadcasted_iota(jnp.int32, (ck, 1), 0) <= t_lane, buf[...], NEG_INF)

        m, l, acc = carry
        m_new = jnp.maximum(m, jnp.max(scores(), axis=0, keepdims=True))
        alpha = jnp.exp2(m - m_new)
        p = jnp.exp2(scores() - m_new).astype(BF16)
        v = vs_ref[pl.ds(pl.multiple_of(c * ck, ck), ck), :]
        pv = lax.dot_general(v, p, tn_dims, preferred_element_type=F32)
        return m_new, alpha * l + pv[hd:hd + 1], acc * alpha + pv[0:hd]

    def slc_pair(i, carry):
        sb_ref[...] = slc_scores(2 * i + 1)
        carry = slc_update(sa_ref, 2 * i, carry, False)
        sa_ref[...] = slc_scores(2 * i + 2)
        return slc_update(sb_ref, 2 * i + 1, carry, False)

    def tail_two(carry):
        sb_ref[...] = slc_scores(n_full)
        return slc_update(sb_ref, n_full, slc_update(sa_ref, n_full - 1, carry, False), True)

    def tail_one(carry):
        return slc_update(sa_ref, n_full, carry, True)

    sa_ref[...] = slc_scores(0)

    nwc = WINDOW // tq + 1
    t_lo = t_lane - WINDOW
    ks_w, vs_w = [], []
    for i in range(nwc):
        rows = pl.ds(pl.multiple_of(jnp.maximum(qi - (nwc - 1) + i, 0) * tq, tq), tq)
        ks_w.append(kcomb_ref[rows, 0:2 * hd])
        vs_w.append(vw_ref[rows, :])
    qw = jnp.concatenate([zero, q], axis=0)
    sw = jnp.dot(jnp.concatenate(ks_w, axis=0), qw, preferred_element_type=F32)
    sw_parts = []
    for i in range(nwc):
        spos = (qi - (nwc - 1) + i) * tq + lax.broadcasted_iota(jnp.int32, (tq, 1), 0)
        if i < nwc - 1:
            ok = jnp.where(spos >= 0, spos, -(1 << 24)) > t_lo
        else:
            ok = spos <= t_lane
        sw_parts.append(jnp.where(ok, sw[i * tq:(i + 1) * tq], NEG_INF))
    sw = jnp.concatenate(sw_parts, axis=0)
    pw = jnp.exp2(sw - jnp.max(sw, axis=0, keepdims=True))
    ow = lax.dot_general(jnp.concatenate(vs_w, axis=0), pw.astype(BF16), tn_dims, preferred_element_type=F32)
    o_win = ow[0:hd] * (1.0 / ow[hd:hd + 1])
    init = (jnp.full((1, nq), NEG_INF, F32), jnp.zeros((1, nq), F32), jnp.zeros((hd, nq), F32))
    carry = lax.fori_loop(0, n_full // 2, slc_pair, init)
    _, l_s, acc_s = lax.cond(n_full % 2 == 1, tail_two, tail_one, carry)
    o_slc = acc_s * (1.0 / l_s)

    gate_ref[...] = _sigmoid(zg_ref[...].astype(F32)).T

    def branch_gate(br):
        rows = [gate_ref[pl.ds(grp * (hpg * 3) + h * 3 + br, 1), :] for h in range(hpg)]
        return jnp.concatenate(rows, axis=1)

    o = branch_gate(0) * o_cmp + branch_gate(1) * o_slc + branch_gate(2) * o_win
    o_heads = jnp.concatenate([o[:, h * tq:(h + 1) * tq] for h in range(hpg)], axis=0)
    o_ref[...] = o_heads.T.astype(o_ref.dtype)


def _nsa_attention(z, kc, vct, batch, seq):
    n = z.shape[0]
    g, hd, tq = NSA_KV_GROUPS, HEAD_DIM, NSA_TQ
    nqt = seq // tq
    ncp = kc.shape[2]
    nsb = seq // SLC_BLOCK
    gw = NSA_HPG * hd
    assert seq % NSA_CK == 0 and NSA_CK % tq == 0 and WINDOW % tq == 0 and nsb <= 2 * hd
    tile = lambda i, j, k: i * nqt + k
    return pl.pallas_call(
        _nsa_kernel,
        out_shape=jax.ShapeDtypeStruct((n, NSA_Q_WIDTH), BF16),
        grid=(batch, g, nqt),
        in_specs=[
            pl.BlockSpec((tq, gw), lambda i, j, k: (tile(i, j, k), Z_Q // gw + j)),
            pl.BlockSpec((seq, 2 * hd), lambda i, j, k: (i, Z_KK // (2 * hd) + j)),
            pl.BlockSpec((seq, 2 * hd), lambda i, j, k: (i, Z_VV // (2 * hd) + j)),
            pl.BlockSpec((tq, LANES), lambda i, j, k: (tile(i, j, k), Z_GATE // LANES)),
            pl.BlockSpec((1, 1, ncp, hd), lambda i, j, k: (i, j, 0, 0)),
            pl.BlockSpec((1, 1, hd, ncp), lambda i, j, k: (i, j, 0, 0)),
        ],
        out_specs=pl.BlockSpec((tq, gw), lambda i, j, k: (tile(i, j, k), j)),
        scratch_shapes=[pltpu.VMEM((seq, 4 * hd), BF16), pltpu.VMEM((seq, 2 * hd), BF16),
                        pltpu.VMEM((seq, 2 * hd), BF16), pltpu.VMEM((LANES, tq), F32), pltpu.VMEM((NSA_CK, NSA_HPG * tq), F32),
                        pltpu.VMEM((NSA_CK, NSA_HPG * tq), F32)],
        compiler_params=_cparams(("parallel", "parallel", "arbitrary")),
        name="nsa_attn",
    )(z, z, z, z, kc, vct)


def _rglru_kernel(x_ref, y_ref, cw_ref, cb_ref, wa_ref, ba_ref, wx_ref, bx_ref, lam_ref, o_ref,
                  tail_ref, h_ref, a_ref, u_ref):
    tc = pl.program_id(2)
    tt, cb = x_ref.shape

    @pl.when(tc == 0)
    def _():
        tail_ref[...] = jnp.zeros_like(tail_ref)
        h_ref[...] = jnp.zeros_like(h_ref)

    x = x_ref[...].astype(F32)
    xe = jnp.concatenate([tail_ref[...], x], axis=0)
    tail_ref[...] = x[tt - SUBLANES:, :]
    cw = cw_ref[...]
    xc = cb_ref[...]
    for k in range(CONV_WIDTH):
        off = SUBLANES - (CONV_WIDTH - 1) + k
        xc = xc + cw[k:k + 1, :] * xe[off:off + tt, :]
    xcb = xc.astype(BF16)
    def gate(w_ref, b_ref):
        parts = [jnp.dot(xcb[:, c * LRU_WT:(c + 1) * LRU_WT], w_ref[c], preferred_element_type=F32)
                 for c in range(cb // LRU_WT)]
        return _sigmoid(jnp.concatenate(parts, axis=1) + b_ref[...])

    r = gate(wa_ref, ba_ref)
    ig = gate(wx_ref, bx_ref)
    nl = -lam_ref[...]
    softplus = jnp.maximum(nl, 0.0) + jnp.log1p(jnp.exp(-jnp.abs(nl)))
    log_a = (-LRU_C) * softplus * r
    a_ref[...] = jnp.exp(log_a)
    th = jnp.tanh(log_a)
    u_ref[...] = jnp.sqrt(-2.0 * th / (1.0 - th)) * (ig * xc)

    row = lax.broadcasted_iota(jnp.int32, (SUBLANES, cb), 0)

    def step(i, h):
        sl = pl.ds(pl.multiple_of(i * SUBLANES, SUBLANES), SUBLANES)
        a = a_ref[sl, :]
        u = u_ref[sl, :]
        for s in (1, 2, 4):
            a_s = jnp.where(row >= s, pltpu.roll(a, s, axis=0), 1.0)
            u_s = jnp.where(row >= s, pltpu.roll(u, s, axis=0), 0.0)
            u = a * u_s + u
            a = a * a_s
        hrows = a * h + u
        u_ref[sl, :] = hrows
        return hrows[SUBLANES - 1:SUBLANES, :]

    h_ref[...] = lax.fori_loop(0, tt // SUBLANES, step, h_ref[...], unroll=8)
    o_ref[...] = (u_ref[...] * _gelu_tanh(y_ref[...].astype(F32))).astype(o_ref.dtype)


def _rglru(z, batch, seq, cw, cbias, wa_bd, ba, wx_bd, bx, lam, tt=512):
    n = z.shape[0]
    ncb = LRU_WIDTH // LRU_CB
    nt = seq // tt
    row = lambda i, j, k: i * nt + k
    vec = lambda r: pl.BlockSpec((r, LRU_CB), lambda i, j, k: (0, j))
    return pl.pallas_call(
        _rglru_kernel,
        out_shape=jax.ShapeDtypeStruct((n, LRU_WIDTH), BF16),
        grid=(batch, ncb, nt),
        in_specs=[
            pl.BlockSpec((tt, LRU_CB), lambda i, j, k: (row(i, j, k), Z_LX // LRU_CB + j)),
            pl.BlockSpec((tt, LRU_CB), lambda i, j, k: (row(i, j, k), Z_LY // LRU_CB + j)),
            vec(CONV_WIDTH), vec(1),
            pl.BlockSpec((LRU_CB // LRU_WT, LRU_WT, LRU_WT), lambda i, j, k: (j, 0, 0)), vec(1),
            pl.BlockSpec((LRU_CB // LRU_WT, LRU_WT, LRU_WT), lambda i, j, k: (j, 0, 0)), vec(1),
            vec(1),
        ],
        out_specs=pl.BlockSpec((tt, LRU_CB), lambda i, j, k: (row(i, j, k), j)),
        scratch_shapes=[pltpu.VMEM((SUBLANES, LRU_CB), F32), pltpu.VMEM((1, LRU_CB), F32),
                        pltpu.VMEM((tt, LRU_CB), F32), pltpu.VMEM((tt, LRU_CB), F32)],
        compiler_params=_cparams(("parallel", "parallel", "arbitrary")),
        name="rglru",
    )(z, z, cw, cbias, wa_bd, ba, wx_bd, bx, lam)


def _merge_kernel(o_ref, l_ref, mga_ref, mgb_ref, wn_ref, wl_ref, m_ref):
    ya = jnp.dot(o_ref[...], wn_ref[...], preferred_element_type=F32)
    yb = jnp.dot(l_ref[...], wl_ref[...], preferred_element_type=F32)
    m = _sigmoid(mga_ref[...].astype(F32)) * ya + _sigmoid(mgb_ref[...].astype(F32)) * yb
    m_ref[...] = m.astype(m_ref.dtype)


def _merge(o, lru, z, wn, wl, tm=512):
    n, d = o.shape[0], wn.shape[1]
    return pl.pallas_call(
        _merge_kernel,
        out_shape=jax.ShapeDtypeStruct((n, d), BF16),
        grid=(n // tm,),
        in_specs=[
            pl.BlockSpec((tm, o.shape[1]), lambda i: (i, 0)),
            pl.BlockSpec((tm, lru.shape[1]), lambda i: (i, 0)),
            pl.BlockSpec((tm, d), lambda i: (i, Z_MGA // d)),
            pl.BlockSpec((tm, d), lambda i: (i, Z_MGB // d)),
            pl.BlockSpec(wn.shape, lambda i: (0, 0)),
            pl.BlockSpec(wl.shape, lambda i: (0, 0)),
        ],
        out_specs=pl.BlockSpec((tm, d), lambda i: (i, 0)),
        compiler_params=_cparams(("parallel",)),
        name="merge",
    )(o, lru, z, z, wn, wl)


def _out_route_kernel(m_ref, x_ref, wo_ref, g_ref, wr_ref, br_ref, x1_ref, xn_ref, eid_ref, ew_ref):
    x1 = x_ref[...] + jnp.dot(m_ref[...], wo_ref[...], preferred_element_type=F32)
    x1_ref[...] = x1
    xn = _rms(x1, g_ref[...])
    _store_rows(xn_ref, 0, xn)
    lg = jnp.dot(xn, wr_ref[...], preferred_element_type=F32).T + br_ref[...]
    tm = lg.shape[1]
    sub = lax.broadcasted_iota(jnp.int32, (SUBLANES, tm), 0)

    def first_argmax(v, vmax):
        return jnp.min(jnp.where(v == vmax, sub, SUBLANES), axis=0, keepdims=True)

    gl = jnp.where(sub < N_GROUPS, lg[0:SUBLANES], -jnp.inf)
    gmax = jnp.max(gl, axis=0, keepdims=True)
    ge = jnp.exp(gl - gmax)
    gprob = ge / jnp.sum(ge, axis=0, keepdims=True)
    g_val = jnp.max(gprob, axis=0, keepdims=True)
    g_idx = first_argmax(gprob, g_val)
    e_in = jnp.zeros((EXPERTS_PER_GROUP, tm), F32)
    for gi in range(N_GROUPS):
        lo = SUBLANES + gi * EXPERTS_PER_GROUP
        e_in = jnp.where(g_idx == gi, lg[lo:lo + EXPERTS_PER_GROUP], e_in)
    ee = jnp.exp(e_in - jnp.max(e_in, axis=0, keepdims=True))
    eprob = ee / jnp.sum(ee, axis=0, keepdims=True)
    v1 = jnp.max(eprob, axis=0, keepdims=True)
    i1 = first_argmax(eprob, v1)
    rest = jnp.where(sub == i1, -1.0, eprob)
    v2 = jnp.max(rest, axis=0, keepdims=True)
    i2 = first_argmax(rest, v2)
    den = v1 + v2
    eid = jnp.where(sub == 0, g_idx * EXPERTS_PER_GROUP + i1, g_idx * EXPERTS_PER_GROUP + i2)
    eid_ref[...] = eid
    ew_ref[...] = jnp.where(sub == 0, g_val * v1 / den, g_val * v2 / den)


def _out_route(m, x2d, wo, g, wr, br, tm=512):
    n, d = x2d.shape
    once = pl.Buffered(1)
    return pl.pallas_call(
        _out_route_kernel,
        out_shape=(jax.ShapeDtypeStruct((n, d), F32), jax.ShapeDtypeStruct((n * _row_pitch(d), LANES), F32),
                   jax.ShapeDtypeStruct((SUBLANES, n), jnp.int32), jax.ShapeDtypeStruct((SUBLANES, n), F32)),
        grid=(n // tm,),
        in_specs=[
            pl.BlockSpec((tm, d), lambda i: (i, 0)),
            pl.BlockSpec((tm, d), lambda i: (i, 0)),
            pl.BlockSpec((d, d), lambda i: (0, 0), pipeline_mode=once),
            pl.BlockSpec((1, d), lambda i: (0, 0)),
            pl.BlockSpec(wr.shape, lambda i: (0, 0), pipeline_mode=once),
            pl.BlockSpec(br.shape, lambda i: (0, 0)),
        ],
        out_specs=(pl.BlockSpec((tm, d), lambda i: (i, 0)), pl.BlockSpec((tm * _row_pitch(d), LANES), lambda i: (i, 0)),
                   pl.BlockSpec((SUBLANES, tm), lambda i: (0, i)), pl.BlockSpec((SUBLANES, tm), lambda i: (0, i))),
        compiler_params=_cparams(("parallel",)),
        name="out_route",
    )(m, x2d, wo, g, wr, br)


def _for_rows(cnt, fn, unroll=8):
    sh = unroll.bit_length() - 1

    def group(gidx, c):
        for u in range(unroll):
            fn(gidx * unroll + u)
        return c

    def single(r, c):
        fn(r)
        return c

    lax.fori_loop(0, cnt >> sh, group, 0)
    lax.fori_loop((cnt >> sh) << sh, cnt, single, 0)


def _moe_kernel(nvalid_ref, texp_ref, cnt_ref, tok_ref, tokn_ref, dst_ref, xn_hbm, wg_ref, wu_ref, wd_ref, y_hbm,
                xbuf, ybuf, gsem, ssem):
    i = pl.program_id(0)
    nv = nvalid_ref[0]
    slot = i % 2
    d = wg_ref.shape[1]
    k = _row_pitch(d)
    tm = xbuf.shape[0] // (2 * k)

    def line(ref, row):
        return ref.at[pl.ds(row * k, k), :]

    def gather_row(idx_ref, s):
        def fn(r):
            pltpu.make_async_copy(line(xn_hbm, idx_ref[0, 0, r]), line(xbuf, s * tm + r), gsem.at[s]).start(priority=1)
        return fn

    def gather_wait_row(s):
        def fn(r):
            pltpu.make_async_copy(line(xn_hbm, 0), line(xbuf, s * tm + r), gsem.at[s]).wait()
        return fn

    def scatter_row(s):
        def fn(r):
            pltpu.make_async_copy(line(ybuf, s * tm + r), line(y_hbm, dst_ref[0, 0, r]), ssem.at[s]).start()
        return fn

    def scatter_wait_row(s):
        def fn(r):
            pltpu.make_async_copy(line(ybuf, s * tm + r), line(y_hbm, 0), ssem.at[s]).wait()
        return fn

    @pl.when(i == 0)
    def _():
        xbuf[...] = jnp.zeros_like(xbuf)
        _for_rows(cnt_ref[0], gather_row(tok_ref, 0))

    @pl.when(i + 1 < nv)
    def _():
        _for_rows(cnt_ref[i + 1], gather_row(tokn_ref, 1 - slot))

    @pl.when(i < nv)
    def _():
        _for_rows(cnt_ref[i], gather_wait_row(slot))
        x = _load_rows(xbuf, slot * tm, tm, d).astype(BF16)
        gp = jnp.dot(x, wg_ref[0], preferred_element_type=F32)
        up = jnp.dot(x, wu_ref[0], preferred_element_type=F32)
        hid = (gp * _sigmoid(gp) * up).astype(BF16)
        y = jnp.dot(hid, wd_ref[0], preferred_element_type=F32)

        @pl.when(i >= 2)
        def _():
            _for_rows(cnt_ref[i - 2], scatter_wait_row(slot))

        _store_rows(ybuf, slot * tm, y)
        _for_rows(cnt_ref[i], scatter_row(slot))

    @pl.when(i == nv - 1)
    def _():
        _for_rows(cnt_ref[i], scatter_wait_row(slot))

        @pl.when(nv >= 2)
        def _():
            _for_rows(cnt_ref[i - 1], scatter_wait_row(1 - slot))


def _moe(nvalid, texp, cnt, tok, dst, xn, wg, wu, wd, n_rows_out):
    nt, _, tm = tok.shape
    d, de = wg.shape[1], wg.shape[2]
    k = _row_pitch(d)
    grid_spec = pltpu.PrefetchScalarGridSpec(
        num_scalar_prefetch=3,
        grid=(nt,),
        in_specs=[
            pl.BlockSpec((1, 1, tm), lambda i, nv, te, ct: (i, 0, 0), memory_space=pltpu.SMEM),
            pl.BlockSpec((1, 1, tm), lambda i, nv, te, ct: (jnp.minimum(i + 1, nt - 1), 0, 0),
                         memory_space=pltpu.SMEM),
            pl.BlockSpec((1, 1, tm), lambda i, nv, te, ct: (i, 0, 0), memory_space=pltpu.SMEM),
            pl.BlockSpec(memory_space=pl.ANY),
            pl.BlockSpec((1, d, de), lambda i, nv, te, ct: (te[i], 0, 0)),
            pl.BlockSpec((1, d, de), lambda i, nv, te, ct: (te[i], 0, 0)),
            pl.BlockSpec((1, de, d), lambda i, nv, te, ct: (te[i], 0, 0)),
        ],
        out_specs=pl.BlockSpec(memory_space=pl.ANY),
        scratch_shapes=[pltpu.VMEM((2 * tm * k, LANES), F32), pltpu.VMEM((2 * tm * k, LANES), F32),
                        pltpu.SemaphoreType.DMA((2,)), pltpu.SemaphoreType.DMA((2,))],
    )
    return pl.pallas_call(
        _moe_kernel,
        out_shape=jax.ShapeDtypeStruct((n_rows_out * k, LANES), F32),
        grid_spec=grid_spec,
        compiler_params=_cparams(("arbitrary",)),
        name="moe",
    )(nvalid, texp, cnt, tok, tok, dst, xn, wg, wu, wd)


def _moe_plan(eid, n, tm):
    e_flat = eid.reshape(-1)
    npairs = e_flat.shape[0]
    experts = jnp.arange(N_EXPERTS, dtype=jnp.int32)
    counts = jnp.sum((e_flat[:, None] == experts[None, :]).astype(jnp.int32), axis=0)
    padded = ((counts + tm - 1) // tm) * tm
    ends = jnp.cumsum(padded)
    offs = ends - padded
    n_rows = npairs + N_EXPERTS * tm
    nt = n_rows // tm
    fill = jnp.arange(n_rows - npairs, dtype=jnp.int32)
    fill_expert = jnp.sum((fill[:, None] >= jnp.cumsum(padded - counts)[None, :]).astype(jnp.int32), axis=1)
    keys = jnp.concatenate([e_flat * (2 * npairs) + jnp.arange(npairs, dtype=jnp.int32),
                            fill_expert * (2 * npairs) + npairs])
    skeys = jnp.sort(keys)
    low = skeys % (2 * npairs)
    dst = jnp.where(low < npairs, low, 0)
    tok = dst % n
    tile_start = jnp.arange(nt, dtype=jnp.int32) * tm
    texp = jnp.minimum(jnp.sum((tile_start[:, None] >= ends[None, :]).astype(jnp.int32), axis=1), N_EXPERTS - 1)
    cnt = jnp.clip(offs[texp] + counts[texp] - tile_start, 0, tm)
    nvalid = (ends[-1] // tm).astype(jnp.int32).reshape(1)
    return (nvalid, texp.astype(jnp.int32), cnt.astype(jnp.int32), tok.reshape(nt, 1, tm), dst.reshape(nt, 1, tm),
            npairs)


def _ple_final_kernel(x1_ref, y0_ref, y1_ref, w_ref, p_ref, gp_ref, wpg_ref, wp_ref, gf_ref, o_ref):
    w = w_ref[...]
    tm, d = x1_ref.shape
    x2 = x1_ref[...] + w[:, 0:1] * _load_rows(y0_ref, 0, tm, d) + w[:, 1:2] * _load_rows(y1_ref, 0, tm, d)
    hn = _rms(x2, gp_ref[...]).astype(BF16)
    gate = _sigmoid(jnp.dot(hn, wpg_ref[...], preferred_element_type=F32))
    pe = jnp.dot(p_ref[...].astype(BF16), wp_ref[...], preferred_element_type=F32)
    x3 = x2 + gate * pe
    o_ref[...] = _rms(x3, gf_ref[...])


def _ple_final(x1, ypairs, wcols, p2d, gp, wpg, wp, gf, tm=256):
    n, d = x1.shape
    nb = n // tm
    k = _row_pitch(d)
    return pl.pallas_call(
        _ple_final_kernel,
        out_shape=jax.ShapeDtypeStruct((n, d), F32),
        grid=(nb,),
        in_specs=[
            pl.BlockSpec((tm, d), lambda i: (i, 0)),
            pl.BlockSpec((tm * k, LANES), lambda i: (i, 0)),
            pl.BlockSpec((tm * k, LANES), lambda i: (nb + i, 0)),
            pl.BlockSpec((tm, wcols.shape[1]), lambda i: (i, 0)),
            pl.BlockSpec((tm, p2d.shape[1]), lambda i: (i, 0)),
            pl.BlockSpec((1, d), lambda i: (0, 0)),
            pl.BlockSpec((d, d), lambda i: (0, 0)),
            pl.BlockSpec(wp.shape, lambda i: (0, 0)),
            pl.BlockSpec((1, d), lambda i: (0, 0)),
        ],
        out_specs=pl.BlockSpec((tm, d), lambda i: (i, 0)),
        compiler_params=_cparams(("parallel",)),
        name="ple_final",
    )(x1, ypairs, ypairs, wcols, p2d, gp, wpg, wp, gf)


def _block_diag(w, per):
    nb, bw, _ = w.shape
    w = w.reshape(nb // per, per, bw, bw)
    eye = jnp.eye(per, dtype=w.dtype)
    return jnp.einsum("cpij,pq->cpiqj", w, eye).reshape(nb // per, per * bw, per * bw)


def _regroup_kernel(wt_ref, o_ref):
    d = Z_MGB - Z_MGA
    c = np.cumsum((0, NSA_Q_WIDTH) + (NSA_KV_WIDTH,) * 6 + (3 * NSA_HEADS, LRU_WIDTH, LRU_WIDTH, d, d))
    seg = lambda k: wt_ref[int(c[k]):int(c[k + 1]), :]
    q, k_c, v_c, k_s, v_s, k_w, v_w, gates, lru_x, lru_y, mg_a, mg_b = (seg(k) for k in range(12))

    def pair(a, b):
        parts = []
        for g in range(NSA_KV_GROUPS):
            parts += [a[g * HEAD_DIM:(g + 1) * HEAD_DIM], b[g * HEAD_DIM:(g + 1) * HEAD_DIM]]
        return jnp.concatenate(parts, axis=0)

    pad = jnp.zeros((Z_WIDTH - int(c[-1]), wt_ref.shape[1]), wt_ref.dtype)
    rows = [mg_a, mg_b, q, pair(k_s, k_w), pair(v_s, v_w), k_c, v_c, lru_x, lru_y, gates, pad]
    o_ref[...] = jnp.concatenate(rows, axis=0).T.astype(o_ref.dtype)


def _regroup_w_in(w_in_t, tr=256):
    nw, d = w_in_t.shape
    return pl.pallas_call(
        _regroup_kernel,
        out_shape=jax.ShapeDtypeStruct((d, Z_WIDTH), BF16),
        grid=(d // tr,),
        in_specs=[pl.BlockSpec((nw, tr), lambda i: (0, i))],
        out_specs=pl.BlockSpec((tr, Z_WIDTH), lambda i: (i, 0)),
        compiler_params=_cparams(("parallel",)),
        name="regroup_w_in",
    )(w_in_t)


def _layer(x2d, batch, seq, ln_mix, w_in, cmp_k_pos, cmp_k_w1, cmp_k_w2, cmp_v_pos, cmp_v_w1, cmp_v_w2, conv_w,
           conv_b, lru_wa, lru_ba, lru_wx, lru_bx, lru_lambda, w_nsa_up, w_lru_up, w_out, ln_ffn, w_grp, b_grp, w_exp,
           b_exp, w_gate, w_up, w_down):
    n, d = x2d.shape
    g, hd = NSA_KV_GROUPS, HEAD_DIM

    z = _in_proj(x2d, ln_mix.reshape(1, d), _regroup_w_in(w_in.T))

    pos8 = lambda pos: jnp.broadcast_to(pos.reshape(1, -1), (SUBLANES, pos.size)).astype(BF16)
    kc, vct = _compress(z, batch, seq, pos8(cmp_k_pos), cmp_k_w1.astype(BF16), cmp_k_w2.astype(BF16),
                        pos8(cmp_v_pos), cmp_v_w1.astype(BF16), cmp_v_w2.T.astype(BF16))
    o = _nsa_attention(z, kc, vct, batch, seq)

    per = LRU_WT // LRU_BW
    lru = _rglru(z, batch, seq, conv_w, conv_b.reshape(1, -1), _block_diag(lru_wa, per).astype(BF16),
                 lru_ba.reshape(1, -1), _block_diag(lru_wx, per).astype(BF16), lru_bx.reshape(1, -1),
                 lru_lambda.reshape(1, -1))

    merged = _merge(o, lru, z, w_nsa_up.astype(BF16), w_lru_up.astype(BF16))

    wr = jnp.zeros((d, LANES), F32).at[:, 0:N_GROUPS].set(w_grp).at[:, SUBLANES:SUBLANES + N_EXPERTS].set(w_exp)
    br = jnp.zeros((LANES, 1), F32).at[0:N_GROUPS, 0].set(b_grp).at[SUBLANES:SUBLANES + N_EXPERTS, 0].set(b_exp)
    x1, xn, eid, ew = _out_route(merged, x2d, w_out.astype(BF16), ln_ffn.reshape(1, d), wr, br)

    nvalid, texp, cnt, tok, dst, n_rows_out = _moe_plan(eid[0:EXPERT_TOPK], n, MOE_TM)
    ypairs = _moe(nvalid, texp, cnt, tok, dst, xn, w_gate, w_up, w_down, n_rows_out)

    return x1, ypairs, ew.T


def kernel(x, p, ln_mix, w_in, cmp_k_pos, cmp_k_w1, cmp_k_w2, cmp_v_pos, cmp_v_w1, cmp_v_w2, conv_w, conv_b, lru_wa, lru_ba, lru_wx, lru_bx, lru_lambda, w_nsa_up, w_lru_up, w_out, ln_ffn, w_grp, b_grp, w_exp, b_exp, w_gate, w_up, w_down, ln_ple, w_ple, w_ple_gate, ln_final):
    batch, seq, d = x.shape
    assert p.shape[0] == 1, "the final norm is fused into the (single) layer's last kernel"
    n = batch * seq
    x1, ypairs, wcols = _layer(
        x.reshape(n, d), batch, seq, ln_mix[0], w_in[0], cmp_k_pos[0], cmp_k_w1[0], cmp_k_w2[0],
        cmp_v_pos[0], cmp_v_w1[0], cmp_v_w2[0], conv_w[0], conv_b[0], lru_wa[0], lru_ba[0], lru_wx[0], lru_bx[0],
        lru_lambda[0], w_nsa_up[0], w_lru_up[0], w_out[0], ln_ffn[0], w_grp[0], b_grp[0], w_exp[0], b_exp[0],
        w_gate[0], w_up[0], w_down[0])
    out = _ple_final(x1, ypairs, wcols, p[0].reshape(n, -1), ln_ple[0].reshape(1, d),
                     w_ple_gate[0].astype(BF16), w_ple[0].astype(BF16), ln_final.reshape(1, d))
    return out.reshape(batch, seq, d)
```

```python
import numpy as np
import jax
import jax.numpy as jnp
from jax import lax
from jax.experimental import pallas as pl
from jax.experimental.pallas import tpu as pltpu

F32 = jnp.float32
BF16 = jnp.bfloat16

NSA_HEADS = 16
NSA_KV_GROUPS = 4
NSA_HPG = NSA_HEADS // NSA_KV_GROUPS
HEAD_DIM = 64
NSA_Q_WIDTH = NSA_HEADS * HEAD_DIM
NSA_KV_WIDTH = NSA_KV_GROUPS * HEAD_DIM
CMP_BLOCK = 32
CMP_STRIDE = 16
CMP_HIDDEN = 2 * HEAD_DIM
SLC_BLOCK = 64
SLC_TOPN = 16
WINDOW = 512
ATTN_SCALE = HEAD_DIM ** -0.5
NEG_INF = -1e30
LOG2E = float(np.log2(np.e))
LRU_WIDTH = 1024
LRU_BLOCKS = 16
LRU_BW = LRU_WIDTH // LRU_BLOCKS
CONV_WIDTH = 4
LRU_C = 8.0
N_GROUPS = 4
EXPERTS_PER_GROUP = 8
N_EXPERTS = N_GROUPS * EXPERTS_PER_GROUP
EXPERT_TOPK = 2
D_EXPERT = 512
EPS = 1e-6

LANES = 128
SUBLANES = 8
VMEM_LIMIT_BYTES = 56 * 1024 * 1024

Z_MGA = 0
Z_MGB = 2048
Z_Q = 4096
Z_KK = 5120
Z_VV = 5632
Z_KC = 6144
Z_VC = 6400
Z_LX = 6656
Z_LY = 7680
Z_GATE = 8704
Z_WIDTH = 9216
LRU_CB = 512
LRU_WT = 256
MOE_TM = 256
NSA_TQ = 256
NSA_CK = 512


def _cparams(sem, vmem=VMEM_LIMIT_BYTES):
    return pltpu.CompilerParams(dimension_semantics=sem, vmem_limit_bytes=vmem)


def _rms(x, g):
    return x * lax.rsqrt(jnp.mean(x * x, axis=-1, keepdims=True) + EPS) * g


def _gelu_tanh(x):
    return 0.5 * x * (1.0 + jnp.tanh(np.sqrt(2.0 / np.pi) * (x + 0.044715 * (x * x * x))))


def _sigmoid(x):
    return 1.0 / (1.0 + jnp.exp(-x))


def _row_pitch(d):
    k = d // LANES
    return k + 1 - (k % 2)


def _store_rows(ref, row0, val):
    rows, d = val.shape
    pitch = _row_pitch(d)
    for j in range(d // LANES):
        ref[pl.ds(row0 * pitch + j, rows, stride=pitch), :] = val[:, j * LANES:(j + 1) * LANES]
    for j in range(d // LANES, pitch):
        ref[pl.ds(row0 * pitch + j, rows, stride=pitch), :] = jnp.zeros((rows, LANES), val.dtype)


def _load_rows(ref, row0, rows, d):
    pitch = _row_pitch(d)
    return jnp.concatenate([ref[pl.ds(row0 * pitch + j, rows, stride=pitch), :] for j in range(d // LANES)], axis=1)


def _in_proj_kernel(x_ref, g_ref, w_ref, o_ref, h_ref):
    @pl.when(pl.program_id(1) == 0)
    def _():
        h_ref[...] = _rms(x_ref[...], g_ref[...]).astype(BF16)

    o_ref[...] = jnp.dot(h_ref[...], w_ref[...], preferred_element_type=F32).astype(o_ref.dtype)


def _in_proj(x2d, g, w, tm=1024, tn=2304):
    n, d = x2d.shape
    nw = w.shape[1]
    return pl.pallas_call(
        _in_proj_kernel,
        out_shape=jax.ShapeDtypeStruct((n, nw), BF16),
        grid=(n // tm, nw // tn),
        in_specs=[
            pl.BlockSpec((tm, d), lambda i, j: (i, 0)),
            pl.BlockSpec((1, d), lambda i, j: (0, 0)),
            pl.BlockSpec((d, tn), lambda i, j: (0, j)),
        ],
        out_specs=pl.BlockSpec((tm, tn), lambda i, j: (i, j)),
        scratch_shapes=[pltpu.VMEM((tm, d), BF16)],
        compiler_params=_cparams(("parallel", "arbitrary")),
        name="in_proj",
    )(x2d, g, w)


def _compress_kernel(zk_ref, zv_ref, kpos_ref, kw1_ref, kw2_ref, vpos_ref, vw1_ref, vw2t_ref, kc_ref, vct_ref,
                     xf_ref):
    seq = zk_ref.shape[0]
    ncp = seq // CMP_STRIDE
    hd = HEAD_DIM
    half = CMP_STRIDE * hd

    def half_windows(z_ref):
        xf_ref[...] = z_ref[...].astype(F32)
        lines = [xf_ref[pl.ds(r, ncp, stride=CMP_STRIDE), :].astype(BF16) for r in range(CMP_STRIDE)]
        return [jnp.concatenate([ln[:, gg * hd:(gg + 1) * hd] for ln in lines], axis=1) for gg in range(LANES // hd)]

    def hidden(x, pos_ref, w1_ref):
        w1 = w1_ref[...]
        ha = jnp.dot(x, w1[:half], preferred_element_type=F32)
        hb = jnp.dot(x, w1[half:], preferred_element_type=F32)
        hb = pltpu.roll(hb, hb.shape[0] - 1, axis=0)
        pc = jnp.dot(pos_ref[...], w1, preferred_element_type=F32)[0:1]
        return _gelu_tanh(ha + hb + pc).astype(BF16)

    for gg, x in enumerate(half_windows(zk_ref)):
        hk = hidden(x, kpos_ref, kw1_ref)
        kc_ref[0, gg] = jnp.dot(hk, kw2_ref[...], preferred_element_type=F32).astype(kc_ref.dtype)
    for gg, x in enumerate(half_windows(zv_ref)):
        hv = hidden(x, vpos_ref, vw1_ref)
        vct_ref[0, gg] = lax.dot_general(vw2t_ref[...], hv, (((1,), (1,)), ((), ())),
                                         preferred_element_type=F32).astype(vct_ref.dtype)


def _compress(z, batch, seq, kpos, kw1, kw2, vpos, vw1, vw2t):
    g, hd = NSA_KV_GROUPS, HEAD_DIM
    gpb = LANES // hd
    ncp = seq // CMP_STRIDE
    full = lambda a: pl.BlockSpec(a.shape, lambda i, j: (0,) * a.ndim)
    return pl.pallas_call(
        _compress_kernel,
        out_shape=(jax.ShapeDtypeStruct((batch, g, ncp, hd), BF16),
                   jax.ShapeDtypeStruct((batch, g, hd, ncp), BF16)),
        grid=(batch, g // gpb),
        in_specs=[
            pl.BlockSpec((seq, LANES), lambda i, j: (i, Z_KC // LANES + j)),
            pl.BlockSpec((seq, LANES), lambda i, j: (i, Z_VC // LANES + j)),
            full(kpos), full(kw1), full(kw2), full(vpos), full(vw1), full(vw2t),
        ],
        out_specs=(pl.BlockSpec((1, gpb, ncp, hd), lambda i, j: (i, j, 0, 0)),
                   pl.BlockSpec((1, gpb, hd, ncp), lambda i, j: (i, j, 0, 0))),
        scratch_shapes=[pltpu.VMEM((seq, LANES), F32)],
        compiler_params=_cparams(("parallel", "parallel")),
        name="compress",
    )(z, z, kpos, kw1, kw2, vpos, vw1, vw2t)


def _nsa_kernel(zq_ref, zkk_ref, zvv_ref, zg_ref, kc_ref, vct_ref, o_ref, kcomb_ref, vs_ref, vw_ref, gate_ref,
                sa_ref, sb_ref):
    grp = pl.program_id(1)
    qi = pl.program_id(2)
    tq, hd, hpg = NSA_TQ, HEAD_DIM, NSA_HPG
    nq = tq * hpg
    seq = kcomb_ref.shape[0]
    ncp = kc_ref.shape[2]
    nsb = seq // SLC_BLOCK
    blk_shift = SLC_BLOCK.bit_length() - 1

    @pl.when(qi == 0)
    def _():
        kcomb_ref[:, 0:2 * hd] = zkk_ref[...]
        blk_of_row = lax.broadcasted_iota(jnp.int32, (seq, 2 * hd), 0) >> blk_shift
        lane = lax.broadcasted_iota(jnp.int32, (seq, 2 * hd), 1)
        kcomb_ref[:, 2 * hd:] = jnp.where(blk_of_row == lane, 1.0, 0.0).astype(BF16)
        ones = jnp.ones((seq, hd), BF16)
        vs_ref[...] = jnp.concatenate([zvv_ref[:, 0:hd], ones], axis=1)
        vw_ref[...] = jnp.concatenate([zvv_ref[:, hd:2 * hd], ones], axis=1)

    qt = zq_ref[...].T
    q = jnp.concatenate([qt[h * hd:(h + 1) * hd, :] for h in range(hpg)], axis=1)
    q = (q.astype(F32) * (ATTN_SCALE * LOG2E)).astype(BF16)
    zero = jnp.zeros((hd, nq), BF16)
    t_lane = qi * tq + (lax.broadcasted_iota(jnp.int32, (1, nq), 1) & (tq - 1))
    tn_dims = (((0,), (0,)), ((), ()))

    sc = jnp.dot(kc_ref[0, 0], q, preferred_element_type=F32)
    cmp_end = lax.broadcasted_iota(jnp.int32, (ncp, 1), 0) * CMP_STRIDE + (CMP_BLOCK - 1)
    cmask = cmp_end <= t_lane
    sc = jnp.where(cmask, sc, NEG_INF)
    pc = jnp.where(cmask, jnp.exp2(sc - jnp.max(sc, axis=0, keepdims=True)), 0.0)
    lc = jnp.sum(pc, axis=0, keepdims=True)
    pc = pc * (1.0 / jnp.where(lc > 0.0, lc, 1.0))
    o_cmp = jnp.dot(vct_ref[0, 0], pc.astype(BF16), preferred_element_type=F32)

    psum = pc[:, 0:tq]
    for h in range(1, hpg):
        psum = psum + pc[:, h * tq:(h + 1) * tq]
    jrow = lax.broadcasted_iota(jnp.int32, (nsb, ncp), 0)
    crel = lax.broadcasted_iota(jnp.int32, (nsb, ncp), 1) - jrow * (SLC_BLOCK // CMP_STRIDE)
    ovt = jnp.where((crel >= 0) & (crel <= 2), 1.0, jnp.where((crel == -1) | (crel == 3), 0.5, 0.0)).astype(F32)
    imp = jnp.dot(ovt, psum, preferred_element_type=F32, precision=lax.Precision.HIGHEST)
    jblk = lax.broadcasted_iota(jnp.int32, (nsb, tq), 0)
    blk = (qi * tq + lax.broadcasted_iota(jnp.int32, (1, tq), 1)) >> blk_shift
    forced = (jblk == 0) | (jblk == blk) | (jblk == blk - 1)
    n_forced = 3
    score = jnp.where(forced, -2.0, jnp.where(jblk <= blk, imp, -1.0))
    selb = jnp.where(forced, 0.0, NEG_INF)
    for _ in range(min(SLC_TOPN, nsb) - n_forced):
        top = jnp.max(score, axis=0, keepdims=True)
        first = jnp.min(jnp.where(score == top, jblk, nsb), axis=0, keepdims=True)
        pick = jblk == first
        selb = jnp.where(pick, 0.0, selb)
        score = jnp.where(pick, -2.0, score)
    selb = selb.astype(BF16)

    ck = NSA_CK
    pad = jnp.zeros((2 * hd - nsb, nq), BF16)
    qa = jnp.concatenate([q, zero, jnp.concatenate([selb] * hpg, axis=1), pad], axis=0)

    def slc_scores(c):
        return jnp.dot(kcomb_ref[pl.ds(pl.multiple_of(c * ck, ck), ck), :], qa, preferred_element_type=F32)

    n_full = (qi * tq) // ck

    def slc_update(buf, c, carry, causal):
        def scores():
            if not causal:
                return buf[...]
            return jnp.where(c * ck + lax.broadcasted_iota(jnp.int32, (ck, 1), 0) <= t_lane, buf[...], NEG_INF)

        m, l, acc = carry
        m_new = jnp.maximum(m, jnp.max(scores(), axis=0, keepdims=True))
        alpha = jnp.exp2(m - m_new)
        p = jnp.exp2(scores() - m_new).astype(BF16)
        v = vs_ref[pl.ds(pl.multiple_of(c * ck, ck), ck), :]
        pv = lax.dot_general(v, p, tn_dims, preferred_element_type=F32)
        return m_new, alpha * l + pv[hd:hd + 1], acc * alpha + pv[0:hd]

    def slc_pair(i, carry):
        sb_ref[...] = slc_scores(2 * i + 1)
        carry = slc_update(sa_ref, 2 * i, carry, False)
        sa_ref[...] = slc_scores(2 * i + 2)
        return slc_update(sb_ref, 2 * i + 1, carry, False)

    def tail_two(carry):
        sb_ref[...] = slc_scores(n_full)
        return slc_update(sb_ref, n_full, slc_update(sa_ref, n_full - 1, carry, False), True)

    def tail_one(carry):
        return slc_update(sa_ref, n_full, carry, True)

    sa_ref[...] = slc_scores(0)

    nwc = WINDOW // tq + 1
    t_lo = t_lane - WINDOW
    ks_w, vs_w = [], []
    for i in range(nwc):
        rows = pl.ds(pl.multiple_of(jnp.maximum(qi - (nwc - 1) + i, 0) * tq, tq), tq)
        ks_w.append(kcomb_ref[rows, 0:2 * hd])
        vs_w.append(vw_ref[rows, :])
    qw = jnp.concatenate([zero, q], axis=0)
    sw = jnp.dot(jnp.concatenate(ks_w, axis=0), qw, preferred_element_type=F32)
    sw_parts = []
    for i in range(nwc):
        spos = (qi - (nwc - 1) + i) * tq + lax.broadcasted_iota(jnp.int32, (tq, 1), 0)
        if i < nwc - 1:
            ok = jnp.where(spos >= 0, spos, -(1 << 24)) > t_lo
        else:
            ok = spos <= t_lane
        sw_parts.append(jnp.where(ok, sw[i * tq:(i + 1) * tq], NEG_INF))
    sw = jnp.concatenate(sw_parts, axis=0)
    pw = jnp.exp2(sw - jnp.max(sw, axis=0, keepdims=True))
    ow = lax.dot_general(jnp.concatenate(vs_w, axis=0), pw.astype(BF16), tn_dims, preferred_element_type=F32)
    o_win = ow[0:hd] * (1.0 / ow[hd:hd + 1])
    init = (jnp.full((1, nq), NEG_INF, F32), jnp.zeros((1, nq), F32), jnp.zeros((hd, nq), F32))
    carry = lax.fori_loop(0, n_full // 2, slc_pair, init)
    _, l_s, acc_s = lax.cond(n_full % 2 == 1, tail_two, tail_one, carry)
    o_slc = acc_s * (1.0 / l_s)

    gate_ref[...] = _sigmoid(zg_ref[...].astype(F32)).T

    def branch_gate(br):
        rows = [gate_ref[pl.ds(grp * (hpg * 3) + h * 3 + br, 1), :] for h in range(hpg)]
        return jnp.concatenate(rows, axis=1)

    o = branch_gate(0) * o_cmp + branch_gate(1) * o_slc + branch_gate(2) * o_win
    o_heads = jnp.concatenate([o[:, h * tq:(h + 1) * tq] for h in range(hpg)], axis=0)
    o_ref[...] = o_heads.T.astype(o_ref.dtype)


def _nsa_attention(z, kc, vct, batch, seq):
    n = z.shape[0]
    g, hd, tq = NSA_KV_GROUPS, HEAD_DIM, NSA_TQ
    nqt = seq // tq
    ncp = kc.shape[2]
    nsb = seq // SLC_BLOCK
    gw = NSA_HPG * hd
    assert seq % NSA_CK == 0 and NSA_CK % tq == 0 and WINDOW % tq == 0 and nsb <= 2 * hd
    tile = lambda i, j, k: i * nqt + k
    return pl.pallas_call(
        _nsa_kernel,
        out_shape=jax.ShapeDtypeStruct((n, NSA_Q_WIDTH), BF16),
        grid=(batch, g, nqt),
        in_specs=[
            pl.BlockSpec((tq, gw), lambda i, j, k: (tile(i, j, k), Z_Q // gw + j)),
            pl.BlockSpec((seq, 2 * hd), lambda i, j, k: (i, Z_KK // (2 * hd) + j)),
            pl.BlockSpec((seq, 2 * hd), lambda i, j, k: (i, Z_VV // (2 * hd) + j)),
            pl.BlockSpec((tq, LANES), lambda i, j, k: (tile(i, j, k), Z_GATE // LANES)),
            pl.BlockSpec((1, 1, ncp, hd), lambda i, j, k: (i, j, 0, 0)),
            pl.BlockSpec((1, 1, hd, ncp), lambda i, j, k: (i, j, 0, 0)),
        ],
        out_specs=pl.BlockSpec((tq, gw), lambda i, j, k: (tile(i, j, k), j)),
        scratch_shapes=[pltpu.VMEM((seq, 4 * hd), BF16), pltpu.VMEM((seq, 2 * hd), BF16),
                        pltpu.VMEM((seq, 2 * hd), BF16), pltpu.VMEM((LANES, tq), F32), pltpu.VMEM((NSA_CK, NSA_HPG * tq), F32),
                        pltpu.VMEM((NSA_CK, NSA_HPG * tq), F32)],
        compiler_params=_cparams(("parallel", "parallel", "arbitrary")),
        name="nsa_attn",
    )(z, z, z, z, kc, vct)


def _rglru_kernel(x_ref, y_ref, cw_ref, cb_ref, wa_ref, ba_ref, wx_ref, bx_ref, lam_ref, o_ref,
                  tail_ref, h_ref, a_ref, u_ref):
    tc = pl.program_id(2)
    tt, cb = x_ref.shape

    @pl.when(tc == 0)
    def _():
        tail_ref[...] = jnp.zeros_like(tail_ref)
        h_ref[...] = jnp.zeros_like(h_ref)

    x = x_ref[...].astype(F32)
    xe = jnp.concatenate([tail_ref[...], x], axis=0)
    tail_ref[...] = x[tt - SUBLANES:, :]
    cw = cw_ref[...]
    xc = cb_ref[...]
    for k in range(CONV_WIDTH):
        off = SUBLANES - (CONV_WIDTH - 1) + k
        xc = xc + cw[k:k + 1, :] * xe[off:off + tt, :]
    xcb = xc.astype(BF16)
    def gate(w_ref, b_ref):
        parts = [jnp.dot(xcb[:, c * LRU_WT:(c + 1) * LRU_WT], w_ref[c], preferred_element_type=F32)
                 for c in range(cb // LRU_WT)]
        return _sigmoid(jnp.concatenate(parts, axis=1) + b_ref[...])

    r = gate(wa_ref, ba_ref)
    ig = gate(wx_ref, bx_ref)
    nl = -lam_ref[...]
    softplus = jnp.maximum(nl, 0.0) + jnp.log1p(jnp.exp(-jnp.abs(nl)))
    log_a = (-LRU_C) * softplus * r
    a_ref[...] = jnp.exp(log_a)
    th = jnp.tanh(log_a)
    u_ref[...] = jnp.sqrt(-2.0 * th / (1.0 - th)) * (ig * xc)

    row = lax.broadcasted_iota(jnp.int32, (SUBLANES, cb), 0)

    def step(i, h):
        sl = pl.ds(pl.multiple_of(i * SUBLANES, SUBLANES), SUBLANES)
        a = a_ref[sl, :]
        u = u_ref[sl, :]
        for s in (1, 2, 4):
            a_s = jnp.where(row >= s, pltpu.roll(a, s, axis=0), 1.0)
            u_s = jnp.where(row >= s, pltpu.roll(u, s, axis=0), 0.0)
            u = a * u_s + u
            a = a * a_s
        hrows = a * h + u
        u_ref[sl, :] = hrows
        return hrows[SUBLANES - 1:SUBLANES, :]

    h_ref[...] = lax.fori_loop(0, tt // SUBLANES, step, h_ref[...], unroll=8)
    o_ref[...] = (u_ref[...] * _gelu_tanh(y_ref[...].astype(F32))).astype(o_ref.dtype)


def _rglru(z, batch, seq, cw, cbias, wa_bd, ba, wx_bd, bx, lam, tt=1024):
    n = z.shape[0]
    ncb = LRU_WIDTH // LRU_CB
    nt = seq // tt
    row = lambda i, j, k: i * nt + k
    vec = lambda r: pl.BlockSpec((r, LRU_CB), lambda i, j, k: (0, j))
    return pl.pallas_call(
        _rglru_kernel,
        out_shape=jax.ShapeDtypeStruct((n, LRU_WIDTH), BF16),
        grid=(batch, ncb, nt),
        in_specs=[
            pl.BlockSpec((tt, LRU_CB), lambda i, j, k: (row(i, j, k), Z_LX // LRU_CB + j)),
            pl.BlockSpec((tt, LRU_CB), lambda i, j, k: (row(i, j, k), Z_LY // LRU_CB + j)),
            vec(CONV_WIDTH), vec(1),
            pl.BlockSpec((LRU_CB // LRU_WT, LRU_WT, LRU_WT), lambda i, j, k: (j, 0, 0)), vec(1),
            pl.BlockSpec((LRU_CB // LRU_WT, LRU_WT, LRU_WT), lambda i, j, k: (j, 0, 0)), vec(1),
            vec(1),
        ],
        out_specs=pl.BlockSpec((tt, LRU_CB), lambda i, j, k: (row(i, j, k), j)),
        scratch_shapes=[pltpu.VMEM((SUBLANES, LRU_CB), F32), pltpu.VMEM((1, LRU_CB), F32),
                        pltpu.VMEM((tt, LRU_CB), F32), pltpu.VMEM((tt, LRU_CB), F32)],
        compiler_params=_cparams(("parallel", "parallel", "arbitrary")),
        name="rglru",
    )(z, z, cw, cbias, wa_bd, ba, wx_bd, bx, lam)


def _merge_kernel(o_ref, l_ref, mga_ref, mgb_ref, wn_ref, wl_ref, m_ref):
    ya = jnp.dot(o_ref[...], wn_ref[...], preferred_element_type=F32)
    yb = jnp.dot(l_ref[...], wl_ref[...], preferred_element_type=F32)
    m = _sigmoid(mga_ref[...].astype(F32)) * ya + _sigmoid(mgb_ref[...].astype(F32)) * yb
    m_ref[...] = m.astype(m_ref.dtype)


def _merge(o, lru, z, wn, wl, tm=512):
    n, d = o.shape[0], wn.shape[1]
    return pl.pallas_call(
        _merge_kernel,
        out_shape=jax.ShapeDtypeStruct((n, d), BF16),
        grid=(n // tm,),
        in_specs=[
            pl.BlockSpec((tm, o.shape[1]), lambda i: (i, 0)),
            pl.BlockSpec((tm, lru.shape[1]), lambda i: (i, 0)),
            pl.BlockSpec((tm, d), lambda i: (i, Z_MGA // d)),
            pl.BlockSpec((tm, d), lambda i: (i, Z_MGB // d)),
            pl.BlockSpec(wn.shape, lambda i: (0, 0)),
            pl.BlockSpec(wl.shape, lambda i: (0, 0)),
        ],
        out_specs=pl.BlockSpec((tm, d), lambda i: (i, 0)),
        compiler_params=_cparams(("parallel",)),
        name="merge",
    )(o, lru, z, z, wn, wl)


def _out_route_kernel(m_ref, x_ref, wo_ref, g_ref, wr_ref, br_ref, x1_ref, xn_ref, eid_ref, ew_ref):
    x1 = x_ref[...] + jnp.dot(m_ref[...], wo_ref[...], preferred_element_type=F32)
    x1_ref[...] = x1
    xn = _rms(x1, g_ref[...])
    _store_rows(xn_ref, 0, xn)
    lg = jnp.dot(xn, wr_ref[...], preferred_element_type=F32).T + br_ref[...]
    tm = lg.shape[1]
    sub = lax.broadcasted_iota(jnp.int32, (SUBLANES, tm), 0)

    def first_argmax(v, vmax):
        return jnp.min(jnp.where(v == vmax, sub, SUBLANES), axis=0, keepdims=True)

    gl = jnp.where(sub < N_GROUPS, lg[0:SUBLANES], -jnp.inf)
    gmax = jnp.max(gl, axis=0, keepdims=True)
    ge = jnp.exp(gl - gmax)
    gprob = ge / jnp.sum(ge, axis=0, keepdims=True)
    g_val = jnp.max(gprob, axis=0, keepdims=True)
    g_idx = first_argmax(gprob, g_val)
    e_in = jnp.zeros((EXPERTS_PER_GROUP, tm), F32)
    for gi in range(N_GROUPS):
        lo = SUBLANES + gi * EXPERTS_PER_GROUP
        e_in = jnp.where(g_idx == gi, lg[lo:lo + EXPERTS_PER_GROUP], e_in)
    ee = jnp.exp(e_in - jnp.max(e_in, axis=0, keepdims=True))
    eprob = ee / jnp.sum(ee, axis=0, keepdims=True)
    v1 = jnp.max(eprob, axis=0, keepdims=True)
    i1 = first_argmax(eprob, v1)
    rest = jnp.where(sub == i1, -1.0, eprob)
    v2 = jnp.max(rest, axis=0, keepdims=True)
    i2 = first_argmax(rest, v2)
    den = v1 + v2
    eid = jnp.where(sub == 0, g_idx * EXPERTS_PER_GROUP + i1, g_idx * EXPERTS_PER_GROUP + i2)
    eid_ref[...] = eid
    ew_ref[...] = jnp.where(sub == 0, g_val * v1 / den, g_val * v2 / den)


def _out_route(m, x2d, wo, g, wr, br, tm=512):
    n, d = x2d.shape
    once = pl.Buffered(1)
    return pl.pallas_call(
        _out_route_kernel,
        out_shape=(jax.ShapeDtypeStruct((n, d), F32), jax.ShapeDtypeStruct((n * _row_pitch(d), LANES), F32),
                   jax.ShapeDtypeStruct((SUBLANES, n), jnp.int32), jax.ShapeDtypeStruct((SUBLANES, n), F32)),
        grid=(n // tm,),
        in_specs=[
            pl.BlockSpec((tm, d), lambda i: (i, 0)),
            pl.BlockSpec((tm, d), lambda i: (i, 0)),
            pl.BlockSpec((d, d), lambda i: (0, 0), pipeline_mode=once),
            pl.BlockSpec((1, d), lambda i: (0, 0)),
            pl.BlockSpec(wr.shape, lambda i: (0, 0), pipeline_mode=once),
            pl.BlockSpec(br.shape, lambda i: (0, 0)),
        ],
        out_specs=(pl.BlockSpec((tm, d), lambda i: (i, 0)), pl.BlockSpec((tm * _row_pitch(d), LANES), lambda i: (i, 0)),
                   pl.BlockSpec((SUBLANES, tm), lambda i: (0, i)), pl.BlockSpec((SUBLANES, tm), lambda i: (0, i))),
        compiler_params=_cparams(("parallel",)),
        name="out_route",
    )(m, x2d, wo, g, wr, br)


def _for_rows(cnt, fn, unroll=8):
    sh = unroll.bit_length() - 1

    def group(gidx, c):
        for u in range(unroll):
            fn(gidx * unroll + u)
        return c

    def single(r, c):
        fn(r)
        return c

    lax.fori_loop(0, cnt >> sh, group, 0)
    lax.fori_loop((cnt >> sh) << sh, cnt, single, 0)


def _moe_kernel(nvalid_ref, texp_ref, cnt_ref, tok_ref, tokn_ref, dst_ref, xn_hbm, wg_ref, wu_ref, wd_ref, y_hbm,
                xbuf, ybuf, gsem, ssem):
    i = pl.program_id(0)
    nv = nvalid_ref[0]
    slot = i % 2
    d = wg_ref.shape[1]
    k = _row_pitch(d)
    tm = xbuf.shape[0] // (2 * k)

    def line(ref, row):
        return ref.at[pl.ds(row * k, k), :]

    def gather_row(idx_ref, s):
        def fn(r):
            pltpu.make_async_copy(line(xn_hbm, idx_ref[0, 0, r]), line(xbuf, s * tm + r), gsem.at[s]).start(priority=1)
        return fn

    def gather_wait_row(s):
        def fn(r):
            pltpu.make_async_copy(line(xn_hbm, 0), line(xbuf, s * tm + r), gsem.at[s]).wait()
        return fn

    def scatter_row(s):
        def fn(r):
            pltpu.make_async_copy(line(ybuf, s * tm + r), line(y_hbm, dst_ref[0, 0, r]), ssem.at[s]).start()
        return fn

    def scatter_wait_row(s):
        def fn(r):
            pltpu.make_async_copy(line(ybuf, s * tm + r), line(y_hbm, 0), ssem.at[s]).wait()
        return fn

    @pl.when(i == 0)
    def _():
        xbuf[...] = jnp.zeros_like(xbuf)
        _for_rows(cnt_ref[0], gather_row(tok_ref, 0))

    @pl.when(i + 1 < nv)
    def _():
        _for_rows(cnt_ref[i + 1], gather_row(tokn_ref, 1 - slot))

    @pl.when(i < nv)
    def _():
        _for_rows(cnt_ref[i], gather_wait_row(slot))
        x = _load_rows(xbuf, slot * tm, tm, d).astype(BF16)
        gp = jnp.dot(x, wg_ref[0], preferred_element_type=F32)
        up = jnp.dot(x, wu_ref[0], preferred_element_type=F32)
        hid = (gp * _sigmoid(gp) * up).astype(BF16)
        y = jnp.dot(hid, wd_ref[0], preferred_element_type=F32)

        @pl.when(i >= 2)
        def _():
            _for_rows(cnt_ref[i - 2], scatter_wait_row(slot))

        _store_rows(ybuf, slot * tm, y)
        _for_rows(cnt_ref[i], scatter_row(slot))

    @pl.when(i == nv - 1)
    def _():
        _for_rows(cnt_ref[i], scatter_wait_row(slot))

        @pl.when(nv >= 2)
        def _():
            _for_rows(cnt_ref[i - 1], scatter_wait_row(1 - slot))


def _moe(nvalid, texp, cnt, tok, dst, xn, wg, wu, wd, n_rows_out):
    nt, _, tm = tok.shape
    d, de = wg.shape[1], wg.shape[2]
    k = _row_pitch(d)
    grid_spec = pltpu.PrefetchScalarGridSpec(
        num_scalar_prefetch=3,
        grid=(nt,),
        in_specs=[
            pl.BlockSpec((1, 1, tm), lambda i, nv, te, ct: (i, 0, 0), memory_space=pltpu.SMEM),
            pl.BlockSpec((1, 1, tm), lambda i, nv, te, ct: (jnp.minimum(i + 1, nt - 1), 0, 0),
                         memory_space=pltpu.SMEM),
            pl.BlockSpec((1, 1, tm), lambda i, nv, te, ct: (i, 0, 0), memory_space=pltpu.SMEM),
            pl.BlockSpec(memory_space=pl.ANY),
            pl.BlockSpec((1, d, de), lambda i, nv, te, ct: (te[i], 0, 0)),
            pl.BlockSpec((1, d, de), lambda i, nv, te, ct: (te[i], 0, 0)),
            pl.BlockSpec((1, de, d), lambda i, nv, te, ct: (te[i], 0, 0)),
        ],
        out_specs=pl.BlockSpec(memory_space=pl.ANY),
        scratch_shapes=[pltpu.VMEM((2 * tm * k, LANES), F32), pltpu.VMEM((2 * tm * k, LANES), F32),
                        pltpu.SemaphoreType.DMA((2,)), pltpu.SemaphoreType.DMA((2,))],
    )
    return pl.pallas_call(
        _moe_kernel,
        out_shape=jax.ShapeDtypeStruct((n_rows_out * k, LANES), F32),
        grid_spec=grid_spec,
        compiler_params=_cparams(("arbitrary",)),
        name="moe",
    )(nvalid, texp, cnt, tok, tok, dst, xn, wg, wu, wd)


def _moe_plan(eid, n, tm):
    e_flat = eid.reshape(-1)
    npairs = e_flat.shape[0]
    experts = jnp.arange(N_EXPERTS, dtype=jnp.int32)
    counts = jnp.sum((e_flat[:, None] == experts[None, :]).astype(jnp.int32), axis=0)
    padded = ((counts + tm - 1) // tm) * tm
    ends = jnp.cumsum(padded)
    offs = ends - padded
    n_rows = npairs + N_EXPERTS * tm
    nt = n_rows // tm
    fill = jnp.arange(n_rows - npairs, dtype=jnp.int32)
    fill_expert = jnp.sum((fill[:, None] >= jnp.cumsum(padded - counts)[None, :]).astype(jnp.int32), axis=1)
    keys = jnp.concatenate([e_flat * (2 * npairs) + jnp.arange(npairs, dtype=jnp.int32),
                            fill_expert * (2 * npairs) + npairs])
    skeys = jnp.sort(keys)
    low = skeys % (2 * npairs)
    dst = jnp.where(low < npairs, low, 0)
    tok = dst % n
    tile_start = jnp.arange(nt, dtype=jnp.int32) * tm
    texp = jnp.minimum(jnp.sum((tile_start[:, None] >= ends[None, :]).astype(jnp.int32), axis=1), N_EXPERTS - 1)
    cnt = jnp.clip(offs[texp] + counts[texp] - tile_start, 0, tm)
    nvalid = (ends[-1] // tm).astype(jnp.int32).reshape(1)
    return (nvalid, texp.astype(jnp.int32), cnt.astype(jnp.int32), tok.reshape(nt, 1, tm), dst.reshape(nt, 1, tm),
            npairs)


def _ple_final_kernel(x1_ref, y0_ref, y1_ref, w_ref, p_ref, gp_ref, wpg_ref, wp_ref, gf_ref, o_ref):
    w = w_ref[...]
    tm, d = x1_ref.shape
    x2 = x1_ref[...] + w[:, 0:1] * _load_rows(y0_ref, 0, tm, d) + w[:, 1:2] * _load_rows(y1_ref, 0, tm, d)
    hn = _rms(x2, gp_ref[...]).astype(BF16)
    gate = _sigmoid(jnp.dot(hn, wpg_ref[...], preferred_element_type=F32))
    pe = jnp.dot(p_ref[...].astype(BF16), wp_ref[...], preferred_element_type=F32)
    x3 = x2 + gate * pe
    o_ref[...] = _rms(x3, gf_ref[...])


def _ple_final(x1, ypairs, wcols, p2d, gp, wpg, wp, gf, tm=256):
    n, d = x1.shape
    nb = n // tm
    k = _row_pitch(d)
    return pl.pallas_call(
        _ple_final_kernel,
        out_shape=jax.ShapeDtypeStruct((n, d), F32),
        grid=(nb,),
        in_specs=[
            pl.BlockSpec((tm, d), lambda i: (i, 0)),
            pl.BlockSpec((tm * k, LANES), lambda i: (i, 0)),
            pl.BlockSpec((tm * k, LANES), lambda i: (nb + i, 0)),
            pl.BlockSpec((tm, wcols.shape[1]), lambda i: (i, 0)),
            pl.BlockSpec((tm, p2d.shape[1]), lambda i: (i, 0)),
            pl.BlockSpec((1, d), lambda i: (0, 0)),
            pl.BlockSpec((d, d), lambda i: (0, 0)),
            pl.BlockSpec(wp.shape, lambda i: (0, 0)),
            pl.BlockSpec((1, d), lambda i: (0, 0)),
        ],
        out_specs=pl.BlockSpec((tm, d), lambda i: (i, 0)),
        compiler_params=_cparams(("parallel",)),
        name="ple_final",
    )(x1, ypairs, ypairs, wcols, p2d, gp, wpg, wp, gf)


def _block_diag(w, per):
    nb, bw, _ = w.shape
    w = w.reshape(nb // per, per, bw, bw)
    eye = jnp.eye(per, dtype=w.dtype)
    return jnp.einsum("cpij,pq->cpiqj", w, eye).reshape(nb // per, per * bw, per * bw)


def _regroup_kernel(wt_ref, o_ref):
    d = Z_MGB - Z_MGA
    c = np.cumsum((0, NSA_Q_WIDTH) + (NSA_KV_WIDTH,) * 6 + (3 * NSA_HEADS, LRU_WIDTH, LRU_WIDTH, d, d))
    seg = lambda k: wt_ref[int(c[k]):int(c[k + 1]), :]
    q, k_c, v_c, k_s, v_s, k_w, v_w, gates, lru_x, lru_y, mg_a, mg_b = (seg(k) for k in range(12))

    def pair(a, b):
        parts = []
        for g in range(NSA_KV_GROUPS):
            parts += [a[g * HEAD_DIM:(g + 1) * HEAD_DIM], b[g * HEAD_DIM:(g + 1) * HEAD_DIM]]
        return jnp.concatenate(parts, axis=0)

    pad = jnp.zeros((Z_WIDTH - int(c[-1]), wt_ref.shape[1]), wt_ref.dtype)
    rows = [mg_a, mg_b, q, pair(k_s, k_w), pair(v_s, v_w), k_c, v_c, lru_x, lru_y, gates, pad]
    o_ref[...] = jnp.concatenate(rows, axis=0).T.astype(o_ref.dtype)


def _regroup_w_in(w_in_t, tr=256):
    nw, d = w_in_t.shape
    return pl.pallas_call(
        _regroup_kernel,
        out_shape=jax.ShapeDtypeStruct((d, Z_WIDTH), BF16),
        grid=(d // tr,),
        in_specs=[pl.BlockSpec((nw, tr), lambda i: (0, i))],
        out_specs=pl.BlockSpec((tr, Z_WIDTH), lambda i: (i, 0)),
        compiler_params=_cparams(("parallel",)),
        name="regroup_w_in",
    )(w_in_t)


def _layer(x2d, batch, seq, ln_mix, w_in, cmp_k_pos, cmp_k_w1, cmp_k_w2, cmp_v_pos, cmp_v_w1, cmp_v_w2, conv_w,
           conv_b, lru_wa, lru_ba, lru_wx, lru_bx, lru_lambda, w_nsa_up, w_lru_up, w_out, ln_ffn, w_grp, b_grp, w_exp,
           b_exp, w_gate, w_up, w_down):
    n, d = x2d.shape
    g, hd = NSA_KV_GROUPS, HEAD_DIM

    z = _in_proj(x2d, ln_mix.reshape(1, d), _regroup_w_in(w_in.T))

    pos8 = lambda pos: jnp.broadcast_to(pos.reshape(1, -1), (SUBLANES, pos.size)).astype(BF16)
    kc, vct = _compress(z, batch, seq, pos8(cmp_k_pos), cmp_k_w1.astype(BF16), cmp_k_w2.astype(BF16),
                        pos8(cmp_v_pos), cmp_v_w1.astype(BF16), cmp_v_w2.T.astype(BF16))
    o = _nsa_attention(z, kc, vct, batch, seq)

    per = LRU_WT // LRU_BW
    lru = _rglru(z, batch, seq, conv_w, conv_b.reshape(1, -1), _block_diag(lru_wa, per).astype(BF16),
                 lru_ba.reshape(1, -1), _block_diag(lru_wx, per).astype(BF16), lru_bx.reshape(1, -1),
                 lru_lambda.reshape(1, -1))

    merged = _merge(o, lru, z, w_nsa_up.astype(BF16), w_lru_up.astype(BF16))

    wr = jnp.zeros((d, LANES), F32).at[:, 0:N_GROUPS].set(w_grp).at[:, SUBLANES:SUBLANES + N_EXPERTS].set(w_exp)
    br = jnp.zeros((LANES, 1), F32).at[0:N_GROUPS, 0].set(b_grp).at[SUBLANES:SUBLANES + N_EXPERTS, 0].set(b_exp)
    x1, xn, eid, ew = _out_route(merged, x2d, w_out.astype(BF16), ln_ffn.reshape(1, d), wr, br)

    nvalid, texp, cnt, tok, dst, n_rows_out = _moe_plan(eid[0:EXPERT_TOPK], n, MOE_TM)
    ypairs = _moe(nvalid, texp, cnt, tok, dst, xn, w_gate, w_up, w_down, n_rows_out)

    return x1, ypairs, ew.T


def kernel(x, p, ln_mix, w_in, cmp_k_pos, cmp_k_w1, cmp_k_w2, cmp_v_pos, cmp_v_w1, cmp_v_w2, conv_w, conv_b, lru_wa, lru_ba, lru_wx, lru_bx, lru_lambda, w_nsa_up, w_lru_up, w_out, ln_ffn, w_grp, b_grp, w_exp, b_exp, w_gate, w_up, w_down, ln_ple, w_ple, w_ple_gate, ln_final):
    batch, seq, d = x.shape
    assert p.shape[0] == 1, "the final norm is fused into the (single) layer's last kernel"
    n = batch * seq
    x1, ypairs, wcols = _layer(
        x.reshape(n, d), batch, seq, ln_mix[0], w_in[0], cmp_k_pos[0], cmp_k_w1[0], cmp_k_w2[0],
        cmp_v_pos[0], cmp_v_w1[0], cmp_v_w2[0], conv_w[0], conv_b[0], lru_wa[0], lru_ba[0], lru_wx[0], lru_bx[0],
        lru_lambda[0], w_nsa_up[0], w_lru_up[0], w_out[0], ln_ffn[0], w_grp[0], b_grp[0], w_exp[0], b_exp[0],
        w_gate[0], w_up[0], w_down[0])
    out = _ple_final(x1, ypairs, wcols, p[0].reshape(n, -1), ln_ple[0].reshape(1, d),
                     w_ple_gate[0].astype(BF16), w_ple[0].astype(BF16), ln_final.reshape(1, d))
    return out.reshape(batch, seq, d)
```

```python
import numpy as np
import jax
import jax.numpy as jnp
from jax import lax
from jax.experimental import pallas as pl
from jax.experimental.pallas import tpu as pltpu

F32 = jnp.float32
BF16 = jnp.bfloat16

NSA_HEADS = 16
NSA_KV_GROUPS = 4
NSA_HPG = NSA_HEADS // NSA_KV_GROUPS
HEAD_DIM = 64
NSA_Q_WIDTH = NSA_HEADS * HEAD_DIM
NSA_KV_WIDTH = NSA_KV_GROUPS * HEAD_DIM
CMP_BLOCK = 32
CMP_STRIDE = 16
CMP_HIDDEN = 2 * HEAD_DIM
SLC_BLOCK = 64
SLC_TOPN = 16
WINDOW = 512
ATTN_SCALE = HEAD_DIM ** -0.5
NEG_INF = -1e30
LOG2E = float(np.log2(np.e))
LRU_WIDTH = 1024
LRU_BLOCKS = 16
LRU_BW = LRU_WIDTH // LRU_BLOCKS
CONV_WIDTH = 4
LRU_C = 8.0
N_GROUPS = 4
EXPERTS_PER_GROUP = 8
N_EXPERTS = N_GROUPS * EXPERTS_PER_GROUP
EXPERT_TOPK = 2
D_EXPERT = 512
EPS = 1e-6

LANES = 128
SUBLANES = 8
VMEM_LIMIT_BYTES = 56 * 1024 * 1024

Z_MGA = 0
Z_MGB = 2048
Z_Q = 4096
Z_KK = 5120
Z_VV = 5632
Z_KC = 6144
Z_VC = 6400
Z_LX = 6656
Z_LY = 7680
Z_GATE = 8704
Z_WIDTH = 9216
LRU_CB = 512
LRU_WT = 256
MOE_TM = 256
NSA_TQ = 256
NSA_CK = 512


def _cparams(sem, vmem=VMEM_LIMIT_BYTES):
    return pltpu.CompilerParams(dimension_semantics=sem, vmem_limit_bytes=vmem)


def _rms(x, g):
    return x * lax.rsqrt(jnp.mean(x * x, axis=-1, keepdims=True) + EPS) * g


def _gelu_tanh(x):
    return 0.5 * x * (1.0 + jnp.tanh(np.sqrt(2.0 / np.pi) * (x + 0.044715 * (x * x * x))))


def _sigmoid(x):
    return 1.0 / (1.0 + jnp.exp(-x))


def _row_pitch(d):
    k = d // LANES
    return k + 1 - (k % 2)


def _store_rows(ref, row0, val):
    rows, d = val.shape
    pitch = _row_pitch(d)
    for j in range(d // LANES):
        ref[pl.ds(row0 * pitch + j, rows, stride=pitch), :] = val[:, j * LANES:(j + 1) * LANES]
    for j in range(d // LANES, pitch):
        ref[pl.ds(row0 * pitch + j, rows, stride=pitch), :] = jnp.zeros((rows, LANES), val.dtype)


def _load_rows(ref, row0, rows, d):
    pitch = _row_pitch(d)
    return jnp.concatenate([ref[pl.ds(row0 * pitch + j, rows, stride=pitch), :] for j in range(d // LANES)], axis=1)


def _in_proj_kernel(x_ref, g_ref, w_ref, o_ref, h_ref):
    @pl.when(pl.program_id(1) == 0)
    def _():
        h_ref[...] = _rms(x_ref[...], g_ref[...]).astype(BF16)

    o_ref[...] = jnp.dot(h_ref[...], w_ref[...], preferred_element_type=F32).astype(o_ref.dtype)


def _in_proj(x2d, g, w, tm=1024, tn=2304):
    n, d = x2d.shape
    nw = w.shape[1]
    return pl.pallas_call(
        _in_proj_kernel,
        out_shape=jax.ShapeDtypeStruct((n, nw), BF16),
        grid=(n // tm, nw // tn),
        in_specs=[
            pl.BlockSpec((tm, d), lambda i, j: (i, 0)),
            pl.BlockSpec((1, d), lambda i, j: (0, 0)),
            pl.BlockSpec((d, tn), lambda i, j: (0, j)),
        ],
        out_specs=pl.BlockSpec((tm, tn), lambda i, j: (i, j)),
        scratch_shapes=[pltpu.VMEM((tm, d), BF16)],
        compiler_params=_cparams(("parallel", "arbitrary")),
        name="in_proj",
    )(x2d, g, w)


def _compress_kernel(zk_ref, zv_ref, kpos_ref, kw1_ref, kw2_ref, vpos_ref, vw1_ref, vw2t_ref, kc_ref, vct_ref,
                     xf_ref):
    seq = zk_ref.shape[0]
    ncp = seq // CMP_STRIDE
    hd = HEAD_DIM
    half = CMP_STRIDE * hd

    def half_windows(z_ref):
        xf_ref[...] = z_ref[...].astype(F32)
        lines = [xf_ref[pl.ds(r, ncp, stride=CMP_STRIDE), :].astype(BF16) for r in range(CMP_STRIDE)]
        return [jnp.concatenate([ln[:, gg * hd:(gg + 1) * hd] for ln in lines], axis=1) for gg in range(LANES // hd)]

    def hidden(x, pos_ref, w1_ref):
        w1 = w1_ref[...]
        ha = jnp.dot(x, w1[:half], preferred_element_type=F32)
        hb = jnp.dot(x, w1[half:], preferred_element_type=F32)
        hb = pltpu.roll(hb, hb.shape[0] - 1, axis=0)
        pc = jnp.dot(pos_ref[...], w1, preferred_element_type=F32)[0:1]
        return _gelu_tanh(ha + hb + pc).astype(BF16)

    for gg, x in enumerate(half_windows(zk_ref)):
        hk = hidden(x, kpos_ref, kw1_ref)
        kc_ref[0, gg] = jnp.dot(hk, kw2_ref[...], preferred_element_type=F32).astype(kc_ref.dtype)
    for gg, x in enumerate(half_windows(zv_ref)):
        hv = hidden(x, vpos_ref, vw1_ref)
        vct_ref[0, gg] = lax.dot_general(vw2t_ref[...], hv, (((1,), (1,)), ((), ())),
                                         preferred_element_type=F32).astype(vct_ref.dtype)


def _compress(z, batch, seq, kpos, kw1, kw2, vpos, vw1, vw2t):
    g, hd = NSA_KV_GROUPS, HEAD_DIM
    gpb = LANES // hd
    ncp = seq // CMP_STRIDE
    full = lambda a: pl.BlockSpec(a.shape, lambda i, j: (0,) * a.ndim)
    return pl.pallas_call(
        _compress_kernel,
        out_shape=(jax.ShapeDtypeStruct((batch, g, ncp, hd), BF16),
                   jax.ShapeDtypeStruct((batch, g, hd, ncp), BF16)),
        grid=(batch, g // gpb),
        in_specs=[
            pl.BlockSpec((seq, LANES), lambda i, j: (i, Z_KC // LANES + j)),
            pl.BlockSpec((seq, LANES), lambda i, j: (i, Z_VC // LANES + j)),
            full(kpos), full(kw1), full(kw2), full(vpos), full(vw1), full(vw2t),
        ],
        out_specs=(pl.BlockSpec((1, gpb, ncp, hd), lambda i, j: (i, j, 0, 0)),
                   pl.BlockSpec((1, gpb, hd, ncp), lambda i, j: (i, j, 0, 0))),
        scratch_shapes=[pltpu.VMEM((seq, LANES), F32)],
        compiler_params=_cparams(("parallel", "parallel")),
        name="compress",
    )(z, z, kpos, kw1, kw2, vpos, vw1, vw2t)


def _nsa_kernel(zq_ref, zkk_ref, zvv_ref, zg_ref, kc_ref, vct_ref, o_ref, kcomb_ref, vs_ref, vw_ref, gate_ref,
                sa_ref, sb_ref):
    grp = pl.program_id(1)
    qi = pl.program_id(2)
    tq, hd, hpg = NSA_TQ, HEAD_DIM, NSA_HPG
    nq = tq * hpg
    seq = kcomb_ref.shape[0]
    ncp = kc_ref.shape[2]
    nsb = seq // SLC_BLOCK
    blk_shift = SLC_BLOCK.bit_length() - 1

    @pl.when(qi == 0)
    def _():
        kcomb_ref[:, 0:2 * hd] = zkk_ref[...]
        blk_of_row = lax.broadcasted_iota(jnp.int32, (seq, 2 * hd), 0) >> blk_shift
        lane = lax.broadcasted_iota(jnp.int32, (seq, 2 * hd), 1)
        kcomb_ref[:, 2 * hd:] = jnp.where(blk_of_row == lane, 1.0, 0.0).astype(BF16)
        ones = jnp.ones((seq, hd), BF16)
        vs_ref[...] = jnp.concatenate([zvv_ref[:, 0:hd], ones], axis=1)
        vw_ref[...] = jnp.concatenate([zvv_ref[:, hd:2 * hd], ones], axis=1)

    qt = zq_ref[...].T
    q = jnp.concatenate([qt[h * hd:(h + 1) * hd, :] for h in range(hpg)], axis=1)
    q = (q.astype(F32) * (ATTN_SCALE * LOG2E)).astype(BF16)
    zero = jnp.zeros((hd, nq), BF16)
    t_lane = qi * tq + (lax.broadcasted_iota(jnp.int32, (1, nq), 1) & (tq - 1))
    tn_dims = (((0,), (0,)), ((), ()))
    ck = NSA_CK

    sa_ref[...] = jnp.dot(kcomb_ref[0:ck, 0:2 * hd], jnp.concatenate([q, zero], axis=0), preferred_element_type=F32)

    sc = jnp.dot(kc_ref[0, 0], q, preferred_element_type=F32)
    cmp_end = lax.broadcasted_iota(jnp.int32, (ncp, 1), 0) * CMP_STRIDE + (CMP_BLOCK - 1)
    cmask = cmp_end <= t_lane
    sc = jnp.where(cmask, sc, NEG_INF)
    pc = jnp.where(cmask, jnp.exp2(sc - jnp.max(sc, axis=0, keepdims=True)), 0.0)
    lc = jnp.sum(pc, axis=0, keepdims=True)
    pc = pc * (1.0 / jnp.where(lc > 0.0, lc, 1.0))
    o_cmp = jnp.dot(vct_ref[0, 0], pc.astype(BF16), preferred_element_type=F32)

    psum = pc[:, 0:tq]
    for h in range(1, hpg):
        psum = psum + pc[:, h * tq:(h + 1) * tq]
    jrow = lax.broadcasted_iota(jnp.int32, (nsb, ncp), 0)
    crel = lax.broadcasted_iota(jnp.int32, (nsb, ncp), 1) - jrow * (SLC_BLOCK // CMP_STRIDE)
    ovt = jnp.where((crel >= 0) & (crel <= 2), 1.0, jnp.where((crel == -1) | (crel == 3), 0.5, 0.0)).astype(F32)
    imp = jnp.dot(ovt, psum, preferred_element_type=F32, precision=lax.Precision.HIGHEST)
    jblk = lax.broadcasted_iota(jnp.int32, (nsb, tq), 0)
    blk = (qi * tq + lax.broadcasted_iota(jnp.int32, (1, tq), 1)) >> blk_shift
    forced = (jblk == 0) | (jblk == blk) | (jblk == blk - 1)
    n_forced = 3
    score = jnp.where(forced, -2.0, jnp.where(jblk <= blk, imp, -1.0))
    selb = jnp.where(forced, 0.0, NEG_INF)
    for _ in range(min(SLC_TOPN, nsb) - n_forced):
        top = jnp.max(score, axis=0, keepdims=True)
        first = jnp.min(jnp.where(score == top, jblk, nsb), axis=0, keepdims=True)
        pick = jblk == first
        selb = jnp.where(pick, 0.0, selb)
        score = jnp.where(pick, -2.0, score)
    bias0 = jnp.concatenate([jnp.broadcast_to(selb[j:j + 1, :], (SLC_BLOCK, tq)) for j in range(ck // SLC_BLOCK)],
                            axis=0)
    sa_ref[...] = sa_ref[...] + jnp.concatenate([bias0] * hpg, axis=1)
    selb = selb.astype(BF16)

    pad = jnp.zeros((2 * hd - nsb, nq), BF16)
    qa = jnp.concatenate([q, zero, jnp.concatenate([selb] * hpg, axis=1), pad], axis=0)

    def slc_scores(c):
        return jnp.dot(kcomb_ref[pl.ds(pl.multiple_of(c * ck, ck), ck), :], qa, preferred_element_type=F32)

    n_full = (qi * tq) // ck

    def slc_update(buf, c, carry, causal):
        def scores():
            if not causal:
                return buf[...]
            return jnp.where(c * ck + lax.broadcasted_iota(jnp.int32, (ck, 1), 0) <= t_lane, buf[...], NEG_INF)

        m, l, acc = carry
        m_new = jnp.maximum(m, jnp.max(scores(), axis=0, keepdims=True))
        alpha = jnp.exp2(m - m_new)
        p = jnp.exp2(scores() - m_new).astype(BF16)
        v = vs_ref[pl.ds(pl.multiple_of(c * ck, ck), ck), :]
        pv = lax.dot_general(v, p, tn_dims, preferred_element_type=F32)
        return m_new, alpha * l + pv[hd:hd + 1], acc * alpha + pv[0:hd]

    def slc_pair(i, carry):
        sb_ref[...] = slc_scores(2 * i + 1)
        carry = slc_update(sa_ref, 2 * i, carry, False)
        sa_ref[...] = slc_scores(2 * i + 2)
        return slc_update(sb_ref, 2 * i + 1, carry, False)

    def tail_two(carry):
        sb_ref[...] = slc_scores(n_full)
        return slc_update(sb_ref, n_full, slc_update(sa_ref, n_full - 1, carry, False), True)

    def tail_one(carry):
        return slc_update(sa_ref, n_full, carry, True)


    nwc = WINDOW // tq + 1
    t_lo = t_lane - WINDOW
    ks_w, vs_w = [], []
    for i in range(nwc):
        rows = pl.ds(pl.multiple_of(jnp.maximum(qi - (nwc - 1) + i, 0) * tq, tq), tq)
        ks_w.append(kcomb_ref[rows, 0:2 * hd])
        vs_w.append(vw_ref[rows, :])
    qw = jnp.concatenate([zero, q], axis=0)
    sw = jnp.dot(jnp.concatenate(ks_w, axis=0), qw, preferred_element_type=F32)
    sw_parts = []
    for i in range(nwc):
        spos = (qi - (nwc - 1) + i) * tq + lax.broadcasted_iota(jnp.int32, (tq, 1), 0)
        if i < nwc - 1:
            ok = jnp.where(spos >= 0, spos, -(1 << 24)) > t_lo
        else:
            ok = spos <= t_lane
        sw_parts.append(jnp.where(ok, sw[i * tq:(i + 1) * tq], NEG_INF))
    sw = jnp.concatenate(sw_parts, axis=0)
    pw = jnp.exp2(sw - jnp.max(sw, axis=0, keepdims=True))
    ow = lax.dot_general(jnp.concatenate(vs_w, axis=0), pw.astype(BF16), tn_dims, preferred_element_type=F32)
    o_win = ow[0:hd] * (1.0 / ow[hd:hd + 1])
    init = (jnp.full((1, nq), NEG_INF, F32), jnp.zeros((1, nq), F32), jnp.zeros((hd, nq), F32))
    carry = lax.fori_loop(0, n_full // 2, slc_pair, init)
    _, l_s, acc_s = lax.cond(n_full % 2 == 1, tail_two, tail_one, carry)
    o_slc = acc_s * (1.0 / l_s)

    gate_ref[...] = _sigmoid(zg_ref[...].astype(F32)).T

    def branch_gate(br):
        rows = [gate_ref[pl.ds(grp * (hpg * 3) + h * 3 + br, 1), :] for h in range(hpg)]
        return jnp.concatenate(rows, axis=1)

    o = branch_gate(0) * o_cmp + branch_gate(1) * o_slc + branch_gate(2) * o_win
    o_heads = jnp.concatenate([o[:, h * tq:(h + 1) * tq] for h in range(hpg)], axis=0)
    o_ref[...] = o_heads.T.astype(o_ref.dtype)


def _nsa_attention(z, kc, vct, batch, seq):
    n = z.shape[0]
    g, hd, tq = NSA_KV_GROUPS, HEAD_DIM, NSA_TQ
    nqt = seq // tq
    ncp = kc.shape[2]
    nsb = seq // SLC_BLOCK
    gw = NSA_HPG * hd
    assert seq % NSA_CK == 0 and NSA_CK % tq == 0 and WINDOW % tq == 0 and nsb <= 2 * hd
    tile = lambda i, j, k: i * nqt + k
    return pl.pallas_call(
        _nsa_kernel,
        out_shape=jax.ShapeDtypeStruct((n, NSA_Q_WIDTH), BF16),
        grid=(batch, g, nqt),
        in_specs=[
            pl.BlockSpec((tq, gw), lambda i, j, k: (tile(i, j, k), Z_Q // gw + j)),
            pl.BlockSpec((seq, 2 * hd), lambda i, j, k: (i, Z_KK // (2 * hd) + j)),
            pl.BlockSpec((seq, 2 * hd), lambda i, j, k: (i, Z_VV // (2 * hd) + j)),
            pl.BlockSpec((tq, LANES), lambda i, j, k: (tile(i, j, k), Z_GATE // LANES)),
            pl.BlockSpec((1, 1, ncp, hd), lambda i, j, k: (i, j, 0, 0)),
            pl.BlockSpec((1, 1, hd, ncp), lambda i, j, k: (i, j, 0, 0)),
        ],
        out_specs=pl.BlockSpec((tq, gw), lambda i, j, k: (tile(i, j, k), j)),
        scratch_shapes=[pltpu.VMEM((seq, 4 * hd), BF16), pltpu.VMEM((seq, 2 * hd), BF16),
                        pltpu.VMEM((seq, 2 * hd), BF16), pltpu.VMEM((LANES, tq), F32), pltpu.VMEM((NSA_CK, NSA_HPG * tq), F32),
                        pltpu.VMEM((NSA_CK, NSA_HPG * tq), F32)],
        compiler_params=_cparams(("parallel", "parallel", "arbitrary")),
        name="nsa_attn",
    )(z, z, z, z, kc, vct)


def _rglru_kernel(x_ref, y_ref, cw_ref, cb_ref, wa_ref, ba_ref, wx_ref, bx_ref, lam_ref, o_ref,
                  tail_ref, h_ref, a_ref, u_ref):
    tc = pl.program_id(2)
    tt, cb = x_ref.shape

    @pl.when(tc == 0)
    def _():
        tail_ref[...] = jnp.zeros_like(tail_ref)
        h_ref[...] = jnp.zeros_like(h_ref)

    x = x_ref[...].astype(F32)
    xe = jnp.concatenate([tail_ref[...], x], axis=0)
    tail_ref[...] = x[tt - SUBLANES:, :]
    cw = cw_ref[...]
    xc = cb_ref[...]
    for k in range(CONV_WIDTH):
        off = SUBLANES - (CONV_WIDTH - 1) + k
        xc = xc + cw[k:k + 1, :] * xe[off:off + tt, :]
    xcb = xc.astype(BF16)
    def gate(w_ref, b_ref):
        parts = [jnp.dot(xcb[:, c * LRU_WT:(c + 1) * LRU_WT], w_ref[c], preferred_element_type=F32)
                 for c in range(cb // LRU_WT)]
        return _sigmoid(jnp.concatenate(parts, axis=1) + b_ref[...])

    r = gate(wa_ref, ba_ref)
    ig = gate(wx_ref, bx_ref)
    nl = -lam_ref[...]
    softplus = jnp.maximum(nl, 0.0) + jnp.log1p(jnp.exp(-jnp.abs(nl)))
    log_a = (-LRU_C) * softplus * r
    a_ref[...] = jnp.exp(log_a)
    th = jnp.tanh(log_a)
    u_ref[...] = jnp.sqrt(-2.0 * th / (1.0 - th)) * (ig * xc)

    row = lax.broadcasted_iota(jnp.int32, (SUBLANES, cb), 0)

    def step(i, h):
        sl = pl.ds(pl.multiple_of(i * SUBLANES, SUBLANES), SUBLANES)
        a = a_ref[sl, :]
        u = u_ref[sl, :]
        for s in (1, 2, 4):
            a_s = jnp.where(row >= s, pltpu.roll(a, s, axis=0), 1.0)
            u_s = jnp.where(row >= s, pltpu.roll(u, s, axis=0), 0.0)
            u = a * u_s + u
            a = a * a_s
        hrows = a * h + u
        u_ref[sl, :] = hrows
        return hrows[SUBLANES - 1:SUBLANES, :]

    h_ref[...] = lax.fori_loop(0, tt // SUBLANES, step, h_ref[...], unroll=8)
    o_ref[...] = (u_ref[...] * _gelu_tanh(y_ref[...].astype(F32))).astype(o_ref.dtype)


def _rglru(z, batch, seq, cw, cbias, wa_bd, ba, wx_bd, bx, lam, tt=1024):
    n = z.shape[0]
    ncb = LRU_WIDTH // LRU_CB
    nt = seq // tt
    row = lambda i, j, k: i * nt + k
    vec = lambda r: pl.BlockSpec((r, LRU_CB), lambda i, j, k: (0, j))
    return pl.pallas_call(
        _rglru_kernel,
        out_shape=jax.ShapeDtypeStruct((n, LRU_WIDTH), BF16),
        grid=(batch, ncb, nt),
        in_specs=[
            pl.BlockSpec((tt, LRU_CB), lambda i, j, k: (row(i, j, k), Z_LX // LRU_CB + j)),
            pl.BlockSpec((tt, LRU_CB), lambda i, j, k: (row(i, j, k), Z_LY // LRU_CB + j)),
            vec(CONV_WIDTH), vec(1),
            pl.BlockSpec((LRU_CB // LRU_WT, LRU_WT, LRU_WT), lambda i, j, k: (j, 0, 0)), vec(1),
            pl.BlockSpec((LRU_CB // LRU_WT, LRU_WT, LRU_WT), lambda i, j, k: (j, 0, 0)), vec(1),
            vec(1),
        ],
        out_specs=pl.BlockSpec((tt, LRU_CB), lambda i, j, k: (row(i, j, k), j)),
        scratch_shapes=[pltpu.VMEM((SUBLANES, LRU_CB), F32), pltpu.VMEM((1, LRU_CB), F32),
                        pltpu.VMEM((tt, LRU_CB), F32), pltpu.VMEM((tt, LRU_CB), F32)],
        compiler_params=_cparams(("parallel", "parallel", "arbitrary")),
        name="rglru",
    )(z, z, cw, cbias, wa_bd, ba, wx_bd, bx, lam)


def _merge_kernel(o_ref, l_ref, mga_ref, mgb_ref, wn_ref, wl_ref, m_ref):
    ya = jnp.dot(o_ref[...], wn_ref[...], preferred_element_type=F32)
    yb = jnp.dot(l_ref[...], wl_ref[...], preferred_element_type=F32)
    m = _sigmoid(mga_ref[...].astype(F32)) * ya + _sigmoid(mgb_ref[...].astype(F32)) * yb
    m_ref[...] = m.astype(m_ref.dtype)


def _merge(o, lru, z, wn, wl, tm=512):
    n, d = o.shape[0], wn.shape[1]
    return pl.pallas_call(
        _merge_kernel,
        out_shape=jax.ShapeDtypeStruct((n, d), BF16),
        grid=(n // tm,),
        in_specs=[
            pl.BlockSpec((tm, o.shape[1]), lambda i: (i, 0)),
            pl.BlockSpec((tm, lru.shape[1]), lambda i: (i, 0)),
            pl.BlockSpec((tm, d), lambda i: (i, Z_MGA // d)),
            pl.BlockSpec((tm, d), lambda i: (i, Z_MGB // d)),
            pl.BlockSpec(wn.shape, lambda i: (0, 0)),
            pl.BlockSpec(wl.shape, lambda i: (0, 0)),
        ],
        out_specs=pl.BlockSpec((tm, d), lambda i: (i, 0)),
        compiler_params=_cparams(("parallel",)),
        name="merge",
    )(o, lru, z, z, wn, wl)


def _out_route_kernel(m_ref, x_ref, wo_ref, g_ref, wr_ref, br_ref, x1_ref, xn_ref, eid_ref, ew_ref):
    x1 = x_ref[...] + jnp.dot(m_ref[...], wo_ref[...], preferred_element_type=F32)
    x1_ref[...] = x1
    xn = _rms(x1, g_ref[...])
    _store_rows(xn_ref, 0, xn)
    lg = jnp.dot(xn, wr_ref[...], preferred_element_type=F32).T + br_ref[...]
    tm = lg.shape[1]
    sub = lax.broadcasted_iota(jnp.int32, (SUBLANES, tm), 0)

    def first_argmax(v, vmax):
        return jnp.min(jnp.where(v == vmax, sub, SUBLANES), axis=0, keepdims=True)

    gl = jnp.where(sub < N_GROUPS, lg[0:SUBLANES], -jnp.inf)
    gmax = jnp.max(gl, axis=0, keepdims=True)
    ge = jnp.exp(gl - gmax)
    gprob = ge / jnp.sum(ge, axis=0, keepdims=True)
    g_val = jnp.max(gprob, axis=0, keepdims=True)
    g_idx = first_argmax(gprob, g_val)
    e_in = jnp.zeros((EXPERTS_PER_GROUP, tm), F32)
    for gi in range(N_GROUPS):
        lo = SUBLANES + gi * EXPERTS_PER_GROUP
        e_in = jnp.where(g_idx == gi, lg[lo:lo + EXPERTS_PER_GROUP], e_in)
    ee = jnp.exp(e_in - jnp.max(e_in, axis=0, keepdims=True))
    eprob = ee / jnp.sum(ee, axis=0, keepdims=True)
    v1 = jnp.max(eprob, axis=0, keepdims=True)
    i1 = first_argmax(eprob, v1)
    rest = jnp.where(sub == i1, -1.0, eprob)
    v2 = jnp.max(rest, axis=0, keepdims=True)
    i2 = first_argmax(rest, v2)
    den = v1 + v2
    eid = jnp.where(sub == 0, g_idx * EXPERTS_PER_GROUP + i1, g_idx * EXPERTS_PER_GROUP + i2)
    eid_ref[...] = eid
    ew_ref[...] = jnp.where(sub == 0, g_val * v1 / den, g_val * v2 / den)


def _out_route(m, x2d, wo, g, wr, br, tm=512):
    n, d = x2d.shape
    once = pl.Buffered(1)
    return pl.pallas_call(
        _out_route_kernel,
        out_shape=(jax.ShapeDtypeStruct((n, d), F32), jax.ShapeDtypeStruct((n * _row_pitch(d), LANES), F32),
                   jax.ShapeDtypeStruct((SUBLANES, n), jnp.int32), jax.ShapeDtypeStruct((SUBLANES, n), F32)),
        grid=(n // tm,),
        in_specs=[
            pl.BlockSpec((tm, d), lambda i: (i, 0)),
            pl.BlockSpec((tm, d), lambda i: (i, 0)),
            pl.BlockSpec((d, d), lambda i: (0, 0), pipeline_mode=once),
            pl.BlockSpec((1, d), lambda i: (0, 0)),
            pl.BlockSpec(wr.shape, lambda i: (0, 0), pipeline_mode=once),
            pl.BlockSpec(br.shape, lambda i: (0, 0)),
        ],
        out_specs=(pl.BlockSpec((tm, d), lambda i: (i, 0)), pl.BlockSpec((tm * _row_pitch(d), LANES), lambda i: (i, 0)),
                   pl.BlockSpec((SUBLANES, tm), lambda i: (0, i)), pl.BlockSpec((SUBLANES, tm), lambda i: (0, i))),
        compiler_params=_cparams(("parallel",)),
        name="out_route",
    )(m, x2d, wo, g, wr, br)


def _for_rows(cnt, fn, unroll=8):
    sh = unroll.bit_length() - 1

    def group(gidx, c):
        for u in range(unroll):
            fn(gidx * unroll + u)
        return c

    def single(r, c):
        fn(r)
        return c

    lax.fori_loop(0, cnt >> sh, group, 0)
    lax.fori_loop((cnt >> sh) << sh, cnt, single, 0)


def _moe_kernel(nvalid_ref, texp_ref, cnt_ref, tok_ref, tokn_ref, dst_ref, xn_hbm, wg_ref, wu_ref, wd_ref, y_hbm,
                xbuf, ybuf, gsem, ssem):
    i = pl.program_id(0)
    nv = nvalid_ref[0]
    slot = i % 2
    d = wg_ref.shape[1]
    k = _row_pitch(d)
    tm = xbuf.shape[0] // (2 * k)

    def lines(ref, first):
        return ref.at[pl.ds(first, k), :]

    def gather_row(idx_ref, s):
        def fn(r):
            pltpu.make_async_copy(lines(xn_hbm, idx_ref[0, 0, r]), lines(xbuf, (s * tm + r) * k), gsem.at[s]).start()
        return fn

    def gather_wait_row(s):
        def fn(r):
            pltpu.make_async_copy(lines(xn_hbm, 0), lines(xbuf, (s * tm + r) * k), gsem.at[s]).wait()
        return fn

    def scatter_row(s):
        def fn(r):
            pltpu.make_async_copy(lines(ybuf, (s * tm + r) * k), lines(y_hbm, dst_ref[0, 0, r]), ssem.at[s]).start()
        return fn

    def scatter_wait_row(s):
        def fn(r):
            pltpu.make_async_copy(lines(ybuf, (s * tm + r) * k), lines(y_hbm, 0), ssem.at[s]).wait()
        return fn

    @pl.when(i == 0)
    def _():
        xbuf[...] = jnp.zeros_like(xbuf)
        _for_rows(cnt_ref[0], gather_row(tok_ref, 0))

    @pl.when(i + 1 < nv)
    def _():
        _for_rows(cnt_ref[i + 1], gather_row(tokn_ref, 1 - slot))

    @pl.when(i < nv)
    def _():
        _for_rows(cnt_ref[i], gather_wait_row(slot))
        x = _load_rows(xbuf, slot * tm, tm, d).astype(BF16)
        gp = jnp.dot(x, wg_ref[0], preferred_element_type=F32)
        up = jnp.dot(x, wu_ref[0], preferred_element_type=F32)
        hid = (gp * _sigmoid(gp) * up).astype(BF16)
        y = jnp.dot(hid, wd_ref[0], preferred_element_type=F32)

        @pl.when(i >= 2)
        def _():
            _for_rows(cnt_ref[i - 2], scatter_wait_row(slot))

        _store_rows(ybuf, slot * tm, y)
        _for_rows(cnt_ref[i], scatter_row(slot))

    @pl.when(i == nv - 1)
    def _():
        _for_rows(cnt_ref[i], scatter_wait_row(slot))

        @pl.when(nv >= 2)
        def _():
            _for_rows(cnt_ref[i - 1], scatter_wait_row(1 - slot))


def _moe(nvalid, texp, cnt, tok, dst, xn, wg, wu, wd, n_rows_out):
    nt, _, tm = tok.shape
    d, de = wg.shape[1], wg.shape[2]
    k = _row_pitch(d)
    grid_spec = pltpu.PrefetchScalarGridSpec(
        num_scalar_prefetch=3,
        grid=(nt,),
        in_specs=[
            pl.BlockSpec((1, 1, tm), lambda i, nv, te, ct: (i, 0, 0), memory_space=pltpu.SMEM),
            pl.BlockSpec((1, 1, tm), lambda i, nv, te, ct: (jnp.minimum(i + 1, nt - 1), 0, 0),
                         memory_space=pltpu.SMEM),
            pl.BlockSpec((1, 1, tm), lambda i, nv, te, ct: (i, 0, 0), memory_space=pltpu.SMEM),
            pl.BlockSpec(memory_space=pl.ANY),
            pl.BlockSpec((1, d, de), lambda i, nv, te, ct: (te[i], 0, 0)),
            pl.BlockSpec((1, d, de), lambda i, nv, te, ct: (te[i], 0, 0)),
            pl.BlockSpec((1, de, d), lambda i, nv, te, ct: (te[i], 0, 0)),
        ],
        out_specs=pl.BlockSpec(memory_space=pl.ANY),
        scratch_shapes=[pltpu.VMEM((2 * tm * k, LANES), F32), pltpu.VMEM((2 * tm * k, LANES), F32),
                        pltpu.SemaphoreType.DMA((2,)), pltpu.SemaphoreType.DMA((2,))],
    )
    return pl.pallas_call(
        _moe_kernel,
        out_shape=jax.ShapeDtypeStruct((n_rows_out * k, LANES), F32),
        grid_spec=grid_spec,
        compiler_params=_cparams(("arbitrary",)),
        name="moe",
    )(nvalid, texp, cnt, tok, tok, dst, xn, wg, wu, wd)


def _moe_plan(eid, n, tm):
    e_flat = eid.reshape(-1)
    npairs = e_flat.shape[0]
    experts = jnp.arange(N_EXPERTS, dtype=jnp.int32)
    counts = jnp.sum((e_flat[:, None] == experts[None, :]).astype(jnp.int32), axis=0)
    padded = ((counts + tm - 1) // tm) * tm
    ends = jnp.cumsum(padded)
    offs = ends - padded
    n_rows = npairs + N_EXPERTS * tm
    nt = n_rows // tm
    fill = jnp.arange(n_rows - npairs, dtype=jnp.int32)
    fill_expert = jnp.sum((fill[:, None] >= jnp.cumsum(padded - counts)[None, :]).astype(jnp.int32), axis=1)
    keys = jnp.concatenate([e_flat * (2 * npairs) + jnp.arange(npairs, dtype=jnp.int32),
                            fill_expert * (2 * npairs) + npairs])
    skeys = jnp.sort(keys)
    low = skeys % (2 * npairs)
    dst = jnp.where(low < npairs, low, 0)
    tok = dst % n
    tile_start = jnp.arange(nt, dtype=jnp.int32) * tm
    texp = jnp.minimum(jnp.sum((tile_start[:, None] >= ends[None, :]).astype(jnp.int32), axis=1), N_EXPERTS - 1)
    cnt = jnp.clip(offs[texp] + counts[texp] - tile_start, 0, tm)
    nvalid = (ends[-1] // tm).astype(jnp.int32).reshape(1)
    return (nvalid, texp.astype(jnp.int32), cnt.astype(jnp.int32), tok.reshape(nt, 1, tm), dst.reshape(nt, 1, tm),
            npairs)


def _ple_final_kernel(x1_ref, y0_ref, y1_ref, w_ref, p_ref, gp_ref, wpg_ref, wp_ref, gf_ref, o_ref):
    w = w_ref[...]
    tm, d = x1_ref.shape
    x2 = x1_ref[...] + w[:, 0:1] * _load_rows(y0_ref, 0, tm, d) + w[:, 1:2] * _load_rows(y1_ref, 0, tm, d)
    hn = _rms(x2, gp_ref[...]).astype(BF16)
    gate = _sigmoid(jnp.dot(hn, wpg_ref[...], preferred_element_type=F32))
    pe = jnp.dot(p_ref[...].astype(BF16), wp_ref[...], preferred_element_type=F32)
    x3 = x2 + gate * pe
    o_ref[...] = _rms(x3, gf_ref[...])


def _ple_final(x1, ypairs, wcols, p2d, gp, wpg, wp, gf, tm=256):
    n, d = x1.shape
    nb = n // tm
    k = _row_pitch(d)
    return pl.pallas_call(
        _ple_final_kernel,
        out_shape=jax.ShapeDtypeStruct((n, d), F32),
        grid=(nb,),
        in_specs=[
            pl.BlockSpec((tm, d), lambda i: (i, 0)),
            pl.BlockSpec((tm * k, LANES), lambda i: (i, 0)),
            pl.BlockSpec((tm * k, LANES), lambda i: (nb + i, 0)),
            pl.BlockSpec((tm, wcols.shape[1]), lambda i: (i, 0)),
            pl.BlockSpec((tm, p2d.shape[1]), lambda i: (i, 0)),
            pl.BlockSpec((1, d), lambda i: (0, 0)),
            pl.BlockSpec((d, d), lambda i: (0, 0)),
            pl.BlockSpec(wp.shape, lambda i: (0, 0)),
            pl.BlockSpec((1, d), lambda i: (0, 0)),
        ],
        out_specs=pl.BlockSpec((tm, d), lambda i: (i, 0)),
        compiler_params=_cparams(("parallel",)),
        name="ple_final",
    )(x1, ypairs, ypairs, wcols, p2d, gp, wpg, wp, gf)


def _block_diag(w, per):
    nb, bw, _ = w.shape
    w = w.reshape(nb // per, per, bw, bw)
    eye = jnp.eye(per, dtype=w.dtype)
    return jnp.einsum("cpij,pq->cpiqj", w, eye).reshape(nb // per, per * bw, per * bw)


def _regroup_kernel(wt_ref, o_ref):
    d = Z_MGB - Z_MGA
    c = np.cumsum((0, NSA_Q_WIDTH) + (NSA_KV_WIDTH,) * 6 + (3 * NSA_HEADS, LRU_WIDTH, LRU_WIDTH, d, d))
    seg = lambda k: wt_ref[int(c[k]):int(c[k + 1]), :]
    q, k_c, v_c, k_s, v_s, k_w, v_w, gates, lru_x, lru_y, mg_a, mg_b = (seg(k) for k in range(12))

    def pair(a, b):
        parts = []
        for g in range(NSA_KV_GROUPS):
            parts += [a[g * HEAD_DIM:(g + 1) * HEAD_DIM], b[g * HEAD_DIM:(g + 1) * HEAD_DIM]]
        return jnp.concatenate(parts, axis=0)

    pad = jnp.zeros((Z_WIDTH - int(c[-1]), wt_ref.shape[1]), wt_ref.dtype)
    rows = [mg_a, mg_b, q, pair(k_s, k_w), pair(v_s, v_w), k_c, v_c, lru_x, lru_y, gates, pad]
    o_ref[...] = jnp.concatenate(rows, axis=0).T.astype(o_ref.dtype)


def _regroup_w_in(w_in_t, tr=256):
    nw, d = w_in_t.shape
    return pl.pallas_call(
        _regroup_kernel,
        out_shape=jax.ShapeDtypeStruct((d, Z_WIDTH), BF16),
        grid=(d // tr,),
        in_specs=[pl.BlockSpec((nw, tr), lambda i: (0, i))],
        out_specs=pl.BlockSpec((tr, Z_WIDTH), lambda i: (i, 0)),
        compiler_params=_cparams(("parallel",)),
        name="regroup_w_in",
    )(w_in_t)


def _layer(x2d, batch, seq, ln_mix, w_in, cmp_k_pos, cmp_k_w1, cmp_k_w2, cmp_v_pos, cmp_v_w1, cmp_v_w2, conv_w,
           conv_b, lru_wa, lru_ba, lru_wx, lru_bx, lru_lambda, w_nsa_up, w_lru_up, w_out, ln_ffn, w_grp, b_grp, w_exp,
           b_exp, w_gate, w_up, w_down):
    n, d = x2d.shape
    g, hd = NSA_KV_GROUPS, HEAD_DIM

    z = _in_proj(x2d, ln_mix.reshape(1, d), _regroup_w_in(w_in.T))

    pos8 = lambda pos: jnp.broadcast_to(pos.reshape(1, -1), (SUBLANES, pos.size)).astype(BF16)
    kc, vct = _compress(z, batch, seq, pos8(cmp_k_pos), cmp_k_w1.astype(BF16), cmp_k_w2.astype(BF16),
                        pos8(cmp_v_pos), cmp_v_w1.astype(BF16), cmp_v_w2.T.astype(BF16))
    o = _nsa_attention(z, kc, vct, batch, seq)

    per = LRU_WT // LRU_BW
    lru = _rglru(z, batch, seq, conv_w, conv_b.reshape(1, -1), _block_diag(lru_wa, per).astype(BF16),
                 lru_ba.reshape(1, -1), _block_diag(lru_wx, per).astype(BF16), lru_bx.reshape(1, -1),
                 lru_lambda.reshape(1, -1))

    merged = _merge(o, lru, z, w_nsa_up.astype(BF16), w_lru_up.astype(BF16))

    wr = jnp.zeros((d, LANES), F32).at[:, 0:N_GROUPS].set(w_grp).at[:, SUBLANES:SUBLANES + N_EXPERTS].set(w_exp)
    br = jnp.zeros((LANES, 1), F32).at[0:N_GROUPS, 0].set(b_grp).at[SUBLANES:SUBLANES + N_EXPERTS, 0].set(b_exp)
    x1, xn, eid, ew = _out_route(merged, x2d, w_out.astype(BF16), ln_ffn.reshape(1, d), wr, br)

    nvalid, texp, cnt, tok, dst, n_rows_out = _moe_plan(eid[0:EXPERT_TOPK], n, MOE_TM)
    pitch = _row_pitch(d)
    ypairs = _moe(nvalid, texp, cnt, tok * pitch, dst * pitch, xn, w_gate, w_up, w_down, n_rows_out)

    return x1, ypairs, ew.T


def kernel(x, p, ln_mix, w_in, cmp_k_pos, cmp_k_w1, cmp_k_w2, cmp_v_pos, cmp_v_w1, cmp_v_w2, conv_w, conv_b, lru_wa, lru_ba, lru_wx, lru_bx, lru_lambda, w_nsa_up, w_lru_up, w_out, ln_ffn, w_grp, b_grp, w_exp, b_exp, w_gate, w_up, w_down, ln_ple, w_ple, w_ple_gate, ln_final):
    batch, seq, d = x.shape
    assert p.shape[0] == 1, "the final norm is fused into the (single) layer's last kernel"
    n = batch * seq
    x1, ypairs, wcols = _layer(
        x.reshape(n, d), batch, seq, ln_mix[0], w_in[0], cmp_k_pos[0], cmp_k_w1[0], cmp_k_w2[0],
        cmp_v_pos[0], cmp_v_w1[0], cmp_v_w2[0], conv_w[0], conv_b[0], lru_wa[0], lru_ba[0], lru_wx[0], lru_bx[0],
        lru_lambda[0], w_nsa_up[0], w_lru_up[0], w_out[0], ln_ffn[0], w_grp[0], b_grp[0], w_exp[0], b_exp[0],
        w_gate[0], w_up[0], w_down[0])
    out = _ple_final(x1, ypairs, wcols, p[0].reshape(n, -1), ln_ple[0].reshape(1, d),
                     w_ple_gate[0].astype(BF16), w_ple[0].astype(BF16), ln_final.reshape(1, d))
    return out.reshape(batch, seq, d)
```

```python
import numpy as np
import jax
import jax.numpy as jnp
from jax import lax
from jax.experimental import pallas as pl
from jax.experimental.pallas import tpu as pltpu

F32 = jnp.float32
BF16 = jnp.bfloat16

NSA_HEADS = 16
NSA_KV_GROUPS = 4
NSA_HPG = NSA_HEADS // NSA_KV_GROUPS
HEAD_DIM = 64
NSA_Q_WIDTH = NSA_HEADS * HEAD_DIM
NSA_KV_WIDTH = NSA_KV_GROUPS * HEAD_DIM
CMP_BLOCK = 32
CMP_STRIDE = 16
CMP_HIDDEN = 2 * HEAD_DIM
SLC_BLOCK = 64
SLC_TOPN = 16
WINDOW = 512
ATTN_SCALE = HEAD_DIM ** -0.5
NEG_INF = -1e30
LOG2E = float(np.log2(np.e))
LRU_WIDTH = 1024
LRU_BLOCKS = 16
LRU_BW = LRU_WIDTH // LRU_BLOCKS
CONV_WIDTH = 4
LRU_C = 8.0
N_GROUPS = 4
EXPERTS_PER_GROUP = 8
N_EXPERTS = N_GROUPS * EXPERTS_PER_GROUP
EXPERT_TOPK = 2
D_EXPERT = 512
EPS = 1e-6

LANES = 128
SUBLANES = 8
VMEM_LIMIT_BYTES = 56 * 1024 * 1024

Z_MGA = 0
Z_MGB = 2048
Z_Q = 4096
Z_KK = 5120
Z_VV = 5632
Z_KC = 6144
Z_VC = 6400
Z_LX = 6656
Z_LY = 7680
Z_GATE = 8704
Z_WIDTH = 9216
LRU_CB = 512
LRU_WT = 256
MOE_TM = 256
NSA_TQ = 256
NSA_CK = 512


def _cparams(sem, vmem=VMEM_LIMIT_BYTES):
    return pltpu.CompilerParams(dimension_semantics=sem, vmem_limit_bytes=vmem)


def _rms(x, g):
    return x * lax.rsqrt(jnp.mean(x * x, axis=-1, keepdims=True) + EPS) * g


def _gelu_tanh(x):
    return 0.5 * x * (1.0 + jnp.tanh(np.sqrt(2.0 / np.pi) * (x + 0.044715 * (x * x * x))))


def _sigmoid(x):
    return 1.0 / (1.0 + jnp.exp(-x))


def _row_pitch(d):
    k = d // LANES
    return k + 1 - (k % 2)


def _store_rows(ref, row0, val):
    rows, d = val.shape
    pitch = _row_pitch(d)
    for j in range(d // LANES):
        ref[pl.ds(row0 * pitch + j, rows, stride=pitch), :] = val[:, j * LANES:(j + 1) * LANES]
    for j in range(d // LANES, pitch):
        ref[pl.ds(row0 * pitch + j, rows, stride=pitch), :] = jnp.zeros((rows, LANES), val.dtype)


def _load_rows(ref, row0, rows, d):
    pitch = _row_pitch(d)
    return jnp.concatenate([ref[pl.ds(row0 * pitch + j, rows, stride=pitch), :] for j in range(d // LANES)], axis=1)


def _in_proj_kernel(x_ref, g_ref, w_ref, o_ref, h_ref):
    @pl.when(pl.program_id(1) == 0)
    def _():
        h_ref[...] = _rms(x_ref[...], g_ref[...]).astype(BF16)

    o_ref[...] = jnp.dot(h_ref[...], w_ref[...], preferred_element_type=F32).astype(o_ref.dtype)


def _in_proj(x2d, g, w, tm=1024, tn=2304):
    n, d = x2d.shape
    nw = w.shape[1]
    return pl.pallas_call(
        _in_proj_kernel,
        out_shape=jax.ShapeDtypeStruct((n, nw), BF16),
        grid=(n // tm, nw // tn),
        in_specs=[
            pl.BlockSpec((tm, d), lambda i, j: (i, 0)),
            pl.BlockSpec((1, d), lambda i, j: (0, 0)),
            pl.BlockSpec((d, tn), lambda i, j: (0, j)),
        ],
        out_specs=pl.BlockSpec((tm, tn), lambda i, j: (i, j)),
        scratch_shapes=[pltpu.VMEM((tm, d), BF16)],
        compiler_params=_cparams(("parallel", "arbitrary")),
        name="in_proj",
    )(x2d, g, w)


def _compress_kernel(zk_ref, zv_ref, kpos_ref, kw1_ref, kw2_ref, vpos_ref, vw1_ref, vw2t_ref, kc_ref, vct_ref,
                     xf_ref):
    seq = zk_ref.shape[0]
    ncp = seq // CMP_STRIDE
    hd = HEAD_DIM
    half = CMP_STRIDE * hd

    def half_windows(z_ref):
        xf_ref[...] = z_ref[...].astype(F32)
        lines = [xf_ref[pl.ds(r, ncp, stride=CMP_STRIDE), :].astype(BF16) for r in range(CMP_STRIDE)]
        return [jnp.concatenate([ln[:, gg * hd:(gg + 1) * hd] for ln in lines], axis=1) for gg in range(LANES // hd)]

    def hidden(x, pos_ref, w1_ref):
        w1 = w1_ref[...]
        ha = jnp.dot(x, w1[:half], preferred_element_type=F32)
        hb = jnp.dot(x, w1[half:], preferred_element_type=F32)
        hb = pltpu.roll(hb, hb.shape[0] - 1, axis=0)
        pc = jnp.dot(pos_ref[...], w1, preferred_element_type=F32)[0:1]
        return _gelu_tanh(ha + hb + pc).astype(BF16)

    for gg, x in enumerate(half_windows(zk_ref)):
        hk = hidden(x, kpos_ref, kw1_ref)
        kc_ref[0, gg] = jnp.dot(hk, kw2_ref[...], preferred_element_type=F32).astype(kc_ref.dtype)
    for gg, x in enumerate(half_windows(zv_ref)):
        hv = hidden(x, vpos_ref, vw1_ref)
        vct_ref[0, gg] = lax.dot_general(vw2t_ref[...], hv, (((1,), (1,)), ((), ())),
                                         preferred_element_type=F32).astype(vct_ref.dtype)


def _compress(z, batch, seq, kpos, kw1, kw2, vpos, vw1, vw2t):
    g, hd = NSA_KV_GROUPS, HEAD_DIM
    gpb = LANES // hd
    ncp = seq // CMP_STRIDE
    full = lambda a: pl.BlockSpec(a.shape, lambda i, j: (0,) * a.ndim)
    return pl.pallas_call(
        _compress_kernel,
        out_shape=(jax.ShapeDtypeStruct((batch, g, ncp, hd), BF16),
                   jax.ShapeDtypeStruct((batch, g, hd, ncp), BF16)),
        grid=(batch, g // gpb),
        in_specs=[
            pl.BlockSpec((seq, LANES), lambda i, j: (i, Z_KC // LANES + j)),
            pl.BlockSpec((seq, LANES), lambda i, j: (i, Z_VC // LANES + j)),
            full(kpos), full(kw1), full(kw2), full(vpos), full(vw1), full(vw2t),
        ],
        out_specs=(pl.BlockSpec((1, gpb, ncp, hd), lambda i, j: (i, j, 0, 0)),
                   pl.BlockSpec((1, gpb, hd, ncp), lambda i, j: (i, j, 0, 0))),
        scratch_shapes=[pltpu.VMEM((seq, LANES), F32)],
        compiler_params=_cparams(("parallel", "parallel")),
        name="compress",
    )(z, z, kpos, kw1, kw2, vpos, vw1, vw2t)


def _nsa_kernel(zq_ref, zkk_ref, zvv_ref, zg_ref, kc_ref, vct_ref, o_ref, kcomb_ref, vs_ref, vw_ref, gate_ref,
                sa_ref, sb_ref):
    grp = pl.program_id(1)
    qi = pl.program_id(2)
    tq, hd, hpg = NSA_TQ, HEAD_DIM, NSA_HPG
    nq = tq * hpg
    seq = kcomb_ref.shape[0]
    ncp = kc_ref.shape[2]
    nsb = seq // SLC_BLOCK
    blk_shift = SLC_BLOCK.bit_length() - 1

    @pl.when(qi == 0)
    def _():
        kcomb_ref[:, 0:2 * hd] = zkk_ref[...]
        blk_of_row = lax.broadcasted_iota(jnp.int32, (seq, 2 * hd), 0) >> blk_shift
        lane = lax.broadcasted_iota(jnp.int32, (seq, 2 * hd), 1)
        kcomb_ref[:, 2 * hd:] = jnp.where(blk_of_row == lane, 1.0, 0.0).astype(BF16)
        ones = jnp.ones((seq, hd), BF16)
        vs_ref[...] = jnp.concatenate([zvv_ref[:, 0:hd], ones], axis=1)
        vw_ref[...] = jnp.concatenate([zvv_ref[:, hd:2 * hd], ones], axis=1)

    qt = zq_ref[...].T
    q = jnp.concatenate([qt[h * hd:(h + 1) * hd, :] for h in range(hpg)], axis=1)
    q = (q.astype(F32) * (ATTN_SCALE * LOG2E)).astype(BF16)
    zero = jnp.zeros((hd, nq), BF16)
    t_lane = qi * tq + (lax.broadcasted_iota(jnp.int32, (1, nq), 1) & (tq - 1))
    tn_dims = (((0,), (0,)), ((), ()))
    ck = NSA_CK

    sa_ref[...] = jnp.dot(kcomb_ref[0:ck, 0:2 * hd], jnp.concatenate([q, zero], axis=0), preferred_element_type=F32)

    sc = jnp.dot(kc_ref[0, 0], q, preferred_element_type=F32)
    cmp_end = lax.broadcasted_iota(jnp.int32, (ncp, 1), 0) * CMP_STRIDE + (CMP_BLOCK - 1)
    cmask = cmp_end <= t_lane
    sc = jnp.where(cmask, sc, NEG_INF)
    pc = jnp.exp2(sc - jnp.max(sc, axis=0, keepdims=True))
    lc = jnp.sum(pc, axis=0, keepdims=True)
    pc = pc * jnp.where(t_lane >= CMP_BLOCK - 1, 1.0 / lc, 0.0)
    o_cmp = jnp.dot(vct_ref[0, 0], pc.astype(BF16), preferred_element_type=F32)

    psum = pc[:, 0:tq]
    for h in range(1, hpg):
        psum = psum + pc[:, h * tq:(h + 1) * tq]
    jrow = lax.broadcasted_iota(jnp.int32, (nsb, ncp), 0)
    crel = lax.broadcasted_iota(jnp.int32, (nsb, ncp), 1) - jrow * (SLC_BLOCK // CMP_STRIDE)
    ovt = jnp.where((crel >= 0) & (crel <= 2), 1.0, jnp.where((crel == -1) | (crel == 3), 0.5, 0.0)).astype(F32)
    imp = jnp.dot(ovt, psum, preferred_element_type=F32, precision=lax.Precision.HIGHEST)
    jblk = lax.broadcasted_iota(jnp.int32, (nsb, tq), 0)
    blk = (qi * tq + lax.broadcasted_iota(jnp.int32, (1, tq), 1)) >> blk_shift
    forced = (jblk == 0) | (jblk == blk) | (jblk == blk - 1)
    n_forced = 3
    score = jnp.where(forced, -2.0, jnp.where(jblk <= blk, imp, -1.0))
    selb = jnp.where(forced, 0.0, NEG_INF)
    for _ in range(min(SLC_TOPN, nsb) - n_forced):
        top = jnp.max(score, axis=0, keepdims=True)
        first = jnp.min(jnp.where(score == top, jblk, nsb), axis=0, keepdims=True)
        pick = jblk == first
        selb = jnp.where(pick, 0.0, selb)
        score = jnp.where(pick, -2.0, score)
    bias0 = jnp.concatenate([jnp.broadcast_to(selb[j:j + 1, :], (SLC_BLOCK, tq)) for j in range(ck // SLC_BLOCK)],
                            axis=0)
    sa_ref[...] = sa_ref[...] + jnp.concatenate([bias0] * hpg, axis=1)
    selb = selb.astype(BF16)

    pad = jnp.zeros((2 * hd - nsb, nq), BF16)
    qa = jnp.concatenate([q, zero, jnp.concatenate([selb] * hpg, axis=1), pad], axis=0)

    def slc_scores(c):
        return jnp.dot(kcomb_ref[pl.ds(pl.multiple_of(c * ck, ck), ck), :], qa, preferred_element_type=F32)

    n_full = (qi * tq) // ck

    def slc_update(buf, c, carry, causal):
        def scores():
            if not causal:
                return buf[...]
            return jnp.where(c * ck + lax.broadcasted_iota(jnp.int32, (ck, 1), 0) <= t_lane, buf[...], NEG_INF)

        m, l, acc = carry
        m_new = jnp.maximum(m, jnp.max(scores(), axis=0, keepdims=True))
        alpha = jnp.exp2(m - m_new)
        p = jnp.exp2(scores() - m_new).astype(BF16)
        v = vs_ref[pl.ds(pl.multiple_of(c * ck, ck), ck), :]
        pv = lax.dot_general(v, p, tn_dims, preferred_element_type=F32)
        return m_new, alpha * l + pv[hd:hd + 1], acc * alpha + pv[0:hd]

    def slc_pair(i, carry):
        sb_ref[...] = slc_scores(2 * i + 1)
        carry = slc_update(sa_ref, 2 * i, carry, False)
        sa_ref[...] = slc_scores(2 * i + 2)
        return slc_update(sb_ref, 2 * i + 1, carry, False)

    def tail_two(carry):
        sb_ref[...] = slc_scores(n_full)
        return slc_update(sb_ref, n_full, slc_update(sa_ref, n_full - 1, carry, False), True)

    def tail_one(carry):
        return slc_update(sa_ref, n_full, carry, True)


    nwc = WINDOW // tq + 1
    t_lo = t_lane - WINDOW
    ks_w, vs_w = [], []
    for i in range(nwc):
        rows = pl.ds(pl.multiple_of(jnp.maximum(qi - (nwc - 1) + i, 0) * tq, tq), tq)
        ks_w.append(kcomb_ref[rows, 0:2 * hd])
        vs_w.append(vw_ref[rows, :])
    qw = jnp.concatenate([zero, q], axis=0)
    sw = jnp.dot(jnp.concatenate(ks_w, axis=0), qw, preferred_element_type=F32)
    sw_parts = []
    for i in range(nwc):
        chunk = qi - (nwc - 1) + i
        spos = chunk * tq + lax.broadcasted_iota(jnp.int32, (tq, 1), 0)
        s_i = sw[i * tq:(i + 1) * tq]
        if i == 0:
            sw_parts.append(jnp.where(jnp.where(spos >= 0, spos, -(1 << 24)) > t_lo, s_i, NEG_INF))
        elif i < nwc - 1:
            sw_parts.append(s_i + jnp.where(chunk >= 0, 0.0, NEG_INF))
        else:
            sw_parts.append(jnp.where(spos <= t_lane, s_i, NEG_INF))
    sw = jnp.concatenate(sw_parts, axis=0)
    pw = jnp.exp2(sw - jnp.max(sw, axis=0, keepdims=True))
    ow = lax.dot_general(jnp.concatenate(vs_w, axis=0), pw.astype(BF16), tn_dims, preferred_element_type=F32)
    o_win = ow[0:hd] * (1.0 / ow[hd:hd + 1])
    init = (jnp.full((1, nq), NEG_INF, F32), jnp.zeros((1, nq), F32), jnp.zeros((hd, nq), F32))
    carry = lax.fori_loop(0, n_full // 2, slc_pair, init)
    _, l_s, acc_s = lax.cond(n_full % 2 == 1, tail_two, tail_one, carry)
    o_slc = acc_s * (1.0 / l_s)

    gate_ref[...] = _sigmoid(zg_ref[...].astype(F32)).T

    def branch_gate(br):
        rows = [gate_ref[pl.ds(grp * (hpg * 3) + h * 3 + br, 1), :] for h in range(hpg)]
        return jnp.concatenate(rows, axis=1)

    o = branch_gate(0) * o_cmp + branch_gate(1) * o_slc + branch_gate(2) * o_win
    o_heads = jnp.concatenate([o[:, h * tq:(h + 1) * tq] for h in range(hpg)], axis=0)
    o_ref[...] = o_heads.T.astype(o_ref.dtype)


def _nsa_attention(z, kc, vct, batch, seq):
    n = z.shape[0]
    g, hd, tq = NSA_KV_GROUPS, HEAD_DIM, NSA_TQ
    nqt = seq // tq
    ncp = kc.shape[2]
    nsb = seq // SLC_BLOCK
    gw = NSA_HPG * hd
    assert seq % NSA_CK == 0 and NSA_CK % tq == 0 and WINDOW % tq == 0 and nsb <= 2 * hd
    tile = lambda i, j, k: i * nqt + k
    return pl.pallas_call(
        _nsa_kernel,
        out_shape=jax.ShapeDtypeStruct((n, NSA_Q_WIDTH), BF16),
        grid=(batch, g, nqt),
        in_specs=[
            pl.BlockSpec((tq, gw), lambda i, j, k: (tile(i, j, k), Z_Q // gw + j)),
            pl.BlockSpec((seq, 2 * hd), lambda i, j, k: (i, Z_KK // (2 * hd) + j)),
            pl.BlockSpec((seq, 2 * hd), lambda i, j, k: (i, Z_VV // (2 * hd) + j)),
            pl.BlockSpec((tq, LANES), lambda i, j, k: (tile(i, j, k), Z_GATE // LANES)),
            pl.BlockSpec((1, 1, ncp, hd), lambda i, j, k: (i, j, 0, 0)),
            pl.BlockSpec((1, 1, hd, ncp), lambda i, j, k: (i, j, 0, 0)),
        ],
        out_specs=pl.BlockSpec((tq, gw), lambda i, j, k: (tile(i, j, k), j)),
        scratch_shapes=[pltpu.VMEM((seq, 4 * hd), BF16), pltpu.VMEM((seq, 2 * hd), BF16),
                        pltpu.VMEM((seq, 2 * hd), BF16), pltpu.VMEM((LANES, tq), F32), pltpu.VMEM((NSA_CK, NSA_HPG * tq), F32),
                        pltpu.VMEM((NSA_CK, NSA_HPG * tq), F32)],
        compiler_params=_cparams(("parallel", "parallel", "arbitrary")),
        name="nsa_attn",
    )(z, z, z, z, kc, vct)


def _rglru_kernel(x_ref, y_ref, cw_ref, cb_ref, wa_ref, ba_ref, wx_ref, bx_ref, lam_ref, o_ref,
                  tail_ref, h_ref, a_ref, u_ref):
    tc = pl.program_id(2)
    tt, cb = x_ref.shape

    @pl.when(tc == 0)
    def _():
        tail_ref[...] = jnp.zeros_like(tail_ref)
        h_ref[...] = jnp.zeros_like(h_ref)

    x = x_ref[...].astype(F32)
    xe = jnp.concatenate([tail_ref[...], x], axis=0)
    tail_ref[...] = x[tt - SUBLANES:, :]
    cw = cw_ref[...]
    xc = cb_ref[...]
    for k in range(CONV_WIDTH):
        off = SUBLANES - (CONV_WIDTH - 1) + k
        xc = xc + cw[k:k + 1, :] * xe[off:off + tt, :]
    xcb = xc.astype(BF16)
    def gate(w_ref, b_ref):
        parts = [jnp.dot(xcb[:, c * LRU_WT:(c + 1) * LRU_WT], w_ref[c], preferred_element_type=F32)
                 for c in range(cb // LRU_WT)]
        return _sigmoid(jnp.concatenate(parts, axis=1) + b_ref[...])

    r = gate(wa_ref, ba_ref)
    ig = gate(wx_ref, bx_ref)
    nl = -lam_ref[...]
    softplus = jnp.maximum(nl, 0.0) + jnp.log1p(jnp.exp(-jnp.abs(nl)))
    log_a = (-LRU_C) * softplus * r
    a_ref[...] = jnp.exp(log_a)
    th = jnp.tanh(log_a)
    u_ref[...] = jnp.sqrt(-2.0 * th / (1.0 - th)) * (ig * xc)

    row = lax.broadcasted_iota(jnp.int32, (SUBLANES, cb), 0)

    def step(i, h):
        sl = pl.ds(pl.multiple_of(i * SUBLANES, SUBLANES), SUBLANES)
        a = a_ref[sl, :]
        u = u_ref[sl, :]
        for s in (1, 2, 4):
            a_s = jnp.where(row >= s, pltpu.roll(a, s, axis=0), 1.0)
            u_s = jnp.where(row >= s, pltpu.roll(u, s, axis=0), 0.0)
            u = a * u_s + u
            a = a * a_s
        hrows = a * h + u
        u_ref[sl, :] = hrows
        return hrows[SUBLANES - 1:SUBLANES, :]

    h_ref[...] = lax.fori_loop(0, tt // SUBLANES, step, h_ref[...], unroll=8)
    o_ref[...] = (u_ref[...] * _gelu_tanh(y_ref[...].astype(F32))).astype(o_ref.dtype)


def _rglru(z, batch, seq, cw, cbias, wa_bd, ba, wx_bd, bx, lam, tt=1024):
    n = z.shape[0]
    ncb = LRU_WIDTH // LRU_CB
    nt = seq // tt
    row = lambda i, j, k: i * nt + k
    vec = lambda r: pl.BlockSpec((r, LRU_CB), lambda i, j, k: (0, j))
    return pl.pallas_call(
        _rglru_kernel,
        out_shape=jax.ShapeDtypeStruct((n, LRU_WIDTH), BF16),
        grid=(batch, ncb, nt),
        in_specs=[
            pl.BlockSpec((tt, LRU_CB), lambda i, j, k: (row(i, j, k), Z_LX // LRU_CB + j)),
            pl.BlockSpec((tt, LRU_CB), lambda i, j, k: (row(i, j, k), Z_LY // LRU_CB + j)),
            vec(CONV_WIDTH), vec(1),
            pl.BlockSpec((LRU_CB // LRU_WT, LRU_WT, LRU_WT), lambda i, j, k: (j, 0, 0)), vec(1),
            pl.BlockSpec((LRU_CB // LRU_WT, LRU_WT, LRU_WT), lambda i, j, k: (j, 0, 0)), vec(1),
            vec(1),
        ],
        out_specs=pl.BlockSpec((tt, LRU_CB), lambda i, j, k: (row(i, j, k), j)),
        scratch_shapes=[pltpu.VMEM((SUBLANES, LRU_CB), F32), pltpu.VMEM((1, LRU_CB), F32),
                        pltpu.VMEM((tt, LRU_CB), F32), pltpu.VMEM((tt, LRU_CB), F32)],
        compiler_params=_cparams(("parallel", "parallel", "arbitrary")),
        name="rglru",
    )(z, z, cw, cbias, wa_bd, ba, wx_bd, bx, lam)


def _merge_kernel(o_ref, l_ref, mga_ref, mgb_ref, wn_ref, wl_ref, m_ref):
    ya = jnp.dot(o_ref[...], wn_ref[...], preferred_element_type=F32)
    yb = jnp.dot(l_ref[...], wl_ref[...], preferred_element_type=F32)
    m = _sigmoid(mga_ref[...].astype(F32)) * ya + _sigmoid(mgb_ref[...].astype(F32)) * yb
    m_ref[...] = m.astype(m_ref.dtype)


def _merge(o, lru, z, wn, wl, tm=512):
    n, d = o.shape[0], wn.shape[1]
    return pl.pallas_call(
        _merge_kernel,
        out_shape=jax.ShapeDtypeStruct((n, d), BF16),
        grid=(n // tm,),
        in_specs=[
            pl.BlockSpec((tm, o.shape[1]), lambda i: (i, 0)),
            pl.BlockSpec((tm, lru.shape[1]), lambda i: (i, 0)),
            pl.BlockSpec((tm, d), lambda i: (i, Z_MGA // d)),
            pl.BlockSpec((tm, d), lambda i: (i, Z_MGB // d)),
            pl.BlockSpec(wn.shape, lambda i: (0, 0)),
            pl.BlockSpec(wl.shape, lambda i: (0, 0)),
        ],
        out_specs=pl.BlockSpec((tm, d), lambda i: (i, 0)),
        compiler_params=_cparams(("parallel",)),
        name="merge",
    )(o, lru, z, z, wn, wl)


def _out_route_kernel(m_ref, x_ref, wo_ref, g_ref, wr_ref, br_ref, x1_ref, xn_ref, eid_ref, ew_ref):
    x1 = x_ref[...] + jnp.dot(m_ref[...], wo_ref[...], preferred_element_type=F32)
    x1_ref[...] = x1
    xn = _rms(x1, g_ref[...])
    _store_rows(xn_ref, 0, xn)
    lg = jnp.dot(xn, wr_ref[...], preferred_element_type=F32).T + br_ref[...]
    tm = lg.shape[1]
    sub = lax.broadcasted_iota(jnp.int32, (SUBLANES, tm), 0)

    def first_argmax(v, vmax):
        return jnp.min(jnp.where(v == vmax, sub, SUBLANES), axis=0, keepdims=True)

    gl = jnp.where(sub < N_GROUPS, lg[0:SUBLANES], -jnp.inf)
    gmax = jnp.max(gl, axis=0, keepdims=True)
    ge = jnp.exp(gl - gmax)
    gprob = ge / jnp.sum(ge, axis=0, keepdims=True)
    g_val = jnp.max(gprob, axis=0, keepdims=True)
    g_idx = first_argmax(gprob, g_val)
    e_in = jnp.zeros((EXPERTS_PER_GROUP, tm), F32)
    for gi in range(N_GROUPS):
        lo = SUBLANES + gi * EXPERTS_PER_GROUP
        e_in = jnp.where(g_idx == gi, lg[lo:lo + EXPERTS_PER_GROUP], e_in)
    ee = jnp.exp(e_in - jnp.max(e_in, axis=0, keepdims=True))
    eprob = ee / jnp.sum(ee, axis=0, keepdims=True)
    v1 = jnp.max(eprob, axis=0, keepdims=True)
    i1 = first_argmax(eprob, v1)
    rest = jnp.where(sub == i1, -1.0, eprob)
    v2 = jnp.max(rest, axis=0, keepdims=True)
    i2 = first_argmax(rest, v2)
    den = v1 + v2
    eid = jnp.where(sub == 0, g_idx * EXPERTS_PER_GROUP + i1, g_idx * EXPERTS_PER_GROUP + i2)
    eid_ref[...] = eid
    ew_ref[...] = jnp.where(sub == 0, g_val * v1 / den, g_val * v2 / den)


def _out_route(m, x2d, wo, g, wr, br, tm=512):
    n, d = x2d.shape
    once = pl.Buffered(1)
    return pl.pallas_call(
        _out_route_kernel,
        out_shape=(jax.ShapeDtypeStruct((n, d), F32), jax.ShapeDtypeStruct((n * _row_pitch(d), LANES), F32),
                   jax.ShapeDtypeStruct((SUBLANES, n), jnp.int32), jax.ShapeDtypeStruct((SUBLANES, n), F32)),
        grid=(n // tm,),
        in_specs=[
            pl.BlockSpec((tm, d), lambda i: (i, 0)),
            pl.BlockSpec((tm, d), lambda i: (i, 0)),
            pl.BlockSpec((d, d), lambda i: (0, 0), pipeline_mode=once),
            pl.BlockSpec((1, d), lambda i: (0, 0)),
            pl.BlockSpec(wr.shape, lambda i: (0, 0), pipeline_mode=once),
            pl.BlockSpec(br.shape, lambda i: (0, 0)),
        ],
        out_specs=(pl.BlockSpec((tm, d), lambda i: (i, 0)), pl.BlockSpec((tm * _row_pitch(d), LANES), lambda i: (i, 0)),
                   pl.BlockSpec((SUBLANES, tm), lambda i: (0, i)), pl.BlockSpec((SUBLANES, tm), lambda i: (0, i))),
        compiler_params=_cparams(("parallel",)),
        name="out_route",
    )(m, x2d, wo, g, wr, br)


def _for_rows(cnt, fn, unroll=8):
    sh = unroll.bit_length() - 1

    def group(gidx, c):
        for u in range(unroll):
            fn(gidx * unroll + u)
        return c

    def single(r, c):
        fn(r)
        return c

    lax.fori_loop(0, cnt >> sh, group, 0)
    lax.fori_loop((cnt >> sh) << sh, cnt, single, 0)


def _moe_kernel(nvalid_ref, texp_ref, cnt_ref, tok_ref, tokn_ref, dst_ref, xn_hbm, wg_ref, wu_ref, wd_ref, y_hbm,
                xbuf, ybuf, gsem, ssem):
    i = pl.program_id(0)
    nv = nvalid_ref[0]
    slot = i % 2
    d = wg_ref.shape[1]
    k = _row_pitch(d)
    tm = xbuf.shape[0] // (2 * k)

    def lines(ref, first):
        return ref.at[pl.ds(first, k), :]

    def gather_row(idx_ref, s):
        def fn(r):
            pltpu.make_async_copy(lines(xn_hbm, idx_ref[0, 0, r]), lines(xbuf, (s * tm + r) * k), gsem.at[s]).start()
        return fn

    def gather_wait_row(s):
        def fn(r):
            pltpu.make_async_copy(lines(xn_hbm, 0), lines(xbuf, (s * tm + r) * k), gsem.at[s]).wait()
        return fn

    def scatter_row(s):
        def fn(r):
            pltpu.make_async_copy(lines(ybuf, (s * tm + r) * k), lines(y_hbm, dst_ref[0, 0, r]), ssem.at[s]).start()
        return fn

    def scatter_wait_row(s):
        def fn(r):
            pltpu.make_async_copy(lines(ybuf, (s * tm + r) * k), lines(y_hbm, 0), ssem.at[s]).wait()
        return fn

    @pl.when(i == 0)
    def _():
        xbuf[...] = jnp.zeros_like(xbuf)
        _for_rows(cnt_ref[0], gather_row(tok_ref, 0))

    @pl.when(i + 1 < nv)
    def _():
        _for_rows(cnt_ref[i + 1], gather_row(tokn_ref, 1 - slot))

    @pl.when(i < nv)
    def _():
        _for_rows(cnt_ref[i], gather_wait_row(slot))
        x = _load_rows(xbuf, slot * tm, tm, d).astype(BF16)
        gp = jnp.dot(x, wg_ref[0], preferred_element_type=F32)
        up = jnp.dot(x, wu_ref[0], preferred_element_type=F32)
        hid = (gp * _sigmoid(gp) * up).astype(BF16)
        y = jnp.dot(hid, wd_ref[0], preferred_element_type=F32)

        @pl.when(i >= 2)
        def _():
            _for_rows(cnt_ref[i - 2], scatter_wait_row(slot))

        _store_rows(ybuf, slot * tm, y)
        _for_rows(cnt_ref[i], scatter_row(slot))

    @pl.when(i == nv - 1)
    def _():
        _for_rows(cnt_ref[i], scatter_wait_row(slot))

        @pl.when(nv >= 2)
        def _():
            _for_rows(cnt_ref[i - 1], scatter_wait_row(1 - slot))


def _moe(nvalid, texp, cnt, tok, dst, xn, wg, wu, wd, n_rows_out):
    nt, _, tm = tok.shape
    d, de = wg.shape[1], wg.shape[2]
    k = _row_pitch(d)
    grid_spec = pltpu.PrefetchScalarGridSpec(
        num_scalar_prefetch=3,
        grid=(nt,),
        in_specs=[
            pl.BlockSpec((1, 1, tm), lambda i, nv, te, ct: (i, 0, 0), memory_space=pltpu.SMEM),
            pl.BlockSpec((1, 1, tm), lambda i, nv, te, ct: (jnp.minimum(i + 1, nt - 1), 0, 0),
                         memory_space=pltpu.SMEM),
            pl.BlockSpec((1, 1, tm), lambda i, nv, te, ct: (i, 0, 0), memory_space=pltpu.SMEM),
            pl.BlockSpec(memory_space=pl.ANY),
            pl.BlockSpec((1, d, de), lambda i, nv, te, ct: (te[i], 0, 0)),
            pl.BlockSpec((1, d, de), lambda i, nv, te, ct: (te[i], 0, 0)),
            pl.BlockSpec((1, de, d), lambda i, nv, te, ct: (te[i], 0, 0)),
        ],
        out_specs=pl.BlockSpec(memory_space=pl.ANY),
        scratch_shapes=[pltpu.VMEM((2 * tm * k, LANES), F32), pltpu.VMEM((2 * tm * k, LANES), F32),
                        pltpu.SemaphoreType.DMA((2,)), pltpu.SemaphoreType.DMA((2,))],
    )
    return pl.pallas_call(
        _moe_kernel,
        out_shape=jax.ShapeDtypeStruct((n_rows_out * k, LANES), F32),
        grid_spec=grid_spec,
        compiler_params=_cparams(("arbitrary",)),
        name="moe",
    )(nvalid, texp, cnt, tok, tok, dst, xn, wg, wu, wd)


def _moe_plan(eid, n, tm):
    e_flat = eid.reshape(-1)
    npairs = e_flat.shape[0]
    experts = jnp.arange(N_EXPERTS, dtype=jnp.int32)
    counts = jnp.sum((e_flat[:, None] == experts[None, :]).astype(jnp.int32), axis=0)
    padded = ((counts + tm - 1) // tm) * tm
    ends = jnp.cumsum(padded)
    offs = ends - padded
    n_rows = npairs + N_EXPERTS * tm
    nt = n_rows // tm
    fill = jnp.arange(n_rows - npairs, dtype=jnp.int32)
    fill_expert = jnp.sum((fill[:, None] >= jnp.cumsum(padded - counts)[None, :]).astype(jnp.int32), axis=1)
    keys = jnp.concatenate([e_flat * (2 * npairs) + jnp.arange(npairs, dtype=jnp.int32),
                            fill_expert * (2 * npairs) + npairs])
    skeys = jnp.sort(keys)
    low = skeys % (2 * npairs)
    dst = jnp.where(low < npairs, low, 0)
    tok = dst % n
    tile_start = jnp.arange(nt, dtype=jnp.int32) * tm
    texp = jnp.minimum(jnp.sum((tile_start[:, None] >= ends[None, :]).astype(jnp.int32), axis=1), N_EXPERTS - 1)
    cnt = jnp.clip(offs[texp] + counts[texp] - tile_start, 0, tm)
    nvalid = (ends[-1] // tm).astype(jnp.int32).reshape(1)
    return (nvalid, texp.astype(jnp.int32), cnt.astype(jnp.int32), tok.reshape(nt, 1, tm), dst.reshape(nt, 1, tm),
            npairs)


def _ple_final_kernel(x1_ref, y0_ref, y1_ref, w_ref, p_ref, gp_ref, wpg_ref, wp_ref, gf_ref, o_ref):
    w = w_ref[...]
    tm, d = x1_ref.shape
    x2 = x1_ref[...] + w[:, 0:1] * _load_rows(y0_ref, 0, tm, d) + w[:, 1:2] * _load_rows(y1_ref, 0, tm, d)
    hn = _rms(x2, gp_ref[...]).astype(BF16)
    gate = _sigmoid(jnp.dot(hn, wpg_ref[...], preferred_element_type=F32))
    pe = jnp.dot(p_ref[...].astype(BF16), wp_ref[...], preferred_element_type=F32)
    x3 = x2 + gate * pe
    o_ref[...] = _rms(x3, gf_ref[...])


def _ple_final(x1, ypairs, wcols, p2d, gp, wpg, wp, gf, tm=256):
    n, d = x1.shape
    nb = n // tm
    k = _row_pitch(d)
    return pl.pallas_call(
        _ple_final_kernel,
        out_shape=jax.ShapeDtypeStruct((n, d), F32),
        grid=(nb,),
        in_specs=[
            pl.BlockSpec((tm, d), lambda i: (i, 0)),
            pl.BlockSpec((tm * k, LANES), lambda i: (i, 0)),
            pl.BlockSpec((tm * k, LANES), lambda i: (nb + i, 0)),
            pl.BlockSpec((tm, wcols.shape[1]), lambda i: (i, 0)),
            pl.BlockSpec((tm, p2d.shape[1]), lambda i: (i, 0)),
            pl.BlockSpec((1, d), lambda i: (0, 0)),
            pl.BlockSpec((d, d), lambda i: (0, 0)),
            pl.BlockSpec(wp.shape, lambda i: (0, 0)),
            pl.BlockSpec((1, d), lambda i: (0, 0)),
        ],
        out_specs=pl.BlockSpec((tm, d), lambda i: (i, 0)),
        compiler_params=_cparams(("parallel",)),
        name="ple_final",
    )(x1, ypairs, ypairs, wcols, p2d, gp, wpg, wp, gf)


def _block_diag(w, per):
    nb, bw, _ = w.shape
    w = w.reshape(nb // per, per, bw, bw)
    eye = jnp.eye(per, dtype=w.dtype)
    return jnp.einsum("cpij,pq->cpiqj", w, eye).reshape(nb // per, per * bw, per * bw)


def _regroup_kernel(wt_ref, o_ref):
    d = Z_MGB - Z_MGA
    c = np.cumsum((0, NSA_Q_WIDTH) + (NSA_KV_WIDTH,) * 6 + (3 * NSA_HEADS, LRU_WIDTH, LRU_WIDTH, d, d))
    seg = lambda k: wt_ref[int(c[k]):int(c[k + 1]), :]
    q, k_c, v_c, k_s, v_s, k_w, v_w, gates, lru_x, lru_y, mg_a, mg_b = (seg(k) for k in range(12))

    def pair(a, b):
        parts = []
        for g in range(NSA_KV_GROUPS):
            parts += [a[g * HEAD_DIM:(g + 1) * HEAD_DIM], b[g * HEAD_DIM:(g + 1) * HEAD_DIM]]
        return jnp.concatenate(parts, axis=0)

    pad = jnp.zeros((Z_WIDTH - int(c[-1]), wt_ref.shape[1]), wt_ref.dtype)
    rows = [mg_a, mg_b, q, pair(k_s, k_w), pair(v_s, v_w), k_c, v_c, lru_x, lru_y, gates, pad]
    o_ref[...] = jnp.concatenate(rows, axis=0).T.astype(o_ref.dtype)


def _regroup_w_in(w_in_t, tr=256):
    nw, d = w_in_t.shape
    return pl.pallas_call(
        _regroup_kernel,
        out_shape=jax.ShapeDtypeStruct((d, Z_WIDTH), BF16),
        grid=(d // tr,),
        in_specs=[pl.BlockSpec((nw, tr), lambda i: (0, i))],
        out_specs=pl.BlockSpec((tr, Z_WIDTH), lambda i: (i, 0)),
        compiler_params=_cparams(("parallel",)),
        name="regroup_w_in",
    )(w_in_t)


def _layer(x2d, batch, seq, ln_mix, w_in, cmp_k_pos, cmp_k_w1, cmp_k_w2, cmp_v_pos, cmp_v_w1, cmp_v_w2, conv_w,
           conv_b, lru_wa, lru_ba, lru_wx, lru_bx, lru_lambda, w_nsa_up, w_lru_up, w_out, ln_ffn, w_grp, b_grp, w_exp,
           b_exp, w_gate, w_up, w_down):
    n, d = x2d.shape
    g, hd = NSA_KV_GROUPS, HEAD_DIM

    z = _in_proj(x2d, ln_mix.reshape(1, d), _regroup_w_in(w_in.T))

    pos8 = lambda pos: jnp.broadcast_to(pos.reshape(1, -1), (SUBLANES, pos.size)).astype(BF16)
    kc, vct = _compress(z, batch, seq, pos8(cmp_k_pos), cmp_k_w1.astype(BF16), cmp_k_w2.astype(BF16),
                        pos8(cmp_v_pos), cmp_v_w1.astype(BF16), cmp_v_w2.T.astype(BF16))
    o = _nsa_attention(z, kc, vct, batch, seq)

    per = LRU_WT // LRU_BW
    lru = _rglru(z, batch, seq, conv_w, conv_b.reshape(1, -1), _block_diag(lru_wa, per).astype(BF16),
                 lru_ba.reshape(1, -1), _block_diag(lru_wx, per).astype(BF16), lru_bx.reshape(1, -1),
                 lru_lambda.reshape(1, -1))

    merged = _merge(o, lru, z, w_nsa_up.astype(BF16), w_lru_up.astype(BF16))

    wr = jnp.zeros((d, LANES), F32).at[:, 0:N_GROUPS].set(w_grp).at[:, SUBLANES:SUBLANES + N_EXPERTS].set(w_exp)
    br = jnp.zeros((LANES, 1), F32).at[0:N_GROUPS, 0].set(b_grp).at[SUBLANES:SUBLANES + N_EXPERTS, 0].set(b_exp)
    x1, xn, eid, ew = _out_route(merged, x2d, w_out.astype(BF16), ln_ffn.reshape(1, d), wr, br)

    nvalid, texp, cnt, tok, dst, n_rows_out = _moe_plan(eid[0:EXPERT_TOPK], n, MOE_TM)
    pitch = _row_pitch(d)
    ypairs = _moe(nvalid, texp, cnt, tok * pitch, dst * pitch, xn, w_gate, w_up, w_down, n_rows_out)

    return x1, ypairs, ew.T


def kernel(x, p, ln_mix, w_in, cmp_k_pos, cmp_k_w1, cmp_k_w2, cmp_v_pos, cmp_v_w1, cmp_v_w2, conv_w, conv_b, lru_wa, lru_ba, lru_wx, lru_bx, lru_lambda, w_nsa_up, w_lru_up, w_out, ln_ffn, w_grp, b_grp, w_exp, b_exp, w_gate, w_up, w_down, ln_ple, w_ple, w_ple_gate, ln_final):
    batch, seq, d = x.shape
    assert p.shape[0] == 1, "the final norm is fused into the (single) layer's last kernel"
    n = batch * seq
    x1, ypairs, wcols = _layer(
        x.reshape(n, d), batch, seq, ln_mix[0], w_in[0], cmp_k_pos[0], cmp_k_w1[0], cmp_k_w2[0],
        cmp_v_pos[0], cmp_v_w1[0], cmp_v_w2[0], conv_w[0], conv_b[0], lru_wa[0], lru_ba[0], lru_wx[0], lru_bx[0],
        lru_lambda[0], w_nsa_up[0], w_lru_up[0], w_out[0], ln_ffn[0], w_grp[0], b_grp[0], w_exp[0], b_exp[0],
        w_gate[0], w_up[0], w_down[0])
    out = _ple_final(x1, ypairs, wcols, p[0].reshape(n, -1), ln_ple[0].reshape(1, d),
                     w_ple_gate[0].astype(BF16), w_ple[0].astype(BF16), ln_final.reshape(1, d))
    return out.reshape(batch, seq, d)
```

```python
import numpy as np
import jax
import jax.numpy as jnp
from jax import lax
from jax.experimental import pallas as pl
from jax.experimental.pallas import tpu as pltpu

F32 = jnp.float32
BF16 = jnp.bfloat16

NSA_HEADS = 16
NSA_KV_GROUPS = 4
NSA_HPG = NSA_HEADS // NSA_KV_GROUPS
HEAD_DIM = 64
NSA_Q_WIDTH = NSA_HEADS * HEAD_DIM
NSA_KV_WIDTH = NSA_KV_GROUPS * HEAD_DIM
CMP_BLOCK = 32
CMP_STRIDE = 16
CMP_HIDDEN = 2 * HEAD_DIM
SLC_BLOCK = 64
SLC_TOPN = 16
WINDOW = 512
ATTN_SCALE = HEAD_DIM ** -0.5
NEG_INF = -1e30
LOG2E = float(np.log2(np.e))
LRU_WIDTH = 1024
LRU_BLOCKS = 16
LRU_BW = LRU_WIDTH // LRU_BLOCKS
CONV_WIDTH = 4
LRU_C = 8.0
N_GROUPS = 4
EXPERTS_PER_GROUP = 8
N_EXPERTS = N_GROUPS * EXPERTS_PER_GROUP
EXPERT_TOPK = 2
D_EXPERT = 512
EPS = 1e-6

LANES = 128
SUBLANES = 8
VMEM_LIMIT_BYTES = 56 * 1024 * 1024

Z_MGA = 0
Z_MGB = 2048
Z_Q = 4096
Z_KK = 5120
Z_VV = 5632
Z_KC = 6144
Z_VC = 6400
Z_LX = 6656
Z_LY = 7680
Z_GATE = 8704
Z_WIDTH = 9216
LRU_CB = 512
LRU_WT = 256
MOE_TM = 256
NSA_TQ = 256
NSA_CK = 512


def _cparams(sem, vmem=VMEM_LIMIT_BYTES):
    return pltpu.CompilerParams(dimension_semantics=sem, vmem_limit_bytes=vmem)


def _rms(x, g):
    return x * lax.rsqrt(jnp.mean(x * x, axis=-1, keepdims=True) + EPS) * g


def _gelu_tanh(x):
    return 0.5 * x * (1.0 + jnp.tanh(np.sqrt(2.0 / np.pi) * (x + 0.044715 * (x * x * x))))


def _sigmoid(x):
    return 1.0 / (1.0 + jnp.exp(-x))


def _row_pitch(d):
    k = d // LANES
    return k + 1 - (k % 2)


def _store_rows(ref, row0, val):
    rows, d = val.shape
    pitch = _row_pitch(d)
    for j in range(d // LANES):
        ref[pl.ds(row0 * pitch + j, rows, stride=pitch), :] = val[:, j * LANES:(j + 1) * LANES]
    for j in range(d // LANES, pitch):
        ref[pl.ds(row0 * pitch + j, rows, stride=pitch), :] = jnp.zeros((rows, LANES), val.dtype)


def _load_rows(ref, row0, rows, d):
    pitch = _row_pitch(d)
    return jnp.concatenate([ref[pl.ds(row0 * pitch + j, rows, stride=pitch), :] for j in range(d // LANES)], axis=1)


def _in_proj_kernel(x_ref, g_ref, w_ref, o_ref, h_ref):
    @pl.when(pl.program_id(1) == 0)
    def _():
        h_ref[...] = _rms(x_ref[...], g_ref[...]).astype(BF16)

    o_ref[...] = jnp.dot(h_ref[...], w_ref[...], preferred_element_type=F32).astype(o_ref.dtype)


def _in_proj(x2d, g, w, tm=1024, tn=2304):
    n, d = x2d.shape
    nw = w.shape[1]
    return pl.pallas_call(
        _in_proj_kernel,
        out_shape=jax.ShapeDtypeStruct((n, nw), BF16),
        grid=(n // tm, nw // tn),
        in_specs=[
            pl.BlockSpec((tm, d), lambda i, j: (i, 0)),
            pl.BlockSpec((1, d), lambda i, j: (0, 0)),
            pl.BlockSpec((d, tn), lambda i, j: (0, j)),
        ],
        out_specs=pl.BlockSpec((tm, tn), lambda i, j: (i, j)),
        scratch_shapes=[pltpu.VMEM((tm, d), BF16)],
        compiler_params=_cparams(("parallel", "arbitrary")),
        name="in_proj",
    )(x2d, g, w)


def _compress_kernel(zk_ref, zv_ref, kpos_ref, kw1_ref, kw2_ref, vpos_ref, vw1_ref, vw2t_ref, kc_ref, vct_ref,
                     xf_ref):
    seq = zk_ref.shape[0]
    ncp = seq // CMP_STRIDE
    hd = HEAD_DIM
    half = CMP_STRIDE * hd

    def half_windows(z_ref):
        xf_ref[...] = z_ref[...].astype(F32)
        lines = [xf_ref[pl.ds(r, ncp, stride=CMP_STRIDE), :].astype(BF16) for r in range(CMP_STRIDE)]
        return [jnp.concatenate([ln[:, gg * hd:(gg + 1) * hd] for ln in lines], axis=1) for gg in range(LANES // hd)]

    def hidden(x, pos_ref, w1_ref):
        w1 = w1_ref[...]
        ha = jnp.dot(x, w1[:half], preferred_element_type=F32)
        hb = jnp.dot(x, w1[half:], preferred_element_type=F32)
        hb = pltpu.roll(hb, hb.shape[0] - 1, axis=0)
        pc = jnp.dot(pos_ref[...], w1, preferred_element_type=F32)[0:1]
        return _gelu_tanh(ha + hb + pc).astype(BF16)

    for gg, x in enumerate(half_windows(zk_ref)):
        hk = hidden(x, kpos_ref, kw1_ref)
        kc_ref[0, gg] = jnp.dot(hk, kw2_ref[...], preferred_element_type=F32).astype(kc_ref.dtype)
    for gg, x in enumerate(half_windows(zv_ref)):
        hv = hidden(x, vpos_ref, vw1_ref)
        vct_ref[0, gg] = lax.dot_general(vw2t_ref[...], hv, (((1,), (1,)), ((), ())),
                                         preferred_element_type=F32).astype(vct_ref.dtype)


def _compress(z, batch, seq, kpos, kw1, kw2, vpos, vw1, vw2t):
    g, hd = NSA_KV_GROUPS, HEAD_DIM
    gpb = LANES // hd
    ncp = seq // CMP_STRIDE
    full = lambda a: pl.BlockSpec(a.shape, lambda i, j: (0,) * a.ndim)
    return pl.pallas_call(
        _compress_kernel,
        out_shape=(jax.ShapeDtypeStruct((batch, g, ncp, hd), BF16),
                   jax.ShapeDtypeStruct((batch, g, hd, ncp), BF16)),
        grid=(batch, g // gpb),
        in_specs=[
            pl.BlockSpec((seq, LANES), lambda i, j: (i, Z_KC // LANES + j)),
            pl.BlockSpec((seq, LANES), lambda i, j: (i, Z_VC // LANES + j)),
            full(kpos), full(kw1), full(kw2), full(vpos), full(vw1), full(vw2t),
        ],
        out_specs=(pl.BlockSpec((1, gpb, ncp, hd), lambda i, j: (i, j, 0, 0)),
                   pl.BlockSpec((1, gpb, hd, ncp), lambda i, j: (i, j, 0, 0))),
        scratch_shapes=[pltpu.VMEM((seq, LANES), F32)],
        compiler_params=_cparams(("parallel", "parallel")),
        name="compress",
    )(z, z, kpos, kw1, kw2, vpos, vw1, vw2t)


def _nsa_kernel(zq_ref, zkk_ref, zvv_ref, zg_ref, kc_ref, vct_ref, o_ref, kcomb_ref, vs_ref, vw_ref, gate_ref,
                sa_ref, sb_ref, tri_ref):
    grp = pl.program_id(1)
    qi = pl.program_id(2)
    tq, hd, hpg = NSA_TQ, HEAD_DIM, NSA_HPG
    nq = tq * hpg
    seq = kcomb_ref.shape[0]
    ncp = kc_ref.shape[2]
    nsb = seq // SLC_BLOCK
    blk_shift = SLC_BLOCK.bit_length() - 1

    @pl.when(qi == 0)
    def _():
        kcomb_ref[:, 0:2 * hd] = zkk_ref[...]
        blk_of_row = lax.broadcasted_iota(jnp.int32, (seq, 2 * hd), 0) >> blk_shift
        lane = lax.broadcasted_iota(jnp.int32, (seq, 2 * hd), 1)
        kcomb_ref[:, 2 * hd:] = jnp.where(blk_of_row == lane, 1.0, 0.0).astype(BF16)
        ones = jnp.ones((seq, hd), BF16)
        vs_ref[...] = jnp.concatenate([zvv_ref[:, 0:hd], ones], axis=1)
        vw_ref[...] = jnp.concatenate([zvv_ref[:, hd:2 * hd], ones], axis=1)
        k_row = lax.broadcasted_iota(jnp.int32, (tq, nq), 0)
        t_col = lax.broadcasted_iota(jnp.int32, (tq, nq), 1) & (tq - 1)
        tri_ref[...] = jnp.where(k_row <= t_col, 0.0, NEG_INF)

    qt = zq_ref[...].T
    q = jnp.concatenate([qt[h * hd:(h + 1) * hd, :] for h in range(hpg)], axis=1)
    q = (q.astype(F32) * (ATTN_SCALE * LOG2E)).astype(BF16)
    zero = jnp.zeros((hd, nq), BF16)
    t_lane = qi * tq + (lax.broadcasted_iota(jnp.int32, (1, nq), 1) & (tq - 1))
    tn_dims = (((0,), (0,)), ((), ()))
    ck = NSA_CK

    sa_ref[...] = jnp.dot(kcomb_ref[0:ck, 0:2 * hd], jnp.concatenate([q, zero], axis=0), preferred_element_type=F32)

    sc = jnp.dot(kc_ref[0, 0], q, preferred_element_type=F32)
    cmp_end = lax.broadcasted_iota(jnp.int32, (ncp, 1), 0) * CMP_STRIDE + (CMP_BLOCK - 1)
    cmask = cmp_end <= t_lane
    sc = jnp.where(cmask, sc, NEG_INF)
    pc = jnp.exp2(sc - jnp.max(sc, axis=0, keepdims=True))
    lc = jnp.sum(pc, axis=0, keepdims=True)
    pc = pc * jnp.where(t_lane >= CMP_BLOCK - 1, 1.0 / lc, 0.0)
    o_cmp = jnp.dot(vct_ref[0, 0], pc.astype(BF16), preferred_element_type=F32)

    psum = pc[:, 0:tq]
    for h in range(1, hpg):
        psum = psum + pc[:, h * tq:(h + 1) * tq]
    jrow = lax.broadcasted_iota(jnp.int32, (nsb, ncp), 0)
    crel = lax.broadcasted_iota(jnp.int32, (nsb, ncp), 1) - jrow * (SLC_BLOCK // CMP_STRIDE)
    ovt = jnp.where((crel >= 0) & (crel <= 2), 1.0, jnp.where((crel == -1) | (crel == 3), 0.5, 0.0)).astype(F32)
    imp = jnp.dot(ovt, psum, preferred_element_type=F32, precision=lax.Precision.HIGHEST)
    jblk = lax.broadcasted_iota(jnp.int32, (nsb, tq), 0)
    blk = (qi * tq + lax.broadcasted_iota(jnp.int32, (1, tq), 1)) >> blk_shift
    forced = (jblk == 0) | (jblk == blk) | (jblk == blk - 1)
    n_forced = 3
    score = jnp.where(forced, -2.0, jnp.where(jblk <= blk, imp, -1.0))
    selb = jnp.where(forced, 0.0, NEG_INF)
    for _ in range(min(SLC_TOPN, nsb) - n_forced):
        top = jnp.max(score, axis=0, keepdims=True)
        first = jnp.min(jnp.where(score == top, jblk, nsb), axis=0, keepdims=True)
        pick = jblk == first
        selb = jnp.where(pick, 0.0, selb)
        score = jnp.where(pick, -2.0, score)
    bias0 = jnp.concatenate([jnp.broadcast_to(selb[j:j + 1, :], (SLC_BLOCK, tq)) for j in range(ck // SLC_BLOCK)],
                            axis=0)
    sa_ref[...] = sa_ref[...] + jnp.concatenate([bias0] * hpg, axis=1)
    selb = selb.astype(BF16)

    pad = jnp.zeros((2 * hd - nsb, nq), BF16)
    qa = jnp.concatenate([q, zero, jnp.concatenate([selb] * hpg, axis=1), pad], axis=0)

    def slc_scores(c):
        return jnp.dot(kcomb_ref[pl.ds(pl.multiple_of(c * ck, ck), ck), :], qa, preferred_element_type=F32)

    n_full = (qi * tq) // ck

    def slc_update(buf, c, carry, read=lambda s: s):
        m, l, acc = carry
        m_new = jnp.maximum(m, jnp.max(read(buf[...]), axis=0, keepdims=True))
        alpha = jnp.exp2(m - m_new)
        p = jnp.exp2(read(buf[...]) - m_new).astype(BF16)
        v = vs_ref[pl.ds(pl.multiple_of(c * ck, ck), ck), :]
        pv = lax.dot_general(v, p, tn_dims, preferred_element_type=F32)
        return m_new, alpha * l + pv[hd:hd + 1], acc * alpha + pv[0:hd]

    def own_chunk_causal(s):
        return jnp.where(n_full * ck + lax.broadcasted_iota(jnp.int32, (ck, 1), 0) <= t_lane, s, NEG_INF)

    def slc_pair(i, carry):
        sb_ref[...] = slc_scores(2 * i + 1)
        carry = slc_update(sa_ref, 2 * i, carry)
        sa_ref[...] = slc_scores(2 * i + 2)
        return slc_update(sb_ref, 2 * i + 1, carry)

    def tail_two(carry):
        sb_ref[...] = own_chunk_causal(slc_scores(n_full))
        return slc_update(sb_ref, n_full, slc_update(sa_ref, n_full - 1, carry))

    def tail_one(carry):
        return slc_update(sa_ref, n_full, carry, own_chunk_causal)


    nwc = WINDOW // tq + 1
    t_lo = t_lane - WINDOW
    ks_w, vs_w = [], []
    for i in range(nwc):
        rows = pl.ds(pl.multiple_of(jnp.maximum(qi - (nwc - 1) + i, 0) * tq, tq), tq)
        ks_w.append(kcomb_ref[rows, 0:2 * hd])
        vs_w.append(vw_ref[rows, :])
    qw = jnp.concatenate([zero, q], axis=0)
    sw = jnp.dot(jnp.concatenate(ks_w, axis=0), qw, preferred_element_type=F32)
    sw_parts = []
    for i in range(nwc):
        chunk = qi - (nwc - 1) + i
        spos = chunk * tq + lax.broadcasted_iota(jnp.int32, (tq, 1), 0)
        s_i = sw[i * tq:(i + 1) * tq]
        if i == 0:
            sw_parts.append(jnp.where(jnp.where(spos >= 0, spos, -(1 << 24)) > t_lo, s_i, NEG_INF))
        elif i < nwc - 1:
            sw_parts.append(s_i + jnp.where(chunk >= 0, 0.0, NEG_INF))
        else:
            sw_parts.append(s_i + tri_ref[...])
    sw = jnp.concatenate(sw_parts, axis=0)
    pw = jnp.exp2(sw - jnp.max(sw, axis=0, keepdims=True))
    ow = lax.dot_general(jnp.concatenate(vs_w, axis=0), pw.astype(BF16), tn_dims, preferred_element_type=F32)
    o_win = ow[0:hd] * (1.0 / ow[hd:hd + 1])
    init = (jnp.full((1, nq), NEG_INF, F32), jnp.zeros((1, nq), F32), jnp.zeros((hd, nq), F32))
    carry = lax.fori_loop(0, n_full // 2, slc_pair, init)
    _, l_s, acc_s = lax.cond(n_full % 2 == 1, tail_two, tail_one, carry)
    o_slc = acc_s * (1.0 / l_s)

    gate_ref[...] = _sigmoid(zg_ref[...].astype(F32)).T

    def branch_gate(br):
        rows = [gate_ref[pl.ds(grp * (hpg * 3) + h * 3 + br, 1), :] for h in range(hpg)]
        return jnp.concatenate(rows, axis=1)

    o = branch_gate(0) * o_cmp + branch_gate(1) * o_slc + branch_gate(2) * o_win
    o_heads = jnp.concatenate([o[:, h * tq:(h + 1) * tq] for h in range(hpg)], axis=0)
    o_ref[...] = o_heads.T.astype(o_ref.dtype)


def _nsa_attention(z, kc, vct, batch, seq):
    n = z.shape[0]
    g, hd, tq = NSA_KV_GROUPS, HEAD_DIM, NSA_TQ
    nqt = seq // tq
    ncp = kc.shape[2]
    nsb = seq // SLC_BLOCK
    gw = NSA_HPG * hd
    assert seq % NSA_CK == 0 and NSA_CK % tq == 0 and WINDOW % tq == 0 and nsb <= 2 * hd
    tile = lambda i, j, k: i * nqt + k
    return pl.pallas_call(
        _nsa_kernel,
        out_shape=jax.ShapeDtypeStruct((n, NSA_Q_WIDTH), BF16),
        grid=(batch, g, nqt),
        in_specs=[
            pl.BlockSpec((tq, gw), lambda i, j, k: (tile(i, j, k), Z_Q // gw + j)),
            pl.BlockSpec((seq, 2 * hd), lambda i, j, k: (i, Z_KK // (2 * hd) + j)),
            pl.BlockSpec((seq, 2 * hd), lambda i, j, k: (i, Z_VV // (2 * hd) + j)),
            pl.BlockSpec((tq, LANES), lambda i, j, k: (tile(i, j, k), Z_GATE // LANES)),
            pl.BlockSpec((1, 1, ncp, hd), lambda i, j, k: (i, j, 0, 0)),
            pl.BlockSpec((1, 1, hd, ncp), lambda i, j, k: (i, j, 0, 0)),
        ],
        out_specs=pl.BlockSpec((tq, gw), lambda i, j, k: (tile(i, j, k), j)),
        scratch_shapes=[pltpu.VMEM((seq, 4 * hd), BF16), pltpu.VMEM((seq, 2 * hd), BF16),
                        pltpu.VMEM((seq, 2 * hd), BF16), pltpu.VMEM((LANES, tq), F32), pltpu.VMEM((NSA_CK, NSA_HPG * tq), F32),
                        pltpu.VMEM((NSA_CK, NSA_HPG * tq), F32), pltpu.VMEM((tq, NSA_HPG * tq), F32)],
        compiler_params=_cparams(("parallel", "parallel", "arbitrary")),
        name="nsa_attn",
    )(z, z, z, z, kc, vct)


def _rglru_kernel(x_ref, y_ref, cw_ref, cb_ref, wa_ref, ba_ref, wx_ref, bx_ref, lam_ref, o_ref,
                  tail_ref, h_ref, a_ref, u_ref):
    tc = pl.program_id(2)
    tt, cb = x_ref.shape

    @pl.when(tc == 0)
    def _():
        tail_ref[...] = jnp.zeros_like(tail_ref)
        h_ref[...] = jnp.zeros_like(h_ref)

    x = x_ref[...].astype(F32)
    xe = jnp.concatenate([tail_ref[...], x], axis=0)
    tail_ref[...] = x[tt - SUBLANES:, :]
    cw = cw_ref[...]
    xc = cb_ref[...]
    for k in range(CONV_WIDTH):
        off = SUBLANES - (CONV_WIDTH - 1) + k
        xc = xc + cw[k:k + 1, :] * xe[off:off + tt, :]
    xcb = xc.astype(BF16)
    def gate(w_ref, b_ref):
        parts = [jnp.dot(xcb[:, c * LRU_WT:(c + 1) * LRU_WT], w_ref[c], preferred_element_type=F32)
                 for c in range(cb // LRU_WT)]
        return _sigmoid(jnp.concatenate(parts, axis=1) + b_ref[...])

    r = gate(wa_ref, ba_ref)
    ig = gate(wx_ref, bx_ref)
    nl = -lam_ref[...]
    softplus = jnp.maximum(nl, 0.0) + jnp.log1p(jnp.exp(-jnp.abs(nl)))
    log_a = (-LRU_C) * softplus * r
    a_ref[...] = jnp.exp(log_a)
    th = jnp.tanh(log_a)
    u_ref[...] = jnp.sqrt(-2.0 * th / (1.0 - th)) * (ig * xc)

    row = lax.broadcasted_iota(jnp.int32, (SUBLANES, cb), 0)

    def step(i, h):
        sl = pl.ds(pl.multiple_of(i * SUBLANES, SUBLANES), SUBLANES)
        a = a_ref[sl, :]
        u = u_ref[sl, :]
        for s in (1, 2, 4):
            a_s = jnp.where(row >= s, pltpu.roll(a, s, axis=0), 1.0)
            u_s = jnp.where(row >= s, pltpu.roll(u, s, axis=0), 0.0)
            u = a * u_s + u
            a = a * a_s
        hrows = a * h + u
        u_ref[sl, :] = hrows
        return hrows[SUBLANES - 1:SUBLANES, :]

    h_ref[...] = lax.fori_loop(0, tt // SUBLANES, step, h_ref[...], unroll=8)
    o_ref[...] = (u_ref[...] * _gelu_tanh(y_ref[...].astype(F32))).astype(o_ref.dtype)


def _rglru(z, batch, seq, cw, cbias, wa_bd, ba, wx_bd, bx, lam, tt=1024):
    n = z.shape[0]
    ncb = LRU_WIDTH // LRU_CB
    nt = seq // tt
    row = lambda i, j, k: i * nt + k
    vec = lambda r: pl.BlockSpec((r, LRU_CB), lambda i, j, k: (0, j))
    return pl.pallas_call(
        _rglru_kernel,
        out_shape=jax.ShapeDtypeStruct((n, LRU_WIDTH), BF16),
        grid=(batch, ncb, nt),
        in_specs=[
            pl.BlockSpec((tt, LRU_CB), lambda i, j, k: (row(i, j, k), Z_LX // LRU_CB + j)),
            pl.BlockSpec((tt, LRU_CB), lambda i, j, k: (row(i, j, k), Z_LY // LRU_CB + j)),
            vec(CONV_WIDTH), vec(1),
            pl.BlockSpec((LRU_CB // LRU_WT, LRU_WT, LRU_WT), lambda i, j, k: (j, 0, 0)), vec(1),
            pl.BlockSpec((LRU_CB // LRU_WT, LRU_WT, LRU_WT), lambda i, j, k: (j, 0, 0)), vec(1),
            vec(1),
        ],
        out_specs=pl.BlockSpec((tt, LRU_CB), lambda i, j, k: (row(i, j, k), j)),
        scratch_shapes=[pltpu.VMEM((SUBLANES, LRU_CB), F32), pltpu.VMEM((1, LRU_CB), F32),
                        pltpu.VMEM((tt, LRU_CB), F32), pltpu.VMEM((tt, LRU_CB), F32)],
        compiler_params=_cparams(("parallel", "parallel", "arbitrary")),
        name="rglru",
    )(z, z, cw, cbias, wa_bd, ba, wx_bd, bx, lam)


def _merge_kernel(o_ref, l_ref, mga_ref, mgb_ref, wn_ref, wl_ref, m_ref):
    ya = jnp.dot(o_ref[...], wn_ref[...], preferred_element_type=F32)
    yb = jnp.dot(l_ref[...], wl_ref[...], preferred_element_type=F32)
    m = _sigmoid(mga_ref[...].astype(F32)) * ya + _sigmoid(mgb_ref[...].astype(F32)) * yb
    m_ref[...] = m.astype(m_ref.dtype)


def _merge(o, lru, z, wn, wl, tm=512):
    n, d = o.shape[0], wn.shape[1]
    return pl.pallas_call(
        _merge_kernel,
        out_shape=jax.ShapeDtypeStruct((n, d), BF16),
        grid=(n // tm,),
        in_specs=[
            pl.BlockSpec((tm, o.shape[1]), lambda i: (i, 0)),
            pl.BlockSpec((tm, lru.shape[1]), lambda i: (i, 0)),
            pl.BlockSpec((tm, d), lambda i: (i, Z_MGA // d)),
            pl.BlockSpec((tm, d), lambda i: (i, Z_MGB // d)),
            pl.BlockSpec(wn.shape, lambda i: (0, 0)),
            pl.BlockSpec(wl.shape, lambda i: (0, 0)),
        ],
        out_specs=pl.BlockSpec((tm, d), lambda i: (i, 0)),
        compiler_params=_cparams(("parallel",)),
        name="merge",
    )(o, lru, z, z, wn, wl)


def _out_route_kernel(m_ref, x_ref, wo_ref, g_ref, wr_ref, br_ref, x1_ref, xn_ref, eid_ref, ew_ref):
    x1 = x_ref[...] + jnp.dot(m_ref[...], wo_ref[...], preferred_element_type=F32)
    x1_ref[...] = x1
    xn = _rms(x1, g_ref[...])
    _store_rows(xn_ref, 0, xn)
    lg = jnp.dot(xn, wr_ref[...], preferred_element_type=F32).T + br_ref[...]
    tm = lg.shape[1]
    sub = lax.broadcasted_iota(jnp.int32, (SUBLANES, tm), 0)

    def first_argmax(v, vmax):
        return jnp.min(jnp.where(v == vmax, sub, SUBLANES), axis=0, keepdims=True)

    gl = jnp.where(sub < N_GROUPS, lg[0:SUBLANES], -jnp.inf)
    gmax = jnp.max(gl, axis=0, keepdims=True)
    ge = jnp.exp(gl - gmax)
    gprob = ge / jnp.sum(ge, axis=0, keepdims=True)
    g_val = jnp.max(gprob, axis=0, keepdims=True)
    g_idx = first_argmax(gprob, g_val)
    e_in = jnp.zeros((EXPERTS_PER_GROUP, tm), F32)
    for gi in range(N_GROUPS):
        lo = SUBLANES + gi * EXPERTS_PER_GROUP
        e_in = jnp.where(g_idx == gi, lg[lo:lo + EXPERTS_PER_GROUP], e_in)
    ee = jnp.exp(e_in - jnp.max(e_in, axis=0, keepdims=True))
    eprob = ee / jnp.sum(ee, axis=0, keepdims=True)
    v1 = jnp.max(eprob, axis=0, keepdims=True)
    i1 = first_argmax(eprob, v1)
    rest = jnp.where(sub == i1, -1.0, eprob)
    v2 = jnp.max(rest, axis=0, keepdims=True)
    i2 = first_argmax(rest, v2)
    den = v1 + v2
    eid = jnp.where(sub == 0, g_idx * EXPERTS_PER_GROUP + i1, g_idx * EXPERTS_PER_GROUP + i2)
    eid_ref[...] = eid
    ew_ref[...] = jnp.where(sub == 0, g_val * v1 / den, g_val * v2 / den)


def _out_route(m, x2d, wo, g, wr, br, tm=512):
    n, d = x2d.shape
    once = pl.Buffered(1)
    return pl.pallas_call(
        _out_route_kernel,
        out_shape=(jax.ShapeDtypeStruct((n, d), F32), jax.ShapeDtypeStruct((n * _row_pitch(d), LANES), F32),
                   jax.ShapeDtypeStruct((SUBLANES, n), jnp.int32), jax.ShapeDtypeStruct((SUBLANES, n), F32)),
        grid=(n // tm,),
        in_specs=[
            pl.BlockSpec((tm, d), lambda i: (i, 0)),
            pl.BlockSpec((tm, d), lambda i: (i, 0)),
            pl.BlockSpec((d, d), lambda i: (0, 0), pipeline_mode=once),
            pl.BlockSpec((1, d), lambda i: (0, 0)),
            pl.BlockSpec(wr.shape, lambda i: (0, 0), pipeline_mode=once),
            pl.BlockSpec(br.shape, lambda i: (0, 0)),
        ],
        out_specs=(pl.BlockSpec((tm, d), lambda i: (i, 0)), pl.BlockSpec((tm * _row_pitch(d), LANES), lambda i: (i, 0)),
                   pl.BlockSpec((SUBLANES, tm), lambda i: (0, i)), pl.BlockSpec((SUBLANES, tm), lambda i: (0, i))),
        compiler_params=_cparams(("parallel",)),
        name="out_route",
    )(m, x2d, wo, g, wr, br)


def _for_rows(cnt, fn, unroll=8):
    sh = unroll.bit_length() - 1

    def group(gidx, c):
        for u in range(unroll):
            fn(gidx * unroll + u)
        return c

    def single(r, c):
        fn(r)
        return c

    lax.fori_loop(0, cnt >> sh, group, 0)
    lax.fori_loop((cnt >> sh) << sh, cnt, single, 0)


def _moe_kernel(nvalid_ref, texp_ref, cnt_ref, tok_ref, tokn_ref, dst_ref, xn_hbm, wg_ref, wu_ref, wd_ref, y_hbm,
                xbuf, ybuf, gsem, ssem):
    i = pl.program_id(0)
    nv = nvalid_ref[0]
    slot = i % 2
    d = wg_ref.shape[1]
    k = _row_pitch(d)
    tm = xbuf.shape[0] // (2 * k)

    def lines(ref, first):
        return ref.at[pl.ds(first, k), :]

    def gather_row(idx_ref, s):
        def fn(r):
            pltpu.make_async_copy(lines(xn_hbm, idx_ref[0, 0, r]), lines(xbuf, (s * tm + r) * k), gsem.at[s]).start()
        return fn

    def gather_wait_row(s):
        def fn(r):
            pltpu.make_async_copy(lines(xn_hbm, 0), lines(xbuf, (s * tm + r) * k), gsem.at[s]).wait()
        return fn

    def scatter_row(s):
        def fn(r):
            pltpu.make_async_copy(lines(ybuf, (s * tm + r) * k), lines(y_hbm, dst_ref[0, 0, r]), ssem.at[s]).start()
        return fn

    def scatter_wait_row(s):
        def fn(r):
            pltpu.make_async_copy(lines(ybuf, (s * tm + r) * k), lines(y_hbm, 0), ssem.at[s]).wait()
        return fn

    @pl.when(i == 0)
    def _():
        xbuf[...] = jnp.zeros_like(xbuf)
        _for_rows(cnt_ref[0], gather_row(tok_ref, 0))

    @pl.when(i + 1 < nv)
    def _():
        _for_rows(cnt_ref[i + 1], gather_row(tokn_ref, 1 - slot))

    @pl.when(i < nv)
    def _():
        _for_rows(cnt_ref[i], gather_wait_row(slot))
        x = _load_rows(xbuf, slot * tm, tm, d).astype(BF16)
        gp = jnp.dot(x, wg_ref[0], preferred_element_type=F32)
        up = jnp.dot(x, wu_ref[0], preferred_element_type=F32)
        hid = (gp * _sigmoid(gp) * up).astype(BF16)
        y = jnp.dot(hid, wd_ref[0], preferred_element_type=F32)

        @pl.when(i >= 2)
        def _():
            _for_rows(cnt_ref[i - 2], scatter_wait_row(slot))

        _store_rows(ybuf, slot * tm, y)
        _for_rows(cnt_ref[i], scatter_row(slot))

    @pl.when(i == nv - 1)
    def _():
        _for_rows(cnt_ref[i], scatter_wait_row(slot))

        @pl.when(nv >= 2)
        def _():
            _for_rows(cnt_ref[i - 1], scatter_wait_row(1 - slot))


def _moe(nvalid, texp, cnt, tok, dst, xn, wg, wu, wd, n_rows_out):
    nt, _, tm = tok.shape
    d, de = wg.shape[1], wg.shape[2]
    k = _row_pitch(d)
    grid_spec = pltpu.PrefetchScalarGridSpec(
        num_scalar_prefetch=3,
        grid=(nt,),
        in_specs=[
            pl.BlockSpec((1, 1, tm), lambda i, nv, te, ct: (i, 0, 0), memory_space=pltpu.SMEM),
            pl.BlockSpec((1, 1, tm), lambda i, nv, te, ct: (jnp.minimum(i + 1, nt - 1), 0, 0),
                         memory_space=pltpu.SMEM),
            pl.BlockSpec((1, 1, tm), lambda i, nv, te, ct: (i, 0, 0), memory_space=pltpu.SMEM),
            pl.BlockSpec(memory_space=pl.ANY),
            pl.BlockSpec((1, d, de), lambda i, nv, te, ct: (te[i], 0, 0)),
            pl.BlockSpec((1, d, de), lambda i, nv, te, ct: (te[i], 0, 0)),
            pl.BlockSpec((1, de, d), lambda i, nv, te, ct: (te[i], 0, 0)),
        ],
        out_specs=pl.BlockSpec(memory_space=pl.ANY),
        scratch_shapes=[pltpu.VMEM((2 * tm * k, LANES), F32), pltpu.VMEM((2 * tm * k, LANES), F32),
                        pltpu.SemaphoreType.DMA((2,)), pltpu.SemaphoreType.DMA((2,))],
    )
    return pl.pallas_call(
        _moe_kernel,
        out_shape=jax.ShapeDtypeStruct((n_rows_out * k, LANES), F32),
        grid_spec=grid_spec,
        compiler_params=_cparams(("arbitrary",)),
        name="moe",
    )(nvalid, texp, cnt, tok, tok, dst, xn, wg, wu, wd)


def _moe_plan(eid, n, tm):
    e_flat = eid.reshape(-1)
    npairs = e_flat.shape[0]
    experts = jnp.arange(N_EXPERTS, dtype=jnp.int32)
    counts = jnp.sum((e_flat[:, None] == experts[None, :]).astype(jnp.int32), axis=0)
    padded = ((counts + tm - 1) // tm) * tm
    ends = jnp.cumsum(padded)
    offs = ends - padded
    n_rows = npairs + N_EXPERTS * tm
    nt = n_rows // tm
    fill = jnp.arange(n_rows - npairs, dtype=jnp.int32)
    fill_expert = jnp.sum((fill[:, None] >= jnp.cumsum(padded - counts)[None, :]).astype(jnp.int32), axis=1)
    keys = jnp.concatenate([e_flat * (2 * npairs) + jnp.arange(npairs, dtype=jnp.int32),
                            fill_expert * (2 * npairs) + npairs])
    skeys = jnp.sort(keys)
    low = skeys % (2 * npairs)
    dst = jnp.where(low < npairs, low, 0)
    tok = dst % n
    tile_start = jnp.arange(nt, dtype=jnp.int32) * tm
    texp = jnp.minimum(jnp.sum((tile_start[:, None] >= ends[None, :]).astype(jnp.int32), axis=1), N_EXPERTS - 1)
    cnt = jnp.clip(offs[texp] + counts[texp] - tile_start, 0, tm)
    nvalid = (ends[-1] // tm).astype(jnp.int32).reshape(1)
    return (nvalid, texp.astype(jnp.int32), cnt.astype(jnp.int32), tok.reshape(nt, 1, tm), dst.reshape(nt, 1, tm),
            npairs)


def _ple_final_kernel(x1_ref, y0_ref, y1_ref, w_ref, p_ref, gp_ref, wpg_ref, wp_ref, gf_ref, o_ref):
    w = w_ref[...]
    tm, d = x1_ref.shape
    x2 = x1_ref[...] + w[:, 0:1] * _load_rows(y0_ref, 0, tm, d) + w[:, 1:2] * _load_rows(y1_ref, 0, tm, d)
    hn = _rms(x2, gp_ref[...]).astype(BF16)
    gate = _sigmoid(jnp.dot(hn, wpg_ref[...], preferred_element_type=F32))
    pe = jnp.dot(p_ref[...].astype(BF16), wp_ref[...], preferred_element_type=F32)
    x3 = x2 + gate * pe
    o_ref[...] = _rms(x3, gf_ref[...])


def _ple_final(x1, ypairs, wcols, p2d, gp, wpg, wp, gf, tm=256):
    n, d = x1.shape
    nb = n // tm
    k = _row_pitch(d)
    return pl.pallas_call(
        _ple_final_kernel,
        out_shape=jax.ShapeDtypeStruct((n, d), F32),
        grid=(nb,),
        in_specs=[
            pl.BlockSpec((tm, d), lambda i: (i, 0)),
            pl.BlockSpec((tm * k, LANES), lambda i: (i, 0)),
            pl.BlockSpec((tm * k, LANES), lambda i: (nb + i, 0)),
            pl.BlockSpec((tm, wcols.shape[1]), lambda i: (i, 0)),
            pl.BlockSpec((tm, p2d.shape[1]), lambda i: (i, 0)),
            pl.BlockSpec((1, d), lambda i: (0, 0)),
            pl.BlockSpec((d, d), lambda i: (0, 0)),
            pl.BlockSpec(wp.shape, lambda i: (0, 0)),
            pl.BlockSpec((1, d), lambda i: (0, 0)),
        ],
        out_specs=pl.BlockSpec((tm, d), lambda i: (i, 0)),
        compiler_params=_cparams(("parallel",)),
        name="ple_final",
    )(x1, ypairs, ypairs, wcols, p2d, gp, wpg, wp, gf)


def _block_diag(w, per):
    nb, bw, _ = w.shape
    w = w.reshape(nb // per, per, bw, bw)
    eye = jnp.eye(per, dtype=w.dtype)
    return jnp.einsum("cpij,pq->cpiqj", w, eye).reshape(nb // per, per * bw, per * bw)


def _regroup_kernel(wt_ref, o_ref):
    d = Z_MGB - Z_MGA
    c = np.cumsum((0, NSA_Q_WIDTH) + (NSA_KV_WIDTH,) * 6 + (3 * NSA_HEADS, LRU_WIDTH, LRU_WIDTH, d, d))
    seg = lambda k: wt_ref[int(c[k]):int(c[k + 1]), :]
    q, k_c, v_c, k_s, v_s, k_w, v_w, gates, lru_x, lru_y, mg_a, mg_b = (seg(k) for k in range(12))

    def pair(a, b):
        parts = []
        for g in range(NSA_KV_GROUPS):
            parts += [a[g * HEAD_DIM:(g + 1) * HEAD_DIM], b[g * HEAD_DIM:(g + 1) * HEAD_DIM]]
        return jnp.concatenate(parts, axis=0)

    pad = jnp.zeros((Z_WIDTH - int(c[-1]), wt_ref.shape[1]), wt_ref.dtype)
    rows = [mg_a, mg_b, q, pair(k_s, k_w), pair(v_s, v_w), k_c, v_c, lru_x, lru_y, gates, pad]
    o_ref[...] = jnp.concatenate(rows, axis=0).T.astype(o_ref.dtype)


def _regroup_w_in(w_in_t, tr=256):
    nw, d = w_in_t.shape
    return pl.pallas_call(
        _regroup_kernel,
        out_shape=jax.ShapeDtypeStruct((d, Z_WIDTH), BF16),
        grid=(d // tr,),
        in_specs=[pl.BlockSpec((nw, tr), lambda i: (0, i))],
        out_specs=pl.BlockSpec((tr, Z_WIDTH), lambda i: (i, 0)),
        compiler_params=_cparams(("parallel",)),
        name="regroup_w_in",
    )(w_in_t)


def _layer(x2d, batch, seq, ln_mix, w_in, cmp_k_pos, cmp_k_w1, cmp_k_w2, cmp_v_pos, cmp_v_w1, cmp_v_w2, conv_w,
           conv_b, lru_wa, lru_ba, lru_wx, lru_bx, lru_lambda, w_nsa_up, w_lru_up, w_out, ln_ffn, w_grp, b_grp, w_exp,
           b_exp, w_gate, w_up, w_down):
    n, d = x2d.shape
    g, hd = NSA_KV_GROUPS, HEAD_DIM

    z = _in_proj(x2d, ln_mix.reshape(1, d), _regroup_w_in(w_in.T))

    pos8 = lambda pos: jnp.broadcast_to(pos.reshape(1, -1), (SUBLANES, pos.size)).astype(BF16)
    kc, vct = _compress(z, batch, seq, pos8(cmp_k_pos), cmp_k_w1.astype(BF16), cmp_k_w2.astype(BF16),
                        pos8(cmp_v_pos), cmp_v_w1.astype(BF16), cmp_v_w2.T.astype(BF16))
    o = _nsa_attention(z, kc, vct, batch, seq)

    per = LRU_WT // LRU_BW
    lru = _rglru(z, batch, seq, conv_w, conv_b.reshape(1, -1), _block_diag(lru_wa, per).astype(BF16),
                 lru_ba.reshape(1, -1), _block_diag(lru_wx, per).astype(BF16), lru_bx.reshape(1, -1),
                 lru_lambda.reshape(1, -1))

    merged = _merge(o, lru, z, w_nsa_up.astype(BF16), w_lru_up.astype(BF16))

    wr = jnp.zeros((d, LANES), F32).at[:, 0:N_GROUPS].set(w_grp).at[:, SUBLANES:SUBLANES + N_EXPERTS].set(w_exp)
    br = jnp.zeros((LANES, 1), F32).at[0:N_GROUPS, 0].set(b_grp).at[SUBLANES:SUBLANES + N_EXPERTS, 0].set(b_exp)
    x1, xn, eid, ew = _out_route(merged, x2d, w_out.astype(BF16), ln_ffn.reshape(1, d), wr, br)

    nvalid, texp, cnt, tok, dst, n_rows_out = _moe_plan(eid[0:EXPERT_TOPK], n, MOE_TM)
    pitch = _row_pitch(d)
    ypairs = _moe(nvalid, texp, cnt, tok * pitch, dst * pitch, xn, w_gate, w_up, w_down, n_rows_out)

    return x1, ypairs, ew.T


def kernel(x, p, ln_mix, w_in, cmp_k_pos, cmp_k_w1, cmp_k_w2, cmp_v_pos, cmp_v_w1, cmp_v_w2, conv_w, conv_b, lru_wa, lru_ba, lru_wx, lru_bx, lru_lambda, w_nsa_up, w_lru_up, w_out, ln_ffn, w_grp, b_grp, w_exp, b_exp, w_gate, w_up, w_down, ln_ple, w_ple, w_ple_gate, ln_final):
    batch, seq, d = x.shape
    assert p.shape[0] == 1, "the final norm is fused into the (single) layer's last kernel"
    n = batch * seq
    x1, ypairs, wcols = _layer(
        x.reshape(n, d), batch, seq, ln_mix[0], w_in[0], cmp_k_pos[0], cmp_k_w1[0], cmp_k_w2[0],
        cmp_v_pos[0], cmp_v_w1[0], cmp_v_w2[0], conv_w[0], conv_b[0], lru_wa[0], lru_ba[0], lru_wx[0], lru_bx[0],
        lru_lambda[0], w_nsa_up[0], w_lru_up[0], w_out[0], ln_ffn[0], w_grp[0], b_grp[0], w_exp[0], b_exp[0],
        w_gate[0], w_up[0], w_down[0])
    out = _ple_final(x1, ypairs, wcols, p[0].reshape(n, -1), ln_ple[0].reshape(1, d),
                     w_ple_gate[0].astype(BF16), w_ple[0].astype(BF16), ln_final.reshape(1, d))
    return out.reshape(batch, seq, d)
```

```python
import functools

import numpy as np
import jax
import jax.numpy as jnp
from jax import lax
from jax.experimental import pallas as pl
from jax.experimental.pallas import tpu as pltpu

F32 = jnp.float32
BF16 = jnp.bfloat16

NSA_HEADS = 16
NSA_KV_GROUPS = 4
NSA_HPG = NSA_HEADS // NSA_KV_GROUPS
HEAD_DIM = 64
NSA_Q_WIDTH = NSA_HEADS * HEAD_DIM
NSA_KV_WIDTH = NSA_KV_GROUPS * HEAD_DIM
CMP_BLOCK = 32
CMP_STRIDE = 16
CMP_HIDDEN = 2 * HEAD_DIM
SLC_BLOCK = 64
SLC_TOPN = 16
WINDOW = 512
ATTN_SCALE = HEAD_DIM ** -0.5
NEG_INF = -1e30
LOG2E = float(np.log2(np.e))
LRU_WIDTH = 1024
LRU_BLOCKS = 16
LRU_BW = LRU_WIDTH // LRU_BLOCKS
CONV_WIDTH = 4
LRU_C = 8.0
N_GROUPS = 4
EXPERTS_PER_GROUP = 8
N_EXPERTS = N_GROUPS * EXPERTS_PER_GROUP
EXPERT_TOPK = 2
D_EXPERT = 512
EPS = 1e-6

LANES = 128
SUBLANES = 8
VMEM_LIMIT_BYTES = 56 * 1024 * 1024

Z_MGA = 0
Z_MGB = 2048
Z_Q = 4096
Z_KK = 5120
Z_VV = 5632
Z_KC = 6144
Z_VC = 6400
Z_LX = 6656
Z_LY = 7680
Z_GATE = 8704
Z_WIDTH = 9216
LRU_CB = 512
LRU_WT = 256
MOE_TM = 256
NSA_TQ = 256
NSA_CK = 512


def _cparams(sem, vmem=VMEM_LIMIT_BYTES):
    return pltpu.CompilerParams(dimension_semantics=sem, vmem_limit_bytes=vmem)


def _rms(x, g):
    return x * lax.rsqrt(jnp.mean(x * x, axis=-1, keepdims=True) + EPS) * g


def _gelu_tanh(x):
    return 0.5 * x * (1.0 + jnp.tanh(np.sqrt(2.0 / np.pi) * (x + 0.044715 * (x * x * x))))


def _sigmoid(x):
    return 1.0 / (1.0 + jnp.exp(-x))


def _row_pitch(d):
    k = d // LANES
    return k + 1 - (k % 2)


def _store_rows(ref, row0, val):
    rows, d = val.shape
    pitch = _row_pitch(d)
    for j in range(d // LANES):
        ref[pl.ds(row0 * pitch + j, rows, stride=pitch), :] = val[:, j * LANES:(j + 1) * LANES]
    for j in range(d // LANES, pitch):
        ref[pl.ds(row0 * pitch + j, rows, stride=pitch), :] = jnp.zeros((rows, LANES), val.dtype)


def _load_rows(ref, row0, rows, d):
    pitch = _row_pitch(d)
    return jnp.concatenate([ref[pl.ds(row0 * pitch + j, rows, stride=pitch), :] for j in range(d // LANES)], axis=1)


def _in_proj_kernel(x_ref, g_ref, w_ref, o_ref, h_ref):
    @pl.when(pl.program_id(1) == 0)
    def _():
        h_ref[...] = _rms(x_ref[...], g_ref[...]).astype(BF16)

    o_ref[...] = jnp.dot(h_ref[...], w_ref[...], preferred_element_type=F32).astype(o_ref.dtype)


def _in_proj(x2d, g, w, tm=1024, tn=2304):
    n, d = x2d.shape
    nw = w.shape[1]
    return pl.pallas_call(
        _in_proj_kernel,
        out_shape=jax.ShapeDtypeStruct((n, nw), BF16),
        grid=(n // tm, nw // tn),
        in_specs=[
            pl.BlockSpec((tm, d), lambda i, j: (i, 0)),
            pl.BlockSpec((1, d), lambda i, j: (0, 0)),
            pl.BlockSpec((d, tn), lambda i, j: (0, j)),
        ],
        out_specs=pl.BlockSpec((tm, tn), lambda i, j: (i, j)),
        scratch_shapes=[pltpu.VMEM((tm, d), BF16)],
        compiler_params=_cparams(("parallel", "arbitrary")),
        name="in_proj",
    )(x2d, g, w)


def _compress_kernel(zk_ref, zv_ref, kpos_ref, kw1_ref, kw2_ref, vpos_ref, vw1_ref, vw2t_ref, kc_ref, vct_ref,
                     xf_ref):
    seq = zk_ref.shape[0]
    ncp = seq // CMP_STRIDE
    hd = HEAD_DIM
    half = CMP_STRIDE * hd

    def half_windows(z_ref):
        xf_ref[...] = z_ref[...].astype(F32)
        lines = [xf_ref[pl.ds(r, ncp, stride=CMP_STRIDE), :].astype(BF16) for r in range(CMP_STRIDE)]
        return [jnp.concatenate([ln[:, gg * hd:(gg + 1) * hd] for ln in lines], axis=1) for gg in range(LANES // hd)]

    def hidden(x, pos_ref, w1_ref):
        w1 = w1_ref[...]
        ha = jnp.dot(x, w1[:half], preferred_element_type=F32)
        hb = jnp.dot(x, w1[half:], preferred_element_type=F32)
        hb = pltpu.roll(hb, hb.shape[0] - 1, axis=0)
        pc = jnp.dot(pos_ref[...], w1, preferred_element_type=F32)[0:1]
        return _gelu_tanh(ha + hb + pc).astype(BF16)

    for gg, x in enumerate(half_windows(zk_ref)):
        hk = hidden(x, kpos_ref, kw1_ref)
        kc_ref[0, gg] = jnp.dot(hk, kw2_ref[...], preferred_element_type=F32).astype(kc_ref.dtype)
    for gg, x in enumerate(half_windows(zv_ref)):
        hv = hidden(x, vpos_ref, vw1_ref)
        vct_ref[0, gg] = lax.dot_general(vw2t_ref[...], hv, (((1,), (1,)), ((), ())),
                                         preferred_element_type=F32).astype(vct_ref.dtype)


def _compress(z, batch, seq, kpos, kw1, kw2, vpos, vw1, vw2t):
    g, hd = NSA_KV_GROUPS, HEAD_DIM
    gpb = LANES // hd
    ncp = seq // CMP_STRIDE
    full = lambda a: pl.BlockSpec(a.shape, lambda i, j: (0,) * a.ndim)
    return pl.pallas_call(
        _compress_kernel,
        out_shape=(jax.ShapeDtypeStruct((batch, g, ncp, hd), BF16),
                   jax.ShapeDtypeStruct((batch, g, hd, ncp), BF16)),
        grid=(batch, g // gpb),
        in_specs=[
            pl.BlockSpec((seq, LANES), lambda i, j: (i, Z_KC // LANES + j)),
            pl.BlockSpec((seq, LANES), lambda i, j: (i, Z_VC // LANES + j)),
            full(kpos), full(kw1), full(kw2), full(vpos), full(vw1), full(vw2t),
        ],
        out_specs=(pl.BlockSpec((1, gpb, ncp, hd), lambda i, j: (i, j, 0, 0)),
                   pl.BlockSpec((1, gpb, hd, ncp), lambda i, j: (i, j, 0, 0))),
        scratch_shapes=[pltpu.VMEM((seq, LANES), F32)],
        compiler_params=_cparams(("parallel", "parallel")),
        name="compress",
    )(z, z, kpos, kw1, kw2, vpos, vw1, vw2t)


def _nsa_kernel(zq_ref, zkk_ref, zvv_ref, zg_ref, kc_ref, vct_ref, o_ref, kcomb_ref, vs_ref, vw_ref, gate_ref,
                sa_ref, sb_ref):
    grp = pl.program_id(1)
    qi = pl.program_id(2)
    tq, hd, hpg = NSA_TQ, HEAD_DIM, NSA_HPG
    nq = tq * hpg
    seq = kcomb_ref.shape[0]
    ncp = kc_ref.shape[2]
    nsb = seq // SLC_BLOCK
    blk_shift = SLC_BLOCK.bit_length() - 1

    @pl.when(qi == 0)
    def _():
        kcomb_ref[:, 0:2 * hd] = zkk_ref[...]
        blk_of_row = lax.broadcasted_iota(jnp.int32, (seq, 2 * hd), 0) >> blk_shift
        lane = lax.broadcasted_iota(jnp.int32, (seq, 2 * hd), 1)
        kcomb_ref[:, 2 * hd:] = jnp.where(blk_of_row == lane, 1.0, 0.0).astype(BF16)
        ones = jnp.ones((seq, hd), BF16)
        vs_ref[...] = jnp.concatenate([zvv_ref[:, 0:hd], ones], axis=1)
        vw_ref[...] = jnp.concatenate([zvv_ref[:, hd:2 * hd], ones], axis=1)

    qt = zq_ref[...].T
    q = jnp.concatenate([qt[h * hd:(h + 1) * hd, :] for h in range(hpg)], axis=1)
    q = (q.astype(F32) * (ATTN_SCALE * LOG2E)).astype(BF16)
    zero = jnp.zeros((hd, nq), BF16)
    t_lane = qi * tq + (lax.broadcasted_iota(jnp.int32, (1, nq), 1) & (tq - 1))
    tn_dims = (((0,), (0,)), ((), ()))
    ck = NSA_CK

    sa_ref[...] = jnp.dot(kcomb_ref[0:ck, 0:2 * hd], jnp.concatenate([q, zero], axis=0), preferred_element_type=F32)

    sc = jnp.dot(kc_ref[0, 0], q, preferred_element_type=F32)
    cmp_end = lax.broadcasted_iota(jnp.int32, (ncp, 1), 0) * CMP_STRIDE + (CMP_BLOCK - 1)
    cmask = cmp_end <= t_lane
    sc = jnp.where(cmask, sc, NEG_INF)
    pc = jnp.exp2(sc - jnp.max(sc, axis=0, keepdims=True))
    lc = jnp.sum(pc, axis=0, keepdims=True)
    pc = pc * jnp.where(t_lane >= CMP_BLOCK - 1, 1.0 / lc, 0.0)
    o_cmp = jnp.dot(vct_ref[0, 0], pc.astype(BF16), preferred_element_type=F32)

    psum = pc[:, 0:tq]
    for h in range(1, hpg):
        psum = psum + pc[:, h * tq:(h + 1) * tq]
    jrow = lax.broadcasted_iota(jnp.int32, (nsb, ncp), 0)
    crel = lax.broadcasted_iota(jnp.int32, (nsb, ncp), 1) - jrow * (SLC_BLOCK // CMP_STRIDE)
    ovt = jnp.where((crel >= 0) & (crel <= 2), 1.0, jnp.where((crel == -1) | (crel == 3), 0.5, 0.0)).astype(F32)
    imp = jnp.dot(ovt, psum, preferred_element_type=F32, precision=lax.Precision.HIGHEST)
    jblk = lax.broadcasted_iota(jnp.int32, (nsb, tq), 0)
    blk = (qi * tq + lax.broadcasted_iota(jnp.int32, (1, tq), 1)) >> blk_shift
    forced = (jblk == 0) | (jblk == blk) | (jblk == blk - 1)
    n_forced = 3
    score = jnp.where(forced, -2.0, jnp.where(jblk <= blk, imp, -1.0))
    selb = jnp.where(forced, 0.0, NEG_INF)
    for _ in range(min(SLC_TOPN, nsb) - n_forced):
        top = jnp.max(score, axis=0, keepdims=True)
        first = jnp.min(jnp.where(score == top, jblk, nsb), axis=0, keepdims=True)
        pick = jblk == first
        selb = jnp.where(pick, 0.0, selb)
        score = jnp.where(pick, -2.0, score)
    bias0 = jnp.concatenate([jnp.broadcast_to(selb[j:j + 1, :], (SLC_BLOCK, tq)) for j in range(ck // SLC_BLOCK)],
                            axis=0)
    sa_ref[...] = sa_ref[...] + jnp.concatenate([bias0] * hpg, axis=1)
    selb = selb.astype(BF16)

    pad = jnp.zeros((2 * hd - nsb, nq), BF16)
    qa = jnp.concatenate([q, zero, jnp.concatenate([selb] * hpg, axis=1), pad], axis=0)

    def slc_scores(c):
        return jnp.dot(kcomb_ref[pl.ds(pl.multiple_of(c * ck, ck), ck), :], qa, preferred_element_type=F32)

    n_full = (qi * tq) // ck

    def slc_update(buf, c, carry, causal):
        def scores():
            if not causal:
                return buf[...]
            return jnp.where(c * ck + lax.broadcasted_iota(jnp.int32, (ck, 1), 0) <= t_lane, buf[...], NEG_INF)

        m, l, acc = carry
        m_new = jnp.maximum(m, jnp.max(scores(), axis=0, keepdims=True))
        alpha = jnp.exp2(m - m_new)
        p = jnp.exp2(scores() - m_new).astype(BF16)
        v = vs_ref[pl.ds(pl.multiple_of(c * ck, ck), ck), :]
        pv = lax.dot_general(v, p, tn_dims, preferred_element_type=F32)
        return m_new, alpha * l + pv[hd:hd + 1], acc * alpha + pv[0:hd]

    def slc_pair(i, carry):
        sb_ref[...] = slc_scores(2 * i + 1)
        carry = slc_update(sa_ref, 2 * i, carry, False)
        sa_ref[...] = slc_scores(2 * i + 2)
        return slc_update(sb_ref, 2 * i + 1, carry, False)

    def tail_two(carry):
        sb_ref[...] = slc_scores(n_full)
        return slc_update(sb_ref, n_full, slc_update(sa_ref, n_full - 1, carry, False), True)

    def tail_one(carry):
        return slc_update(sa_ref, n_full, carry, True)


    nwc = WINDOW // tq + 1
    t_lo = t_lane - WINDOW
    ks_w, vs_w = [], []
    for i in range(nwc):
        rows = pl.ds(pl.multiple_of(jnp.maximum(qi - (nwc - 1) + i, 0) * tq, tq), tq)
        ks_w.append(kcomb_ref[rows, 0:2 * hd])
        vs_w.append(vw_ref[rows, :])
    qw = jnp.concatenate([zero, q], axis=0)
    sw = jnp.dot(jnp.concatenate(ks_w, axis=0), qw, preferred_element_type=F32)
    sw_parts = []
    for i in range(nwc):
        chunk = qi - (nwc - 1) + i
        spos = chunk * tq + lax.broadcasted_iota(jnp.int32, (tq, 1), 0)
        s_i = sw[i * tq:(i + 1) * tq]
        if i == 0:
            sw_parts.append(jnp.where(jnp.where(spos >= 0, spos, -(1 << 24)) > t_lo, s_i, NEG_INF))
        elif i < nwc - 1:
            sw_parts.append(s_i + jnp.where(chunk >= 0, 0.0, NEG_INF))
        else:
            sw_parts.append(jnp.where(spos <= t_lane, s_i, NEG_INF))
    sw = jnp.concatenate(sw_parts, axis=0)
    pw = jnp.exp2(sw - jnp.max(sw, axis=0, keepdims=True))
    ow = lax.dot_general(jnp.concatenate(vs_w, axis=0), pw.astype(BF16), tn_dims, preferred_element_type=F32)
    o_win = ow[0:hd] * (1.0 / ow[hd:hd + 1])
    init = (jnp.full((1, nq), NEG_INF, F32), jnp.zeros((1, nq), F32), jnp.zeros((hd, nq), F32))
    carry = lax.fori_loop(0, n_full // 2, slc_pair, init)
    _, l_s, acc_s = lax.cond(n_full % 2 == 1, tail_two, tail_one, carry)
    o_slc = acc_s * (1.0 / l_s)

    gate_ref[...] = _sigmoid(zg_ref[...].astype(F32)).T

    def branch_gate(br):
        rows = [gate_ref[pl.ds(grp * (hpg * 3) + h * 3 + br, 1), :] for h in range(hpg)]
        return jnp.concatenate(rows, axis=1)

    o = branch_gate(0) * o_cmp + branch_gate(1) * o_slc + branch_gate(2) * o_win
    o_heads = jnp.concatenate([o[:, h * tq:(h + 1) * tq] for h in range(hpg)], axis=0)
    o_ref[...] = o_heads.T.astype(o_ref.dtype)


def _nsa_attention(z, kc, vct, batch, seq):
    n = z.shape[0]
    g, hd, tq = NSA_KV_GROUPS, HEAD_DIM, NSA_TQ
    nqt = seq // tq
    ncp = kc.shape[2]
    nsb = seq // SLC_BLOCK
    gw = NSA_HPG * hd
    assert seq % NSA_CK == 0 and NSA_CK % tq == 0 and WINDOW % tq == 0 and nsb <= 2 * hd
    tile = lambda i, j, k: i * nqt + k
    return pl.pallas_call(
        _nsa_kernel,
        out_shape=jax.ShapeDtypeStruct((n, NSA_Q_WIDTH), BF16),
        grid=(batch, g, nqt),
        in_specs=[
            pl.BlockSpec((tq, gw), lambda i, j, k: (tile(i, j, k), Z_Q // gw + j)),
            pl.BlockSpec((seq, 2 * hd), lambda i, j, k: (i, Z_KK // (2 * hd) + j)),
            pl.BlockSpec((seq, 2 * hd), lambda i, j, k: (i, Z_VV // (2 * hd) + j)),
            pl.BlockSpec((tq, LANES), lambda i, j, k: (tile(i, j, k), Z_GATE // LANES)),
            pl.BlockSpec((1, 1, ncp, hd), lambda i, j, k: (i, j, 0, 0)),
            pl.BlockSpec((1, 1, hd, ncp), lambda i, j, k: (i, j, 0, 0)),
        ],
        out_specs=pl.BlockSpec((tq, gw), lambda i, j, k: (tile(i, j, k), j)),
        scratch_shapes=[pltpu.VMEM((seq, 4 * hd), BF16), pltpu.VMEM((seq, 2 * hd), BF16),
                        pltpu.VMEM((seq, 2 * hd), BF16), pltpu.VMEM((LANES, tq), F32), pltpu.VMEM((NSA_CK, NSA_HPG * tq), F32),
                        pltpu.VMEM((NSA_CK, NSA_HPG * tq), F32)],
        compiler_params=_cparams(("parallel", "parallel", "arbitrary")),
        name="nsa_attn",
    )(z, z, z, z, kc, vct)


def _rglru_kernel(x_ref, y_ref, cw_ref, cb_ref, wa_ref, ba_ref, wx_ref, bx_ref, lam_ref, o_ref,
                  tail_ref, h_ref, a_ref, u_ref):
    tc = pl.program_id(2)
    tt, cb = x_ref.shape

    @pl.when(tc == 0)
    def _():
        tail_ref[...] = jnp.zeros_like(tail_ref)
        h_ref[...] = jnp.zeros_like(h_ref)

    x = x_ref[...].astype(F32)
    xe = jnp.concatenate([tail_ref[...], x], axis=0)
    tail_ref[...] = x[tt - SUBLANES:, :]
    cw = cw_ref[...]
    xc = cb_ref[...]
    for k in range(CONV_WIDTH):
        off = SUBLANES - (CONV_WIDTH - 1) + k
        xc = xc + cw[k:k + 1, :] * xe[off:off + tt, :]
    xcb = xc.astype(BF16)
    def gate(w_ref, b_ref):
        parts = [jnp.dot(xcb[:, c * LRU_WT:(c + 1) * LRU_WT], w_ref[c], preferred_element_type=F32)
                 for c in range(cb // LRU_WT)]
        return _sigmoid(jnp.concatenate(parts, axis=1) + b_ref[...])

    r = gate(wa_ref, ba_ref)
    ig = gate(wx_ref, bx_ref)
    nl = -lam_ref[...]
    softplus = jnp.maximum(nl, 0.0) + jnp.log1p(jnp.exp(-jnp.abs(nl)))
    log_a = (-LRU_C) * softplus * r
    a_ref[...] = jnp.exp(log_a)
    th = jnp.tanh(log_a)
    u_ref[...] = jnp.sqrt(-2.0 * th / (1.0 - th)) * (ig * xc)

    row = lax.broadcasted_iota(jnp.int32, (SUBLANES, cb), 0)

    def step(i, h):
        sl = pl.ds(pl.multiple_of(i * SUBLANES, SUBLANES), SUBLANES)
        a = a_ref[sl, :]
        u = u_ref[sl, :]
        for s in (1, 2, 4):
            a_s = jnp.where(row >= s, pltpu.roll(a, s, axis=0), 1.0)
            u_s = jnp.where(row >= s, pltpu.roll(u, s, axis=0), 0.0)
            u = a * u_s + u
            a = a * a_s
        hrows = a * h + u
        u_ref[sl, :] = hrows
        return hrows[SUBLANES - 1:SUBLANES, :]

    h_ref[...] = lax.fori_loop(0, tt // SUBLANES, step, h_ref[...], unroll=8)
    o_ref[...] = (u_ref[...] * _gelu_tanh(y_ref[...].astype(F32))).astype(o_ref.dtype)


def _rglru(z, batch, seq, cw, cbias, wa_bd, ba, wx_bd, bx, lam, tt=1024):
    n = z.shape[0]
    ncb = LRU_WIDTH // LRU_CB
    nt = seq // tt
    row = lambda i, j, k: i * nt + k
    vec = lambda r: pl.BlockSpec((r, LRU_CB), lambda i, j, k: (0, j))
    return pl.pallas_call(
        _rglru_kernel,
        out_shape=jax.ShapeDtypeStruct((n, LRU_WIDTH), BF16),
        grid=(batch, ncb, nt),
        in_specs=[
            pl.BlockSpec((tt, LRU_CB), lambda i, j, k: (row(i, j, k), Z_LX // LRU_CB + j)),
            pl.BlockSpec((tt, LRU_CB), lambda i, j, k: (row(i, j, k), Z_LY // LRU_CB + j)),
            vec(CONV_WIDTH), vec(1),
            pl.BlockSpec((LRU_CB // LRU_WT, LRU_WT, LRU_WT), lambda i, j, k: (j, 0, 0)), vec(1),
            pl.BlockSpec((LRU_CB // LRU_WT, LRU_WT, LRU_WT), lambda i, j, k: (j, 0, 0)), vec(1),
            vec(1),
        ],
        out_specs=pl.BlockSpec((tt, LRU_CB), lambda i, j, k: (row(i, j, k), j)),
        scratch_shapes=[pltpu.VMEM((SUBLANES, LRU_CB), F32), pltpu.VMEM((1, LRU_CB), F32),
                        pltpu.VMEM((tt, LRU_CB), F32), pltpu.VMEM((tt, LRU_CB), F32)],
        compiler_params=_cparams(("parallel", "parallel", "arbitrary")),
        name="rglru",
    )(z, z, cw, cbias, wa_bd, ba, wx_bd, bx, lam)


def _merge_kernel(o_ref, l_ref, mga_ref, mgb_ref, wn_ref, wl_ref, m_ref):
    ya = jnp.dot(o_ref[...], wn_ref[...], preferred_element_type=F32)
    yb = jnp.dot(l_ref[...], wl_ref[...], preferred_element_type=F32)
    m = _sigmoid(mga_ref[...].astype(F32)) * ya + _sigmoid(mgb_ref[...].astype(F32)) * yb
    m_ref[...] = m.astype(m_ref.dtype)


def _merge(o, lru, z, wn, wl, tm=512):
    n, d = o.shape[0], wn.shape[1]
    return pl.pallas_call(
        _merge_kernel,
        out_shape=jax.ShapeDtypeStruct((n, d), BF16),
        grid=(n // tm,),
        in_specs=[
            pl.BlockSpec((tm, o.shape[1]), lambda i: (i, 0)),
            pl.BlockSpec((tm, lru.shape[1]), lambda i: (i, 0)),
            pl.BlockSpec((tm, d), lambda i: (i, Z_MGA // d)),
            pl.BlockSpec((tm, d), lambda i: (i, Z_MGB // d)),
            pl.BlockSpec(wn.shape, lambda i: (0, 0)),
            pl.BlockSpec(wl.shape, lambda i: (0, 0)),
        ],
        out_specs=pl.BlockSpec((tm, d), lambda i: (i, 0)),
        compiler_params=_cparams(("parallel",)),
        name="merge",
    )(o, lru, z, z, wn, wl)


def _out_route_kernel(m_ref, x_ref, wo_ref, g_ref, wr_ref, br_ref, x1_ref, xn_ref, eid_ref, ew_ref, *, th):
    for r0 in range(0, m_ref.shape[0], th):
        _out_route_rows(r0, th, m_ref, x_ref, wo_ref, g_ref, wr_ref, br_ref, x1_ref, xn_ref, eid_ref, ew_ref)


def _out_route_rows(r0, tm, m_ref, x_ref, wo_ref, g_ref, wr_ref, br_ref, x1_ref, xn_ref, eid_ref, ew_ref):
    x1 = x_ref[r0:r0 + tm] + jnp.dot(m_ref[r0:r0 + tm], wo_ref[...], preferred_element_type=F32)
    x1_ref[r0:r0 + tm] = x1
    xn = _rms(x1, g_ref[...])
    _store_rows(xn_ref, r0, xn)
    lg = jnp.dot(xn, wr_ref[...], preferred_element_type=F32).T + br_ref[...]
    sub = lax.broadcasted_iota(jnp.int32, (SUBLANES, tm), 0)

    def first_argmax(v, vmax):
        return jnp.min(jnp.where(v == vmax, sub, SUBLANES), axis=0, keepdims=True)

    gl = jnp.where(sub < N_GROUPS, lg[0:SUBLANES], -jnp.inf)
    gmax = jnp.max(gl, axis=0, keepdims=True)
    ge = jnp.exp(gl - gmax)
    gprob = ge / jnp.sum(ge, axis=0, keepdims=True)
    g_val = jnp.max(gprob, axis=0, keepdims=True)
    g_idx = first_argmax(gprob, g_val)
    e_in = jnp.zeros((EXPERTS_PER_GROUP, tm), F32)
    for gi in range(N_GROUPS):
        lo = SUBLANES + gi * EXPERTS_PER_GROUP
        e_in = jnp.where(g_idx == gi, lg[lo:lo + EXPERTS_PER_GROUP], e_in)
    ee = jnp.exp(e_in - jnp.max(e_in, axis=0, keepdims=True))
    eprob = ee / jnp.sum(ee, axis=0, keepdims=True)
    v1 = jnp.max(eprob, axis=0, keepdims=True)
    i1 = first_argmax(eprob, v1)
    rest = jnp.where(sub == i1, -1.0, eprob)
    v2 = jnp.max(rest, axis=0, keepdims=True)
    i2 = first_argmax(rest, v2)
    den = v1 + v2
    eid = jnp.where(sub == 0, g_idx * EXPERTS_PER_GROUP + i1, g_idx * EXPERTS_PER_GROUP + i2)
    eid_ref[:, r0:r0 + tm] = eid
    ew_ref[:, r0:r0 + tm] = jnp.where(sub == 0, g_val * v1 / den, g_val * v2 / den)


def _out_route(m, x2d, wo, g, wr, br, tm=512, th=256):
    n, d = x2d.shape
    once = pl.Buffered(1)
    return pl.pallas_call(
        functools.partial(_out_route_kernel, th=th),
        out_shape=(jax.ShapeDtypeStruct((n, d), F32), jax.ShapeDtypeStruct((n * _row_pitch(d), LANES), F32),
                   jax.ShapeDtypeStruct((SUBLANES, n), jnp.int32), jax.ShapeDtypeStruct((SUBLANES, n), F32)),
        grid=(n // tm,),
        in_specs=[
            pl.BlockSpec((tm, d), lambda i: (i, 0)),
            pl.BlockSpec((tm, d), lambda i: (i, 0)),
            pl.BlockSpec((d, d), lambda i: (0, 0), pipeline_mode=once),
            pl.BlockSpec((1, d), lambda i: (0, 0)),
            pl.BlockSpec(wr.shape, lambda i: (0, 0), pipeline_mode=once),
            pl.BlockSpec(br.shape, lambda i: (0, 0)),
        ],
        out_specs=(pl.BlockSpec((tm, d), lambda i: (i, 0)), pl.BlockSpec((tm * _row_pitch(d), LANES), lambda i: (i, 0)),
                   pl.BlockSpec((SUBLANES, tm), lambda i: (0, i)), pl.BlockSpec((SUBLANES, tm), lambda i: (0, i))),
        compiler_params=_cparams(("parallel",)),
        name="out_route",
    )(m, x2d, wo, g, wr, br)


def _for_rows(cnt, fn, unroll=8):
    sh = unroll.bit_length() - 1

    def group(gidx, c):
        for u in range(unroll):
            fn(gidx * unroll + u)
        return c

    def single(r, c):
        fn(r)
        return c

    lax.fori_loop(0, cnt >> sh, group, 0)
    lax.fori_loop((cnt >> sh) << sh, cnt, single, 0)


def _moe_kernel(nvalid_ref, texp_ref, cnt_ref, tok_ref, tokn_ref, dst_ref, xn_hbm, wg_ref, wu_ref, wd_ref, y_hbm,
                xbuf, ybuf, gsem, ssem):
    i = pl.program_id(0)
    nv = nvalid_ref[0]
    slot = i % 2
    d = wg_ref.shape[1]
    k = _row_pitch(d)
    tm = xbuf.shape[0] // (2 * k)

    def lines(ref, first):
        return ref.at[pl.ds(first, k), :]

    def gather_row(idx_ref, s):
        def fn(r):
            pltpu.make_async_copy(lines(xn_hbm, idx_ref[0, 0, r]), lines(xbuf, (s * tm + r) * k), gsem.at[s]).start()
        return fn

    def gather_wait_row(s):
        def fn(r):
            pltpu.make_async_copy(lines(xn_hbm, 0), lines(xbuf, (s * tm + r) * k), gsem.at[s]).wait()
        return fn

    def scatter_row(s):
        def fn(r):
            pltpu.make_async_copy(lines(ybuf, (s * tm + r) * k), lines(y_hbm, dst_ref[0, 0, r]), ssem.at[s]).start()
        return fn

    def scatter_wait_row(s):
        def fn(r):
            pltpu.make_async_copy(lines(ybuf, (s * tm + r) * k), lines(y_hbm, 0), ssem.at[s]).wait()
        return fn

    @pl.when(i == 0)
    def _():
        xbuf[...] = jnp.zeros_like(xbuf)
        _for_rows(cnt_ref[0], gather_row(tok_ref, 0))

    @pl.when(i + 1 < nv)
    def _():
        _for_rows(cnt_ref[i + 1], gather_row(tokn_ref, 1 - slot))

    @pl.when(i < nv)
    def _():
        _for_rows(cnt_ref[i], gather_wait_row(slot))
        x = _load_rows(xbuf, slot * tm, tm, d).astype(BF16)
        gp = jnp.dot(x, wg_ref[0], preferred_element_type=F32)
        up = jnp.dot(x, wu_ref[0], preferred_element_type=F32)
        hid = (gp * _sigmoid(gp) * up).astype(BF16)
        y = jnp.dot(hid, wd_ref[0], preferred_element_type=F32)

        @pl.when(i >= 2)
        def _():
            _for_rows(cnt_ref[i - 2], scatter_wait_row(slot))

        _store_rows(ybuf, slot * tm, y)
        _for_rows(cnt_ref[i], scatter_row(slot))

    @pl.when(i == nv - 1)
    def _():
        _for_rows(cnt_ref[i], scatter_wait_row(slot))

        @pl.when(nv >= 2)
        def _():
            _for_rows(cnt_ref[i - 1], scatter_wait_row(1 - slot))


def _moe(nvalid, texp, cnt, tok, dst, xn, wg, wu, wd, n_rows_out):
    nt, _, tm = tok.shape
    d, de = wg.shape[1], wg.shape[2]
    k = _row_pitch(d)
    grid_spec = pltpu.PrefetchScalarGridSpec(
        num_scalar_prefetch=3,
        grid=(nt,),
        in_specs=[
            pl.BlockSpec((1, 1, tm), lambda i, nv, te, ct: (i, 0, 0), memory_space=pltpu.SMEM),
            pl.BlockSpec((1, 1, tm), lambda i, nv, te, ct: (jnp.minimum(i + 1, nt - 1), 0, 0),
                         memory_space=pltpu.SMEM),
            pl.BlockSpec((1, 1, tm), lambda i, nv, te, ct: (i, 0, 0), memory_space=pltpu.SMEM),
            pl.BlockSpec(memory_space=pl.ANY),
            pl.BlockSpec((1, d, de), lambda i, nv, te, ct: (te[i], 0, 0)),
            pl.BlockSpec((1, d, de), lambda i, nv, te, ct: (te[i], 0, 0)),
            pl.BlockSpec((1, de, d), lambda i, nv, te, ct: (te[i], 0, 0)),
        ],
        out_specs=pl.BlockSpec(memory_space=pl.ANY),
        scratch_shapes=[pltpu.VMEM((2 * tm * k, LANES), F32), pltpu.VMEM((2 * tm * k, LANES), F32),
                        pltpu.SemaphoreType.DMA((2,)), pltpu.SemaphoreType.DMA((2,))],
    )
    return pl.pallas_call(
        _moe_kernel,
        out_shape=jax.ShapeDtypeStruct((n_rows_out * k, LANES), F32),
        grid_spec=grid_spec,
        compiler_params=_cparams(("arbitrary",)),
        name="moe",
    )(nvalid, texp, cnt, tok, tok, dst, xn, wg, wu, wd)


def _moe_plan(eid, n, tm):
    e_flat = eid.reshape(-1)
    npairs = e_flat.shape[0]
    experts = jnp.arange(N_EXPERTS, dtype=jnp.int32)
    counts = jnp.sum((e_flat[:, None] == experts[None, :]).astype(jnp.int32), axis=0)
    padded = ((counts + tm - 1) // tm) * tm
    ends = jnp.cumsum(padded)
    offs = ends - padded
    n_rows = npairs + N_EXPERTS * tm
    nt = n_rows // tm
    fill = jnp.arange(n_rows - npairs, dtype=jnp.int32)
    fill_expert = jnp.sum((fill[:, None] >= jnp.cumsum(padded - counts)[None, :]).astype(jnp.int32), axis=1)
    keys = jnp.concatenate([e_flat * (2 * npairs) + jnp.arange(npairs, dtype=jnp.int32),
                            fill_expert * (2 * npairs) + npairs])
    skeys = jnp.sort(keys)
    low = skeys % (2 * npairs)
    dst = jnp.where(low < npairs, low, 0)
    tok = dst % n
    tile_start = jnp.arange(nt, dtype=jnp.int32) * tm
    texp = jnp.minimum(jnp.sum((tile_start[:, None] >= ends[None, :]).astype(jnp.int32), axis=1), N_EXPERTS - 1)
    cnt = jnp.clip(offs[texp] + counts[texp] - tile_start, 0, tm)
    nvalid = (ends[-1] // tm).astype(jnp.int32).reshape(1)
    return (nvalid, texp.astype(jnp.int32), cnt.astype(jnp.int32), tok.reshape(nt, 1, tm), dst.reshape(nt, 1, tm),
            npairs)


def _ple_final_kernel(x1_ref, y0_ref, y1_ref, w_ref, p_ref, gp_ref, wpg_ref, wp_ref, gf_ref, o_ref, *, th):
    d = x1_ref.shape[1]
    for r0 in range(0, x1_ref.shape[0], th):
        w = w_ref[r0:r0 + th]
        x2 = x1_ref[r0:r0 + th] + w[:, 0:1] * _load_rows(y0_ref, r0, th, d) + w[:, 1:2] * _load_rows(y1_ref, r0, th, d)
        hn = _rms(x2, gp_ref[...]).astype(BF16)
        gate = _sigmoid(jnp.dot(hn, wpg_ref[...], preferred_element_type=F32))
        pe = jnp.dot(p_ref[r0:r0 + th].astype(BF16), wp_ref[...], preferred_element_type=F32)
        x3 = x2 + gate * pe
        o_ref[r0:r0 + th] = _rms(x3, gf_ref[...])


def _ple_final(x1, ypairs, wcols, p2d, gp, wpg, wp, gf, tm=512, th=256):
    n, d = x1.shape
    nb = n // tm
    k = _row_pitch(d)
    return pl.pallas_call(
        functools.partial(_ple_final_kernel, th=th),
        out_shape=jax.ShapeDtypeStruct((n, d), F32),
        grid=(nb,),
        in_specs=[
            pl.BlockSpec((tm, d), lambda i: (i, 0)),
            pl.BlockSpec((tm * k, LANES), lambda i: (i, 0)),
            pl.BlockSpec((tm * k, LANES), lambda i: (nb + i, 0)),
            pl.BlockSpec((tm, wcols.shape[1]), lambda i: (i, 0)),
            pl.BlockSpec((tm, p2d.shape[1]), lambda i: (i, 0)),
            pl.BlockSpec((1, d), lambda i: (0, 0)),
            pl.BlockSpec((d, d), lambda i: (0, 0), pipeline_mode=pl.Buffered(1)),
            pl.BlockSpec(wp.shape, lambda i: (0, 0)),
            pl.BlockSpec((1, d), lambda i: (0, 0)),
        ],
        out_specs=pl.BlockSpec((tm, d), lambda i: (i, 0)),
        compiler_params=_cparams(("parallel",)),
        name="ple_final",
    )(x1, ypairs, ypairs, wcols, p2d, gp, wpg, wp, gf)


def _block_diag(w, per):
    nb, bw, _ = w.shape
    w = w.reshape(nb // per, per, bw, bw)
    eye = jnp.eye(per, dtype=w.dtype)
    return jnp.einsum("cpij,pq->cpiqj", w, eye).reshape(nb // per, per * bw, per * bw)


def _regroup_kernel(wt_ref, o_ref):
    d = Z_MGB - Z_MGA
    c = np.cumsum((0, NSA_Q_WIDTH) + (NSA_KV_WIDTH,) * 6 + (3 * NSA_HEADS, LRU_WIDTH, LRU_WIDTH, d, d))
    seg = lambda k: wt_ref[int(c[k]):int(c[k + 1]), :]
    q, k_c, v_c, k_s, v_s, k_w, v_w, gates, lru_x, lru_y, mg_a, mg_b = (seg(k) for k in range(12))

    def pair(a, b):
        parts = []
        for g in range(NSA_KV_GROUPS):
            parts += [a[g * HEAD_DIM:(g + 1) * HEAD_DIM], b[g * HEAD_DIM:(g + 1) * HEAD_DIM]]
        return jnp.concatenate(parts, axis=0)

    pad = jnp.zeros((Z_WIDTH - int(c[-1]), wt_ref.shape[1]), wt_ref.dtype)
    rows = [mg_a, mg_b, q, pair(k_s, k_w), pair(v_s, v_w), k_c, v_c, lru_x, lru_y, gates, pad]
    o_ref[...] = jnp.concatenate(rows, axis=0).T.astype(o_ref.dtype)


def _regroup_w_in(w_in_t, tr=256):
    nw, d = w_in_t.shape
    return pl.pallas_call(
        _regroup_kernel,
        out_shape=jax.ShapeDtypeStruct((d, Z_WIDTH), BF16),
        grid=(d // tr,),
        in_specs=[pl.BlockSpec((nw, tr), lambda i: (0, i))],
        out_specs=pl.BlockSpec((tr, Z_WIDTH), lambda i: (i, 0)),
        compiler_params=_cparams(("parallel",)),
        name="regroup_w_in",
    )(w_in_t)


def _layer(x2d, batch, seq, ln_mix, w_in, cmp_k_pos, cmp_k_w1, cmp_k_w2, cmp_v_pos, cmp_v_w1, cmp_v_w2, conv_w,
           conv_b, lru_wa, lru_ba, lru_wx, lru_bx, lru_lambda, w_nsa_up, w_lru_up, w_out, ln_ffn, w_grp, b_grp, w_exp,
           b_exp, w_gate, w_up, w_down):
    n, d = x2d.shape
    g, hd = NSA_KV_GROUPS, HEAD_DIM

    z = _in_proj(x2d, ln_mix.reshape(1, d), _regroup_w_in(w_in.T))

    pos8 = lambda pos: jnp.broadcast_to(pos.reshape(1, -1), (SUBLANES, pos.size)).astype(BF16)
    kc, vct = _compress(z, batch, seq, pos8(cmp_k_pos), cmp_k_w1.astype(BF16), cmp_k_w2.astype(BF16),
                        pos8(cmp_v_pos), cmp_v_w1.astype(BF16), cmp_v_w2.T.astype(BF16))
    o = _nsa_attention(z, kc, vct, batch, seq)

    per = LRU_WT // LRU_BW
    lru = _rglru(z, batch, seq, conv_w, conv_b.reshape(1, -1), _block_diag(lru_wa, per).astype(BF16),
                 lru_ba.reshape(1, -1), _block_diag(lru_wx, per).astype(BF16), lru_bx.reshape(1, -1),
                 lru_lambda.reshape(1, -1))

    merged = _merge(o, lru, z, w_nsa_up.astype(BF16), w_lru_up.astype(BF16))

    wr = jnp.zeros((d, LANES), F32).at[:, 0:N_GROUPS].set(w_grp).at[:, SUBLANES:SUBLANES + N_EXPERTS].set(w_exp)
    br = jnp.zeros((LANES, 1), F32).at[0:N_GROUPS, 0].set(b_grp).at[SUBLANES:SUBLANES + N_EXPERTS, 0].set(b_exp)
    x1, xn, eid, ew = _out_route(merged, x2d, w_out.astype(BF16), ln_ffn.reshape(1, d), wr, br)

    nvalid, texp, cnt, tok, dst, n_rows_out = _moe_plan(eid[0:EXPERT_TOPK], n, MOE_TM)
    pitch = _row_pitch(d)
    ypairs = _moe(nvalid, texp, cnt, tok * pitch, dst * pitch, xn, w_gate, w_up, w_down, n_rows_out)

    return x1, ypairs, ew.T


def kernel(x, p, ln_mix, w_in, cmp_k_pos, cmp_k_w1, cmp_k_w2, cmp_v_pos, cmp_v_w1, cmp_v_w2, conv_w, conv_b, lru_wa, lru_ba, lru_wx, lru_bx, lru_lambda, w_nsa_up, w_lru_up, w_out, ln_ffn, w_grp, b_grp, w_exp, b_exp, w_gate, w_up, w_down, ln_ple, w_ple, w_ple_gate, ln_final):
    batch, seq, d = x.shape
    assert p.shape[0] == 1, "the final norm is fused into the (single) layer's last kernel"
    n = batch * seq
    x1, ypairs, wcols = _layer(
        x.reshape(n, d), batch, seq, ln_mix[0], w_in[0], cmp_k_pos[0], cmp_k_w1[0], cmp_k_w2[0],
        cmp_v_pos[0], cmp_v_w1[0], cmp_v_w2[0], conv_w[0], conv_b[0], lru_wa[0], lru_ba[0], lru_wx[0], lru_bx[0],
        lru_lambda[0], w_nsa_up[0], w_lru_up[0], w_out[0], ln_ffn[0], w_grp[0], b_grp[0], w_exp[0], b_exp[0],
        w_gate[0], w_up[0], w_down[0])
    out = _ple_final(x1, ypairs, wcols, p[0].reshape(n, -1), ln_ple[0].reshape(1, d),
                     w_ple_gate[0].astype(BF16), w_ple[0].astype(BF16), ln_final.reshape(1, d))
    return out.reshape(batch, seq, d)
```

```python
import functools

import numpy as np
import jax
import jax.numpy as jnp
from jax import lax
from jax.experimental import pallas as pl
from jax.experimental.pallas import tpu as pltpu

F32 = jnp.float32
BF16 = jnp.bfloat16

NSA_HEADS = 16
NSA_KV_GROUPS = 4
NSA_HPG = NSA_HEADS // NSA_KV_GROUPS
HEAD_DIM = 64
NSA_Q_WIDTH = NSA_HEADS * HEAD_DIM
NSA_KV_WIDTH = NSA_KV_GROUPS * HEAD_DIM
CMP_BLOCK = 32
CMP_STRIDE = 16
CMP_HIDDEN = 2 * HEAD_DIM
SLC_BLOCK = 64
SLC_TOPN = 16
WINDOW = 512
ATTN_SCALE = HEAD_DIM ** -0.5
NEG_INF = -1e30
LOG2E = float(np.log2(np.e))
LRU_WIDTH = 1024
LRU_BLOCKS = 16
LRU_BW = LRU_WIDTH // LRU_BLOCKS
CONV_WIDTH = 4
LRU_C = 8.0
N_GROUPS = 4
EXPERTS_PER_GROUP = 8
N_EXPERTS = N_GROUPS * EXPERTS_PER_GROUP
EXPERT_TOPK = 2
D_EXPERT = 512
EPS = 1e-6

LANES = 128
SUBLANES = 8
VMEM_LIMIT_BYTES = 56 * 1024 * 1024

Z_MGA = 0
Z_MGB = 2048
Z_Q = 4096
Z_KK = 5120
Z_VV = 5632
Z_KC = 6144
Z_VC = 6400
Z_LX = 6656
Z_LY = 7680
Z_GATE = 8704
Z_WIDTH = 9216
LRU_CB = 512
LRU_WT = 256
MOE_TM = 256
NSA_TQ = 256
NSA_CK = 512


def _cparams(sem, vmem=VMEM_LIMIT_BYTES):
    return pltpu.CompilerParams(dimension_semantics=sem, vmem_limit_bytes=vmem)


def _rms(x, g):
    return x * lax.rsqrt(jnp.mean(x * x, axis=-1, keepdims=True) + EPS) * g


def _gelu_tanh(x):
    return 0.5 * x * (1.0 + jnp.tanh(np.sqrt(2.0 / np.pi) * (x + 0.044715 * (x * x * x))))


def _sigmoid(x):
    return 1.0 / (1.0 + jnp.exp(-x))


def _row_pitch(d):
    k = d // LANES
    return k + 1 - (k % 2)


def _store_rows(ref, row0, val):
    rows, d = val.shape
    pitch = _row_pitch(d)
    for j in range(d // LANES):
        ref[pl.ds(row0 * pitch + j, rows, stride=pitch), :] = val[:, j * LANES:(j + 1) * LANES]
    for j in range(d // LANES, pitch):
        ref[pl.ds(row0 * pitch + j, rows, stride=pitch), :] = jnp.zeros((rows, LANES), val.dtype)


def _load_rows(ref, row0, rows, d):
    pitch = _row_pitch(d)
    return jnp.concatenate([ref[pl.ds(row0 * pitch + j, rows, stride=pitch), :] for j in range(d // LANES)], axis=1)


def _in_proj_kernel(x_ref, g_ref, w_ref, o_ref, h_ref):
    @pl.when(pl.program_id(1) == 0)
    def _():
        h_ref[...] = _rms(x_ref[...], g_ref[...]).astype(BF16)

    o_ref[...] = jnp.dot(h_ref[...], w_ref[...], preferred_element_type=F32).astype(o_ref.dtype)


def _in_proj(x2d, g, w, tm=1024, tn=2304):
    n, d = x2d.shape
    nw = w.shape[1]
    return pl.pallas_call(
        _in_proj_kernel,
        out_shape=jax.ShapeDtypeStruct((n, nw), BF16),
        grid=(n // tm, nw // tn),
        in_specs=[
            pl.BlockSpec((tm, d), lambda i, j: (i, 0)),
            pl.BlockSpec((1, d), lambda i, j: (0, 0)),
            pl.BlockSpec((d, tn), lambda i, j: (0, j)),
        ],
        out_specs=pl.BlockSpec((tm, tn), lambda i, j: (i, j)),
        scratch_shapes=[pltpu.VMEM((tm, d), BF16)],
        compiler_params=_cparams(("parallel", "arbitrary")),
        name="in_proj",
    )(x2d, g, w)


def _compress_kernel(zk_ref, zv_ref, kpos_ref, kw1_ref, kw2_ref, vpos_ref, vw1_ref, vw2t_ref, kc_ref, vct_ref,
                     xf_ref):
    seq = zk_ref.shape[0]
    ncp = seq // CMP_STRIDE
    hd = HEAD_DIM
    half = CMP_STRIDE * hd

    def half_windows(z_ref):
        xf_ref[...] = z_ref[...].astype(F32)
        lines = [xf_ref[pl.ds(r, ncp, stride=CMP_STRIDE), :].astype(BF16) for r in range(CMP_STRIDE)]
        return [jnp.concatenate([ln[:, gg * hd:(gg + 1) * hd] for ln in lines], axis=1) for gg in range(LANES // hd)]

    def hidden(x, pos_ref, w1_ref):
        w1 = w1_ref[...]
        ha = jnp.dot(x, w1[:half], preferred_element_type=F32)
        hb = jnp.dot(x, w1[half:], preferred_element_type=F32)
        hb = pltpu.roll(hb, hb.shape[0] - 1, axis=0)
        pc = jnp.dot(pos_ref[...], w1, preferred_element_type=F32)[0:1]
        return _gelu_tanh(ha + hb + pc).astype(BF16)

    for gg, x in enumerate(half_windows(zk_ref)):
        hk = hidden(x, kpos_ref, kw1_ref)
        kc_ref[0, gg] = jnp.dot(hk, kw2_ref[...], preferred_element_type=F32).astype(kc_ref.dtype)
    for gg, x in enumerate(half_windows(zv_ref)):
        hv = hidden(x, vpos_ref, vw1_ref)
        vct_ref[0, gg] = lax.dot_general(vw2t_ref[...], hv, (((1,), (1,)), ((), ())),
                                         preferred_element_type=F32).astype(vct_ref.dtype)


def _compress(z, batch, seq, kpos, kw1, kw2, vpos, vw1, vw2t):
    g, hd = NSA_KV_GROUPS, HEAD_DIM
    gpb = LANES // hd
    ncp = seq // CMP_STRIDE
    full = lambda a: pl.BlockSpec(a.shape, lambda i, j: (0,) * a.ndim)
    return pl.pallas_call(
        _compress_kernel,
        out_shape=(jax.ShapeDtypeStruct((batch, g, ncp, hd), BF16),
                   jax.ShapeDtypeStruct((batch, g, hd, ncp), BF16)),
        grid=(batch, g // gpb),
        in_specs=[
            pl.BlockSpec((seq, LANES), lambda i, j: (i, Z_KC // LANES + j)),
            pl.BlockSpec((seq, LANES), lambda i, j: (i, Z_VC // LANES + j)),
            full(kpos), full(kw1), full(kw2), full(vpos), full(vw1), full(vw2t),
        ],
        out_specs=(pl.BlockSpec((1, gpb, ncp, hd), lambda i, j: (i, j, 0, 0)),
                   pl.BlockSpec((1, gpb, hd, ncp), lambda i, j: (i, j, 0, 0))),
        scratch_shapes=[pltpu.VMEM((seq, LANES), F32)],
        compiler_params=_cparams(("parallel", "parallel")),
        name="compress",
    )(z, z, kpos, kw1, kw2, vpos, vw1, vw2t)


def _nsa_kernel(zq_ref, zkk_ref, zvv_ref, zg_ref, kc_ref, vct_ref, o_ref, kcomb_ref, vs_ref, vw_ref, gate_ref,
                sa_ref, sb_ref):
    grp = pl.program_id(1)
    qi = pl.program_id(2)
    tq, hd, hpg = NSA_TQ, HEAD_DIM, NSA_HPG
    nq = tq * hpg
    seq = kcomb_ref.shape[0]
    ncp = kc_ref.shape[2]
    nsb = seq // SLC_BLOCK
    blk_shift = SLC_BLOCK.bit_length() - 1

    @pl.when(qi == 0)
    def _():
        kcomb_ref[:, 0:2 * hd] = zkk_ref[...]
        blk_of_row = lax.broadcasted_iota(jnp.int32, (seq, 2 * hd), 0) >> blk_shift
        lane = lax.broadcasted_iota(jnp.int32, (seq, 2 * hd), 1)
        kcomb_ref[:, 2 * hd:] = jnp.where(blk_of_row == lane, 1.0, 0.0).astype(BF16)
        ones = jnp.ones((seq, hd), BF16)
        vs_ref[...] = jnp.concatenate([zvv_ref[:, 0:hd], ones], axis=1)
        vw_ref[...] = jnp.concatenate([zvv_ref[:, hd:2 * hd], ones], axis=1)

    qt = zq_ref[...].T
    q = jnp.concatenate([qt[h * hd:(h + 1) * hd, :] for h in range(hpg)], axis=1)
    q = (q.astype(F32) * (ATTN_SCALE * LOG2E)).astype(BF16)
    zero = jnp.zeros((hd, nq), BF16)
    t_lane = qi * tq + (lax.broadcasted_iota(jnp.int32, (1, nq), 1) & (tq - 1))
    tn_dims = (((0,), (0,)), ((), ()))
    ck = NSA_CK

    sa_ref[...] = jnp.dot(kcomb_ref[0:ck, 0:2 * hd], jnp.concatenate([q, zero], axis=0), preferred_element_type=F32)

    sc = jnp.dot(kc_ref[0, 0], q, preferred_element_type=F32)
    cmp_end = lax.broadcasted_iota(jnp.int32, (ncp, 1), 0) * CMP_STRIDE + (CMP_BLOCK - 1)
    cmask = cmp_end <= t_lane
    sc = jnp.where(cmask, sc, NEG_INF)
    pc = jnp.exp2(sc - jnp.max(sc, axis=0, keepdims=True))
    lc = jnp.sum(pc, axis=0, keepdims=True)
    pc = pc * jnp.where(t_lane >= CMP_BLOCK - 1, 1.0 / lc, 0.0)
    o_cmp = jnp.dot(vct_ref[0, 0], pc.astype(BF16), preferred_element_type=F32)

    psum = pc[:, 0:tq]
    for h in range(1, hpg):
        psum = psum + pc[:, h * tq:(h + 1) * tq]
    jrow = lax.broadcasted_iota(jnp.int32, (nsb, ncp), 0)
    crel = lax.broadcasted_iota(jnp.int32, (nsb, ncp), 1) - jrow * (SLC_BLOCK // CMP_STRIDE)
    ovt = jnp.where((crel >= 0) & (crel <= 2), 1.0, jnp.where((crel == -1) | (crel == 3), 0.5, 0.0)).astype(F32)
    imp = jnp.dot(ovt, psum, preferred_element_type=F32, precision=lax.Precision.HIGHEST)
    jblk = lax.broadcasted_iota(jnp.int32, (nsb, tq), 0)
    blk = (qi * tq + lax.broadcasted_iota(jnp.int32, (1, tq), 1)) >> blk_shift
    forced = (jblk == 0) | (jblk == blk) | (jblk == blk - 1)
    n_forced = 3
    score = jnp.where(forced, -2.0, jnp.where(jblk <= blk, imp, -1.0))
    selb = jnp.where(forced, 0.0, NEG_INF)
    for _ in range(min(SLC_TOPN, nsb) - n_forced):
        top = jnp.max(score, axis=0, keepdims=True)
        first = jnp.min(jnp.where(score == top, jblk, nsb), axis=0, keepdims=True)
        pick = jblk == first
        selb = jnp.where(pick, 0.0, selb)
        score = jnp.where(pick, -2.0, score)
    bias0 = jnp.concatenate([jnp.broadcast_to(selb[j:j + 1, :], (SLC_BLOCK, tq)) for j in range(ck // SLC_BLOCK)],
                            axis=0)
    sa_ref[...] = sa_ref[...] + jnp.concatenate([bias0] * hpg, axis=1)
    selb = selb.astype(BF16)

    pad = jnp.zeros((2 * hd - nsb, nq), BF16)
    qa = jnp.concatenate([q, zero, jnp.concatenate([selb] * hpg, axis=1), pad], axis=0)

    def slc_scores(c):
        return jnp.dot(kcomb_ref[pl.ds(pl.multiple_of(c * ck, ck), ck), :], qa, preferred_element_type=F32)

    n_full = (qi * tq) // ck

    def slc_update(buf, c, carry, causal):
        def scores():
            if not causal:
                return buf[...]
            return jnp.where(c * ck + lax.broadcasted_iota(jnp.int32, (ck, 1), 0) <= t_lane, buf[...], NEG_INF)

        m, l, acc = carry
        m_new = jnp.maximum(m, jnp.max(scores(), axis=0, keepdims=True))
        alpha = jnp.exp2(m - m_new)
        p = jnp.exp2(scores() - m_new).astype(BF16)
        v = vs_ref[pl.ds(pl.multiple_of(c * ck, ck), ck), :]
        pv = lax.dot_general(v, p, tn_dims, preferred_element_type=F32)
        return m_new, alpha * l + pv[hd:hd + 1], acc * alpha + pv[0:hd]

    def slc_pair(i, carry):
        sb_ref[...] = slc_scores(2 * i + 1)
        carry = slc_update(sa_ref, 2 * i, carry, False)
        sa_ref[...] = slc_scores(2 * i + 2)
        return slc_update(sb_ref, 2 * i + 1, carry, False)

    def tail_two(carry):
        sb_ref[...] = slc_scores(n_full)
        return slc_update(sb_ref, n_full, slc_update(sa_ref, n_full - 1, carry, False), True)

    def tail_one(carry):
        return slc_update(sa_ref, n_full, carry, True)


    nwc = WINDOW // tq + 1
    t_lo = t_lane - WINDOW
    ks_w, vs_w = [], []
    for i in range(nwc):
        rows = pl.ds(pl.multiple_of(jnp.maximum(qi - (nwc - 1) + i, 0) * tq, tq), tq)
        ks_w.append(kcomb_ref[rows, 0:2 * hd])
        vs_w.append(vw_ref[rows, :])
    qw = jnp.concatenate([zero, q], axis=0)
    sw = jnp.dot(jnp.concatenate(ks_w, axis=0), qw, preferred_element_type=F32)
    sw_parts = []
    for i in range(nwc):
        chunk = qi - (nwc - 1) + i
        spos = chunk * tq + lax.broadcasted_iota(jnp.int32, (tq, 1), 0)
        s_i = sw[i * tq:(i + 1) * tq]
        if i == 0:
            sw_parts.append(jnp.where(jnp.where(spos >= 0, spos, -(1 << 24)) > t_lo, s_i, NEG_INF))
        elif i < nwc - 1:
            sw_parts.append(s_i + jnp.where(chunk >= 0, 0.0, NEG_INF))
        else:
            sw_parts.append(jnp.where(spos <= t_lane, s_i, NEG_INF))
    sw = jnp.concatenate(sw_parts, axis=0)
    pw = jnp.exp2(sw - jnp.max(sw, axis=0, keepdims=True))
    ow = lax.dot_general(jnp.concatenate(vs_w, axis=0), pw.astype(BF16), tn_dims, preferred_element_type=F32)
    o_win = ow[0:hd] * (1.0 / ow[hd:hd + 1])
    init = (jnp.full((1, nq), NEG_INF, F32), jnp.zeros((1, nq), F32), jnp.zeros((hd, nq), F32))
    carry = lax.fori_loop(0, n_full // 2, slc_pair, init)
    _, l_s, acc_s = lax.cond(n_full % 2 == 1, tail_two, tail_one, carry)
    o_slc = acc_s * (1.0 / l_s)

    gate_ref[...] = _sigmoid(zg_ref[...].astype(F32)).T

    def branch_gate(br):
        rows = [gate_ref[pl.ds(grp * (hpg * 3) + h * 3 + br, 1), :] for h in range(hpg)]
        return jnp.concatenate(rows, axis=1)

    o = branch_gate(0) * o_cmp + branch_gate(1) * o_slc + branch_gate(2) * o_win
    o_heads = jnp.concatenate([o[:, h * tq:(h + 1) * tq] for h in range(hpg)], axis=0)
    o_ref[...] = o_heads.T.astype(o_ref.dtype)


def _nsa_attention(z, kc, vct, batch, seq):
    n = z.shape[0]
    g, hd, tq = NSA_KV_GROUPS, HEAD_DIM, NSA_TQ
    nqt = seq // tq
    ncp = kc.shape[2]
    nsb = seq // SLC_BLOCK
    gw = NSA_HPG * hd
    assert seq % NSA_CK == 0 and NSA_CK % tq == 0 and WINDOW % tq == 0 and nsb <= 2 * hd
    tile = lambda i, j, k: i * nqt + k
    return pl.pallas_call(
        _nsa_kernel,
        out_shape=jax.ShapeDtypeStruct((n, NSA_Q_WIDTH), BF16),
        grid=(batch, g, nqt),
        in_specs=[
            pl.BlockSpec((tq, gw), lambda i, j, k: (tile(i, j, k), Z_Q // gw + j)),
            pl.BlockSpec((seq, 2 * hd), lambda i, j, k: (i, Z_KK // (2 * hd) + j)),
            pl.BlockSpec((seq, 2 * hd), lambda i, j, k: (i, Z_VV // (2 * hd) + j)),
            pl.BlockSpec((tq, LANES), lambda i, j, k: (tile(i, j, k), Z_GATE // LANES)),
            pl.BlockSpec((1, 1, ncp, hd), lambda i, j, k: (i, j, 0, 0)),
            pl.BlockSpec((1, 1, hd, ncp), lambda i, j, k: (i, j, 0, 0)),
        ],
        out_specs=pl.BlockSpec((tq, gw), lambda i, j, k: (tile(i, j, k), j)),
        scratch_shapes=[pltpu.VMEM((seq, 4 * hd), BF16), pltpu.VMEM((seq, 2 * hd), BF16),
                        pltpu.VMEM((seq, 2 * hd), BF16), pltpu.VMEM((LANES, tq), F32), pltpu.VMEM((NSA_CK, NSA_HPG * tq), F32),
                        pltpu.VMEM((NSA_CK, NSA_HPG * tq), F32)],
        compiler_params=_cparams(("parallel", "parallel", "arbitrary")),
        name="nsa_attn",
    )(z, z, z, z, kc, vct)


def _rglru_kernel(x_ref, y_ref, cw_ref, cb_ref, wa_ref, ba_ref, wx_ref, bx_ref, lam_ref, o_ref,
                  tail_ref, h_ref, a_ref, u_ref):
    tc = pl.program_id(2)
    tt, cb = x_ref.shape

    @pl.when(tc == 0)
    def _():
        tail_ref[...] = jnp.zeros_like(tail_ref)
        h_ref[...] = jnp.zeros_like(h_ref)

    x = x_ref[...].astype(F32)
    xe = jnp.concatenate([tail_ref[...], x], axis=0)
    tail_ref[...] = x[tt - SUBLANES:, :]
    cw = cw_ref[...]
    xc = cb_ref[...]
    for k in range(CONV_WIDTH):
        off = SUBLANES - (CONV_WIDTH - 1) + k
        xc = xc + cw[k:k + 1, :] * xe[off:off + tt, :]
    xcb = xc.astype(BF16)
    def gate(w_ref, b_ref):
        parts = [jnp.dot(xcb[:, c * LRU_WT:(c + 1) * LRU_WT], w_ref[c], preferred_element_type=F32)
                 for c in range(cb // LRU_WT)]
        return _sigmoid(jnp.concatenate(parts, axis=1) + b_ref[...])

    r = gate(wa_ref, ba_ref)
    ig = gate(wx_ref, bx_ref)
    nl = -lam_ref[...]
    softplus = jnp.maximum(nl, 0.0) + jnp.log1p(jnp.exp(-jnp.abs(nl)))
    log_a = (-LRU_C) * softplus * r
    a_ref[...] = jnp.exp(log_a)
    th = jnp.tanh(log_a)
    u_ref[...] = jnp.sqrt(-2.0 * th / (1.0 - th)) * (ig * xc)

    row = lax.broadcasted_iota(jnp.int32, (SUBLANES, cb), 0)

    def step(i, h):
        sl = pl.ds(pl.multiple_of(i * SUBLANES, SUBLANES), SUBLANES)
        a = a_ref[sl, :]
        u = u_ref[sl, :]
        for s in (1, 2, 4):
            a_s = jnp.where(row >= s, pltpu.roll(a, s, axis=0), 1.0)
            u_s = jnp.where(row >= s, pltpu.roll(u, s, axis=0), 0.0)
            u = a * u_s + u
            a = a * a_s
        hrows = a * h + u
        u_ref[sl, :] = hrows
        return hrows[SUBLANES - 1:SUBLANES, :]

    h_ref[...] = lax.fori_loop(0, tt // SUBLANES, step, h_ref[...], unroll=8)
    o_ref[...] = (u_ref[...] * _gelu_tanh(y_ref[...].astype(F32))).astype(o_ref.dtype)


def _rglru(z, batch, seq, cw, cbias, wa_bd, ba, wx_bd, bx, lam, tt=1024):
    n = z.shape[0]
    ncb = LRU_WIDTH // LRU_CB
    nt = seq // tt
    row = lambda i, j, k: i * nt + k
    vec = lambda r: pl.BlockSpec((r, LRU_CB), lambda i, j, k: (0, j))
    return pl.pallas_call(
        _rglru_kernel,
        out_shape=jax.ShapeDtypeStruct((n, LRU_WIDTH), BF16),
        grid=(batch, ncb, nt),
        in_specs=[
            pl.BlockSpec((tt, LRU_CB), lambda i, j, k: (row(i, j, k), Z_LX // LRU_CB + j)),
            pl.BlockSpec((tt, LRU_CB), lambda i, j, k: (row(i, j, k), Z_LY // LRU_CB + j)),
            vec(CONV_WIDTH), vec(1),
            pl.BlockSpec((LRU_CB // LRU_WT, LRU_WT, LRU_WT), lambda i, j, k: (j, 0, 0)), vec(1),
            pl.BlockSpec((LRU_CB // LRU_WT, LRU_WT, LRU_WT), lambda i, j, k: (j, 0, 0)), vec(1),
            vec(1),
        ],
        out_specs=pl.BlockSpec((tt, LRU_CB), lambda i, j, k: (row(i, j, k), j)),
        scratch_shapes=[pltpu.VMEM((SUBLANES, LRU_CB), F32), pltpu.VMEM((1, LRU_CB), F32),
                        pltpu.VMEM((tt, LRU_CB), F32), pltpu.VMEM((tt, LRU_CB), F32)],
        compiler_params=_cparams(("parallel", "parallel", "arbitrary")),
        name="rglru",
    )(z, z, cw, cbias, wa_bd, ba, wx_bd, bx, lam)


def _merge_kernel(o_ref, l_ref, mga_ref, mgb_ref, wn_ref, wl_ref, m_ref):
    ya = jnp.dot(o_ref[...], wn_ref[...], preferred_element_type=F32)
    yb = jnp.dot(l_ref[...], wl_ref[...], preferred_element_type=F32)
    m = _sigmoid(mga_ref[...].astype(F32)) * ya + _sigmoid(mgb_ref[...].astype(F32)) * yb
    m_ref[...] = m.astype(m_ref.dtype)


def _merge(o, lru, z, wn, wl, tm=512):
    n, d = o.shape[0], wn.shape[1]
    return pl.pallas_call(
        _merge_kernel,
        out_shape=jax.ShapeDtypeStruct((n, d), BF16),
        grid=(n // tm,),
        in_specs=[
            pl.BlockSpec((tm, o.shape[1]), lambda i: (i, 0)),
            pl.BlockSpec((tm, lru.shape[1]), lambda i: (i, 0)),
            pl.BlockSpec((tm, d), lambda i: (i, Z_MGA // d)),
            pl.BlockSpec((tm, d), lambda i: (i, Z_MGB // d)),
            pl.BlockSpec(wn.shape, lambda i: (0, 0)),
            pl.BlockSpec(wl.shape, lambda i: (0, 0)),
        ],
        out_specs=pl.BlockSpec((tm, d), lambda i: (i, 0)),
        compiler_params=_cparams(("parallel",)),
        name="merge",
    )(o, lru, z, z, wn, wl)


def _out_route_kernel(m_ref, x_ref, wo_ref, g_ref, wr_ref, br_ref, x1_ref, xn_ref, eid_ref, ew_ref, *, th):
    for r0 in range(0, m_ref.shape[0], th):
        _out_route_rows(r0, th, m_ref, x_ref, wo_ref, g_ref, wr_ref, br_ref, x1_ref, xn_ref, eid_ref, ew_ref)


def _out_route_rows(r0, tm, m_ref, x_ref, wo_ref, g_ref, wr_ref, br_ref, x1_ref, xn_ref, eid_ref, ew_ref):
    x1 = x_ref[r0:r0 + tm] + jnp.dot(m_ref[r0:r0 + tm], wo_ref[...], preferred_element_type=F32)
    x1_ref[r0:r0 + tm] = x1
    xn = _rms(x1, g_ref[...])
    _store_rows(xn_ref, r0, xn)
    lg = jnp.dot(xn, wr_ref[...], preferred_element_type=F32).T + br_ref[...]
    sub = lax.broadcasted_iota(jnp.int32, (SUBLANES, tm), 0)

    def first_argmax(v, vmax):
        return jnp.min(jnp.where(v == vmax, sub, SUBLANES), axis=0, keepdims=True)

    gl = jnp.where(sub < N_GROUPS, lg[0:SUBLANES], -jnp.inf)
    gmax = jnp.max(gl, axis=0, keepdims=True)
    ge = jnp.exp(gl - gmax)
    gprob = ge / jnp.sum(ge, axis=0, keepdims=True)
    g_val = jnp.max(gprob, axis=0, keepdims=True)
    g_idx = first_argmax(gprob, g_val)
    e_in = jnp.zeros((EXPERTS_PER_GROUP, tm), F32)
    for gi in range(N_GROUPS):
        lo = SUBLANES + gi * EXPERTS_PER_GROUP
        e_in = jnp.where(g_idx == gi, lg[lo:lo + EXPERTS_PER_GROUP], e_in)
    ee = jnp.exp(e_in - jnp.max(e_in, axis=0, keepdims=True))
    eprob = ee / jnp.sum(ee, axis=0, keepdims=True)
    v1 = jnp.max(eprob, axis=0, keepdims=True)
    i1 = first_argmax(eprob, v1)
    rest = jnp.where(sub == i1, -1.0, eprob)
    v2 = jnp.max(rest, axis=0, keepdims=True)
    i2 = first_argmax(rest, v2)
    den = v1 + v2
    eid = jnp.where(sub == 0, g_idx * EXPERTS_PER_GROUP + i1, g_idx * EXPERTS_PER_GROUP + i2)
    eid_ref[:, r0:r0 + tm] = eid
    ew_ref[:, r0:r0 + tm] = jnp.where(sub == 0, g_val * v1 / den, g_val * v2 / den)


def _out_route(m, x2d, wo, g, wr, br, tm=512, th=256):
    n, d = x2d.shape
    once = pl.Buffered(1)
    return pl.pallas_call(
        functools.partial(_out_route_kernel, th=th),
        out_shape=(jax.ShapeDtypeStruct((n, d), F32), jax.ShapeDtypeStruct((n * _row_pitch(d), LANES), F32),
                   jax.ShapeDtypeStruct((SUBLANES, n), jnp.int32), jax.ShapeDtypeStruct((SUBLANES, n), F32)),
        grid=(n // tm,),
        in_specs=[
            pl.BlockSpec((tm, d), lambda i: (i, 0)),
            pl.BlockSpec((tm, d), lambda i: (i, 0)),
            pl.BlockSpec((d, d), lambda i: (0, 0), pipeline_mode=once),
            pl.BlockSpec((1, d), lambda i: (0, 0)),
            pl.BlockSpec(wr.shape, lambda i: (0, 0), pipeline_mode=once),
            pl.BlockSpec(br.shape, lambda i: (0, 0)),
        ],
        out_specs=(pl.BlockSpec((tm, d), lambda i: (i, 0)), pl.BlockSpec((tm * _row_pitch(d), LANES), lambda i: (i, 0)),
                   pl.BlockSpec((SUBLANES, tm), lambda i: (0, i)), pl.BlockSpec((SUBLANES, tm), lambda i: (0, i))),
        compiler_params=_cparams(("parallel",)),
        name="out_route",
    )(m, x2d, wo, g, wr, br)


def _for_rows(cnt, fn, unroll=8):
    sh = unroll.bit_length() - 1

    def group(gidx, c):
        for u in range(unroll):
            fn(gidx * unroll + u)
        return c

    def single(r, c):
        fn(r)
        return c

    lax.fori_loop(0, cnt >> sh, group, 0)
    lax.fori_loop((cnt >> sh) << sh, cnt, single, 0)


def _moe_kernel(nvalid_ref, texp_ref, cnt_ref, wslot_ref, nxt_ref, tok_ref, tokn_ref, dst_ref, xn_hbm, wg_hbm, wu_hbm,
                wd_hbm, y_hbm, xbuf, ybuf, wgbuf, wubuf, wdbuf, gsem, ssem, wsem):
    i = pl.program_id(0)
    nv = nvalid_ref[0]
    slot = i % 2
    d = wg_hbm.shape[1]
    expert = texp_ref[i]
    ws = wslot_ref[i]
    run_start = jnp.logical_or(i == 0, texp_ref[jnp.maximum(i - 1, 0)] != expert)

    def weight_copies(e, s):
        return [pltpu.make_async_copy(src.at[e], buf.at[s], wsem.at[s])
                for src, buf in ((wg_hbm, wgbuf), (wu_hbm, wubuf), (wd_hbm, wdbuf))]
    k = _row_pitch(d)
    tm = xbuf.shape[0] // (2 * k)

    def lines(ref, first):
        return ref.at[pl.ds(first, k), :]

    def gather_row(idx_ref, s):
        def fn(r):
            pltpu.make_async_copy(lines(xn_hbm, idx_ref[0, 0, r]), lines(xbuf, (s * tm + r) * k), gsem.at[s]).start()
        return fn

    def gather_wait_row(s):
        def fn(r):
            pltpu.make_async_copy(lines(xn_hbm, 0), lines(xbuf, (s * tm + r) * k), gsem.at[s]).wait()
        return fn

    def scatter_row(s):
        def fn(r):
            pltpu.make_async_copy(lines(ybuf, (s * tm + r) * k), lines(y_hbm, dst_ref[0, 0, r]), ssem.at[s]).start()
        return fn

    def scatter_wait_row(s):
        def fn(r):
            pltpu.make_async_copy(lines(ybuf, (s * tm + r) * k), lines(y_hbm, 0), ssem.at[s]).wait()
        return fn

    @pl.when(i == 0)
    def _():
        xbuf[...] = jnp.zeros_like(xbuf)
        _for_rows(cnt_ref[0], gather_row(tok_ref, 0))
        for c in weight_copies(expert, 0):
            c.start()

    @pl.when(i + 1 < nv)
    def _():
        _for_rows(cnt_ref[i + 1], gather_row(tokn_ref, 1 - slot))

    @pl.when(jnp.logical_and(jnp.logical_and(i < nv, run_start), nxt_ref[i] >= 0))
    def _():
        for c in weight_copies(nxt_ref[i], 1 - ws):
            c.start()

    @pl.when(i < nv)
    def _():
        _for_rows(cnt_ref[i], gather_wait_row(slot))

        @pl.when(run_start)
        def _():
            for c in weight_copies(expert, ws):
                c.wait()

        x = _load_rows(xbuf, slot * tm, tm, d).astype(BF16)
        gp = jnp.dot(x, wgbuf[ws], preferred_element_type=F32)
        up = jnp.dot(x, wubuf[ws], preferred_element_type=F32)
        hid = (gp * _sigmoid(gp) * up).astype(BF16)
        y = jnp.dot(hid, wdbuf[ws], preferred_element_type=F32)

        @pl.when(i >= 2)
        def _():
            _for_rows(cnt_ref[i - 2], scatter_wait_row(slot))

        _store_rows(ybuf, slot * tm, y)
        _for_rows(cnt_ref[i], scatter_row(slot))

    @pl.when(i == nv - 1)
    def _():
        _for_rows(cnt_ref[i], scatter_wait_row(slot))

        @pl.when(nv >= 2)
        def _():
            _for_rows(cnt_ref[i - 1], scatter_wait_row(1 - slot))


def _moe(nvalid, texp, cnt, wslot, nxt, tok, dst, xn, wg, wu, wd, n_rows_out):
    nt, _, tm = tok.shape
    d, de = wg.shape[1], wg.shape[2]
    k = _row_pitch(d)
    grid_spec = pltpu.PrefetchScalarGridSpec(
        num_scalar_prefetch=5,
        grid=(nt,),
        in_specs=[
            pl.BlockSpec((1, 1, tm), lambda i, *_: (i, 0, 0), memory_space=pltpu.SMEM),
            pl.BlockSpec((1, 1, tm), lambda i, *_: (jnp.minimum(i + 1, nt - 1), 0, 0), memory_space=pltpu.SMEM),
            pl.BlockSpec((1, 1, tm), lambda i, *_: (i, 0, 0), memory_space=pltpu.SMEM),
            pl.BlockSpec(memory_space=pl.ANY),
            pl.BlockSpec(memory_space=pl.ANY),
            pl.BlockSpec(memory_space=pl.ANY),
            pl.BlockSpec(memory_space=pl.ANY),
        ],
        out_specs=pl.BlockSpec(memory_space=pl.ANY),
        scratch_shapes=[pltpu.VMEM((2 * tm * k, LANES), F32), pltpu.VMEM((2 * tm * k, LANES), F32),
                        pltpu.VMEM((2, d, de), F32), pltpu.VMEM((2, d, de), F32), pltpu.VMEM((2, de, d), F32),
                        pltpu.SemaphoreType.DMA((2,)), pltpu.SemaphoreType.DMA((2,)), pltpu.SemaphoreType.DMA((2,))],
    )
    return pl.pallas_call(
        _moe_kernel,
        out_shape=jax.ShapeDtypeStruct((n_rows_out * k, LANES), F32),
        grid_spec=grid_spec,
        compiler_params=_cparams(("arbitrary",)),
        name="moe",
    )(nvalid, texp, cnt, wslot, nxt, tok, tok, dst, xn, wg, wu, wd)


def _moe_plan(eid, n, tm):
    e_flat = eid.reshape(-1)
    npairs = e_flat.shape[0]
    experts = jnp.arange(N_EXPERTS, dtype=jnp.int32)
    counts = jnp.sum((e_flat[:, None] == experts[None, :]).astype(jnp.int32), axis=0)
    padded = ((counts + tm - 1) // tm) * tm
    ends = jnp.cumsum(padded)
    offs = ends - padded
    n_rows = npairs + N_EXPERTS * tm
    nt = n_rows // tm
    fill = jnp.arange(n_rows - npairs, dtype=jnp.int32)
    fill_expert = jnp.sum((fill[:, None] >= jnp.cumsum(padded - counts)[None, :]).astype(jnp.int32), axis=1)
    keys = jnp.concatenate([e_flat * (2 * npairs) + jnp.arange(npairs, dtype=jnp.int32),
                            fill_expert * (2 * npairs) + npairs])
    skeys = jnp.sort(keys)
    low = skeys % (2 * npairs)
    dst = jnp.where(low < npairs, low, 0)
    tok = dst % n
    tile_start = jnp.arange(nt, dtype=jnp.int32) * tm
    texp = jnp.minimum(jnp.sum((tile_start[:, None] >= ends[None, :]).astype(jnp.int32), axis=1), N_EXPERTS - 1)
    cnt = jnp.clip(offs[texp] + counts[texp] - tile_start, 0, tm)
    nvalid = (ends[-1] // tm).astype(jnp.int32).reshape(1)
    wslot = jnp.cumsum(jnp.concatenate([jnp.zeros((1,), jnp.int32), (texp[1:] != texp[:-1]).astype(jnp.int32)])) % 2
    live = jnp.where(tile_start < ends[-1], texp, N_EXPERTS)
    nxt = jnp.min(jnp.where(live[None, :] > texp[:, None], live[None, :], N_EXPERTS), axis=1)
    nxt = jnp.where(nxt < N_EXPERTS, nxt, -1)
    return (nvalid, texp.astype(jnp.int32), cnt.astype(jnp.int32), wslot.astype(jnp.int32), nxt.astype(jnp.int32),
            tok.reshape(nt, 1, tm), dst.reshape(nt, 1, tm), npairs)


def _ple_final_kernel(x1_ref, y0_ref, y1_ref, w_ref, p_ref, gp_ref, wpg_ref, wp_ref, gf_ref, o_ref, *, th):
    d = x1_ref.shape[1]
    for r0 in range(0, x1_ref.shape[0], th):
        w = w_ref[r0:r0 + th]
        x2 = x1_ref[r0:r0 + th] + w[:, 0:1] * _load_rows(y0_ref, r0, th, d) + w[:, 1:2] * _load_rows(y1_ref, r0, th, d)
        hn = _rms(x2, gp_ref[...]).astype(BF16)
        gate = _sigmoid(jnp.dot(hn, wpg_ref[...], preferred_element_type=F32))
        pe = jnp.dot(p_ref[r0:r0 + th].astype(BF16), wp_ref[...], preferred_element_type=F32)
        x3 = x2 + gate * pe
        o_ref[r0:r0 + th] = _rms(x3, gf_ref[...])


def _ple_final(x1, ypairs, wcols, p2d, gp, wpg, wp, gf, tm=512, th=256):
    n, d = x1.shape
    nb = n // tm
    k = _row_pitch(d)
    return pl.pallas_call(
        functools.partial(_ple_final_kernel, th=th),
        out_shape=jax.ShapeDtypeStruct((n, d), F32),
        grid=(nb,),
        in_specs=[
            pl.BlockSpec((tm, d), lambda i: (i, 0)),
            pl.BlockSpec((tm * k, LANES), lambda i: (i, 0)),
            pl.BlockSpec((tm * k, LANES), lambda i: (nb + i, 0)),
            pl.BlockSpec((tm, wcols.shape[1]), lambda i: (i, 0)),
            pl.BlockSpec((tm, p2d.shape[1]), lambda i: (i, 0)),
            pl.BlockSpec((1, d), lambda i: (0, 0)),
            pl.BlockSpec((d, d), lambda i: (0, 0), pipeline_mode=pl.Buffered(1)),
            pl.BlockSpec(wp.shape, lambda i: (0, 0)),
            pl.BlockSpec((1, d), lambda i: (0, 0)),
        ],
        out_specs=pl.BlockSpec((tm, d), lambda i: (i, 0)),
        compiler_params=_cparams(("parallel",)),
        name="ple_final",
    )(x1, ypairs, ypairs, wcols, p2d, gp, wpg, wp, gf)


def _block_diag(w, per):
    nb, bw, _ = w.shape
    w = w.reshape(nb // per, per, bw, bw)
    eye = jnp.eye(per, dtype=w.dtype)
    return jnp.einsum("cpij,pq->cpiqj", w, eye).reshape(nb // per, per * bw, per * bw)


def _regroup_kernel(wt_ref, o_ref):
    d = Z_MGB - Z_MGA
    c = np.cumsum((0, NSA_Q_WIDTH) + (NSA_KV_WIDTH,) * 6 + (3 * NSA_HEADS, LRU_WIDTH, LRU_WIDTH, d, d))
    seg = lambda k: wt_ref[int(c[k]):int(c[k + 1]), :]
    q, k_c, v_c, k_s, v_s, k_w, v_w, gates, lru_x, lru_y, mg_a, mg_b = (seg(k) for k in range(12))

    def pair(a, b):
        parts = []
        for g in range(NSA_KV_GROUPS):
            parts += [a[g * HEAD_DIM:(g + 1) * HEAD_DIM], b[g * HEAD_DIM:(g + 1) * HEAD_DIM]]
        return jnp.concatenate(parts, axis=0)

    pad = jnp.zeros((Z_WIDTH - int(c[-1]), wt_ref.shape[1]), wt_ref.dtype)
    rows = [mg_a, mg_b, q, pair(k_s, k_w), pair(v_s, v_w), k_c, v_c, lru_x, lru_y, gates, pad]
    o_ref[...] = jnp.concatenate(rows, axis=0).T.astype(o_ref.dtype)


def _regroup_w_in(w_in_t, tr=256):
    nw, d = w_in_t.shape
    return pl.pallas_call(
        _regroup_kernel,
        out_shape=jax.ShapeDtypeStruct((d, Z_WIDTH), BF16),
        grid=(d // tr,),
        in_specs=[pl.BlockSpec((nw, tr), lambda i: (0, i))],
        out_specs=pl.BlockSpec((tr, Z_WIDTH), lambda i: (i, 0)),
        compiler_params=_cparams(("parallel",)),
        name="regroup_w_in",
    )(w_in_t)


def _layer(x2d, batch, seq, ln_mix, w_in, cmp_k_pos, cmp_k_w1, cmp_k_w2, cmp_v_pos, cmp_v_w1, cmp_v_w2, conv_w,
           conv_b, lru_wa, lru_ba, lru_wx, lru_bx, lru_lambda, w_nsa_up, w_lru_up, w_out, ln_ffn, w_grp, b_grp, w_exp,
           b_exp, w_gate, w_up, w_down):
    n, d = x2d.shape
    g, hd = NSA_KV_GROUPS, HEAD_DIM

    z = _in_proj(x2d, ln_mix.reshape(1, d), _regroup_w_in(w_in.T))

    pos8 = lambda pos: jnp.broadcast_to(pos.reshape(1, -1), (SUBLANES, pos.size)).astype(BF16)
    kc, vct = _compress(z, batch, seq, pos8(cmp_k_pos), cmp_k_w1.astype(BF16), cmp_k_w2.astype(BF16),
                        pos8(cmp_v_pos), cmp_v_w1.astype(BF16), cmp_v_w2.T.astype(BF16))
    o = _nsa_attention(z, kc, vct, batch, seq)

    per = LRU_WT // LRU_BW
    lru = _rglru(z, batch, seq, conv_w, conv_b.reshape(1, -1), _block_diag(lru_wa, per).astype(BF16),
                 lru_ba.reshape(1, -1), _block_diag(lru_wx, per).astype(BF16), lru_bx.reshape(1, -1),
                 lru_lambda.reshape(1, -1))

    merged = _merge(o, lru, z, w_nsa_up.astype(BF16), w_lru_up.astype(BF16))

    wr = jnp.zeros((d, LANES), F32).at[:, 0:N_GROUPS].set(w_grp).at[:, SUBLANES:SUBLANES + N_EXPERTS].set(w_exp)
    br = jnp.zeros((LANES, 1), F32).at[0:N_GROUPS, 0].set(b_grp).at[SUBLANES:SUBLANES + N_EXPERTS, 0].set(b_exp)
    x1, xn, eid, ew = _out_route(merged, x2d, w_out.astype(BF16), ln_ffn.reshape(1, d), wr, br)

    nvalid, texp, cnt, wslot, nxt, tok, dst, n_rows_out = _moe_plan(eid[0:EXPERT_TOPK], n, MOE_TM)
    pitch = _row_pitch(d)
    ypairs = _moe(nvalid, texp, cnt, wslot, nxt, tok * pitch, dst * pitch, xn, w_gate, w_up, w_down, n_rows_out)

    return x1, ypairs, ew.T


def kernel(x, p, ln_mix, w_in, cmp_k_pos, cmp_k_w1, cmp_k_w2, cmp_v_pos, cmp_v_w1, cmp_v_w2, conv_w, conv_b, lru_wa, lru_ba, lru_wx, lru_bx, lru_lambda, w_nsa_up, w_lru_up, w_out, ln_ffn, w_grp, b_grp, w_exp, b_exp, w_gate, w_up, w_down, ln_ple, w_ple, w_ple_gate, ln_final):
    batch, seq, d = x.shape
    assert p.shape[0] == 1, "the final norm is fused into the (single) layer's last kernel"
    n = batch * seq
    x1, ypairs, wcols = _layer(
        x.reshape(n, d), batch, seq, ln_mix[0], w_in[0], cmp_k_pos[0], cmp_k_w1[0], cmp_k_w2[0],
        cmp_v_pos[0], cmp_v_w1[0], cmp_v_w2[0], conv_w[0], conv_b[0], lru_wa[0], lru_ba[0], lru_wx[0], lru_bx[0],
        lru_lambda[0], w_nsa_up[0], w_lru_up[0], w_out[0], ln_ffn[0], w_grp[0], b_grp[0], w_exp[0], b_exp[0],
        w_gate[0], w_up[0], w_down[0])
    out = _ple_final(x1, ypairs, wcols, p[0].reshape(n, -1), ln_ple[0].reshape(1, d),
                     w_ple_gate[0].astype(BF16), w_ple[0].astype(BF16), ln_final.reshape(1, d))
    return out.reshape(batch, seq, d)
```

```python
import functools

import numpy as np
import jax
import jax.numpy as jnp
from jax import lax
from jax.experimental import pallas as pl
from jax.experimental.pallas import tpu as pltpu

F32 = jnp.float32
BF16 = jnp.bfloat16

NSA_HEADS = 16
NSA_KV_GROUPS = 4
NSA_HPG = NSA_HEADS // NSA_KV_GROUPS
HEAD_DIM = 64
NSA_Q_WIDTH = NSA_HEADS * HEAD_DIM
NSA_KV_WIDTH = NSA_KV_GROUPS * HEAD_DIM
CMP_BLOCK = 32
CMP_STRIDE = 16
CMP_HIDDEN = 2 * HEAD_DIM
SLC_BLOCK = 64
SLC_TOPN = 16
WINDOW = 512
ATTN_SCALE = HEAD_DIM ** -0.5
NEG_INF = -1e30
LOG2E = float(np.log2(np.e))
LRU_WIDTH = 1024
LRU_BLOCKS = 16
LRU_BW = LRU_WIDTH // LRU_BLOCKS
CONV_WIDTH = 4
LRU_C = 8.0
N_GROUPS = 4
EXPERTS_PER_GROUP = 8
N_EXPERTS = N_GROUPS * EXPERTS_PER_GROUP
EXPERT_TOPK = 2
D_EXPERT = 512
EPS = 1e-6

LANES = 128
SUBLANES = 8
VMEM_LIMIT_BYTES = 56 * 1024 * 1024

Z_MGA = 0
Z_MGB = 2048
Z_Q = 4096
Z_KK = 5120
Z_VV = 5632
Z_KC = 6144
Z_VC = 6400
Z_LX = 6656
Z_LY = 7680
Z_GATE = 8704
Z_WIDTH = 9216
LRU_CB = 512
LRU_WT = 256
MOE_TM = 256
NSA_TQ = 256
NSA_CK = 512


def _cparams(sem, vmem=VMEM_LIMIT_BYTES):
    return pltpu.CompilerParams(dimension_semantics=sem, vmem_limit_bytes=vmem)


def _rms(x, g):
    return x * lax.rsqrt(jnp.mean(x * x, axis=-1, keepdims=True) + EPS) * g


def _gelu_tanh(x):
    return 0.5 * x * (1.0 + jnp.tanh(np.sqrt(2.0 / np.pi) * (x + 0.044715 * (x * x * x))))


def _sigmoid(x):
    return 1.0 / (1.0 + jnp.exp(-x))


def _row_pitch(d):
    k = d // LANES
    return k + 1 - (k % 2)


def _store_rows(ref, row0, val):
    rows, d = val.shape
    pitch = _row_pitch(d)
    for j in range(d // LANES):
        ref[pl.ds(row0 * pitch + j, rows, stride=pitch), :] = val[:, j * LANES:(j + 1) * LANES]
    for j in range(d // LANES, pitch):
        ref[pl.ds(row0 * pitch + j, rows, stride=pitch), :] = jnp.zeros((rows, LANES), val.dtype)


def _load_rows(ref, row0, rows, d):
    pitch = _row_pitch(d)
    return jnp.concatenate([ref[pl.ds(row0 * pitch + j, rows, stride=pitch), :] for j in range(d // LANES)], axis=1)


def _in_proj_kernel(x_ref, g_ref, w_ref, o_ref, h_ref):
    @pl.when(pl.program_id(1) == 0)
    def _():
        h_ref[...] = _rms(x_ref[...], g_ref[...]).astype(BF16)

    o_ref[...] = jnp.dot(h_ref[...], w_ref[...], preferred_element_type=F32).astype(o_ref.dtype)


def _in_proj(x2d, g, w, tm=1024, tn=2304):
    n, d = x2d.shape
    nw = w.shape[1]
    return pl.pallas_call(
        _in_proj_kernel,
        out_shape=jax.ShapeDtypeStruct((n, nw), BF16),
        grid=(n // tm, nw // tn),
        in_specs=[
            pl.BlockSpec((tm, d), lambda i, j: (i, 0)),
            pl.BlockSpec((1, d), lambda i, j: (0, 0)),
            pl.BlockSpec((d, tn), lambda i, j: (0, j)),
        ],
        out_specs=pl.BlockSpec((tm, tn), lambda i, j: (i, j)),
        scratch_shapes=[pltpu.VMEM((tm, d), BF16)],
        compiler_params=_cparams(("parallel", "arbitrary")),
        name="in_proj",
    )(x2d, g, w)


def _compress_kernel(zk_ref, zv_ref, kpos_ref, kw1_ref, kw2_ref, vpos_ref, vw1_ref, vw2t_ref, kc_ref, vct_ref,
                     xf_ref):
    seq = zk_ref.shape[0]
    ncp = seq // CMP_STRIDE
    hd = HEAD_DIM
    half = CMP_STRIDE * hd

    def half_windows(z_ref):
        xf_ref[...] = z_ref[...].astype(F32)
        lines = [xf_ref[pl.ds(r, ncp, stride=CMP_STRIDE), :].astype(BF16) for r in range(CMP_STRIDE)]
        return [jnp.concatenate([ln[:, gg * hd:(gg + 1) * hd] for ln in lines], axis=1) for gg in range(LANES // hd)]

    def hidden(x, pos_ref, w1_ref):
        w1 = w1_ref[...]
        ha = jnp.dot(x, w1[:half], preferred_element_type=F32)
        hb = jnp.dot(x, w1[half:], preferred_element_type=F32)
        hb = pltpu.roll(hb, hb.shape[0] - 1, axis=0)
        pc = jnp.dot(pos_ref[...], w1, preferred_element_type=F32)[0:1]
        return _gelu_tanh(ha + hb + pc).astype(BF16)

    for gg, x in enumerate(half_windows(zk_ref)):
        hk = hidden(x, kpos_ref, kw1_ref)
        kc_ref[0, gg] = jnp.dot(hk, kw2_ref[...], preferred_element_type=F32).astype(kc_ref.dtype)
    for gg, x in enumerate(half_windows(zv_ref)):
        hv = hidden(x, vpos_ref, vw1_ref)
        vct_ref[0, gg] = lax.dot_general(vw2t_ref[...], hv, (((1,), (1,)), ((), ())),
                                         preferred_element_type=F32).astype(vct_ref.dtype)


def _compress(z, batch, seq, kpos, kw1, kw2, vpos, vw1, vw2t):
    g, hd = NSA_KV_GROUPS, HEAD_DIM
    gpb = LANES // hd
    ncp = seq // CMP_STRIDE
    full = lambda a: pl.BlockSpec(a.shape, lambda i, j: (0,) * a.ndim)
    return pl.pallas_call(
        _compress_kernel,
        out_shape=(jax.ShapeDtypeStruct((batch, g, ncp, hd), BF16),
                   jax.ShapeDtypeStruct((batch, g, hd, ncp), BF16)),
        grid=(batch, g // gpb),
        in_specs=[
            pl.BlockSpec((seq, LANES), lambda i, j: (i, Z_KC // LANES + j)),
            pl.BlockSpec((seq, LANES), lambda i, j: (i, Z_VC // LANES + j)),
            full(kpos), full(kw1), full(kw2), full(vpos), full(vw1), full(vw2t),
        ],
        out_specs=(pl.BlockSpec((1, gpb, ncp, hd), lambda i, j: (i, j, 0, 0)),
                   pl.BlockSpec((1, gpb, hd, ncp), lambda i, j: (i, j, 0, 0))),
        scratch_shapes=[pltpu.VMEM((seq, LANES), F32)],
        compiler_params=_cparams(("parallel", "parallel")),
        name="compress",
    )(z, z, kpos, kw1, kw2, vpos, vw1, vw2t)


def _nsa_kernel(zq_ref, zkk_ref, zvv_ref, zg_ref, kc_ref, vct_ref, o_ref, kcomb_ref, vs_ref, vw_ref, gate_ref,
                sa_ref, sb_ref):
    grp = pl.program_id(1)
    qi = pl.program_id(2)
    tq, hd, hpg = NSA_TQ, HEAD_DIM, NSA_HPG
    nq = tq * hpg
    seq = kcomb_ref.shape[0]
    ncp = kc_ref.shape[2]
    nsb = seq // SLC_BLOCK
    blk_shift = SLC_BLOCK.bit_length() - 1

    @pl.when(qi == 0)
    def _():
        kcomb_ref[:, 0:2 * hd] = zkk_ref[...]
        blk_of_row = lax.broadcasted_iota(jnp.int32, (seq, 2 * hd), 0) >> blk_shift
        lane = lax.broadcasted_iota(jnp.int32, (seq, 2 * hd), 1)
        kcomb_ref[:, 2 * hd:] = jnp.where(blk_of_row == lane, 1.0, 0.0).astype(BF16)
        ones = jnp.ones((seq, hd), BF16)
        vs_ref[...] = jnp.concatenate([zvv_ref[:, 0:hd], ones], axis=1)
        vw_ref[...] = jnp.concatenate([zvv_ref[:, hd:2 * hd], ones], axis=1)

    qt = zq_ref[...].T
    q = jnp.concatenate([qt[h * hd:(h + 1) * hd, :] for h in range(hpg)], axis=1)
    q = (q.astype(F32) * (ATTN_SCALE * LOG2E)).astype(BF16)
    zero = jnp.zeros((hd, nq), BF16)
    t_lane = qi * tq + (lax.broadcasted_iota(jnp.int32, (1, nq), 1) & (tq - 1))
    tn_dims = (((0,), (0,)), ((), ()))
    ck = NSA_CK

    sa_ref[...] = jnp.dot(kcomb_ref[0:ck, 0:2 * hd], jnp.concatenate([q, zero], axis=0), preferred_element_type=F32)

    sc = jnp.dot(kc_ref[0, 0], q, preferred_element_type=F32)
    cmp_end = lax.broadcasted_iota(jnp.int32, (ncp, 1), 0) * CMP_STRIDE + (CMP_BLOCK - 1)
    cmask = cmp_end <= t_lane
    sc = jnp.where(cmask, sc, NEG_INF)
    pc = jnp.exp2(sc - jnp.max(sc, axis=0, keepdims=True))
    lc = jnp.sum(pc, axis=0, keepdims=True)
    pc = pc * jnp.where(t_lane >= CMP_BLOCK - 1, 1.0 / lc, 0.0)
    o_cmp = jnp.dot(vct_ref[0, 0], pc.astype(BF16), preferred_element_type=F32)

    psum = pc[:, 0:tq]
    for h in range(1, hpg):
        psum = psum + pc[:, h * tq:(h + 1) * tq]
    jrow = lax.broadcasted_iota(jnp.int32, (nsb, ncp), 0)
    crel = lax.broadcasted_iota(jnp.int32, (nsb, ncp), 1) - jrow * (SLC_BLOCK // CMP_STRIDE)
    ovt = jnp.where((crel >= 0) & (crel <= 2), 1.0, jnp.where((crel == -1) | (crel == 3), 0.5, 0.0)).astype(F32)
    imp = jnp.dot(ovt, psum, preferred_element_type=F32, precision=lax.Precision.HIGHEST)
    jblk = lax.broadcasted_iota(jnp.int32, (nsb, tq), 0)
    blk = (qi * tq + lax.broadcasted_iota(jnp.int32, (1, tq), 1)) >> blk_shift
    forced = (jblk == 0) | (jblk == blk) | (jblk == blk - 1)
    n_forced = 3
    score = jnp.where(forced, -2.0, jnp.where(jblk <= blk, imp, -1.0))
    selb = jnp.where(forced, 0.0, NEG_INF)
    for _ in range(min(SLC_TOPN, nsb) - n_forced):
        top = jnp.max(score, axis=0, keepdims=True)
        first = jnp.min(jnp.where(score == top, jblk, nsb), axis=0, keepdims=True)
        pick = jblk == first
        selb = jnp.where(pick, 0.0, selb)
        score = jnp.where(pick, -2.0, score)
    bias0 = jnp.concatenate([jnp.broadcast_to(selb[j:j + 1, :], (SLC_BLOCK, tq)) for j in range(ck // SLC_BLOCK)],
                            axis=0)
    sa_ref[...] = sa_ref[...] + jnp.concatenate([bias0] * hpg, axis=1)
    selb = selb.astype(BF16)

    pad = jnp.zeros((2 * hd - nsb, nq), BF16)
    qa = jnp.concatenate([q, zero, jnp.concatenate([selb] * hpg, axis=1), pad], axis=0)

    def slc_scores(c):
        return jnp.dot(kcomb_ref[pl.ds(pl.multiple_of(c * ck, ck), ck), :], qa, preferred_element_type=F32)

    n_full = (qi * tq) // ck

    def slc_update(buf, c, carry, causal):
        def scores():
            if not causal:
                return buf[...]
            return jnp.where(c * ck + lax.broadcasted_iota(jnp.int32, (ck, 1), 0) <= t_lane, buf[...], NEG_INF)

        m, l, acc = carry
        m_new = jnp.maximum(m, jnp.max(scores(), axis=0, keepdims=True))
        alpha = jnp.exp2(m - m_new)
        p = jnp.exp2(scores() - m_new).astype(BF16)
        v = vs_ref[pl.ds(pl.multiple_of(c * ck, ck), ck), :]
        pv = lax.dot_general(v, p, tn_dims, preferred_element_type=F32)
        return m_new, alpha * l + pv[hd:hd + 1], acc * alpha + pv[0:hd]

    def slc_pair(i, carry):
        sb_ref[...] = slc_scores(2 * i + 1)
        carry = slc_update(sa_ref, 2 * i, carry, False)
        sa_ref[...] = slc_scores(2 * i + 2)
        return slc_update(sb_ref, 2 * i + 1, carry, False)

    def tail_two(carry):
        sb_ref[...] = slc_scores(n_full)
        return slc_update(sb_ref, n_full, slc_update(sa_ref, n_full - 1, carry, False), True)

    def tail_one(carry):
        return slc_update(sa_ref, n_full, carry, True)


    nwc = WINDOW // tq + 1
    t_lo = t_lane - WINDOW
    ks_w, vs_w = [], []
    for i in range(nwc):
        rows = pl.ds(pl.multiple_of(jnp.maximum(qi - (nwc - 1) + i, 0) * tq, tq), tq)
        ks_w.append(kcomb_ref[rows, 0:2 * hd])
        vs_w.append(vw_ref[rows, :])
    qw = jnp.concatenate([zero, q], axis=0)
    sw = jnp.dot(jnp.concatenate(ks_w, axis=0), qw, preferred_element_type=F32)
    sw_parts = []
    for i in range(nwc):
        chunk = qi - (nwc - 1) + i
        spos = chunk * tq + lax.broadcasted_iota(jnp.int32, (tq, 1), 0)
        s_i = sw[i * tq:(i + 1) * tq]
        if i == 0:
            sw_parts.append(jnp.where(jnp.where(spos >= 0, spos, -(1 << 24)) > t_lo, s_i, NEG_INF))
        elif i < nwc - 1:
            sw_parts.append(s_i + jnp.where(chunk >= 0, 0.0, NEG_INF))
        else:
            sw_parts.append(jnp.where(spos <= t_lane, s_i, NEG_INF))
    sw = jnp.concatenate(sw_parts, axis=0)
    pw = jnp.exp2(sw - jnp.max(sw, axis=0, keepdims=True))
    ow = lax.dot_general(jnp.concatenate(vs_w, axis=0), pw.astype(BF16), tn_dims, preferred_element_type=F32)
    o_win = ow[0:hd] * (1.0 / ow[hd:hd + 1])
    init = (jnp.full((1, nq), NEG_INF, F32), jnp.zeros((1, nq), F32), jnp.zeros((hd, nq), F32))
    carry = lax.fori_loop(0, n_full // 2, slc_pair, init)
    _, l_s, acc_s = lax.cond(n_full % 2 == 1, tail_two, tail_one, carry)
    o_slc = acc_s * (1.0 / l_s)

    gate_ref[...] = _sigmoid(zg_ref[...].astype(F32)).T

    def branch_gate(br):
        rows = [gate_ref[pl.ds(grp * (hpg * 3) + h * 3 + br, 1), :] for h in range(hpg)]
        return jnp.concatenate(rows, axis=1)

    o = branch_gate(0) * o_cmp + branch_gate(1) * o_slc + branch_gate(2) * o_win
    o_heads = jnp.concatenate([o[:, h * tq:(h + 1) * tq] for h in range(hpg)], axis=0)
    o_ref[...] = o_heads.T.astype(o_ref.dtype)


def _nsa_attention(z, kc, vct, batch, seq):
    n = z.shape[0]
    g, hd, tq = NSA_KV_GROUPS, HEAD_DIM, NSA_TQ
    nqt = seq // tq
    ncp = kc.shape[2]
    nsb = seq // SLC_BLOCK
    gw = NSA_HPG * hd
    assert seq % NSA_CK == 0 and NSA_CK % tq == 0 and WINDOW % tq == 0 and nsb <= 2 * hd
    tile = lambda i, j, k: i * nqt + k
    return pl.pallas_call(
        _nsa_kernel,
        out_shape=jax.ShapeDtypeStruct((n, NSA_Q_WIDTH), BF16),
        grid=(batch, g, nqt),
        in_specs=[
            pl.BlockSpec((tq, gw), lambda i, j, k: (tile(i, j, k), Z_Q // gw + j)),
            pl.BlockSpec((seq, 2 * hd), lambda i, j, k: (i, Z_KK // (2 * hd) + j)),
            pl.BlockSpec((seq, 2 * hd), lambda i, j, k: (i, Z_VV // (2 * hd) + j)),
            pl.BlockSpec((tq, LANES), lambda i, j, k: (tile(i, j, k), Z_GATE // LANES)),
            pl.BlockSpec((1, 1, ncp, hd), lambda i, j, k: (i, j, 0, 0)),
            pl.BlockSpec((1, 1, hd, ncp), lambda i, j, k: (i, j, 0, 0)),
        ],
        out_specs=pl.BlockSpec((tq, gw), lambda i, j, k: (tile(i, j, k), j)),
        scratch_shapes=[pltpu.VMEM((seq, 4 * hd), BF16), pltpu.VMEM((seq, 2 * hd), BF16),
                        pltpu.VMEM((seq, 2 * hd), BF16), pltpu.VMEM((LANES, tq), F32), pltpu.VMEM((NSA_CK, NSA_HPG * tq), F32),
                        pltpu.VMEM((NSA_CK, NSA_HPG * tq), F32)],
        compiler_params=_cparams(("parallel", "parallel", "arbitrary")),
        name="nsa_attn",
    )(z, z, z, z, kc, vct)


def _rglru_kernel(x_ref, y_ref, cw_ref, cb_ref, wa_ref, ba_ref, wx_ref, bx_ref, lam_ref, o_ref,
                  tail_ref, h_ref, a_ref, u_ref):
    tc = pl.program_id(2)
    tt, cb = x_ref.shape

    @pl.when(tc == 0)
    def _():
        tail_ref[...] = jnp.zeros_like(tail_ref)
        h_ref[...] = jnp.zeros_like(h_ref)

    x = x_ref[...].astype(F32)
    xe = jnp.concatenate([tail_ref[...], x], axis=0)
    tail_ref[...] = x[tt - SUBLANES:, :]
    cw = cw_ref[...]
    xc = cb_ref[...]
    for k in range(CONV_WIDTH):
        off = SUBLANES - (CONV_WIDTH - 1) + k
        xc = xc + cw[k:k + 1, :] * xe[off:off + tt, :]
    xcb = xc.astype(BF16)
    def gate(w_ref, b_ref):
        parts = [jnp.dot(xcb[:, c * LRU_WT:(c + 1) * LRU_WT], w_ref[c], preferred_element_type=F32)
                 for c in range(cb // LRU_WT)]
        return _sigmoid(jnp.concatenate(parts, axis=1) + b_ref[...])

    r = gate(wa_ref, ba_ref)
    ig = gate(wx_ref, bx_ref)
    nl = -lam_ref[...]
    softplus = jnp.maximum(nl, 0.0) + jnp.log1p(jnp.exp(-jnp.abs(nl)))
    log_a = (-LRU_C) * softplus * r
    a_ref[...] = jnp.exp(log_a)
    th = jnp.tanh(log_a)
    u_ref[...] = jnp.sqrt(-2.0 * th / (1.0 - th)) * (ig * xc)

    row = lax.broadcasted_iota(jnp.int32, (SUBLANES, cb), 0)

    def step(i, h):
        sl = pl.ds(pl.multiple_of(i * SUBLANES, SUBLANES), SUBLANES)
        a = a_ref[sl, :]
        u = u_ref[sl, :]
        for s in (1, 2, 4):
            a_s = jnp.where(row >= s, pltpu.roll(a, s, axis=0), 1.0)
            u_s = jnp.where(row >= s, pltpu.roll(u, s, axis=0), 0.0)
            u = a * u_s + u
            a = a * a_s
        hrows = a * h + u
        u_ref[sl, :] = hrows
        return hrows[SUBLANES - 1:SUBLANES, :]

    h_ref[...] = lax.fori_loop(0, tt // SUBLANES, step, h_ref[...], unroll=8)
    o_ref[...] = (u_ref[...] * _gelu_tanh(y_ref[...].astype(F32))).astype(o_ref.dtype)


def _rglru(z, batch, seq, cw, cbias, wa_bd, ba, wx_bd, bx, lam, tt=1024):
    n = z.shape[0]
    ncb = LRU_WIDTH // LRU_CB
    nt = seq // tt
    row = lambda i, j, k: i * nt + k
    vec = lambda r: pl.BlockSpec((r, LRU_CB), lambda i, j, k: (0, j))
    return pl.pallas_call(
        _rglru_kernel,
        out_shape=jax.ShapeDtypeStruct((n, LRU_WIDTH), BF16),
        grid=(batch, ncb, nt),
        in_specs=[
            pl.BlockSpec((tt, LRU_CB), lambda i, j, k: (row(i, j, k), Z_LX // LRU_CB + j)),
            pl.BlockSpec((tt, LRU_CB), lambda i, j, k: (row(i, j, k), Z_LY // LRU_CB + j)),
            vec(CONV_WIDTH), vec(1),
            pl.BlockSpec((LRU_CB // LRU_WT, LRU_WT, LRU_WT), lambda i, j, k: (j, 0, 0)), vec(1),
            pl.BlockSpec((LRU_CB // LRU_WT, LRU_WT, LRU_WT), lambda i, j, k: (j, 0, 0)), vec(1),
            vec(1),
        ],
        out_specs=pl.BlockSpec((tt, LRU_CB), lambda i, j, k: (row(i, j, k), j)),
        scratch_shapes=[pltpu.VMEM((SUBLANES, LRU_CB), F32), pltpu.VMEM((1, LRU_CB), F32),
                        pltpu.VMEM((tt, LRU_CB), F32), pltpu.VMEM((tt, LRU_CB), F32)],
        compiler_params=_cparams(("parallel", "parallel", "arbitrary")),
        name="rglru",
    )(z, z, cw, cbias, wa_bd, ba, wx_bd, bx, lam)


def _merge_kernel(o_ref, l_ref, mga_ref, mgb_ref, wn_ref, wl_ref, m_ref):
    ya = jnp.dot(o_ref[...], wn_ref[...], preferred_element_type=F32)
    yb = jnp.dot(l_ref[...], wl_ref[...], preferred_element_type=F32)
    m = _sigmoid(mga_ref[...].astype(F32)) * ya + _sigmoid(mgb_ref[...].astype(F32)) * yb
    m_ref[...] = m.astype(m_ref.dtype)


def _merge(o, lru, z, wn, wl, tm=512):
    n, d = o.shape[0], wn.shape[1]
    return pl.pallas_call(
        _merge_kernel,
        out_shape=jax.ShapeDtypeStruct((n, d), BF16),
        grid=(n // tm,),
        in_specs=[
            pl.BlockSpec((tm, o.shape[1]), lambda i: (i, 0)),
            pl.BlockSpec((tm, lru.shape[1]), lambda i: (i, 0)),
            pl.BlockSpec((tm, d), lambda i: (i, Z_MGA // d)),
            pl.BlockSpec((tm, d), lambda i: (i, Z_MGB // d)),
            pl.BlockSpec(wn.shape, lambda i: (0, 0)),
            pl.BlockSpec(wl.shape, lambda i: (0, 0)),
        ],
        out_specs=pl.BlockSpec((tm, d), lambda i: (i, 0)),
        compiler_params=_cparams(("parallel",)),
        name="merge",
    )(o, lru, z, z, wn, wl)


def _out_route_kernel(m_ref, x_ref, wo_ref, g_ref, wr_ref, br_ref, x1_ref, xn_ref, eid_ref, ew_ref, *, th):
    for r0 in range(0, m_ref.shape[0], th):
        _out_route_rows(r0, th, m_ref, x_ref, wo_ref, g_ref, wr_ref, br_ref, x1_ref, xn_ref, eid_ref, ew_ref)


def _out_route_rows(r0, tm, m_ref, x_ref, wo_ref, g_ref, wr_ref, br_ref, x1_ref, xn_ref, eid_ref, ew_ref):
    x1 = x_ref[r0:r0 + tm] + jnp.dot(m_ref[r0:r0 + tm], wo_ref[...], preferred_element_type=F32)
    x1_ref[r0:r0 + tm] = x1
    xn = _rms(x1, g_ref[...])
    _store_rows(xn_ref, r0, xn)
    lg = jnp.dot(xn, wr_ref[...], preferred_element_type=F32).T + br_ref[...]
    sub = lax.broadcasted_iota(jnp.int32, (SUBLANES, tm), 0)

    def first_argmax(v, vmax):
        return jnp.min(jnp.where(v == vmax, sub, SUBLANES), axis=0, keepdims=True)

    gl = jnp.where(sub < N_GROUPS, lg[0:SUBLANES], -jnp.inf)
    gmax = jnp.max(gl, axis=0, keepdims=True)
    ge = jnp.exp(gl - gmax)
    gprob = ge / jnp.sum(ge, axis=0, keepdims=True)
    g_val = jnp.max(gprob, axis=0, keepdims=True)
    g_idx = first_argmax(gprob, g_val)
    e_in = jnp.zeros((EXPERTS_PER_GROUP, tm), F32)
    for gi in range(N_GROUPS):
        lo = SUBLANES + gi * EXPERTS_PER_GROUP
        e_in = jnp.where(g_idx == gi, lg[lo:lo + EXPERTS_PER_GROUP], e_in)
    ee = jnp.exp(e_in - jnp.max(e_in, axis=0, keepdims=True))
    eprob = ee / jnp.sum(ee, axis=0, keepdims=True)
    v1 = jnp.max(eprob, axis=0, keepdims=True)
    i1 = first_argmax(eprob, v1)
    rest = jnp.where(sub == i1, -1.0, eprob)
    v2 = jnp.max(rest, axis=0, keepdims=True)
    i2 = first_argmax(rest, v2)
    den = v1 + v2
    eid = jnp.where(sub == 0, g_idx * EXPERTS_PER_GROUP + i1, g_idx * EXPERTS_PER_GROUP + i2)
    eid_ref[:, r0:r0 + tm] = eid
    ew_ref[:, r0:r0 + tm] = jnp.where(sub == 0, g_val * v1 / den, g_val * v2 / den)


def _out_route(m, x2d, wo, g, wr, br, tm=512, th=256):
    n, d = x2d.shape
    once = pl.Buffered(1)
    return pl.pallas_call(
        functools.partial(_out_route_kernel, th=th),
        out_shape=(jax.ShapeDtypeStruct((n, d), F32), jax.ShapeDtypeStruct((n * _row_pitch(d), LANES), F32),
                   jax.ShapeDtypeStruct((SUBLANES, n), jnp.int32), jax.ShapeDtypeStruct((SUBLANES, n), F32)),
        grid=(n // tm,),
        in_specs=[
            pl.BlockSpec((tm, d), lambda i: (i, 0)),
            pl.BlockSpec((tm, d), lambda i: (i, 0)),
            pl.BlockSpec((d, d), lambda i: (0, 0), pipeline_mode=once),
            pl.BlockSpec((1, d), lambda i: (0, 0)),
            pl.BlockSpec(wr.shape, lambda i: (0, 0), pipeline_mode=once),
            pl.BlockSpec(br.shape, lambda i: (0, 0)),
        ],
        out_specs=(pl.BlockSpec((tm, d), lambda i: (i, 0)), pl.BlockSpec((tm * _row_pitch(d), LANES), lambda i: (i, 0)),
                   pl.BlockSpec((SUBLANES, tm), lambda i: (0, i)), pl.BlockSpec((SUBLANES, tm), lambda i: (0, i))),
        compiler_params=_cparams(("parallel",)),
        name="out_route",
    )(m, x2d, wo, g, wr, br)


def _for_rows(cnt, fn, unroll=8):
    sh = unroll.bit_length() - 1

    def group(gidx, c):
        for u in range(unroll):
            fn(gidx * unroll + u)
        return c

    def single(r, c):
        fn(r)
        return c

    lax.fori_loop(0, cnt >> sh, group, 0)
    lax.fori_loop((cnt >> sh) << sh, cnt, single, 0)


def _moe_kernel(nvalid_ref, texp_ref, cnt_ref, wslot_ref, nxt_ref, tok_ref, tokn_ref, dst_ref, xn_hbm, wg_hbm, wu_hbm,
                wd_hbm, y_hbm, xbuf, ybuf, wgbuf, wubuf, wdbuf, gsem, ssem, wsem):
    i = pl.program_id(0)
    nv = nvalid_ref[0]
    slot = i % 2
    d = wg_hbm.shape[1]
    expert = texp_ref[i]
    ws = wslot_ref[i]
    run_start = jnp.logical_or(i == 0, texp_ref[jnp.maximum(i - 1, 0)] != expert)

    def weight_copies(e, s):
        return [pltpu.make_async_copy(src.at[e], buf.at[s], wsem.at[s])
                for src, buf in ((wg_hbm, wgbuf), (wu_hbm, wubuf), (wd_hbm, wdbuf))]
    k = _row_pitch(d)
    tm = xbuf.shape[0] // (2 * k)

    def lines(ref, first):
        return ref.at[pl.ds(first, k), :]

    def gather_row(idx_ref, s):
        def fn(r):
            pltpu.make_async_copy(lines(xn_hbm, idx_ref[0, 0, r]), lines(xbuf, (s * tm + r) * k), gsem.at[s]).start()
        return fn

    def gather_wait_row(s):
        def fn(r):
            pltpu.make_async_copy(lines(xn_hbm, 0), lines(xbuf, (s * tm + r) * k), gsem.at[s]).wait()
        return fn

    def scatter_row(s):
        def fn(r):
            pltpu.make_async_copy(lines(ybuf, (s * tm + r) * k), lines(y_hbm, dst_ref[0, 0, r]), ssem.at[s]).start()
        return fn

    def scatter_wait_row(s):
        def fn(r):
            pltpu.make_async_copy(lines(ybuf, (s * tm + r) * k), lines(y_hbm, 0), ssem.at[s]).wait()
        return fn

    @pl.when(i == 0)
    def _():
        xbuf[...] = jnp.zeros_like(xbuf)
        _for_rows(cnt_ref[0], gather_row(tok_ref, 0))
        for c in weight_copies(expert, 0):
            c.start(priority=1)

    @pl.when(i + 1 < nv)
    def _():
        _for_rows(cnt_ref[i + 1], gather_row(tokn_ref, 1 - slot))

    @pl.when(jnp.logical_and(jnp.logical_and(i < nv, run_start), nxt_ref[i] >= 0))
    def _():
        for c in weight_copies(nxt_ref[i], 1 - ws):
            c.start(priority=1)

    @pl.when(i < nv)
    def _():
        _for_rows(cnt_ref[i], gather_wait_row(slot))

        @pl.when(run_start)
        def _():
            for c in weight_copies(expert, ws):
                c.wait()

        x = _load_rows(xbuf, slot * tm, tm, d).astype(BF16)
        gp = jnp.dot(x, wgbuf[ws], preferred_element_type=F32)
        up = jnp.dot(x, wubuf[ws], preferred_element_type=F32)
        hid = (gp * _sigmoid(gp) * up).astype(BF16)
        y = jnp.dot(hid, wdbuf[ws], preferred_element_type=F32)

        @pl.when(i >= 2)
        def _():
            _for_rows(cnt_ref[i - 2], scatter_wait_row(slot))

        _store_rows(ybuf, slot * tm, y)
        _for_rows(cnt_ref[i], scatter_row(slot))

    @pl.when(i == nv - 1)
    def _():
        _for_rows(cnt_ref[i], scatter_wait_row(slot))

        @pl.when(nv >= 2)
        def _():
            _for_rows(cnt_ref[i - 1], scatter_wait_row(1 - slot))


def _moe(nvalid, texp, cnt, wslot, nxt, tok, dst, xn, wg, wu, wd, n_rows_out):
    nt, _, tm = tok.shape
    d, de = wg.shape[1], wg.shape[2]
    k = _row_pitch(d)
    grid_spec = pltpu.PrefetchScalarGridSpec(
        num_scalar_prefetch=5,
        grid=(nt,),
        in_specs=[
            pl.BlockSpec((1, 1, tm), lambda i, *_: (i, 0, 0), memory_space=pltpu.SMEM),
            pl.BlockSpec((1, 1, tm), lambda i, *_: (jnp.minimum(i + 1, nt - 1), 0, 0), memory_space=pltpu.SMEM),
            pl.BlockSpec((1, 1, tm), lambda i, *_: (i, 0, 0), memory_space=pltpu.SMEM),
            pl.BlockSpec(memory_space=pl.ANY),
            pl.BlockSpec(memory_space=pl.ANY),
            pl.BlockSpec(memory_space=pl.ANY),
            pl.BlockSpec(memory_space=pl.ANY),
        ],
        out_specs=pl.BlockSpec(memory_space=pl.ANY),
        scratch_shapes=[pltpu.VMEM((2 * tm * k, LANES), F32), pltpu.VMEM((2 * tm * k, LANES), F32),
                        pltpu.VMEM((2, d, de), F32), pltpu.VMEM((2, d, de), F32), pltpu.VMEM((2, de, d), F32),
                        pltpu.SemaphoreType.DMA((2,)), pltpu.SemaphoreType.DMA((2,)), pltpu.SemaphoreType.DMA((2,))],
    )
    return pl.pallas_call(
        _moe_kernel,
        out_shape=jax.ShapeDtypeStruct((n_rows_out * k, LANES), F32),
        grid_spec=grid_spec,
        compiler_params=_cparams(("arbitrary",)),
        name="moe",
    )(nvalid, texp, cnt, wslot, nxt, tok, tok, dst, xn, wg, wu, wd)


def _moe_plan(eid, n, tm):
    e_flat = eid.reshape(-1)
    npairs = e_flat.shape[0]
    experts = jnp.arange(N_EXPERTS, dtype=jnp.int32)
    counts = jnp.sum((e_flat[:, None] == experts[None, :]).astype(jnp.int32), axis=0)
    padded = ((counts + tm - 1) // tm) * tm
    ends = jnp.cumsum(padded)
    offs = ends - padded
    n_rows = npairs + N_EXPERTS * tm
    nt = n_rows // tm
    fill = jnp.arange(n_rows - npairs, dtype=jnp.int32)
    fill_expert = jnp.sum((fill[:, None] >= jnp.cumsum(padded - counts)[None, :]).astype(jnp.int32), axis=1)
    keys = jnp.concatenate([e_flat * (2 * npairs) + jnp.arange(npairs, dtype=jnp.int32),
                            fill_expert * (2 * npairs) + npairs])
    skeys = jnp.sort(keys)
    low = skeys % (2 * npairs)
    dst = jnp.where(low < npairs, low, 0)
    tok = dst % n
    tile_start = jnp.arange(nt, dtype=jnp.int32) * tm
    texp = jnp.minimum(jnp.sum((tile_start[:, None] >= ends[None, :]).astype(jnp.int32), axis=1), N_EXPERTS - 1)
    cnt = jnp.clip(offs[texp] + counts[texp] - tile_start, 0, tm)
    nvalid = (ends[-1] // tm).astype(jnp.int32).reshape(1)
    wslot = jnp.cumsum(jnp.concatenate([jnp.zeros((1,), jnp.int32), (texp[1:] != texp[:-1]).astype(jnp.int32)])) % 2
    live = jnp.where(tile_start < ends[-1], texp, N_EXPERTS)
    nxt = jnp.min(jnp.where(live[None, :] > texp[:, None], live[None, :], N_EXPERTS), axis=1)
    nxt = jnp.where(nxt < N_EXPERTS, nxt, -1)
    return (nvalid, texp.astype(jnp.int32), cnt.astype(jnp.int32), wslot.astype(jnp.int32), nxt.astype(jnp.int32),
            tok.reshape(nt, 1, tm), dst.reshape(nt, 1, tm), npairs)


def _ple_final_kernel(x1_ref, y0_ref, y1_ref, w_ref, p_ref, gp_ref, wpg_ref, wp_ref, gf_ref, o_ref, *, th):
    d = x1_ref.shape[1]
    for r0 in range(0, x1_ref.shape[0], th):
        w = w_ref[r0:r0 + th]
        x2 = x1_ref[r0:r0 + th] + w[:, 0:1] * _load_rows(y0_ref, r0, th, d) + w[:, 1:2] * _load_rows(y1_ref, r0, th, d)
        hn = _rms(x2, gp_ref[...]).astype(BF16)
        gate = _sigmoid(jnp.dot(hn, wpg_ref[...], preferred_element_type=F32))
        pe = jnp.dot(p_ref[r0:r0 + th].astype(BF16), wp_ref[...], preferred_element_type=F32)
        x3 = x2 + gate * pe
        o_ref[r0:r0 + th] = _rms(x3, gf_ref[...])


def _ple_final(x1, ypairs, wcols, p2d, gp, wpg, wp, gf, tm=512, th=256):
    n, d = x1.shape
    nb = n // tm
    k = _row_pitch(d)
    return pl.pallas_call(
        functools.partial(_ple_final_kernel, th=th),
        out_shape=jax.ShapeDtypeStruct((n, d), F32),
        grid=(nb,),
        in_specs=[
            pl.BlockSpec((tm, d), lambda i: (i, 0)),
            pl.BlockSpec((tm * k, LANES), lambda i: (i, 0)),
            pl.BlockSpec((tm * k, LANES), lambda i: (nb + i, 0)),
            pl.BlockSpec((tm, wcols.shape[1]), lambda i: (i, 0)),
            pl.BlockSpec((tm, p2d.shape[1]), lambda i: (i, 0)),
            pl.BlockSpec((1, d), lambda i: (0, 0)),
            pl.BlockSpec((d, d), lambda i: (0, 0), pipeline_mode=pl.Buffered(1)),
            pl.BlockSpec(wp.shape, lambda i: (0, 0)),
            pl.BlockSpec((1, d), lambda i: (0, 0)),
        ],
        out_specs=pl.BlockSpec((tm, d), lambda i: (i, 0)),
        compiler_params=_cparams(("parallel",)),
        name="ple_final",
    )(x1, ypairs, ypairs, wcols, p2d, gp, wpg, wp, gf)


def _block_diag(w, per):
    nb, bw, _ = w.shape
    w = w.reshape(nb // per, per, bw, bw)
    eye = jnp.eye(per, dtype=w.dtype)
    return jnp.einsum("cpij,pq->cpiqj", w, eye).reshape(nb // per, per * bw, per * bw)


def _regroup_kernel(wt_ref, o_ref):
    d = Z_MGB - Z_MGA
    c = np.cumsum((0, NSA_Q_WIDTH) + (NSA_KV_WIDTH,) * 6 + (3 * NSA_HEADS, LRU_WIDTH, LRU_WIDTH, d, d))
    seg = lambda k: wt_ref[int(c[k]):int(c[k + 1]), :]
    q, k_c, v_c, k_s, v_s, k_w, v_w, gates, lru_x, lru_y, mg_a, mg_b = (seg(k) for k in range(12))

    def pair(a, b):
        parts = []
        for g in range(NSA_KV_GROUPS):
            parts += [a[g * HEAD_DIM:(g + 1) * HEAD_DIM], b[g * HEAD_DIM:(g + 1) * HEAD_DIM]]
        return jnp.concatenate(parts, axis=0)

    pad = jnp.zeros((Z_WIDTH - int(c[-1]), wt_ref.shape[1]), wt_ref.dtype)
    rows = [mg_a, mg_b, q, pair(k_s, k_w), pair(v_s, v_w), k_c, v_c, lru_x, lru_y, gates, pad]
    o_ref[...] = jnp.concatenate(rows, axis=0).T.astype(o_ref.dtype)


def _regroup_w_in(w_in_t, tr=256):
    nw, d = w_in_t.shape
    return pl.pallas_call(
        _regroup_kernel,
        out_shape=jax.ShapeDtypeStruct((d, Z_WIDTH), BF16),
        grid=(d // tr,),
        in_specs=[pl.BlockSpec((nw, tr), lambda i: (0, i))],
        out_specs=pl.BlockSpec((tr, Z_WIDTH), lambda i: (i, 0)),
        compiler_params=_cparams(("parallel",)),
        name="regroup_w_in",
    )(w_in_t)


def _layer(x2d, batch, seq, ln_mix, w_in, cmp_k_pos, cmp_k_w1, cmp_k_w2, cmp_v_pos, cmp_v_w1, cmp_v_w2, conv_w,
           conv_b, lru_wa, lru_ba, lru_wx, lru_bx, lru_lambda, w_nsa_up, w_lru_up, w_out, ln_ffn, w_grp, b_grp, w_exp,
           b_exp, w_gate, w_up, w_down):
    n, d = x2d.shape
    g, hd = NSA_KV_GROUPS, HEAD_DIM

    z = _in_proj(x2d, ln_mix.reshape(1, d), _regroup_w_in(w_in.T))

    pos8 = lambda pos: jnp.broadcast_to(pos.reshape(1, -1), (SUBLANES, pos.size)).astype(BF16)
    kc, vct = _compress(z, batch, seq, pos8(cmp_k_pos), cmp_k_w1.astype(BF16), cmp_k_w2.astype(BF16),
                        pos8(cmp_v_pos), cmp_v_w1.astype(BF16), cmp_v_w2.T.astype(BF16))
    o = _nsa_attention(z, kc, vct, batch, seq)

    per = LRU_WT // LRU_BW
    lru = _rglru(z, batch, seq, conv_w, conv_b.reshape(1, -1), _block_diag(lru_wa, per).astype(BF16),
                 lru_ba.reshape(1, -1), _block_diag(lru_wx, per).astype(BF16), lru_bx.reshape(1, -1),
                 lru_lambda.reshape(1, -1))

    merged = _merge(o, lru, z, w_nsa_up.astype(BF16), w_lru_up.astype(BF16))

    wr = jnp.zeros((d, LANES), F32).at[:, 0:N_GROUPS].set(w_grp).at[:, SUBLANES:SUBLANES + N_EXPERTS].set(w_exp)
    br = jnp.zeros((LANES, 1), F32).at[0:N_GROUPS, 0].set(b_grp).at[SUBLANES:SUBLANES + N_EXPERTS, 0].set(b_exp)
    x1, xn, eid, ew = _out_route(merged, x2d, w_out.astype(BF16), ln_ffn.reshape(1, d), wr, br)

    nvalid, texp, cnt, wslot, nxt, tok, dst, n_rows_out = _moe_plan(eid[0:EXPERT_TOPK], n, MOE_TM)
    pitch = _row_pitch(d)
    ypairs = _moe(nvalid, texp, cnt, wslot, nxt, tok * pitch, dst * pitch, xn, w_gate, w_up, w_down, n_rows_out)

    return x1, ypairs, ew.T


def kernel(x, p, ln_mix, w_in, cmp_k_pos, cmp_k_w1, cmp_k_w2, cmp_v_pos, cmp_v_w1, cmp_v_w2, conv_w, conv_b, lru_wa, lru_ba, lru_wx, lru_bx, lru_lambda, w_nsa_up, w_lru_up, w_out, ln_ffn, w_grp, b_grp, w_exp, b_exp, w_gate, w_up, w_down, ln_ple, w_ple, w_ple_gate, ln_final):
    batch, seq, d = x.shape
    assert p.shape[0] == 1, "the final norm is fused into the (single) layer's last kernel"
    n = batch * seq
    x1, ypairs, wcols = _layer(
        x.reshape(n, d), batch, seq, ln_mix[0], w_in[0], cmp_k_pos[0], cmp_k_w1[0], cmp_k_w2[0],
        cmp_v_pos[0], cmp_v_w1[0], cmp_v_w2[0], conv_w[0], conv_b[0], lru_wa[0], lru_ba[0], lru_wx[0], lru_bx[0],
        lru_lambda[0], w_nsa_up[0], w_lru_up[0], w_out[0], ln_ffn[0], w_grp[0], b_grp[0], w_exp[0], b_exp[0],
        w_gate[0], w_up[0], w_down[0])
    out = _ple_final(x1, ypairs, wcols, p[0].reshape(n, -1), ln_ple[0].reshape(1, d),
                     w_ple_gate[0].astype(BF16), w_ple[0].astype(BF16), ln_final.reshape(1, d))
    return out.reshape(batch, seq, d)
```
